```python
import math
import jax
import jax.numpy as jnp
from jax import lax
import numpy as np

D_MODEL = 1024
BATCH = 4
SEQ = 8192
DEPTH = 2

GRID_W = 64
CTX_LEN = 256
N_BRANCH = 4
BRANCH_W = D_MODEL // N_BRANCH
HEAD_DIM = 64
BRANCH_HEADS = BRANCH_W // HEAD_DIM
BLOCK = 128
ROPE_BASE = 10000.0
NORM_EPS = 1e-6
NEG_INF = -1e30
ATT_Q_HEADS = BRANCH_HEADS
ATT_KV_HEADS = BRANCH_HEADS // 2
WINDOW = 128
RET_HEADS = BRANCH_HEADS
RET_CHUNK = 128
RET_GN_EPS = 1e-5
MLA_HEADS = BRANCH_HEADS
MLA_Q_RANK = 256
MLA_KV_RANK = 128
MLA_NOPE = HEAD_DIM
MLA_ROPE = HEAD_DIM // 2
MLA_V = HEAD_DIM
RWKV_HEADS = BRANCH_HEADS
RWKV_DECAY_RANK = 64
RWKV_A_RANK = 64
RWKV_GATE_RANK = 128
RWKV_GN_EPS = 64e-5
N_EXPERTS = 64
N_GROUPS = 8
EXPERTS_PER_GROUP = N_EXPERTS // N_GROUPS
TOP_K = 2
EXPERT_FF = 512
MOE_BLOCK = 256
A_SIZES = (ATT_Q_HEADS * HEAD_DIM, ATT_KV_HEADS * HEAD_DIM, ATT_KV_HEADS * HEAD_DIM)
RET_SIZES = (BRANCH_W,) * 5
MLA_SIZES = (MLA_Q_RANK, MLA_KV_RANK, MLA_ROPE)
RWKV_SIZES = (BRANCH_W, BRANCH_W, BRANCH_W, RWKV_DECAY_RANK, RWKV_DECAY_RANK,
              RWKV_A_RANK, RWKV_A_RANK, RWKV_GATE_RANK)
GROUP_SIZES = (sum(A_SIZES), sum(RET_SIZES), sum(MLA_SIZES), sum(RWKV_SIZES))
N_IN = sum(GROUP_SIZES)
F32 = jnp.float32

kernel_name = 'hybrid_gated_four_mixer_moe_dit'


def split_sizes(x, sizes):
    out, o = [], 0
    for n in sizes:
        out.append(x[..., o:o + n])
        o += n
    return out


def heads(x, h):
    return x.reshape(x.shape[:-1] + (h, x.shape[-1] // h))


def rms_norm(x, g):
    xf = x.astype(F32)
    y = xf * lax.rsqrt(jnp.mean(xf * xf, axis=-1, keepdims=True) + NORM_EPS)
    return (y * g.astype(F32)).astype(x.dtype)


def head_norm(x, eps):
    xf = x.astype(F32)
    xc = xf - jnp.mean(xf, axis=-1, keepdims=True)
    return xc * lax.rsqrt(jnp.mean(xc * xc, axis=-1, keepdims=True) + eps)


def adaln(h, g, shift, scale):
    return rms_norm(h, g) * (1 + scale) + shift


def rope_tables(pos, dim):
    inv = ROPE_BASE ** (-jnp.arange(0, dim, 2, dtype=F32) / dim)
    ang = pos[:, None] * inv[None, :]
    return jnp.cos(ang), jnp.sin(ang)


def rope_rotate(x, cos, sin):
    m = x.shape[-1] // 2
    x1, x2 = x[..., :m], x[..., m:]
    c = cos[None, :, None, :]
    s = sin[None, :, None, :]
    return jnp.concatenate([x1 * c - x2 * s, x2 * c + x1 * s], axis=-1).astype(x.dtype)


def axial_tables(seq, dim):
    rows = seq // GRID_W
    row = jnp.repeat(jnp.arange(rows, dtype=F32), GRID_W)
    col = jnp.broadcast_to(jnp.arange(GRID_W, dtype=F32)[None, :], (rows, GRID_W)).reshape(-1)
    return rope_tables(row, dim // 2), rope_tables(col, dim // 2)


def axial_rope(x, tabs):
    (cos_r, sin_r), (cos_c, sin_c) = tabs
    half = x.shape[-1] // 2
    return jnp.concatenate([rope_rotate(x[..., :half], cos_r, sin_r),
                            rope_rotate(x[..., half:], cos_c, sin_c)], axis=-1)


def window_gqa(q, k, v, k_ctx, v_ctx, sink):
    b, s, hq, d = q.shape
    g = hq // ATT_KV_HEADS
    nb = s // BLOCK
    nw = 3 * BLOCK
    nc = k_ctx.shape[1]
    qb = (q * d ** -0.5).reshape(b, nb, BLOCK, ATT_KV_HEADS, g, d)
    pad = ((0, 0), (BLOCK, BLOCK), (0, 0), (0, 0))
    kp = jnp.pad(k, pad).reshape(b, nb + 2, BLOCK, ATT_KV_HEADS, d)
    vp = jnp.pad(v, pad).reshape(b, nb + 2, BLOCK, ATT_KV_HEADS, d)
    kw = jnp.concatenate([kp[:, :-2], kp[:, 1:-1], kp[:, 2:]], axis=2)
    vw = jnp.concatenate([vp[:, :-2], vp[:, 1:-1], vp[:, 2:]], axis=2)
    blk0 = jnp.arange(nb)[:, None, None] * BLOCK
    q_pos = blk0 + jnp.arange(BLOCK)[None, :, None]
    k_pos = blk0 - BLOCK + jnp.arange(nw)[None, None, :]
    valid = (jnp.abs(k_pos - q_pos) <= WINDOW) & (k_pos >= 0) & (k_pos < s)
    s_win = jnp.einsum('bnqhgd,bnkhd->bnhgqk', qb, kw).astype(F32)
    s_win = jnp.where(valid[None, :, None, None], s_win, NEG_INF)
    s_ctx = jnp.einsum('bnqhgd,bchd->bnhgqc', qb, k_ctx).astype(F32)
    s_sink = jnp.broadcast_to(sink.astype(F32).reshape(ATT_KV_HEADS, g, 1, 1), s_win.shape[:-1] + (1,))
    p = jax.nn.softmax(jnp.concatenate([s_win, s_ctx, s_sink], axis=-1), axis=-1).astype(v.dtype)
    o = (jnp.einsum('bnhgqk,bnkhd->bnqhgd', p[..., :nw], vw)
         + jnp.einsum('bnhgqc,bchd->bnqhgd', p[..., nw:nw + nc], v_ctx))
    return o.reshape(b, s, hq * d)


def ctx_gqa(q, k, v, sink):
    b, l, hq, d = q.shape
    g = hq // ATT_KV_HEADS
    qg = (q * d ** -0.5).reshape(b, l, ATT_KV_HEADS, g, d)
    sc = jnp.einsum('bqhgd,bkhd->bhgqk', qg, k).astype(F32)
    s_sink = jnp.broadcast_to(sink.astype(F32).reshape(ATT_KV_HEADS, g, 1, 1), sc.shape[:-1] + (1,))
    p = jax.nn.softmax(jnp.concatenate([sc, s_sink], axis=-1), axis=-1).astype(v.dtype)
    o = jnp.einsum('bhgqk,bkhd->bqhgd', p[..., :l], v)
    return o.reshape(b, l, hq * d)


def retention_log_gammas():
    lg = jnp.log(1.0 - jnp.exp(jnp.linspace(math.log(1.0 / 32), math.log(1.0 / 512), 2 * RET_HEADS, dtype=F32)))
    return lg[0::2], lg[1::2]


def retention_chunked(q, k, v, log_gamma, s0):
    b, t, h, d = q.shape
    n = t // RET_CHUNK
    cl = RET_CHUNK
    qc = q.reshape(b, n, cl, h, d).astype(F32)
    kc = k.reshape(b, n, cl, h, d).astype(F32)
    vc = v.reshape(b, n, cl, h, d).astype(F32)
    idx = jnp.arange(cl, dtype=F32)
    rel = idx[:, None] - idx[None, :]
    d_intra = jnp.where(rel >= 0, jnp.exp(jnp.maximum(rel, 0.0)[None] * log_gamma[:, None, None]), 0.0)
    a_int = jnp.einsum('bnihd,bnjhd->bnhij', qc, kc) * d_intra
    o = jnp.einsum('bnhij,bnjhe->bnihe', a_int, vc)
    k_dec = kc * jnp.exp((cl - 1 - idx)[:, None] * log_gamma[None, :])[..., None]
    kv = jnp.einsum('bnjhd,bnjhe->nbhde', k_dec, vc)
    g_chunk = jnp.exp(cl * log_gamma)[:, None, None]

    def step(state, kv_n):
        return state * g_chunk + kv_n, state

    s_fin, s_prev = lax.scan(step, s0, kv)
    q_dec = qc * jnp.exp((idx + 1)[:, None] * log_gamma[None, :])[..., None]
    o = o + jnp.einsum('bnihd,nbhde->bnihe', q_dec, s_prev)
    return o.reshape(b, t, h, d), s_fin


def retention_mix(pb, cos, sin, lg_f, lg_b, s0f, s0b):
    bsz, t = pb.shape[:2]
    q, k, v, gf, gb = split_sizes(pb, RET_SIZES)
    q = rope_rotate(heads(q, RET_HEADS), cos, sin)
    k = rope_rotate(heads(k, RET_HEADS), cos, sin) * HEAD_DIM ** -0.5
    v = heads(v, RET_HEADS)
    yf, sf = retention_chunked(q, k, v, lg_f, s0f)
    yb, sb = retention_chunked(q[:, ::-1], k[:, ::-1], v[:, ::-1], lg_b, s0b)
    out = (jax.nn.silu(gf) * head_norm(yf, RET_GN_EPS).reshape(bsz, t, -1)
           + jax.nn.silu(gb) * head_norm(yb[:, ::-1], RET_GN_EPS).reshape(bsz, t, -1))
    return out.astype(pb.dtype), sf, sb


def mla_project(pc, lp):
    cq, ckv, kr = split_sizes(pc, MLA_SIZES)
    q = heads(rms_norm(cq, lp['g_qnorm']) @ lp['w_uq'], MLA_HEADS)
    kv = heads(rms_norm(ckv, lp['g_kvnorm']) @ lp['w_ukv'], MLA_HEADS)
    return q[..., :MLA_NOPE], q[..., MLA_NOPE:], kv[..., :MLA_NOPE], kr[:, :, None, :], kv[..., MLA_NOPE:]


def mla_attend(qn, qr, kn, kr, v):
    b, t, h, _ = qn.shape
    nb = t // BLOCK
    scale = (MLA_NOPE + MLA_ROPE) ** -0.5
    kr2 = kr[:, :, 0]

    def one_block(args):
        qn_b, qr_b = args
        sc = (jnp.einsum('bqhd,bkhd->bhqk', qn_b, kn) + jnp.einsum('bqhr,bkr->bhqk', qr_b, kr2)).astype(F32) * scale
        p = jax.nn.softmax(sc, axis=-1).astype(v.dtype)
        return jnp.einsum('bhqk,bkhd->bqhd', p, v)

    def to_blocks(z):
        return jnp.moveaxis(z.reshape(b, nb, BLOCK, h, z.shape[-1]), 1, 0)

    o = lax.map(one_block, (to_blocks(qn), to_blocks(qr)))
    return jnp.moveaxis(o, 0, 1).reshape(b, t, h * MLA_V)


def centred_shift(p):
    prev = jnp.pad(p, ((0, 0), (1, 0), (0, 0)))[:, :-1]
    nxt = jnp.pad(p, ((0, 0), (0, 1), (0, 0)))[:, 1:]
    return 0.5 * (prev + nxt)


def rwkv_features(pd, lp):
    z = pd + (centred_shift(pd) - pd) * lp['rwkv_mu']
    zr, zk, zv, zwf, zwb, zaf, zab, zg = split_sizes(z, RWKV_SIZES)
    r = heads(zr, RWKV_HEADS)
    k = heads(zk, RWKV_HEADS)
    v = heads(zv, RWKV_HEADS)
    g = jax.nn.sigmoid(zg) @ lp['rwkv_g_up']
    kk = (k * heads(lp['rwkv_k_k'], RWKV_HEADS)).astype(F32)
    kk = kk / jnp.maximum(jnp.sqrt(jnp.sum(kk * kk, axis=-1, keepdims=True)), 1e-12)
    k_a = heads(lp['rwkv_k_a'], RWKV_HEADS)
    dirs = []
    for di, (zw, za) in enumerate(((zwf, zaf), (zwb, zab))):
        logw = -jax.nn.softplus(-(lp['rwkv_w0'][di] + jnp.tanh(zw) @ lp['rwkv_w_up'][di]).astype(F32)) - 0.5
        w = heads(jnp.exp(-jnp.exp(logw)), RWKV_HEADS)
        a = heads(jax.nn.sigmoid((lp['rwkv_a0'][di] + za @ lp['rwkv_a_up'][di]).astype(F32)), RWKV_HEADS)
        dirs.append((w, k * (1 + (a - 1) * k_a), kk * a))
    return r, v, g, kk, dirs


def rwkv_scan(r, w, k, v, kk, b_, s0, reverse):
    xs = tuple(jnp.moveaxis(z.astype(F32), 1, 0) for z in (r, w, k, v, kk, b_))

    def step(st, inp):
        r_t, w_t, k_t, v_t, kk_t, b_t = inp
        sa = jnp.einsum('bhvk,bhk->bhv', st, -kk_t)
        st = st * w_t[:, :, None, :] + sa[..., None] * b_t[:, :, None, :] + v_t[..., None] * k_t[:, :, None, :]
        return st, jnp.einsum('bhvk,bhk->bhv', st, r_t)

    s_fin, y = lax.scan(step, s0, xs, reverse=reverse)
    return jnp.moveaxis(y, 0, 1), s_fin


def rwkv_mix(feat, lp, s0f, s0b):
    r, v, g, kk, dirs = feat
    (wf, kf, bf), (wb, kb, bb) = dirs
    yf, sf = rwkv_scan(r, wf, kf, v, kk, bf, s0f, False)
    yb, sb = rwkv_scan(r, wb, kb, v, kk, bb, s0b, True)
    y = head_norm(yf + yb, RWKV_GN_EPS) * heads(lp['rwkv_lnx_w'], RWKV_HEADS) + heads(lp['rwkv_lnx_b'], RWKV_HEADS)
    bonus = jnp.sum(r * (kf + kb) * lp['rwkv_r_k'], axis=-1, keepdims=True) * v
    out = (y + bonus).reshape(g.shape) * g
    return out.astype(g.dtype), sf, sb


def merge_branches(u, ys, lp):
    acc = jnp.zeros_like(u)
    for n in range(N_BRANCH):
        gate = jax.nn.sigmoid(u @ lp['w_gate'][n] + lp['b_gate'][n])
        acc = acc + gate * (ys[n].astype(u.dtype) @ lp['w_branch'][n])
    return acc @ lp['w_out']


def token_mixers(u_ctx, u_lat, lp, need_ctx):
    b, s, _ = u_lat.shape
    l = u_ctx.shape[1]
    pa_l, pb_l, pc_l, pd_l = split_sizes(u_lat @ lp['w_in'], GROUP_SIZES)
    pa_c, pb_c, pc_c, pd_c = split_sizes(u_ctx @ lp['w_in'], GROUP_SIZES)
    tab_att = axial_tables(s, HEAD_DIM)
    tab_mla = axial_tables(s, MLA_ROPE)

    aq_l, ak_l, av_l = split_sizes(pa_l, A_SIZES)
    aq_c, ak_c, av_c = split_sizes(pa_c, A_SIZES)
    k_c = heads(ak_c, ATT_KV_HEADS)
    v_c = heads(av_c, ATT_KV_HEADS)
    ya_l = window_gqa(axial_rope(heads(aq_l, ATT_Q_HEADS), tab_att),
                      axial_rope(heads(ak_l, ATT_KV_HEADS), tab_att),
                      heads(av_l, ATT_KV_HEADS), k_c, v_c, lp['sink'])

    lg_f, lg_b = retention_log_gammas()
    cos_c, sin_c = rope_tables(jnp.arange(l, dtype=F32), HEAD_DIM)
    cos_l, sin_l = rope_tables(l + jnp.arange(s, dtype=F32), HEAD_DIM)
    zero_ret = jnp.zeros((b, RET_HEADS, HEAD_DIM, HEAD_DIM), F32)
    yb_c, sbf, sbb = retention_mix(pb_c, cos_c, sin_c, lg_f, lg_b, zero_ret, zero_ret)
    yb_l, _, _ = retention_mix(pb_l, cos_l, sin_l, lg_f, lg_b, sbf, sbb)

    qn_l, qr_l, kn_l, kr_l, vm_l = mla_project(pc_l, lp)
    qn_c, qr_c, kn_c, kr_c, vm_c = mla_project(pc_c, lp)
    qr_l = axial_rope(qr_l, tab_mla)
    kr_l = axial_rope(kr_l, tab_mla)
    yc_l = mla_attend(qn_l, qr_l, jnp.concatenate([kn_c, kn_l], axis=1),
                      jnp.concatenate([kr_c, kr_l], axis=1), jnp.concatenate([vm_c, vm_l], axis=1))

    zero_wkv = jnp.zeros((b, RWKV_HEADS, HEAD_DIM, HEAD_DIM), F32)
    yd_c, sdf, sdb = rwkv_mix(rwkv_features(pd_c, lp), lp, zero_wkv, zero_wkv)
    yd_l, _, _ = rwkv_mix(rwkv_features(pd_l, lp), lp, sdf, sdb)

    m_lat = merge_branches(u_lat, (ya_l, yb_l, yc_l, yd_l), lp)
    if not need_ctx:
        return None, m_lat
    ya_c = ctx_gqa(heads(aq_c, ATT_Q_HEADS), k_c, v_c, lp['sink'])
    yc_c = mla_attend(qn_c, qr_c, kn_c, kr_c, vm_c)
    m_ctx = merge_branches(u_ctx, (ya_c, yb_c, yc_c, yd_c), lp)
    return m_ctx, m_lat


def moe_ffn(x, router_w, router_b, w_e_gate, w_e_up, w_e_down):
    n_tok, dm = x.shape
    scores = jax.nn.sigmoid(x.astype(F32) @ router_w.astype(F32))
    grp = (scores + router_b.astype(F32)).reshape(n_tok, N_GROUPS, EXPERTS_PER_GROUP)
    g_sel = jnp.argmax(jnp.sum(lax.top_k(grp, TOP_K)[0], axis=-1), axis=-1)
    in_grp = lax.top_k(jnp.take_along_axis(grp, g_sel[:, None, None], axis=1)[:, 0], TOP_K)[1]
    e_idx = g_sel[:, None] * EXPERTS_PER_GROUP + in_grp
    w_sel = jnp.take_along_axis(scores, e_idx, axis=1)
    w_sel = w_sel / jnp.sum(w_sel, axis=-1, keepdims=True)
    n_asg = n_tok * TOP_K
    flat_e = e_idx.reshape(-1)
    flat_tok = jnp.repeat(jnp.arange(n_tok, dtype=jnp.int32), TOP_K)
    flat_w = w_sel.reshape(-1)
    order = jnp.argsort(flat_e)
    se = flat_e[order]
    counts = jnp.bincount(flat_e, length=N_EXPERTS)
    padded = (counts + MOE_BLOCK - 1) // MOE_BLOCK * MOE_BLOCK
    pad_end = jnp.cumsum(padded)
    pad_start = pad_end - padded
    start = jnp.cumsum(counts) - counts
    dest = pad_start[se] + jnp.arange(n_asg) - start[se]
    n_blocks = -(-n_asg // MOE_BLOCK) + N_EXPERTS
    n_rows = n_blocks * MOE_BLOCK
    row_tok = jnp.zeros((n_rows,), jnp.int32).at[dest].set(flat_tok[order])
    row_w = jnp.zeros((n_rows,), F32).at[dest].set(flat_w[order])
    blk_e = jnp.minimum(jnp.searchsorted(pad_end, jnp.arange(n_blocks) * MOE_BLOCK, side='right'), N_EXPERTS - 1)
    xb = x[row_tok].reshape(n_blocks, MOE_BLOCK, dm)

    def expert_block(args):
        xblk, e = args
        hid = jax.nn.silu(xblk @ w_e_gate[e]) * (xblk @ w_e_up[e])
        return hid @ w_e_down[e]

    yb = lax.map(expert_block, (xb, blk_e)).reshape(n_rows, dm)
    yb = yb * row_w[:, None].astype(yb.dtype)
    return jnp.zeros_like(x).at[row_tok].add(yb.astype(x.dtype))


def layer(h_ctx, h_lat, c, c_ctx, lp, router_w, router_b, need_ctx):
    b, s, d = h_lat.shape
    l = h_ctx.shape[1]
    m_lat = jnp.split((jax.nn.silu(c) @ lp['w_ada'] + lp['b_ada'])[:, None, :], 6, axis=-1)
    m_ctx = jnp.split(jax.nn.silu(c_ctx) @ lp['w_ada'] + lp['b_ada'], 6, axis=-1)
    u_lat = adaln(h_lat, lp['g_norm1'], m_lat[0], m_lat[1])
    u_ctx = adaln(h_ctx, lp['g_norm1'], m_ctx[0], m_ctx[1])
    mix_ctx, mix_lat = token_mixers(u_ctx, u_lat, lp, need_ctx)
    h_lat = h_lat + m_lat[2] * mix_lat
    f_lat = adaln(h_lat, lp['g_norm2'], m_lat[3], m_lat[4]).reshape(b * s, d)
    if not need_ctx:
        ffn = moe_ffn(f_lat, router_w, router_b, lp['w_e_gate'], lp['w_e_up'], lp['w_e_down'])
        return h_ctx, h_lat + m_lat[5] * ffn.reshape(b, s, d)
    h_ctx = h_ctx + m_ctx[2] * mix_ctx
    f_ctx = adaln(h_ctx, lp['g_norm2'], m_ctx[3], m_ctx[4]).reshape(b * l, d)
    ffn = moe_ffn(jnp.concatenate([f_ctx, f_lat], axis=0), router_w, router_b,
                  lp['w_e_gate'], lp['w_e_up'], lp['w_e_down'])
    h_ctx = h_ctx + m_ctx[5] * ffn[:b * l].reshape(b, l, d)
    h_lat = h_lat + m_lat[5] * ffn[b * l:].reshape(b, s, d)
    return h_ctx, h_lat


def setup_inputs(seed: int = 0) -> dict:
    key = jax.random.key(seed)
    ks = iter(jax.random.split(key, 48))
    D = D_MODEL
    L = DEPTH

    def nrm(shape, scale):
        return scale * jax.random.normal(next(ks), shape, F32)

    rwkv_in = sum(RWKV_SIZES)
    return {
        'x': nrm((BATCH, SEQ, D), 1.0),
        'c': nrm((BATCH, D), 1.0),
        'ctx': nrm((BATCH, CTX_LEN, D), 1.0),
        'c_ctx': nrm((D,), 1.0),
        'w_ada': nrm((L, D, 6 * D), 0.5 * D ** -0.5),
        'b_ada': nrm((L, 6 * D), 0.02),
        'g_norm1': 1.0 + nrm((L, D), 0.02),
        'g_norm2': 1.0 + nrm((L, D), 0.02),
        'w_in': nrm((L, D, N_IN), D ** -0.5),
        'sink': nrm((L, ATT_Q_HEADS), 0.5),
        'g_qnorm': 1.0 + nrm((L, MLA_Q_RANK), 0.02),
        'g_kvnorm': 1.0 + nrm((L, MLA_KV_RANK), 0.02),
        'w_uq': nrm((L, MLA_Q_RANK, MLA_HEADS * (MLA_NOPE + MLA_ROPE)), MLA_Q_RANK ** -0.5),
        'w_ukv': nrm((L, MLA_KV_RANK, MLA_HEADS * (MLA_NOPE + MLA_V)), MLA_KV_RANK ** -0.5),
        'rwkv_mu': jax.random.uniform(next(ks), (L, rwkv_in), F32, 0.0, 1.0),
        'rwkv_w0': jnp.linspace(-6.0, -1.0, BRANCH_W, dtype=F32)[None, None, :] + nrm((L, 2, BRANCH_W), 0.1),
        'rwkv_w_up': nrm((L, 2, RWKV_DECAY_RANK, BRANCH_W), 0.1 * RWKV_DECAY_RANK ** -0.5),
        'rwkv_a0': nrm((L, 2, BRANCH_W), 0.1),
        'rwkv_a_up': nrm((L, 2, RWKV_A_RANK, BRANCH_W), 0.5 * RWKV_A_RANK ** -0.5),
        'rwkv_g_up': nrm((L, RWKV_GATE_RANK, BRANCH_W), RWKV_GATE_RANK ** -0.5),
        'rwkv_k_k': 0.85 + nrm((L, BRANCH_W), 0.02),
        'rwkv_k_a': 1.0 + nrm((L, BRANCH_W), 0.02),
        'rwkv_r_k': nrm((L, RWKV_HEADS, HEAD_DIM), 0.1),
        'rwkv_lnx_w': 1.0 + nrm((L, BRANCH_W), 0.02),
        'rwkv_lnx_b': nrm((L, BRANCH_W), 0.02),
        'w_gate': nrm((L, N_BRANCH, D, D), D ** -0.5),
        'b_gate': nrm((L, N_BRANCH, D), 0.02),
        'w_branch': nrm((L, N_BRANCH, BRANCH_W, D), BRANCH_W ** -0.5),
        'w_out': nrm((L, D, D), D ** -0.5),
        'router_w': nrm((D, N_EXPERTS), D ** -0.5),
        'router_b': nrm((N_EXPERTS,), 0.01),
        'w_e_gate': nrm((L, N_EXPERTS, D, EXPERT_FF), D ** -0.5),
        'w_e_up': nrm((L, N_EXPERTS, D, EXPERT_FF), D ** -0.5),
        'w_e_down': nrm((L, N_EXPERTS, EXPERT_FF, D), EXPERT_FF ** -0.5),
        'g_final': 1.0 + nrm((D,), 0.02),
    }


def reference(x, c, ctx, c_ctx, w_ada, b_ada, g_norm1, g_norm2, w_in, sink, g_qnorm, g_kvnorm,
              w_uq, w_ukv, rwkv_mu, rwkv_w0, rwkv_w_up, rwkv_a0, rwkv_a_up, rwkv_g_up, rwkv_k_k,
              rwkv_k_a, rwkv_r_k, rwkv_lnx_w, rwkv_lnx_b, w_gate, b_gate, w_branch, w_out,
              router_w, router_b, w_e_gate, w_e_up, w_e_down, g_final):
    names = ('w_ada', 'b_ada', 'g_norm1', 'g_norm2', 'w_in', 'sink', 'g_qnorm', 'g_kvnorm', 'w_uq',
             'w_ukv', 'rwkv_mu', 'rwkv_w0', 'rwkv_w_up', 'rwkv_a0', 'rwkv_a_up', 'rwkv_g_up',
             'rwkv_k_k', 'rwkv_k_a', 'rwkv_r_k', 'rwkv_lnx_w', 'rwkv_lnx_b', 'w_gate', 'b_gate',
             'w_branch', 'w_out', 'w_e_gate', 'w_e_up', 'w_e_down')
    stacked = (w_ada, b_ada, g_norm1, g_norm2, w_in, sink, g_qnorm, g_kvnorm, w_uq, w_ukv, rwkv_mu,
               rwkv_w0, rwkv_w_up, rwkv_a0, rwkv_a_up, rwkv_g_up, rwkv_k_k, rwkv_k_a, rwkv_r_k,
               rwkv_lnx_w, rwkv_lnx_b, w_gate, b_gate, w_branch, w_out, w_e_gate, w_e_up, w_e_down)
    h_ctx, h_lat = ctx, x
    for i in range(DEPTH):
        lp = {n: a[i] for n, a in zip(names, stacked)}
        h_ctx, h_lat = layer(h_ctx, h_lat, c, c_ctx, lp, router_w, router_b, i < DEPTH - 1)
    return rms_norm(h_lat, g_final)
```

```python
import math

import jax
import jax.numpy as jnp
from jax import lax
from jax.experimental import pallas as pl
from jax.experimental.pallas import tpu as pltpu

D_MODEL = 1024
BATCH = 4
SEQ = 8192
DEPTH = 2

GRID_W = 64
CTX_LEN = 256
N_BRANCH = 4
BRANCH_W = D_MODEL // N_BRANCH
HEAD_DIM = 64
BRANCH_HEADS = BRANCH_W // HEAD_DIM
BLOCK = 128
ROPE_BASE = 10000.0
NORM_EPS = 1e-6
NEG_INF = -1e30
ATT_Q_HEADS = BRANCH_HEADS
ATT_KV_HEADS = BRANCH_HEADS // 2
WINDOW = 128
RET_HEADS = BRANCH_HEADS
RET_CHUNK = 128
RET_GN_EPS = 1e-5
MLA_HEADS = BRANCH_HEADS
MLA_Q_RANK = 256
MLA_KV_RANK = 128
MLA_NOPE = HEAD_DIM
MLA_ROPE = HEAD_DIM // 2
MLA_V = HEAD_DIM
RWKV_HEADS = BRANCH_HEADS
RWKV_DECAY_RANK = 64
RWKV_A_RANK = 64
RWKV_GATE_RANK = 128
RWKV_GN_EPS = 64e-5
N_EXPERTS = 64
N_GROUPS = 8
EXPERTS_PER_GROUP = N_EXPERTS // N_GROUPS
TOP_K = 2
EXPERT_FF = 512
MOE_BLOCK = 256
A_SIZES = (ATT_Q_HEADS * HEAD_DIM, ATT_KV_HEADS * HEAD_DIM, ATT_KV_HEADS * HEAD_DIM)
RET_SIZES = (BRANCH_W,) * 5
MLA_SIZES = (MLA_Q_RANK, MLA_KV_RANK, MLA_ROPE)
RWKV_SIZES = (BRANCH_W, BRANCH_W, BRANCH_W, RWKV_DECAY_RANK, RWKV_DECAY_RANK,
              RWKV_A_RANK, RWKV_A_RANK, RWKV_GATE_RANK)
GROUP_SIZES = (sum(A_SIZES), sum(RET_SIZES), sum(MLA_SIZES), sum(RWKV_SIZES))
N_IN = sum(GROUP_SIZES)
F32 = jnp.float32


def split_sizes(x, sizes):
    out, o = [], 0
    for n in sizes:
        out.append(x[..., o:o + n])
        o += n
    return out


def heads(x, h):
    return x.reshape(x.shape[:-1] + (h, x.shape[-1] // h))


def rms_norm(x, g):
    xf = x.astype(F32)
    y = xf * lax.rsqrt(jnp.mean(xf * xf, axis=-1, keepdims=True) + NORM_EPS)
    return (y * g.astype(F32)).astype(x.dtype)


def head_norm(x, eps):
    xf = x.astype(F32)
    xc = xf - jnp.mean(xf, axis=-1, keepdims=True)
    return xc * lax.rsqrt(jnp.mean(xc * xc, axis=-1, keepdims=True) + eps)


def adaln(h, g, shift, scale):
    return rms_norm(h, g) * (1 + scale) + shift


def rope_tables(pos, dim):
    inv = ROPE_BASE ** (-jnp.arange(0, dim, 2, dtype=F32) / dim)
    ang = pos[:, None] * inv[None, :]
    return jnp.cos(ang), jnp.sin(ang)


def rope_rotate(x, cos, sin):
    m = x.shape[-1] // 2
    x1, x2 = x[..., :m], x[..., m:]
    c = cos[None, :, None, :]
    s = sin[None, :, None, :]
    return jnp.concatenate([x1 * c - x2 * s, x2 * c + x1 * s], axis=-1).astype(x.dtype)


def axial_tables(seq, dim):
    rows = seq // GRID_W
    row = jnp.repeat(jnp.arange(rows, dtype=F32), GRID_W)
    col = jnp.broadcast_to(jnp.arange(GRID_W, dtype=F32)[None, :], (rows, GRID_W)).reshape(-1)
    return rope_tables(row, dim // 2), rope_tables(col, dim // 2)


def axial_rope(x, tabs):
    (cos_r, sin_r), (cos_c, sin_c) = tabs
    half = x.shape[-1] // 2
    return jnp.concatenate([rope_rotate(x[..., :half], cos_r, sin_r),
                            rope_rotate(x[..., half:], cos_c, sin_c)], axis=-1)


def window_gqa(q, k, v, k_ctx, v_ctx, sink):
    b, s, hq, d = q.shape
    g = hq // ATT_KV_HEADS
    nb = s // BLOCK
    nw = 3 * BLOCK
    nc = k_ctx.shape[1]
    qb = (q * d ** -0.5).reshape(b, nb, BLOCK, ATT_KV_HEADS, g, d)
    pad = ((0, 0), (BLOCK, BLOCK), (0, 0), (0, 0))
    kp = jnp.pad(k, pad).reshape(b, nb + 2, BLOCK, ATT_KV_HEADS, d)
    vp = jnp.pad(v, pad).reshape(b, nb + 2, BLOCK, ATT_KV_HEADS, d)
    kw = jnp.concatenate([kp[:, :-2], kp[:, 1:-1], kp[:, 2:]], axis=2)
    vw = jnp.concatenate([vp[:, :-2], vp[:, 1:-1], vp[:, 2:]], axis=2)
    blk0 = jnp.arange(nb)[:, None, None] * BLOCK
    q_pos = blk0 + jnp.arange(BLOCK)[None, :, None]
    k_pos = blk0 - BLOCK + jnp.arange(nw)[None, None, :]
    valid = (jnp.abs(k_pos - q_pos) <= WINDOW) & (k_pos >= 0) & (k_pos < s)
    s_win = jnp.einsum('bnqhgd,bnkhd->bnhgqk', qb, kw).astype(F32)
    s_win = jnp.where(valid[None, :, None, None], s_win, NEG_INF)
    s_ctx = jnp.einsum('bnqhgd,bchd->bnhgqc', qb, k_ctx).astype(F32)
    s_sink = jnp.broadcast_to(sink.astype(F32).reshape(ATT_KV_HEADS, g, 1, 1), s_win.shape[:-1] + (1,))
    p = jax.nn.softmax(jnp.concatenate([s_win, s_ctx, s_sink], axis=-1), axis=-1).astype(v.dtype)
    o = (jnp.einsum('bnhgqk,bnkhd->bnqhgd', p[..., :nw], vw)
         + jnp.einsum('bnhgqc,bchd->bnqhgd', p[..., nw:nw + nc], v_ctx))
    return o.reshape(b, s, hq * d)


def ctx_gqa(q, k, v, sink):
    b, l, hq, d = q.shape
    g = hq // ATT_KV_HEADS
    qg = (q * d ** -0.5).reshape(b, l, ATT_KV_HEADS, g, d)
    sc = jnp.einsum('bqhgd,bkhd->bhgqk', qg, k).astype(F32)
    s_sink = jnp.broadcast_to(sink.astype(F32).reshape(ATT_KV_HEADS, g, 1, 1), sc.shape[:-1] + (1,))
    p = jax.nn.softmax(jnp.concatenate([sc, s_sink], axis=-1), axis=-1).astype(v.dtype)
    o = jnp.einsum('bhgqk,bkhd->bqhgd', p[..., :l], v)
    return o.reshape(b, l, hq * d)


def retention_log_gammas():
    lg = jnp.log(1.0 - jnp.exp(jnp.linspace(math.log(1.0 / 32), math.log(1.0 / 512), 2 * RET_HEADS, dtype=F32)))
    return lg[0::2], lg[1::2]


def retention_chunked(q, k, v, log_gamma, s0):
    b, t, h, d = q.shape
    n = t // RET_CHUNK
    cl = RET_CHUNK
    qc = q.reshape(b, n, cl, h, d).astype(F32)
    kc = k.reshape(b, n, cl, h, d).astype(F32)
    vc = v.reshape(b, n, cl, h, d).astype(F32)
    idx = jnp.arange(cl, dtype=F32)
    rel = idx[:, None] - idx[None, :]
    d_intra = jnp.where(rel >= 0, jnp.exp(jnp.maximum(rel, 0.0)[None] * log_gamma[:, None, None]), 0.0)
    a_int = jnp.einsum('bnihd,bnjhd->bnhij', qc, kc) * d_intra
    o = jnp.einsum('bnhij,bnjhe->bnihe', a_int, vc)
    k_dec = kc * jnp.exp((cl - 1 - idx)[:, None] * log_gamma[None, :])[..., None]
    kv = jnp.einsum('bnjhd,bnjhe->nbhde', k_dec, vc)
    g_chunk = jnp.exp(cl * log_gamma)[:, None, None]

    def step(state, kv_n):
        return state * g_chunk + kv_n, state

    s_fin, s_prev = lax.scan(step, s0, kv)
    q_dec = qc * jnp.exp((idx + 1)[:, None] * log_gamma[None, :])[..., None]
    o = o + jnp.einsum('bnihd,nbhde->bnihe', q_dec, s_prev)
    return o.reshape(b, t, h, d), s_fin


def retention_mix(pb, cos, sin, lg_f, lg_b, s0f, s0b):
    bsz, t = pb.shape[:2]
    q, k, v, gf, gb = split_sizes(pb, RET_SIZES)
    q = rope_rotate(heads(q, RET_HEADS), cos, sin)
    k = rope_rotate(heads(k, RET_HEADS), cos, sin) * HEAD_DIM ** -0.5
    v = heads(v, RET_HEADS)
    yf, sf = retention_chunked(q, k, v, lg_f, s0f)
    yb, sb = retention_chunked(q[:, ::-1], k[:, ::-1], v[:, ::-1], lg_b, s0b)
    out = (jax.nn.silu(gf) * head_norm(yf, RET_GN_EPS).reshape(bsz, t, -1)
           + jax.nn.silu(gb) * head_norm(yb[:, ::-1], RET_GN_EPS).reshape(bsz, t, -1))
    return out.astype(pb.dtype), sf, sb


def mla_project(pc, lp):
    cq, ckv, kr = split_sizes(pc, MLA_SIZES)
    q = heads(rms_norm(cq, lp['g_qnorm']) @ lp['w_uq'], MLA_HEADS)
    kv = heads(rms_norm(ckv, lp['g_kvnorm']) @ lp['w_ukv'], MLA_HEADS)
    return q[..., :MLA_NOPE], q[..., MLA_NOPE:], kv[..., :MLA_NOPE], kr[:, :, None, :], kv[..., MLA_NOPE:]


def mla_attend(qn, qr, kn, kr, v):
    b, t, h, _ = qn.shape
    nb = t // BLOCK
    scale = (MLA_NOPE + MLA_ROPE) ** -0.5
    kr2 = kr[:, :, 0]

    def one_block(args):
        qn_b, qr_b = args
        sc = (jnp.einsum('bqhd,bkhd->bhqk', qn_b, kn) + jnp.einsum('bqhr,bkr->bhqk', qr_b, kr2)).astype(F32) * scale
        p = jax.nn.softmax(sc, axis=-1).astype(v.dtype)
        return jnp.einsum('bhqk,bkhd->bqhd', p, v)

    def to_blocks(z):
        return jnp.moveaxis(z.reshape(b, nb, BLOCK, h, z.shape[-1]), 1, 0)

    o = lax.map(one_block, (to_blocks(qn), to_blocks(qr)))
    return jnp.moveaxis(o, 0, 1).reshape(b, t, h * MLA_V)


def centred_shift(p):
    prev = jnp.pad(p, ((0, 0), (1, 0), (0, 0)))[:, :-1]
    nxt = jnp.pad(p, ((0, 0), (0, 1), (0, 0)))[:, 1:]
    return 0.5 * (prev + nxt)


def rwkv_features(pd, lp):
    z = pd + (centred_shift(pd) - pd) * lp['rwkv_mu']
    zr, zk, zv, zwf, zwb, zaf, zab, zg = split_sizes(z, RWKV_SIZES)
    r = heads(zr, RWKV_HEADS)
    k = heads(zk, RWKV_HEADS)
    v = heads(zv, RWKV_HEADS)
    g = jax.nn.sigmoid(zg) @ lp['rwkv_g_up']
    kk = (k * heads(lp['rwkv_k_k'], RWKV_HEADS)).astype(F32)
    kk = kk / jnp.maximum(jnp.sqrt(jnp.sum(kk * kk, axis=-1, keepdims=True)), 1e-12)
    k_a = heads(lp['rwkv_k_a'], RWKV_HEADS)
    dirs = []
    for di, (zw, za) in enumerate(((zwf, zaf), (zwb, zab))):
        logw = -jax.nn.softplus(-(lp['rwkv_w0'][di] + jnp.tanh(zw) @ lp['rwkv_w_up'][di]).astype(F32)) - 0.5
        w = heads(jnp.exp(-jnp.exp(logw)), RWKV_HEADS)
        a = heads(jax.nn.sigmoid((lp['rwkv_a0'][di] + za @ lp['rwkv_a_up'][di]).astype(F32)), RWKV_HEADS)
        dirs.append((w, k * (1 + (a - 1) * k_a), kk * a))
    return r, v, g, kk, dirs


def rwkv_scan(r, w, k, v, kk, b_, s0, reverse):
    xs = tuple(jnp.moveaxis(z.astype(F32), 1, 0) for z in (r, w, k, v, kk, b_))

    def step(st, inp):
        r_t, w_t, k_t, v_t, kk_t, b_t = inp
        sa = jnp.einsum('bhvk,bhk->bhv', st, -kk_t)
        st = st * w_t[:, :, None, :] + sa[..., None] * b_t[:, :, None, :] + v_t[..., None] * k_t[:, :, None, :]
        return st, jnp.einsum('bhvk,bhk->bhv', st, r_t)

    s_fin, y = lax.scan(step, s0, xs, reverse=reverse)
    return jnp.moveaxis(y, 0, 1), s_fin


def rwkv_mix(feat, lp, s0f, s0b):
    r, v, g, kk, dirs = feat
    (wf, kf, bf), (wb, kb, bb) = dirs
    yf, sf = rwkv_scan(r, wf, kf, v, kk, bf, s0f, False)
    yb, sb = rwkv_scan(r, wb, kb, v, kk, bb, s0b, True)
    y = head_norm(yf + yb, RWKV_GN_EPS) * heads(lp['rwkv_lnx_w'], RWKV_HEADS) + heads(lp['rwkv_lnx_b'], RWKV_HEADS)
    bonus = jnp.sum(r * (kf + kb) * lp['rwkv_r_k'], axis=-1, keepdims=True) * v
    out = (y + bonus).reshape(g.shape) * g
    return out.astype(g.dtype), sf, sb


def merge_branches(u, ys, lp):
    acc = jnp.zeros_like(u)
    for n in range(N_BRANCH):
        gate = jax.nn.sigmoid(u @ lp['w_gate'][n] + lp['b_gate'][n])
        acc = acc + gate * (ys[n].astype(u.dtype) @ lp['w_branch'][n])
    return acc @ lp['w_out']


def token_mixers(u_ctx, u_lat, lp, need_ctx):
    b, s, _ = u_lat.shape
    l = u_ctx.shape[1]
    pa_l, pb_l, pc_l, pd_l = split_sizes(u_lat @ lp['w_in'], GROUP_SIZES)
    pa_c, pb_c, pc_c, pd_c = split_sizes(u_ctx @ lp['w_in'], GROUP_SIZES)
    tab_att = axial_tables(s, HEAD_DIM)
    tab_mla = axial_tables(s, MLA_ROPE)

    aq_l, ak_l, av_l = split_sizes(pa_l, A_SIZES)
    aq_c, ak_c, av_c = split_sizes(pa_c, A_SIZES)
    k_c = heads(ak_c, ATT_KV_HEADS)
    v_c = heads(av_c, ATT_KV_HEADS)
    ya_l = window_gqa(axial_rope(heads(aq_l, ATT_Q_HEADS), tab_att),
                      axial_rope(heads(ak_l, ATT_KV_HEADS), tab_att),
                      heads(av_l, ATT_KV_HEADS), k_c, v_c, lp['sink'])

    lg_f, lg_b = retention_log_gammas()
    cos_c, sin_c = rope_tables(jnp.arange(l, dtype=F32), HEAD_DIM)
    cos_l, sin_l = rope_tables(l + jnp.arange(s, dtype=F32), HEAD_DIM)
    zero_ret = jnp.zeros((b, RET_HEADS, HEAD_DIM, HEAD_DIM), F32)
    yb_c, sbf, sbb = retention_mix(pb_c, cos_c, sin_c, lg_f, lg_b, zero_ret, zero_ret)
    yb_l, _, _ = retention_mix(pb_l, cos_l, sin_l, lg_f, lg_b, sbf, sbb)

    qn_l, qr_l, kn_l, kr_l, vm_l = mla_project(pc_l, lp)
    qn_c, qr_c, kn_c, kr_c, vm_c = mla_project(pc_c, lp)
    qr_l = axial_rope(qr_l, tab_mla)
    kr_l = axial_rope(kr_l, tab_mla)
    yc_l = mla_attend(qn_l, qr_l, jnp.concatenate([kn_c, kn_l], axis=1),
                      jnp.concatenate([kr_c, kr_l], axis=1), jnp.concatenate([vm_c, vm_l], axis=1))

    zero_wkv = jnp.zeros((b, RWKV_HEADS, HEAD_DIM, HEAD_DIM), F32)
    yd_c, sdf, sdb = rwkv_mix(rwkv_features(pd_c, lp), lp, zero_wkv, zero_wkv)
    yd_l, _, _ = rwkv_mix(rwkv_features(pd_l, lp), lp, sdf, sdb)

    m_lat = merge_branches(u_lat, (ya_l, yb_l, yc_l, yd_l), lp)
    if not need_ctx:
        return None, m_lat
    ya_c = ctx_gqa(heads(aq_c, ATT_Q_HEADS), k_c, v_c, lp['sink'])
    yc_c = mla_attend(qn_c, qr_c, kn_c, kr_c, vm_c)
    m_ctx = merge_branches(u_ctx, (ya_c, yb_c, yc_c, yd_c), lp)
    return m_ctx, m_lat


def moe_ffn(x, router_w, router_b, w_e_gate, w_e_up, w_e_down):
    n_tok, dm = x.shape
    scores = jax.nn.sigmoid(x.astype(F32) @ router_w.astype(F32))
    grp = (scores + router_b.astype(F32)).reshape(n_tok, N_GROUPS, EXPERTS_PER_GROUP)
    g_sel = jnp.argmax(jnp.sum(lax.top_k(grp, TOP_K)[0], axis=-1), axis=-1)
    in_grp = lax.top_k(jnp.take_along_axis(grp, g_sel[:, None, None], axis=1)[:, 0], TOP_K)[1]
    e_idx = g_sel[:, None] * EXPERTS_PER_GROUP + in_grp
    w_sel = jnp.take_along_axis(scores, e_idx, axis=1)
    w_sel = w_sel / jnp.sum(w_sel, axis=-1, keepdims=True)
    n_asg = n_tok * TOP_K
    flat_e = e_idx.reshape(-1)
    flat_tok = jnp.repeat(jnp.arange(n_tok, dtype=jnp.int32), TOP_K)
    flat_w = w_sel.reshape(-1)
    order = jnp.argsort(flat_e)
    se = flat_e[order]
    counts = jnp.bincount(flat_e, length=N_EXPERTS)
    padded = (counts + MOE_BLOCK - 1) // MOE_BLOCK * MOE_BLOCK
    pad_end = jnp.cumsum(padded)
    pad_start = pad_end - padded
    start = jnp.cumsum(counts) - counts
    dest = pad_start[se] + jnp.arange(n_asg) - start[se]
    n_blocks = -(-n_asg // MOE_BLOCK) + N_EXPERTS
    n_rows = n_blocks * MOE_BLOCK
    row_tok = jnp.zeros((n_rows,), jnp.int32).at[dest].set(flat_tok[order])
    row_w = jnp.zeros((n_rows,), F32).at[dest].set(flat_w[order])
    blk_e = jnp.minimum(jnp.searchsorted(pad_end, jnp.arange(n_blocks) * MOE_BLOCK, side='right'), N_EXPERTS - 1)
    xb = x[row_tok].reshape(n_blocks, MOE_BLOCK, dm)

    def expert_block(args):
        xblk, e = args
        hid = jax.nn.silu(xblk @ w_e_gate[e]) * (xblk @ w_e_up[e])
        return hid @ w_e_down[e]

    yb = lax.map(expert_block, (xb, blk_e)).reshape(n_rows, dm)
    yb = yb * row_w[:, None].astype(yb.dtype)
    return jnp.zeros_like(x).at[row_tok].add(yb.astype(x.dtype))


def layer(h_ctx, h_lat, c, c_ctx, lp, router_w, router_b, need_ctx):
    b, s, d = h_lat.shape
    l = h_ctx.shape[1]
    m_lat = jnp.split((jax.nn.silu(c) @ lp['w_ada'] + lp['b_ada'])[:, None, :], 6, axis=-1)
    m_ctx = jnp.split(jax.nn.silu(c_ctx) @ lp['w_ada'] + lp['b_ada'], 6, axis=-1)
    u_lat = adaln(h_lat, lp['g_norm1'], m_lat[0], m_lat[1])
    u_ctx = adaln(h_ctx, lp['g_norm1'], m_ctx[0], m_ctx[1])
    mix_ctx, mix_lat = token_mixers(u_ctx, u_lat, lp, need_ctx)
    h_lat = h_lat + m_lat[2] * mix_lat
    f_lat = adaln(h_lat, lp['g_norm2'], m_lat[3], m_lat[4]).reshape(b * s, d)
    if not need_ctx:
        ffn = moe_ffn(f_lat, router_w, router_b, lp['w_e_gate'], lp['w_e_up'], lp['w_e_down'])
        return h_ctx, h_lat + m_lat[5] * ffn.reshape(b, s, d)
    h_ctx = h_ctx + m_ctx[2] * mix_ctx
    f_ctx = adaln(h_ctx, lp['g_norm2'], m_ctx[3], m_ctx[4]).reshape(b * l, d)
    ffn = moe_ffn(jnp.concatenate([f_ctx, f_lat], axis=0), router_w, router_b,
                  lp['w_e_gate'], lp['w_e_up'], lp['w_e_down'])
    h_ctx = h_ctx + m_ctx[5] * ffn[:b * l].reshape(b, l, d)
    h_lat = h_lat + m_lat[5] * ffn[b * l:].reshape(b, s, d)
    return h_ctx, h_lat


def _final_norm_kernel(x_ref, g_ref, o_ref):
    x = x_ref[...]
    y = x * lax.rsqrt(jnp.mean(x * x, axis=-1, keepdims=True) + NORM_EPS)
    o_ref[...] = y * g_ref[...]


def final_rms_norm(x, g):
    b, s, d = x.shape
    rows = b * s
    tile = 1024
    out = pl.pallas_call(
        _final_norm_kernel,
        grid=(rows // tile,),
        in_specs=[pl.BlockSpec((tile, d), lambda i: (i, 0)), pl.BlockSpec((1, d), lambda i: (0, 0))],
        out_specs=pl.BlockSpec((tile, d), lambda i: (i, 0)),
        out_shape=jax.ShapeDtypeStruct((rows, d), x.dtype),
        name="final_rms_norm",
    )(x.reshape(rows, d), g.reshape(1, d))
    return out.reshape(b, s, d)


_LAYER_PARAM_NAMES = (
    'w_ada', 'b_ada', 'g_norm1', 'g_norm2', 'w_in', 'sink', 'g_qnorm', 'g_kvnorm', 'w_uq',
    'w_ukv', 'rwkv_mu', 'rwkv_w0', 'rwkv_w_up', 'rwkv_a0', 'rwkv_a_up', 'rwkv_g_up',
    'rwkv_k_k', 'rwkv_k_a', 'rwkv_r_k', 'rwkv_lnx_w', 'rwkv_lnx_b', 'w_gate', 'b_gate',
    'w_branch', 'w_out', 'w_e_gate', 'w_e_up', 'w_e_down')


def kernel(x, c, ctx, c_ctx, w_ada, b_ada, g_norm1, g_norm2, w_in, sink, g_qnorm, g_kvnorm,
           w_uq, w_ukv, rwkv_mu, rwkv_w0, rwkv_w_up, rwkv_a0, rwkv_a_up, rwkv_g_up, rwkv_k_k,
           rwkv_k_a, rwkv_r_k, rwkv_lnx_w, rwkv_lnx_b, w_gate, b_gate, w_branch, w_out,
           router_w, router_b, w_e_gate, w_e_up, w_e_down, g_final):
    stacked = (w_ada, b_ada, g_norm1, g_norm2, w_in, sink, g_qnorm, g_kvnorm, w_uq, w_ukv, rwkv_mu,
               rwkv_w0, rwkv_w_up, rwkv_a0, rwkv_a_up, rwkv_g_up, rwkv_k_k, rwkv_k_a, rwkv_r_k,
               rwkv_lnx_w, rwkv_lnx_b, w_gate, b_gate, w_branch, w_out, w_e_gate, w_e_up, w_e_down)
    h_ctx, h_lat = ctx, x
    for i in range(DEPTH):
        lp = {n: a[i] for n, a in zip(_LAYER_PARAM_NAMES, stacked)}
        h_ctx, h_lat = layer(h_ctx, h_lat, c, c_ctx, lp, router_w, router_b, i < DEPTH - 1)
    return final_rms_norm(h_lat, g_final)
```

```python
import functools
import math

import jax
import jax.numpy as jnp
from jax import lax
from jax.experimental import pallas as pl
from jax.experimental.pallas import tpu as pltpu

D_MODEL = 1024
BATCH = 4
SEQ = 8192
DEPTH = 2

GRID_W = 64
CTX_LEN = 256
N_BRANCH = 4
BRANCH_W = D_MODEL // N_BRANCH
HEAD_DIM = 64
BRANCH_HEADS = BRANCH_W // HEAD_DIM
BLOCK = 128
ROPE_BASE = 10000.0
NORM_EPS = 1e-6
NEG_INF = -1e30
ATT_Q_HEADS = BRANCH_HEADS
ATT_KV_HEADS = BRANCH_HEADS // 2
WINDOW = 128
RET_HEADS = BRANCH_HEADS
RET_CHUNK = 128
RET_GN_EPS = 1e-5
MLA_HEADS = BRANCH_HEADS
MLA_Q_RANK = 256
MLA_KV_RANK = 128
MLA_NOPE = HEAD_DIM
MLA_ROPE = HEAD_DIM // 2
MLA_V = HEAD_DIM
RWKV_HEADS = BRANCH_HEADS
RWKV_DECAY_RANK = 64
RWKV_A_RANK = 64
RWKV_GATE_RANK = 128
RWKV_GN_EPS = 64e-5
N_EXPERTS = 64
N_GROUPS = 8
EXPERTS_PER_GROUP = N_EXPERTS // N_GROUPS
TOP_K = 2
EXPERT_FF = 512
MOE_BLOCK = 256
A_SIZES = (ATT_Q_HEADS * HEAD_DIM, ATT_KV_HEADS * HEAD_DIM, ATT_KV_HEADS * HEAD_DIM)
RET_SIZES = (BRANCH_W,) * 5
MLA_SIZES = (MLA_Q_RANK, MLA_KV_RANK, MLA_ROPE)
RWKV_SIZES = (BRANCH_W, BRANCH_W, BRANCH_W, RWKV_DECAY_RANK, RWKV_DECAY_RANK,
              RWKV_A_RANK, RWKV_A_RANK, RWKV_GATE_RANK)
GROUP_SIZES = (sum(A_SIZES), sum(RET_SIZES), sum(MLA_SIZES), sum(RWKV_SIZES))
N_IN = sum(GROUP_SIZES)
F32 = jnp.float32


def split_sizes(x, sizes):
    out, o = [], 0
    for n in sizes:
        out.append(x[..., o:o + n])
        o += n
    return out


def heads(x, h):
    return x.reshape(x.shape[:-1] + (h, x.shape[-1] // h))


def rms_norm(x, g):
    xf = x.astype(F32)
    y = xf * lax.rsqrt(jnp.mean(xf * xf, axis=-1, keepdims=True) + NORM_EPS)
    return (y * g.astype(F32)).astype(x.dtype)


def head_norm(x, eps):
    xf = x.astype(F32)
    xc = xf - jnp.mean(xf, axis=-1, keepdims=True)
    return xc * lax.rsqrt(jnp.mean(xc * xc, axis=-1, keepdims=True) + eps)


def adaln(h, g, shift, scale):
    return rms_norm(h, g) * (1 + scale) + shift


def rope_tables(pos, dim):
    inv = ROPE_BASE ** (-jnp.arange(0, dim, 2, dtype=F32) / dim)
    ang = pos[:, None] * inv[None, :]
    return jnp.cos(ang), jnp.sin(ang)


def rope_rotate(x, cos, sin):
    m = x.shape[-1] // 2
    x1, x2 = x[..., :m], x[..., m:]
    c = cos[None, :, None, :]
    s = sin[None, :, None, :]
    return jnp.concatenate([x1 * c - x2 * s, x2 * c + x1 * s], axis=-1).astype(x.dtype)


def axial_tables(seq, dim):
    rows = seq // GRID_W
    row = jnp.repeat(jnp.arange(rows, dtype=F32), GRID_W)
    col = jnp.broadcast_to(jnp.arange(GRID_W, dtype=F32)[None, :], (rows, GRID_W)).reshape(-1)
    return rope_tables(row, dim // 2), rope_tables(col, dim // 2)


def axial_rope(x, tabs):
    (cos_r, sin_r), (cos_c, sin_c) = tabs
    half = x.shape[-1] // 2
    return jnp.concatenate([rope_rotate(x[..., :half], cos_r, sin_r),
                            rope_rotate(x[..., half:], cos_c, sin_c)], axis=-1)


def window_gqa(q, k, v, k_ctx, v_ctx, sink):
    b, s, hq, d = q.shape
    g = hq // ATT_KV_HEADS
    nb = s // BLOCK
    nw = 3 * BLOCK
    nc = k_ctx.shape[1]
    qb = (q * d ** -0.5).reshape(b, nb, BLOCK, ATT_KV_HEADS, g, d)
    pad = ((0, 0), (BLOCK, BLOCK), (0, 0), (0, 0))
    kp = jnp.pad(k, pad).reshape(b, nb + 2, BLOCK, ATT_KV_HEADS, d)
    vp = jnp.pad(v, pad).reshape(b, nb + 2, BLOCK, ATT_KV_HEADS, d)
    kw = jnp.concatenate([kp[:, :-2], kp[:, 1:-1], kp[:, 2:]], axis=2)
    vw = jnp.concatenate([vp[:, :-2], vp[:, 1:-1], vp[:, 2:]], axis=2)
    blk0 = jnp.arange(nb)[:, None, None] * BLOCK
    q_pos = blk0 + jnp.arange(BLOCK)[None, :, None]
    k_pos = blk0 - BLOCK + jnp.arange(nw)[None, None, :]
    valid = (jnp.abs(k_pos - q_pos) <= WINDOW) & (k_pos >= 0) & (k_pos < s)
    s_win = jnp.einsum('bnqhgd,bnkhd->bnhgqk', qb, kw).astype(F32)
    s_win = jnp.where(valid[None, :, None, None], s_win, NEG_INF)
    s_ctx = jnp.einsum('bnqhgd,bchd->bnhgqc', qb, k_ctx).astype(F32)
    s_sink = jnp.broadcast_to(sink.astype(F32).reshape(ATT_KV_HEADS, g, 1, 1), s_win.shape[:-1] + (1,))
    p = jax.nn.softmax(jnp.concatenate([s_win, s_ctx, s_sink], axis=-1), axis=-1).astype(v.dtype)
    o = (jnp.einsum('bnhgqk,bnkhd->bnqhgd', p[..., :nw], vw)
         + jnp.einsum('bnhgqc,bchd->bnqhgd', p[..., nw:nw + nc], v_ctx))
    return o.reshape(b, s, hq * d)


def ctx_gqa(q, k, v, sink):
    b, l, hq, d = q.shape
    g = hq // ATT_KV_HEADS
    qg = (q * d ** -0.5).reshape(b, l, ATT_KV_HEADS, g, d)
    sc = jnp.einsum('bqhgd,bkhd->bhgqk', qg, k).astype(F32)
    s_sink = jnp.broadcast_to(sink.astype(F32).reshape(ATT_KV_HEADS, g, 1, 1), sc.shape[:-1] + (1,))
    p = jax.nn.softmax(jnp.concatenate([sc, s_sink], axis=-1), axis=-1).astype(v.dtype)
    o = jnp.einsum('bhgqk,bkhd->bqhgd', p[..., :l], v)
    return o.reshape(b, l, hq * d)


def retention_log_gammas():
    lg = jnp.log(1.0 - jnp.exp(jnp.linspace(math.log(1.0 / 32), math.log(1.0 / 512), 2 * RET_HEADS, dtype=F32)))
    return lg[0::2], lg[1::2]


def retention_chunked(q, k, v, log_gamma, s0):
    b, t, h, d = q.shape
    n = t // RET_CHUNK
    cl = RET_CHUNK
    qc = q.reshape(b, n, cl, h, d).astype(F32)
    kc = k.reshape(b, n, cl, h, d).astype(F32)
    vc = v.reshape(b, n, cl, h, d).astype(F32)
    idx = jnp.arange(cl, dtype=F32)
    rel = idx[:, None] - idx[None, :]
    d_intra = jnp.where(rel >= 0, jnp.exp(jnp.maximum(rel, 0.0)[None] * log_gamma[:, None, None]), 0.0)
    a_int = jnp.einsum('bnihd,bnjhd->bnhij', qc, kc) * d_intra
    o = jnp.einsum('bnhij,bnjhe->bnihe', a_int, vc)
    k_dec = kc * jnp.exp((cl - 1 - idx)[:, None] * log_gamma[None, :])[..., None]
    kv = jnp.einsum('bnjhd,bnjhe->nbhde', k_dec, vc)
    g_chunk = jnp.exp(cl * log_gamma)[:, None, None]

    def step(state, kv_n):
        return state * g_chunk + kv_n, state

    s_fin, s_prev = lax.scan(step, s0, kv)
    q_dec = qc * jnp.exp((idx + 1)[:, None] * log_gamma[None, :])[..., None]
    o = o + jnp.einsum('bnihd,nbhde->bnihe', q_dec, s_prev)
    return o.reshape(b, t, h, d), s_fin


def retention_mix(pb, cos, sin, lg_f, lg_b, s0f, s0b):
    bsz, t = pb.shape[:2]
    q, k, v, gf, gb = split_sizes(pb, RET_SIZES)
    q = rope_rotate(heads(q, RET_HEADS), cos, sin)
    k = rope_rotate(heads(k, RET_HEADS), cos, sin) * HEAD_DIM ** -0.5
    v = heads(v, RET_HEADS)
    yf, sf = retention_chunked(q, k, v, lg_f, s0f)
    yb, sb = retention_chunked(q[:, ::-1], k[:, ::-1], v[:, ::-1], lg_b, s0b)
    out = (jax.nn.silu(gf) * head_norm(yf, RET_GN_EPS).reshape(bsz, t, -1)
           + jax.nn.silu(gb) * head_norm(yb[:, ::-1], RET_GN_EPS).reshape(bsz, t, -1))
    return out.astype(pb.dtype), sf, sb


def mla_project(pc, lp):
    cq, ckv, kr = split_sizes(pc, MLA_SIZES)
    q = heads(rms_norm(cq, lp['g_qnorm']) @ lp['w_uq'], MLA_HEADS)
    kv = heads(rms_norm(ckv, lp['g_kvnorm']) @ lp['w_ukv'], MLA_HEADS)
    return q[..., :MLA_NOPE], q[..., MLA_NOPE:], kv[..., :MLA_NOPE], kr[:, :, None, :], kv[..., MLA_NOPE:]


def mla_attend(qn, qr, kn, kr, v):
    b, t, h, _ = qn.shape
    nb = t // BLOCK
    scale = (MLA_NOPE + MLA_ROPE) ** -0.5
    kr2 = kr[:, :, 0]

    def one_block(args):
        qn_b, qr_b = args
        sc = (jnp.einsum('bqhd,bkhd->bhqk', qn_b, kn) + jnp.einsum('bqhr,bkr->bhqk', qr_b, kr2)).astype(F32) * scale
        p = jax.nn.softmax(sc, axis=-1).astype(v.dtype)
        return jnp.einsum('bhqk,bkhd->bqhd', p, v)

    def to_blocks(z):
        return jnp.moveaxis(z.reshape(b, nb, BLOCK, h, z.shape[-1]), 1, 0)

    o = lax.map(one_block, (to_blocks(qn), to_blocks(qr)))
    return jnp.moveaxis(o, 0, 1).reshape(b, t, h * MLA_V)


def centred_shift(p):
    prev = jnp.pad(p, ((0, 0), (1, 0), (0, 0)))[:, :-1]
    nxt = jnp.pad(p, ((0, 0), (0, 1), (0, 0)))[:, 1:]
    return 0.5 * (prev + nxt)


def rwkv_features(pd, lp):
    z = pd + (centred_shift(pd) - pd) * lp['rwkv_mu']
    zr, zk, zv, zwf, zwb, zaf, zab, zg = split_sizes(z, RWKV_SIZES)
    r = heads(zr, RWKV_HEADS)
    k = heads(zk, RWKV_HEADS)
    v = heads(zv, RWKV_HEADS)
    g = jax.nn.sigmoid(zg) @ lp['rwkv_g_up']
    kk = (k * heads(lp['rwkv_k_k'], RWKV_HEADS)).astype(F32)
    kk = kk / jnp.maximum(jnp.sqrt(jnp.sum(kk * kk, axis=-1, keepdims=True)), 1e-12)
    k_a = heads(lp['rwkv_k_a'], RWKV_HEADS)
    dirs = []
    for di, (zw, za) in enumerate(((zwf, zaf), (zwb, zab))):
        logw = -jax.nn.softplus(-(lp['rwkv_w0'][di] + jnp.tanh(zw) @ lp['rwkv_w_up'][di]).astype(F32)) - 0.5
        w = heads(-jnp.exp(logw), RWKV_HEADS)
        a = heads(jax.nn.sigmoid((lp['rwkv_a0'][di] + za @ lp['rwkv_a_up'][di]).astype(F32)), RWKV_HEADS)
        dirs.append((w, k * (1 + (a - 1) * k_a), kk * a))
    return r, v, g, kk, dirs


def rwkv_scan(r, w, k, v, kk, b_, s0, reverse):
    xs = tuple(jnp.moveaxis(z.astype(F32), 1, 0) for z in (r, w, k, v, kk, b_))

    def step(st, inp):
        r_t, w_t, k_t, v_t, kk_t, b_t = inp
        sa = jnp.einsum('bhvk,bhk->bhv', st, -kk_t)
        st = st * w_t[:, :, None, :] + sa[..., None] * b_t[:, :, None, :] + v_t[..., None] * k_t[:, :, None, :]
        return st, jnp.einsum('bhvk,bhk->bhv', st, r_t)

    s_fin, y = lax.scan(step, s0, xs, reverse=reverse)
    return jnp.moveaxis(y, 0, 1), s_fin


BF16 = jnp.bfloat16
RWKV_CHUNK = 64
RWKV_SUB = 16
RWKV_TOKENS_PER_STEP = 512
_NN = (((2,), (1,)), ((0,), (0,)))
_NT = (((2,), (2,)), ((0,), (0,)))
_TN = (((1,), (1,)), ((0,), (0,)))


def _bf16_parts(a, n):
    parts, rem = [], a
    for i in range(n):
        p = rem.astype(BF16)
        parts.append(p)
        if i + 1 < n:
            rem = rem - p.astype(F32)
    return parts


def _dot(a, b, dims, passes):
    if passes == 1:
        return lax.dot_general(a.astype(BF16), b.astype(BF16), dims, preferred_element_type=F32)
    a_hi, a_lo = _bf16_parts(a, 2)
    b_hi, b_lo = _bf16_parts(b, 2)
    out = lax.dot_general(a_hi, b_hi, dims, preferred_element_type=F32)
    out = out + lax.dot_general(a_hi, b_lo, dims, preferred_element_type=F32)
    return out + lax.dot_general(a_lo, b_hi, dims, preferred_element_type=F32)


def _unit_lower_inverse(l_mat, same_sub, passes):
    mm = lambda a, b: _dot(a, b, _NN, passes)
    ld = jnp.where(same_sub, l_mat, 0.0)
    lo = l_mat - ld
    p2 = mm(ld, ld)
    d = p2 - ld - mm(ld, p2)
    pw = p2
    span = 4
    while span < RWKV_SUB:
        pw = mm(pw, pw)
        d = d + pw + mm(d, pw)
        span *= 2
    n1 = lo + mm(d, lo)
    n2 = mm(n1, n1)
    x = n2 - n1 - mm(n1, n2)
    span = 4
    pw = n2
    while span < RWKV_CHUNK // RWKV_SUB:
        pw = mm(pw, pw)
        x = x + pw + mm(x, pw)
        span *= 2
    return x + d + mm(x, d)


def _rwkv_scan_kernel(r_ref, lw_ref, k_ref, v_ref, kk_ref, b_ref, s0_ref, y_ref, sfin_ref, s_scr,
                      *, reverse, n_chunks):
    cl = RWKV_CHUNK
    hd = r_ref.shape[-1]

    @pl.when(pl.program_id(1) == 0)
    def _():
        s_scr[...] = s0_ref[0, 0]

    shp = (n_chunks, cl, hd)
    r = r_ref[0, 0].reshape(shp)
    lw = lw_ref[0, 0].reshape(shp)
    k = k_ref[0, 0].reshape(shp)
    v = v_ref[0, 0].reshape(shp)
    kk = kk_ref[0, 0].reshape(shp)
    b = b_ref[0, 0].reshape(shp)

    row = lax.broadcasted_iota(jnp.int32, (cl, cl), 0)
    col = lax.broadcasted_iota(jnp.int32, (cl, cl), 1)
    before = (col > row) if reverse else (col < row)
    upto = before | (col == row)
    same_sub = (row // RWKV_SUB) == (col // RWKV_SUB)

    tri = jnp.broadcast_to(jnp.where(upto, 1.0, 0.0).astype(BF16)[None], (n_chunks, cl, cl))
    cum = sum(lax.dot_general(tri, p, _NN, preferred_element_type=F32) for p in _bf16_parts(lw, 3))
    last = 0 if reverse else cl - 1
    mid = cl // 2 if reverse else cl // 2 - 1
    tot = cum[:, last:last + 1, :]
    rho = cum[:, mid:mid + 1, :]
    cum_ex = cum - lw

    a_t = kk * jnp.exp(cum_ex - rho)
    r_t = r * jnp.exp(cum - rho)
    e_out = jnp.exp(rho - cum)
    b_t = b * e_out
    k_t = k * e_out
    a_0 = kk * jnp.exp(cum_ex)
    r_0 = r * jnp.exp(cum)
    e_end = jnp.exp(tot - cum)
    b_e = b * e_end
    k_e = k * e_end

    ps, pt = 3, 3
    l_mat = jnp.where(before, _dot(a_t, b_t, _NT, ps), 0.0)
    m_ak = jnp.where(before, _dot(a_t, k_t, _NT, ps), 0.0)
    m_rb = jnp.where(upto, _dot(r_t, b_t, _NT, ps), 0.0)
    m_rk = jnp.where(upto, _dot(r_t, k_t, _NT, ps), 0.0)
    t_m1 = _unit_lower_inverse(l_mat, same_sub, pt)

    mv = _dot(m_ak, v, _NN, ps)
    q = -(a_0 + _dot(t_m1, a_0, _NN, ps))
    w = -(mv + _dot(t_m1, mv, _NN, ps))
    r_h = r_0 + _dot(m_rb, q, _NN, ps)
    y_loc = _dot(m_rb, w, _NN, ps) + _dot(m_rk, v, _NN, ps)
    eye = lax.broadcasted_iota(jnp.int32, (hd, hd), 0) == lax.broadcasted_iota(jnp.int32, (hd, hd), 1)
    phi = jnp.where(eye, jnp.exp(tot), 0.0) + _dot(b_e, q, _TN, ps)
    psi = _dot(b_e, w, _TN, ps) + _dot(k_e, v, _TN, ps)

    s = s_scr[...]
    starts = [None] * n_chunks
    order = range(n_chunks - 1, -1, -1) if reverse else range(n_chunks)
    for n in order:
        starts[n] = s
        s = _dot(phi[n], s, (((1,), (0,)), ((), ())), 3) + psi[n]
    s_scr[...] = s
    sfin_ref[0, 0] = s
    y = _dot(r_h, jnp.stack(starts), _NN, ps) + y_loc
    y_ref[0, 0] = y.reshape(n_chunks * cl, hd)


def rwkv_scan_blocked(r, lw, k, v, kk, b_, s0t, reverse):
    bsz, nh, t, hd = r.shape
    tb = min(RWKV_TOKENS_PER_STEP, t)
    nt = t // tb
    assert t == nt * tb and tb % RWKV_CHUNK == 0
    if reverse:
        tmap = lambda bh, j: (bh // nh, bh % nh, nt - 1 - j, 0)
    else:
        tmap = lambda bh, j: (bh // nh, bh % nh, j, 0)
    smap = lambda bh, j: (bh // nh, bh % nh, 0, 0)
    seq_spec = pl.BlockSpec((1, 1, tb, hd), tmap)
    st_spec = pl.BlockSpec((1, 1, hd, hd), smap)
    y, sfin = pl.pallas_call(
        functools.partial(_rwkv_scan_kernel, reverse=reverse, n_chunks=tb // RWKV_CHUNK),
        grid=(bsz * nh, nt),
        in_specs=[seq_spec] * 6 + [st_spec],
        out_specs=[seq_spec, st_spec],
        out_shape=[jax.ShapeDtypeStruct((bsz, nh, t, hd), F32),
                   jax.ShapeDtypeStruct((bsz, nh, hd, hd), F32)],
        scratch_shapes=[pltpu.VMEM((hd, hd), F32)],
        compiler_params=pltpu.CompilerParams(dimension_semantics=("parallel", "arbitrary")),
        name="rwkv_scan_rev" if reverse else "rwkv_scan_fwd",
    )(r, lw, k, v, kk, b_, s0t)
    return y, sfin


def rwkv_mix(feat, lp, s0f, s0b):
    r, v, g, kk, dirs = feat
    (lwf, kf, bf), (lwb, kb, bb) = dirs
    tr = lambda z: jnp.swapaxes(z.astype(F32), 1, 2)
    rt, vt, kkt = tr(r), tr(v), tr(kk)
    yf, sf = rwkv_scan_blocked(rt, tr(lwf), tr(kf), vt, kkt, tr(bf), s0f, False)
    yb, sb = rwkv_scan_blocked(rt, tr(lwb), tr(kb), vt, kkt, tr(bb), s0b, True)
    yf = jnp.swapaxes(yf, 1, 2)
    yb = jnp.swapaxes(yb, 1, 2)
    y = head_norm(yf + yb, RWKV_GN_EPS) * heads(lp['rwkv_lnx_w'], RWKV_HEADS) + heads(lp['rwkv_lnx_b'], RWKV_HEADS)
    bonus = jnp.sum(r * (kf + kb) * lp['rwkv_r_k'], axis=-1, keepdims=True) * v
    out = (y + bonus).reshape(g.shape) * g
    return out.astype(g.dtype), sf, sb


def merge_branches(u, ys, lp):
    acc = jnp.zeros_like(u)
    for n in range(N_BRANCH):
        gate = jax.nn.sigmoid(u @ lp['w_gate'][n] + lp['b_gate'][n])
        acc = acc + gate * (ys[n].astype(u.dtype) @ lp['w_branch'][n])
    return acc @ lp['w_out']


def token_mixers(u_ctx, u_lat, lp, need_ctx):
    b, s, _ = u_lat.shape
    l = u_ctx.shape[1]
    pa_l, pb_l, pc_l, pd_l = split_sizes(u_lat @ lp['w_in'], GROUP_SIZES)
    pa_c, pb_c, pc_c, pd_c = split_sizes(u_ctx @ lp['w_in'], GROUP_SIZES)
    tab_att = axial_tables(s, HEAD_DIM)
    tab_mla = axial_tables(s, MLA_ROPE)

    aq_l, ak_l, av_l = split_sizes(pa_l, A_SIZES)
    aq_c, ak_c, av_c = split_sizes(pa_c, A_SIZES)
    k_c = heads(ak_c, ATT_KV_HEADS)
    v_c = heads(av_c, ATT_KV_HEADS)
    ya_l = window_gqa(axial_rope(heads(aq_l, ATT_Q_HEADS), tab_att),
                      axial_rope(heads(ak_l, ATT_KV_HEADS), tab_att),
                      heads(av_l, ATT_KV_HEADS), k_c, v_c, lp['sink'])

    lg_f, lg_b = retention_log_gammas()
    cos_c, sin_c = rope_tables(jnp.arange(l, dtype=F32), HEAD_DIM)
    cos_l, sin_l = rope_tables(l + jnp.arange(s, dtype=F32), HEAD_DIM)
    zero_ret = jnp.zeros((b, RET_HEADS, HEAD_DIM, HEAD_DIM), F32)
    yb_c, sbf, sbb = retention_mix(pb_c, cos_c, sin_c, lg_f, lg_b, zero_ret, zero_ret)
    yb_l, _, _ = retention_mix(pb_l, cos_l, sin_l, lg_f, lg_b, sbf, sbb)

    qn_l, qr_l, kn_l, kr_l, vm_l = mla_project(pc_l, lp)
    qn_c, qr_c, kn_c, kr_c, vm_c = mla_project(pc_c, lp)
    qr_l = axial_rope(qr_l, tab_mla)
    kr_l = axial_rope(kr_l, tab_mla)
    yc_l = mla_attend(qn_l, qr_l, jnp.concatenate([kn_c, kn_l], axis=1),
                      jnp.concatenate([kr_c, kr_l], axis=1), jnp.concatenate([vm_c, vm_l], axis=1))

    zero_wkv = jnp.zeros((b, RWKV_HEADS, HEAD_DIM, HEAD_DIM), F32)
    yd_c, sdf, sdb = rwkv_mix(rwkv_features(pd_c, lp), lp, zero_wkv, zero_wkv)
    yd_l, _, _ = rwkv_mix(rwkv_features(pd_l, lp), lp, sdf, sdb)

    m_lat = merge_branches(u_lat, (ya_l, yb_l, yc_l, yd_l), lp)
    if not need_ctx:
        return None, m_lat
    ya_c = ctx_gqa(heads(aq_c, ATT_Q_HEADS), k_c, v_c, lp['sink'])
    yc_c = mla_attend(qn_c, qr_c, kn_c, kr_c, vm_c)
    m_ctx = merge_branches(u_ctx, (ya_c, yb_c, yc_c, yd_c), lp)
    return m_ctx, m_lat


def moe_ffn(x, router_w, router_b, w_e_gate, w_e_up, w_e_down):
    n_tok, dm = x.shape
    scores = jax.nn.sigmoid(x.astype(F32) @ router_w.astype(F32))
    grp = (scores + router_b.astype(F32)).reshape(n_tok, N_GROUPS, EXPERTS_PER_GROUP)
    g_sel = jnp.argmax(jnp.sum(lax.top_k(grp, TOP_K)[0], axis=-1), axis=-1)
    in_grp = lax.top_k(jnp.take_along_axis(grp, g_sel[:, None, None], axis=1)[:, 0], TOP_K)[1]
    e_idx = g_sel[:, None] * EXPERTS_PER_GROUP + in_grp
    w_sel = jnp.take_along_axis(scores, e_idx, axis=1)
    w_sel = w_sel / jnp.sum(w_sel, axis=-1, keepdims=True)
    n_asg = n_tok * TOP_K
    flat_e = e_idx.reshape(-1)
    flat_tok = jnp.repeat(jnp.arange(n_tok, dtype=jnp.int32), TOP_K)
    flat_w = w_sel.reshape(-1)
    order = jnp.argsort(flat_e)
    se = flat_e[order]
    counts = jnp.bincount(flat_e, length=N_EXPERTS)
    padded = (counts + MOE_BLOCK - 1) // MOE_BLOCK * MOE_BLOCK
    pad_end = jnp.cumsum(padded)
    pad_start = pad_end - padded
    start = jnp.cumsum(counts) - counts
    dest = pad_start[se] + jnp.arange(n_asg) - start[se]
    n_blocks = -(-n_asg // MOE_BLOCK) + N_EXPERTS
    n_rows = n_blocks * MOE_BLOCK
    row_tok = jnp.zeros((n_rows,), jnp.int32).at[dest].set(flat_tok[order])
    row_w = jnp.zeros((n_rows,), F32).at[dest].set(flat_w[order])
    blk_e = jnp.minimum(jnp.searchsorted(pad_end, jnp.arange(n_blocks) * MOE_BLOCK, side='right'), N_EXPERTS - 1)
    xb = x[row_tok].reshape(n_blocks, MOE_BLOCK, dm)

    def expert_block(args):
        xblk, e = args
        hid = jax.nn.silu(xblk @ w_e_gate[e]) * (xblk @ w_e_up[e])
        return hid @ w_e_down[e]

    yb = lax.map(expert_block, (xb, blk_e)).reshape(n_rows, dm)
    yb = yb * row_w[:, None].astype(yb.dtype)
    return jnp.zeros_like(x).at[row_tok].add(yb.astype(x.dtype))


def layer(h_ctx, h_lat, c, c_ctx, lp, router_w, router_b, need_ctx):
    b, s, d = h_lat.shape
    l = h_ctx.shape[1]
    m_lat = jnp.split((jax.nn.silu(c) @ lp['w_ada'] + lp['b_ada'])[:, None, :], 6, axis=-1)
    m_ctx = jnp.split(jax.nn.silu(c_ctx) @ lp['w_ada'] + lp['b_ada'], 6, axis=-1)
    u_lat = adaln(h_lat, lp['g_norm1'], m_lat[0], m_lat[1])
    u_ctx = adaln(h_ctx, lp['g_norm1'], m_ctx[0], m_ctx[1])
    mix_ctx, mix_lat = token_mixers(u_ctx, u_lat, lp, need_ctx)
    h_lat = h_lat + m_lat[2] * mix_lat
    f_lat = adaln(h_lat, lp['g_norm2'], m_lat[3], m_lat[4]).reshape(b * s, d)
    if not need_ctx:
        ffn = moe_ffn(f_lat, router_w, router_b, lp['w_e_gate'], lp['w_e_up'], lp['w_e_down'])
        return h_ctx, h_lat + m_lat[5] * ffn.reshape(b, s, d)
    h_ctx = h_ctx + m_ctx[2] * mix_ctx
    f_ctx = adaln(h_ctx, lp['g_norm2'], m_ctx[3], m_ctx[4]).reshape(b * l, d)
    ffn = moe_ffn(jnp.concatenate([f_ctx, f_lat], axis=0), router_w, router_b,
                  lp['w_e_gate'], lp['w_e_up'], lp['w_e_down'])
    h_ctx = h_ctx + m_ctx[5] * ffn[:b * l].reshape(b, l, d)
    h_lat = h_lat + m_lat[5] * ffn[b * l:].reshape(b, s, d)
    return h_ctx, h_lat


def _final_norm_kernel(x_ref, g_ref, o_ref):
    x = x_ref[...]
    y = x * lax.rsqrt(jnp.mean(x * x, axis=-1, keepdims=True) + NORM_EPS)
    o_ref[...] = y * g_ref[...]


def final_rms_norm(x, g):
    b, s, d = x.shape
    rows = b * s
    tile = 1024
    out = pl.pallas_call(
        _final_norm_kernel,
        grid=(rows // tile,),
        in_specs=[pl.BlockSpec((tile, d), lambda i: (i, 0)), pl.BlockSpec((1, d), lambda i: (0, 0))],
        out_specs=pl.BlockSpec((tile, d), lambda i: (i, 0)),
        out_shape=jax.ShapeDtypeStruct((rows, d), x.dtype),
        name="final_rms_norm",
    )(x.reshape(rows, d), g.reshape(1, d))
    return out.reshape(b, s, d)


_LAYER_PARAM_NAMES = (
    'w_ada', 'b_ada', 'g_norm1', 'g_norm2', 'w_in', 'sink', 'g_qnorm', 'g_kvnorm', 'w_uq',
    'w_ukv', 'rwkv_mu', 'rwkv_w0', 'rwkv_w_up', 'rwkv_a0', 'rwkv_a_up', 'rwkv_g_up',
    'rwkv_k_k', 'rwkv_k_a', 'rwkv_r_k', 'rwkv_lnx_w', 'rwkv_lnx_b', 'w_gate', 'b_gate',
    'w_branch', 'w_out', 'w_e_gate', 'w_e_up', 'w_e_down')


def kernel(x, c, ctx, c_ctx, w_ada, b_ada, g_norm1, g_norm2, w_in, sink, g_qnorm, g_kvnorm,
           w_uq, w_ukv, rwkv_mu, rwkv_w0, rwkv_w_up, rwkv_a0, rwkv_a_up, rwkv_g_up, rwkv_k_k,
           rwkv_k_a, rwkv_r_k, rwkv_lnx_w, rwkv_lnx_b, w_gate, b_gate, w_branch, w_out,
           router_w, router_b, w_e_gate, w_e_up, w_e_down, g_final):
    stacked = (w_ada, b_ada, g_norm1, g_norm2, w_in, sink, g_qnorm, g_kvnorm, w_uq, w_ukv, rwkv_mu,
               rwkv_w0, rwkv_w_up, rwkv_a0, rwkv_a_up, rwkv_g_up, rwkv_k_k, rwkv_k_a, rwkv_r_k,
               rwkv_lnx_w, rwkv_lnx_b, w_gate, b_gate, w_branch, w_out, w_e_gate, w_e_up, w_e_down)
    h_ctx, h_lat = ctx, x
    for i in range(DEPTH):
        lp = {n: a[i] for n, a in zip(_LAYER_PARAM_NAMES, stacked)}
        h_ctx, h_lat = layer(h_ctx, h_lat, c, c_ctx, lp, router_w, router_b, i < DEPTH - 1)
    return final_rms_norm(h_lat, g_final)
```

```python
import functools
import math

import jax
import jax.numpy as jnp
from jax import lax
from jax.experimental import pallas as pl
from jax.experimental.pallas import tpu as pltpu

D_MODEL = 1024
BATCH = 4
SEQ = 8192
DEPTH = 2

GRID_W = 64
CTX_LEN = 256
N_BRANCH = 4
BRANCH_W = D_MODEL // N_BRANCH
HEAD_DIM = 64
BRANCH_HEADS = BRANCH_W // HEAD_DIM
BLOCK = 128
ROPE_BASE = 10000.0
NORM_EPS = 1e-6
NEG_INF = -1e30
ATT_Q_HEADS = BRANCH_HEADS
ATT_KV_HEADS = BRANCH_HEADS // 2
WINDOW = 128
RET_HEADS = BRANCH_HEADS
RET_CHUNK = 128
RET_GN_EPS = 1e-5
MLA_HEADS = BRANCH_HEADS
MLA_Q_RANK = 256
MLA_KV_RANK = 128
MLA_NOPE = HEAD_DIM
MLA_ROPE = HEAD_DIM // 2
MLA_V = HEAD_DIM
RWKV_HEADS = BRANCH_HEADS
RWKV_DECAY_RANK = 64
RWKV_A_RANK = 64
RWKV_GATE_RANK = 128
RWKV_GN_EPS = 64e-5
N_EXPERTS = 64
N_GROUPS = 8
EXPERTS_PER_GROUP = N_EXPERTS // N_GROUPS
TOP_K = 2
EXPERT_FF = 512
MOE_BLOCK = 256
A_SIZES = (ATT_Q_HEADS * HEAD_DIM, ATT_KV_HEADS * HEAD_DIM, ATT_KV_HEADS * HEAD_DIM)
RET_SIZES = (BRANCH_W,) * 5
MLA_SIZES = (MLA_Q_RANK, MLA_KV_RANK, MLA_ROPE)
RWKV_SIZES = (BRANCH_W, BRANCH_W, BRANCH_W, RWKV_DECAY_RANK, RWKV_DECAY_RANK,
              RWKV_A_RANK, RWKV_A_RANK, RWKV_GATE_RANK)
GROUP_SIZES = (sum(A_SIZES), sum(RET_SIZES), sum(MLA_SIZES), sum(RWKV_SIZES))
N_IN = sum(GROUP_SIZES)
F32 = jnp.float32


def split_sizes(x, sizes):
    out, o = [], 0
    for n in sizes:
        out.append(x[..., o:o + n])
        o += n
    return out


def heads(x, h):
    return x.reshape(x.shape[:-1] + (h, x.shape[-1] // h))


def rms_norm(x, g):
    xf = x.astype(F32)
    y = xf * lax.rsqrt(jnp.mean(xf * xf, axis=-1, keepdims=True) + NORM_EPS)
    return (y * g.astype(F32)).astype(x.dtype)


def head_norm(x, eps):
    xf = x.astype(F32)
    xc = xf - jnp.mean(xf, axis=-1, keepdims=True)
    return xc * lax.rsqrt(jnp.mean(xc * xc, axis=-1, keepdims=True) + eps)


def adaln(h, g, shift, scale):
    return rms_norm(h, g) * (1 + scale) + shift


def rope_tables(pos, dim):
    inv = ROPE_BASE ** (-jnp.arange(0, dim, 2, dtype=F32) / dim)
    ang = pos[:, None] * inv[None, :]
    return jnp.cos(ang), jnp.sin(ang)


def rope_rotate(x, cos, sin):
    m = x.shape[-1] // 2
    x1, x2 = x[..., :m], x[..., m:]
    c = cos[None, :, None, :]
    s = sin[None, :, None, :]
    return jnp.concatenate([x1 * c - x2 * s, x2 * c + x1 * s], axis=-1).astype(x.dtype)


def axial_tables(seq, dim):
    rows = seq // GRID_W
    row = jnp.repeat(jnp.arange(rows, dtype=F32), GRID_W)
    col = jnp.broadcast_to(jnp.arange(GRID_W, dtype=F32)[None, :], (rows, GRID_W)).reshape(-1)
    return rope_tables(row, dim // 2), rope_tables(col, dim // 2)


def axial_rope(x, tabs):
    (cos_r, sin_r), (cos_c, sin_c) = tabs
    half = x.shape[-1] // 2
    return jnp.concatenate([rope_rotate(x[..., :half], cos_r, sin_r),
                            rope_rotate(x[..., half:], cos_c, sin_c)], axis=-1)


def window_gqa(q, k, v, k_ctx, v_ctx, sink):
    b, s, hq, d = q.shape
    g = hq // ATT_KV_HEADS
    nb = s // BLOCK
    nw = 3 * BLOCK
    nc = k_ctx.shape[1]
    qb = (q * d ** -0.5).reshape(b, nb, BLOCK, ATT_KV_HEADS, g, d)
    pad = ((0, 0), (BLOCK, BLOCK), (0, 0), (0, 0))
    kp = jnp.pad(k, pad).reshape(b, nb + 2, BLOCK, ATT_KV_HEADS, d)
    vp = jnp.pad(v, pad).reshape(b, nb + 2, BLOCK, ATT_KV_HEADS, d)
    kw = jnp.concatenate([kp[:, :-2], kp[:, 1:-1], kp[:, 2:]], axis=2)
    vw = jnp.concatenate([vp[:, :-2], vp[:, 1:-1], vp[:, 2:]], axis=2)
    blk0 = jnp.arange(nb)[:, None, None] * BLOCK
    q_pos = blk0 + jnp.arange(BLOCK)[None, :, None]
    k_pos = blk0 - BLOCK + jnp.arange(nw)[None, None, :]
    valid = (jnp.abs(k_pos - q_pos) <= WINDOW) & (k_pos >= 0) & (k_pos < s)
    s_win = jnp.einsum('bnqhgd,bnkhd->bnhgqk', qb, kw).astype(F32)
    s_win = jnp.where(valid[None, :, None, None], s_win, NEG_INF)
    s_ctx = jnp.einsum('bnqhgd,bchd->bnhgqc', qb, k_ctx).astype(F32)
    s_sink = jnp.broadcast_to(sink.astype(F32).reshape(ATT_KV_HEADS, g, 1, 1), s_win.shape[:-1] + (1,))
    p = jax.nn.softmax(jnp.concatenate([s_win, s_ctx, s_sink], axis=-1), axis=-1).astype(v.dtype)
    o = (jnp.einsum('bnhgqk,bnkhd->bnqhgd', p[..., :nw], vw)
         + jnp.einsum('bnhgqc,bchd->bnqhgd', p[..., nw:nw + nc], v_ctx))
    return o.reshape(b, s, hq * d)


def ctx_gqa(q, k, v, sink):
    b, l, hq, d = q.shape
    g = hq // ATT_KV_HEADS
    qg = (q * d ** -0.5).reshape(b, l, ATT_KV_HEADS, g, d)
    sc = jnp.einsum('bqhgd,bkhd->bhgqk', qg, k).astype(F32)
    s_sink = jnp.broadcast_to(sink.astype(F32).reshape(ATT_KV_HEADS, g, 1, 1), sc.shape[:-1] + (1,))
    p = jax.nn.softmax(jnp.concatenate([sc, s_sink], axis=-1), axis=-1).astype(v.dtype)
    o = jnp.einsum('bhgqk,bkhd->bqhgd', p[..., :l], v)
    return o.reshape(b, l, hq * d)


def retention_log_gammas():
    lg = jnp.log(1.0 - jnp.exp(jnp.linspace(math.log(1.0 / 32), math.log(1.0 / 512), 2 * RET_HEADS, dtype=F32)))
    return lg[0::2], lg[1::2]


def retention_chunked(q, k, v, log_gamma, s0):
    b, t, h, d = q.shape
    n = t // RET_CHUNK
    cl = RET_CHUNK
    qc = q.reshape(b, n, cl, h, d).astype(F32)
    kc = k.reshape(b, n, cl, h, d).astype(F32)
    vc = v.reshape(b, n, cl, h, d).astype(F32)
    idx = jnp.arange(cl, dtype=F32)
    rel = idx[:, None] - idx[None, :]
    d_intra = jnp.where(rel >= 0, jnp.exp(jnp.maximum(rel, 0.0)[None] * log_gamma[:, None, None]), 0.0)
    a_int = jnp.einsum('bnihd,bnjhd->bnhij', qc, kc) * d_intra
    o = jnp.einsum('bnhij,bnjhe->bnihe', a_int, vc)
    k_dec = kc * jnp.exp((cl - 1 - idx)[:, None] * log_gamma[None, :])[..., None]
    kv = jnp.einsum('bnjhd,bnjhe->nbhde', k_dec, vc)
    g_chunk = jnp.exp(cl * log_gamma)[:, None, None]

    def step(state, kv_n):
        return state * g_chunk + kv_n, state

    s_fin, s_prev = lax.scan(step, s0, kv)
    q_dec = qc * jnp.exp((idx + 1)[:, None] * log_gamma[None, :])[..., None]
    o = o + jnp.einsum('bnihd,nbhde->bnihe', q_dec, s_prev)
    return o.reshape(b, t, h, d), s_fin


def retention_mix(pb, cos, sin, lg_f, lg_b, s0f, s0b):
    bsz, t = pb.shape[:2]
    q, k, v, gf, gb = split_sizes(pb, RET_SIZES)
    q = rope_rotate(heads(q, RET_HEADS), cos, sin)
    k = rope_rotate(heads(k, RET_HEADS), cos, sin) * HEAD_DIM ** -0.5
    v = heads(v, RET_HEADS)
    yf, sf = retention_chunked(q, k, v, lg_f, s0f)
    yb, sb = retention_chunked(q[:, ::-1], k[:, ::-1], v[:, ::-1], lg_b, s0b)
    out = (jax.nn.silu(gf) * head_norm(yf, RET_GN_EPS).reshape(bsz, t, -1)
           + jax.nn.silu(gb) * head_norm(yb[:, ::-1], RET_GN_EPS).reshape(bsz, t, -1))
    return out.astype(pb.dtype), sf, sb


def mla_project(pc, lp):
    cq, ckv, kr = split_sizes(pc, MLA_SIZES)
    q = heads(rms_norm(cq, lp['g_qnorm']) @ lp['w_uq'], MLA_HEADS)
    kv = heads(rms_norm(ckv, lp['g_kvnorm']) @ lp['w_ukv'], MLA_HEADS)
    return q[..., :MLA_NOPE], q[..., MLA_NOPE:], kv[..., :MLA_NOPE], kr[:, :, None, :], kv[..., MLA_NOPE:]


LANES = 128
MLA_Q_TILE = 1024
MLA_KEY_TILE_MAX = 1408
MLA_VMEM_LIMIT = 40 * 1024 * 1024


def _largest_tile(n, cap, unit):
    if n <= cap:
        return n
    best = None
    for t in range(unit, cap + 1, unit):
        if n % t == 0:
            best = t
    assert best is not None, (n, cap, unit)
    return best


def _mla_flash_kernel(q_ref, k_ref, v_ref, o_ref, m_scr, l_scr, acc_scr, *, scale):
    kv = pl.program_id(3)

    @pl.when(kv == 0)
    def _():
        m_scr[...] = jnp.full_like(m_scr, NEG_INF)
        l_scr[...] = jnp.zeros_like(l_scr)
        acc_scr[...] = jnp.zeros_like(acc_scr)

    q = (q_ref[0, 0] * scale).astype(BF16)
    s = lax.dot_general(q, k_ref[0, 0], (((1,), (1,)), ((), ())), preferred_element_type=F32)
    m_prev = m_scr[...]
    m_new = jnp.maximum(m_prev, jnp.max(s, axis=-1, keepdims=True))
    alpha = jnp.exp(m_prev - m_new)
    p = jnp.exp(s - m_new)
    l_scr[...] = alpha * l_scr[...] + jnp.sum(p, axis=-1, keepdims=True)
    acc_scr[...] = alpha * acc_scr[...] + jnp.dot(p.astype(BF16), v_ref[0, 0], preferred_element_type=F32)
    m_scr[...] = m_new

    @pl.when(kv == pl.num_programs(3) - 1)
    def _():
        o_ref[0, 0] = acc_scr[...] / l_scr[...]


def mla_attend(qn, qr, kn, kr, v):
    b, t, h, _ = qn.shape
    n = kn.shape[1]
    scale = (MLA_NOPE + MLA_ROPE) ** -0.5
    q = jnp.swapaxes(jnp.concatenate([qn, qr], axis=-1), 1, 2)
    kr_h = jnp.broadcast_to(kr, (b, n, h, MLA_ROPE))
    k = jnp.swapaxes(jnp.concatenate([kn, kr_h], axis=-1), 1, 2).astype(BF16)
    vt = jnp.swapaxes(v, 1, 2).astype(BF16)
    dq, dv = q.shape[-1], vt.shape[-1]
    tq = _largest_tile(t, MLA_Q_TILE, 8)
    tk = _largest_tile(n, MLA_KEY_TILE_MAX, LANES)
    o = pl.pallas_call(
        functools.partial(_mla_flash_kernel, scale=scale),
        grid=(b, h, t // tq, n // tk),
        in_specs=[pl.BlockSpec((1, 1, tq, dq), lambda bi, hi, qi, ki: (bi, hi, qi, 0)),
                  pl.BlockSpec((1, 1, tk, dq), lambda bi, hi, qi, ki: (bi, hi, ki, 0)),
                  pl.BlockSpec((1, 1, tk, dv), lambda bi, hi, qi, ki: (bi, hi, ki, 0))],
        out_specs=pl.BlockSpec((1, 1, tq, dv), lambda bi, hi, qi, ki: (bi, hi, qi, 0)),
        out_shape=jax.ShapeDtypeStruct((b, h, t, dv), F32),
        scratch_shapes=[pltpu.VMEM((tq, 1), F32), pltpu.VMEM((tq, 1), F32), pltpu.VMEM((tq, dv), F32)],
        compiler_params=pltpu.CompilerParams(
            dimension_semantics=("parallel", "parallel", "parallel", "arbitrary"),
            vmem_limit_bytes=MLA_VMEM_LIMIT),
        name="mla_flash",
    )(q, k, vt)
    return jnp.swapaxes(o, 1, 2).reshape(b, t, h * dv)


def centred_shift(p):
    prev = jnp.pad(p, ((0, 0), (1, 0), (0, 0)))[:, :-1]
    nxt = jnp.pad(p, ((0, 0), (0, 1), (0, 0)))[:, 1:]
    return 0.5 * (prev + nxt)


def rwkv_features(pd, lp):
    z = pd + (centred_shift(pd) - pd) * lp['rwkv_mu']
    zr, zk, zv, zwf, zwb, zaf, zab, zg = split_sizes(z, RWKV_SIZES)
    r = heads(zr, RWKV_HEADS)
    k = heads(zk, RWKV_HEADS)
    v = heads(zv, RWKV_HEADS)
    g = jax.nn.sigmoid(zg) @ lp['rwkv_g_up']
    kk = (k * heads(lp['rwkv_k_k'], RWKV_HEADS)).astype(F32)
    kk = kk / jnp.maximum(jnp.sqrt(jnp.sum(kk * kk, axis=-1, keepdims=True)), 1e-12)
    k_a = heads(lp['rwkv_k_a'], RWKV_HEADS)
    dirs = []
    for di, (zw, za) in enumerate(((zwf, zaf), (zwb, zab))):
        logw = -jax.nn.softplus(-(lp['rwkv_w0'][di] + jnp.tanh(zw) @ lp['rwkv_w_up'][di]).astype(F32)) - 0.5
        w = heads(-jnp.exp(logw), RWKV_HEADS)
        a = heads(jax.nn.sigmoid((lp['rwkv_a0'][di] + za @ lp['rwkv_a_up'][di]).astype(F32)), RWKV_HEADS)
        dirs.append((w, k * (1 + (a - 1) * k_a), kk * a))
    return r, v, g, kk, dirs


def rwkv_scan(r, w, k, v, kk, b_, s0, reverse):
    xs = tuple(jnp.moveaxis(z.astype(F32), 1, 0) for z in (r, w, k, v, kk, b_))

    def step(st, inp):
        r_t, w_t, k_t, v_t, kk_t, b_t = inp
        sa = jnp.einsum('bhvk,bhk->bhv', st, -kk_t)
        st = st * w_t[:, :, None, :] + sa[..., None] * b_t[:, :, None, :] + v_t[..., None] * k_t[:, :, None, :]
        return st, jnp.einsum('bhvk,bhk->bhv', st, r_t)

    s_fin, y = lax.scan(step, s0, xs, reverse=reverse)
    return jnp.moveaxis(y, 0, 1), s_fin


BF16 = jnp.bfloat16
RWKV_CHUNK = 64
RWKV_SUB = 16
RWKV_TOKENS_PER_STEP = 512
_NN = (((2,), (1,)), ((0,), (0,)))
_NT = (((2,), (2,)), ((0,), (0,)))
_TN = (((1,), (1,)), ((0,), (0,)))


def _bf16_parts(a, n):
    parts, rem = [], a
    for i in range(n):
        p = rem.astype(BF16)
        parts.append(p)
        if i + 1 < n:
            rem = rem - p.astype(F32)
    return parts


def _dot(a, b, dims, passes):
    if passes == 1:
        return lax.dot_general(a.astype(BF16), b.astype(BF16), dims, preferred_element_type=F32)
    a_hi, a_lo = _bf16_parts(a, 2)
    b_hi, b_lo = _bf16_parts(b, 2)
    out = lax.dot_general(a_hi, b_hi, dims, preferred_element_type=F32)
    out = out + lax.dot_general(a_hi, b_lo, dims, preferred_element_type=F32)
    return out + lax.dot_general(a_lo, b_hi, dims, preferred_element_type=F32)


def _unit_lower_inverse(l_mat, same_sub, passes):
    mm = lambda a, b: _dot(a, b, _NN, passes)
    ld = jnp.where(same_sub, l_mat, 0.0)
    lo = l_mat - ld
    p2 = mm(ld, ld)
    d = p2 - ld - mm(ld, p2)
    pw = p2
    span = 4
    while span < RWKV_SUB:
        pw = mm(pw, pw)
        d = d + pw + mm(d, pw)
        span *= 2
    n1 = lo + mm(d, lo)
    n2 = mm(n1, n1)
    x = n2 - n1 - mm(n1, n2)
    span = 4
    pw = n2
    while span < RWKV_CHUNK // RWKV_SUB:
        pw = mm(pw, pw)
        x = x + pw + mm(x, pw)
        span *= 2
    return x + d + mm(x, d)


def _rwkv_scan_kernel(r_ref, lw_ref, k_ref, v_ref, kk_ref, b_ref, s0_ref, y_ref, sfin_ref, s_scr,
                      *, reverse, n_chunks):
    cl = RWKV_CHUNK
    hd = r_ref.shape[-1]

    @pl.when(pl.program_id(1) == 0)
    def _():
        s_scr[...] = s0_ref[0, 0]

    shp = (n_chunks, cl, hd)
    r = r_ref[0, 0].reshape(shp)
    lw = lw_ref[0, 0].reshape(shp)
    k = k_ref[0, 0].reshape(shp)
    v = v_ref[0, 0].reshape(shp)
    kk = kk_ref[0, 0].reshape(shp)
    b = b_ref[0, 0].reshape(shp)

    row = lax.broadcasted_iota(jnp.int32, (cl, cl), 0)
    col = lax.broadcasted_iota(jnp.int32, (cl, cl), 1)
    before = (col > row) if reverse else (col < row)
    upto = before | (col == row)
    same_sub = (row // RWKV_SUB) == (col // RWKV_SUB)

    tri = jnp.broadcast_to(jnp.where(upto, 1.0, 0.0).astype(BF16)[None], (n_chunks, cl, cl))
    cum = sum(lax.dot_general(tri, p, _NN, preferred_element_type=F32) for p in _bf16_parts(lw, 3))
    last = 0 if reverse else cl - 1
    mid = cl // 2 if reverse else cl // 2 - 1
    tot = cum[:, last:last + 1, :]
    rho = cum[:, mid:mid + 1, :]
    cum_ex = cum - lw

    a_t = kk * jnp.exp(cum_ex - rho)
    r_t = r * jnp.exp(cum - rho)
    e_out = jnp.exp(rho - cum)
    b_t = b * e_out
    k_t = k * e_out
    a_0 = kk * jnp.exp(cum_ex)
    r_0 = r * jnp.exp(cum)
    e_end = jnp.exp(tot - cum)
    b_e = b * e_end
    k_e = k * e_end

    ps, pt = 3, 3
    l_mat = jnp.where(before, _dot(a_t, b_t, _NT, ps), 0.0)
    m_ak = jnp.where(before, _dot(a_t, k_t, _NT, ps), 0.0)
    m_rb = jnp.where(upto, _dot(r_t, b_t, _NT, ps), 0.0)
    m_rk = jnp.where(upto, _dot(r_t, k_t, _NT, ps), 0.0)
    t_m1 = _unit_lower_inverse(l_mat, same_sub, pt)

    mv = _dot(m_ak, v, _NN, ps)
    q = -(a_0 + _dot(t_m1, a_0, _NN, ps))
    w = -(mv + _dot(t_m1, mv, _NN, ps))
    r_h = r_0 + _dot(m_rb, q, _NN, ps)
    y_loc = _dot(m_rb, w, _NN, ps) + _dot(m_rk, v, _NN, ps)
    eye = lax.broadcasted_iota(jnp.int32, (hd, hd), 0) == lax.broadcasted_iota(jnp.int32, (hd, hd), 1)
    phi = jnp.where(eye, jnp.exp(tot), 0.0) + _dot(b_e, q, _TN, ps)
    psi = _dot(b_e, w, _TN, ps) + _dot(k_e, v, _TN, ps)

    s = s_scr[...]
    starts = [None] * n_chunks
    order = range(n_chunks - 1, -1, -1) if reverse else range(n_chunks)
    for n in order:
        starts[n] = s
        s = _dot(phi[n], s, (((1,), (0,)), ((), ())), 3) + psi[n]
    s_scr[...] = s
    sfin_ref[0, 0] = s
    y = _dot(r_h, jnp.stack(starts), _NN, ps) + y_loc
    y_ref[0, 0] = y.reshape(n_chunks * cl, hd)


def rwkv_scan_blocked(r, lw, k, v, kk, b_, s0t, reverse):
    bsz, nh, t, hd = r.shape
    tb = min(RWKV_TOKENS_PER_STEP, t)
    nt = t // tb
    assert t == nt * tb and tb % RWKV_CHUNK == 0
    if reverse:
        tmap = lambda bh, j: (bh // nh, bh % nh, nt - 1 - j, 0)
    else:
        tmap = lambda bh, j: (bh // nh, bh % nh, j, 0)
    smap = lambda bh, j: (bh // nh, bh % nh, 0, 0)
    seq_spec = pl.BlockSpec((1, 1, tb, hd), tmap)
    st_spec = pl.BlockSpec((1, 1, hd, hd), smap)
    y, sfin = pl.pallas_call(
        functools.partial(_rwkv_scan_kernel, reverse=reverse, n_chunks=tb // RWKV_CHUNK),
        grid=(bsz * nh, nt),
        in_specs=[seq_spec] * 6 + [st_spec],
        out_specs=[seq_spec, st_spec],
        out_shape=[jax.ShapeDtypeStruct((bsz, nh, t, hd), F32),
                   jax.ShapeDtypeStruct((bsz, nh, hd, hd), F32)],
        scratch_shapes=[pltpu.VMEM((hd, hd), F32)],
        compiler_params=pltpu.CompilerParams(dimension_semantics=("parallel", "arbitrary")),
        name="rwkv_scan_rev" if reverse else "rwkv_scan_fwd",
    )(r, lw, k, v, kk, b_, s0t)
    return y, sfin


def rwkv_mix(feat, lp, s0f, s0b):
    r, v, g, kk, dirs = feat
    (lwf, kf, bf), (lwb, kb, bb) = dirs
    tr = lambda z: jnp.swapaxes(z.astype(F32), 1, 2)
    rt, vt, kkt = tr(r), tr(v), tr(kk)
    yf, sf = rwkv_scan_blocked(rt, tr(lwf), tr(kf), vt, kkt, tr(bf), s0f, False)
    yb, sb = rwkv_scan_blocked(rt, tr(lwb), tr(kb), vt, kkt, tr(bb), s0b, True)
    yf = jnp.swapaxes(yf, 1, 2)
    yb = jnp.swapaxes(yb, 1, 2)
    y = head_norm(yf + yb, RWKV_GN_EPS) * heads(lp['rwkv_lnx_w'], RWKV_HEADS) + heads(lp['rwkv_lnx_b'], RWKV_HEADS)
    bonus = jnp.sum(r * (kf + kb) * lp['rwkv_r_k'], axis=-1, keepdims=True) * v
    out = (y + bonus).reshape(g.shape) * g
    return out.astype(g.dtype), sf, sb


def merge_branches(u, ys, lp):
    acc = jnp.zeros_like(u)
    for n in range(N_BRANCH):
        gate = jax.nn.sigmoid(u @ lp['w_gate'][n] + lp['b_gate'][n])
        acc = acc + gate * (ys[n].astype(u.dtype) @ lp['w_branch'][n])
    return acc @ lp['w_out']


def token_mixers(u_ctx, u_lat, lp, need_ctx):
    b, s, _ = u_lat.shape
    l = u_ctx.shape[1]
    pa_l, pb_l, pc_l, pd_l = split_sizes(u_lat @ lp['w_in'], GROUP_SIZES)
    pa_c, pb_c, pc_c, pd_c = split_sizes(u_ctx @ lp['w_in'], GROUP_SIZES)
    tab_att = axial_tables(s, HEAD_DIM)
    tab_mla = axial_tables(s, MLA_ROPE)

    aq_l, ak_l, av_l = split_sizes(pa_l, A_SIZES)
    aq_c, ak_c, av_c = split_sizes(pa_c, A_SIZES)
    k_c = heads(ak_c, ATT_KV_HEADS)
    v_c = heads(av_c, ATT_KV_HEADS)
    ya_l = window_gqa(axial_rope(heads(aq_l, ATT_Q_HEADS), tab_att),
                      axial_rope(heads(ak_l, ATT_KV_HEADS), tab_att),
                      heads(av_l, ATT_KV_HEADS), k_c, v_c, lp['sink'])

    lg_f, lg_b = retention_log_gammas()
    cos_c, sin_c = rope_tables(jnp.arange(l, dtype=F32), HEAD_DIM)
    cos_l, sin_l = rope_tables(l + jnp.arange(s, dtype=F32), HEAD_DIM)
    zero_ret = jnp.zeros((b, RET_HEADS, HEAD_DIM, HEAD_DIM), F32)
    yb_c, sbf, sbb = retention_mix(pb_c, cos_c, sin_c, lg_f, lg_b, zero_ret, zero_ret)
    yb_l, _, _ = retention_mix(pb_l, cos_l, sin_l, lg_f, lg_b, sbf, sbb)

    qn_l, qr_l, kn_l, kr_l, vm_l = mla_project(pc_l, lp)
    qn_c, qr_c, kn_c, kr_c, vm_c = mla_project(pc_c, lp)
    qr_l = axial_rope(qr_l, tab_mla)
    kr_l = axial_rope(kr_l, tab_mla)
    yc_l = mla_attend(qn_l, qr_l, jnp.concatenate([kn_c, kn_l], axis=1),
                      jnp.concatenate([kr_c, kr_l], axis=1), jnp.concatenate([vm_c, vm_l], axis=1))

    zero_wkv = jnp.zeros((b, RWKV_HEADS, HEAD_DIM, HEAD_DIM), F32)
    yd_c, sdf, sdb = rwkv_mix(rwkv_features(pd_c, lp), lp, zero_wkv, zero_wkv)
    yd_l, _, _ = rwkv_mix(rwkv_features(pd_l, lp), lp, sdf, sdb)

    m_lat = merge_branches(u_lat, (ya_l, yb_l, yc_l, yd_l), lp)
    if not need_ctx:
        return None, m_lat
    ya_c = ctx_gqa(heads(aq_c, ATT_Q_HEADS), k_c, v_c, lp['sink'])
    yc_c = mla_attend(qn_c, qr_c, kn_c, kr_c, vm_c)
    m_ctx = merge_branches(u_ctx, (ya_c, yb_c, yc_c, yd_c), lp)
    return m_ctx, m_lat


def moe_ffn(x, router_w, router_b, w_e_gate, w_e_up, w_e_down):
    n_tok, dm = x.shape
    scores = jax.nn.sigmoid(x.astype(F32) @ router_w.astype(F32))
    grp = (scores + router_b.astype(F32)).reshape(n_tok, N_GROUPS, EXPERTS_PER_GROUP)
    g_sel = jnp.argmax(jnp.sum(lax.top_k(grp, TOP_K)[0], axis=-1), axis=-1)
    in_grp = lax.top_k(jnp.take_along_axis(grp, g_sel[:, None, None], axis=1)[:, 0], TOP_K)[1]
    e_idx = g_sel[:, None] * EXPERTS_PER_GROUP + in_grp
    w_sel = jnp.take_along_axis(scores, e_idx, axis=1)
    w_sel = w_sel / jnp.sum(w_sel, axis=-1, keepdims=True)
    n_asg = n_tok * TOP_K
    flat_e = e_idx.reshape(-1)
    flat_tok = jnp.repeat(jnp.arange(n_tok, dtype=jnp.int32), TOP_K)
    flat_w = w_sel.reshape(-1)
    order = jnp.argsort(flat_e)
    se = flat_e[order]
    counts = jnp.bincount(flat_e, length=N_EXPERTS)
    padded = (counts + MOE_BLOCK - 1) // MOE_BLOCK * MOE_BLOCK
    pad_end = jnp.cumsum(padded)
    pad_start = pad_end - padded
    start = jnp.cumsum(counts) - counts
    dest = pad_start[se] + jnp.arange(n_asg) - start[se]
    n_blocks = -(-n_asg // MOE_BLOCK) + N_EXPERTS
    n_rows = n_blocks * MOE_BLOCK
    row_tok = jnp.zeros((n_rows,), jnp.int32).at[dest].set(flat_tok[order])
    row_w = jnp.zeros((n_rows,), F32).at[dest].set(flat_w[order])
    blk_e = jnp.minimum(jnp.searchsorted(pad_end, jnp.arange(n_blocks) * MOE_BLOCK, side='right'), N_EXPERTS - 1)
    xb = x.astype(BF16)[row_tok]
    yb = moe_expert_blocks(blk_e.astype(jnp.int32), xb, row_w[:, None], w_e_gate, w_e_up, w_e_down)
    slot = jnp.zeros((n_asg,), jnp.int32).at[order].set(dest.astype(jnp.int32)).reshape(n_tok, TOP_K)
    out = yb[slot[:, 0]]
    for j in range(1, TOP_K):
        out = out + yb[slot[:, j]]
    return out.astype(x.dtype)


MOE_VMEM_LIMIT = 40 * 1024 * 1024


def _moe_expert_kernel(blk_e_ref, x_ref, w_ref, wg_ref, wu_ref, wd_ref, o_ref):
    del blk_e_ref
    x = x_ref[...]
    g = jnp.dot(x, wg_ref[0].astype(BF16), preferred_element_type=F32)
    u = jnp.dot(x, wu_ref[0].astype(BF16), preferred_element_type=F32)
    hid = (g * jax.nn.sigmoid(g)) * u
    y = jnp.dot(hid.astype(BF16), wd_ref[0].astype(BF16), preferred_element_type=F32)
    o_ref[...] = y * w_ref[...]


def moe_expert_blocks(blk_e, xb, row_w, w_e_gate, w_e_up, w_e_down):
    n_rows, dm = xb.shape
    n_blocks = n_rows // MOE_BLOCK
    ff = w_e_gate.shape[-1]
    row_spec = lambda width: pl.BlockSpec((MOE_BLOCK, width), lambda i, be: (i, 0))
    return pl.pallas_call(
        _moe_expert_kernel,
        grid_spec=pltpu.PrefetchScalarGridSpec(
            num_scalar_prefetch=1,
            grid=(n_blocks,),
            in_specs=[row_spec(dm), row_spec(1),
                      pl.BlockSpec((1, dm, ff), lambda i, be: (be[i], 0, 0)),
                      pl.BlockSpec((1, dm, ff), lambda i, be: (be[i], 0, 0)),
                      pl.BlockSpec((1, ff, dm), lambda i, be: (be[i], 0, 0))],
            out_specs=row_spec(dm)),
        out_shape=jax.ShapeDtypeStruct((n_rows, dm), F32),
        compiler_params=pltpu.CompilerParams(dimension_semantics=("arbitrary",),
                                             vmem_limit_bytes=MOE_VMEM_LIMIT),
        name="moe_experts",
    )(blk_e, xb, row_w, w_e_gate, w_e_up, w_e_down)


def layer(h_ctx, h_lat, c, c_ctx, lp, router_w, router_b, need_ctx):
    b, s, d = h_lat.shape
    l = h_ctx.shape[1]
    m_lat = jnp.split((jax.nn.silu(c) @ lp['w_ada'] + lp['b_ada'])[:, None, :], 6, axis=-1)
    m_ctx = jnp.split(jax.nn.silu(c_ctx) @ lp['w_ada'] + lp['b_ada'], 6, axis=-1)
    u_lat = adaln(h_lat, lp['g_norm1'], m_lat[0], m_lat[1])
    u_ctx = adaln(h_ctx, lp['g_norm1'], m_ctx[0], m_ctx[1])
    mix_ctx, mix_lat = token_mixers(u_ctx, u_lat, lp, need_ctx)
    h_lat = h_lat + m_lat[2] * mix_lat
    f_lat = adaln(h_lat, lp['g_norm2'], m_lat[3], m_lat[4]).reshape(b * s, d)
    if not need_ctx:
        ffn = moe_ffn(f_lat, router_w, router_b, lp['w_e_gate'], lp['w_e_up'], lp['w_e_down'])
        return h_ctx, h_lat + m_lat[5] * ffn.reshape(b, s, d)
    h_ctx = h_ctx + m_ctx[2] * mix_ctx
    f_ctx = adaln(h_ctx, lp['g_norm2'], m_ctx[3], m_ctx[4]).reshape(b * l, d)
    ffn = moe_ffn(jnp.concatenate([f_ctx, f_lat], axis=0), router_w, router_b,
                  lp['w_e_gate'], lp['w_e_up'], lp['w_e_down'])
    h_ctx = h_ctx + m_ctx[5] * ffn[:b * l].reshape(b, l, d)
    h_lat = h_lat + m_lat[5] * ffn[b * l:].reshape(b, s, d)
    return h_ctx, h_lat


def _final_norm_kernel(x_ref, g_ref, o_ref):
    x = x_ref[...]
    y = x * lax.rsqrt(jnp.mean(x * x, axis=-1, keepdims=True) + NORM_EPS)
    o_ref[...] = y * g_ref[...]


def final_rms_norm(x, g):
    b, s, d = x.shape
    rows = b * s
    tile = 1024
    out = pl.pallas_call(
        _final_norm_kernel,
        grid=(rows // tile,),
        in_specs=[pl.BlockSpec((tile, d), lambda i: (i, 0)), pl.BlockSpec((1, d), lambda i: (0, 0))],
        out_specs=pl.BlockSpec((tile, d), lambda i: (i, 0)),
        out_shape=jax.ShapeDtypeStruct((rows, d), x.dtype),
        name="final_rms_norm",
    )(x.reshape(rows, d), g.reshape(1, d))
    return out.reshape(b, s, d)


_LAYER_PARAM_NAMES = (
    'w_ada', 'b_ada', 'g_norm1', 'g_norm2', 'w_in', 'sink', 'g_qnorm', 'g_kvnorm', 'w_uq',
    'w_ukv', 'rwkv_mu', 'rwkv_w0', 'rwkv_w_up', 'rwkv_a0', 'rwkv_a_up', 'rwkv_g_up',
    'rwkv_k_k', 'rwkv_k_a', 'rwkv_r_k', 'rwkv_lnx_w', 'rwkv_lnx_b', 'w_gate', 'b_gate',
    'w_branch', 'w_out', 'w_e_gate', 'w_e_up', 'w_e_down')


def kernel(x, c, ctx, c_ctx, w_ada, b_ada, g_norm1, g_norm2, w_in, sink, g_qnorm, g_kvnorm,
           w_uq, w_ukv, rwkv_mu, rwkv_w0, rwkv_w_up, rwkv_a0, rwkv_a_up, rwkv_g_up, rwkv_k_k,
           rwkv_k_a, rwkv_r_k, rwkv_lnx_w, rwkv_lnx_b, w_gate, b_gate, w_branch, w_out,
           router_w, router_b, w_e_gate, w_e_up, w_e_down, g_final):
    stacked = (w_ada, b_ada, g_norm1, g_norm2, w_in, sink, g_qnorm, g_kvnorm, w_uq, w_ukv, rwkv_mu,
               rwkv_w0, rwkv_w_up, rwkv_a0, rwkv_a_up, rwkv_g_up, rwkv_k_k, rwkv_k_a, rwkv_r_k,
               rwkv_lnx_w, rwkv_lnx_b, w_gate, b_gate, w_branch, w_out, w_e_gate, w_e_up, w_e_down)
    h_ctx, h_lat = ctx, x
    for i in range(DEPTH):
        lp = {n: a[i] for n, a in zip(_LAYER_PARAM_NAMES, stacked)}
        h_ctx, h_lat = layer(h_ctx, h_lat, c, c_ctx, lp, router_w, router_b, i < DEPTH - 1)
    return final_rms_norm(h_lat, g_final)
```

```python
import functools
import math

import jax
import jax.numpy as jnp
from jax import lax
from jax.experimental import pallas as pl
from jax.experimental.pallas import tpu as pltpu

D_MODEL = 1024
BATCH = 4
SEQ = 8192
DEPTH = 2

GRID_W = 64
CTX_LEN = 256
N_BRANCH = 4
BRANCH_W = D_MODEL // N_BRANCH
HEAD_DIM = 64
BRANCH_HEADS = BRANCH_W // HEAD_DIM
BLOCK = 128
ROPE_BASE = 10000.0
NORM_EPS = 1e-6
NEG_INF = -1e30
ATT_Q_HEADS = BRANCH_HEADS
ATT_KV_HEADS = BRANCH_HEADS // 2
WINDOW = 128
RET_HEADS = BRANCH_HEADS
RET_CHUNK = 128
RET_GN_EPS = 1e-5
MLA_HEADS = BRANCH_HEADS
MLA_Q_RANK = 256
MLA_KV_RANK = 128
MLA_NOPE = HEAD_DIM
MLA_ROPE = HEAD_DIM // 2
MLA_V = HEAD_DIM
RWKV_HEADS = BRANCH_HEADS
RWKV_DECAY_RANK = 64
RWKV_A_RANK = 64
RWKV_GATE_RANK = 128
RWKV_GN_EPS = 64e-5
N_EXPERTS = 64
N_GROUPS = 8
EXPERTS_PER_GROUP = N_EXPERTS // N_GROUPS
TOP_K = 2
EXPERT_FF = 512
MOE_BLOCK = 256
A_SIZES = (ATT_Q_HEADS * HEAD_DIM, ATT_KV_HEADS * HEAD_DIM, ATT_KV_HEADS * HEAD_DIM)
RET_SIZES = (BRANCH_W,) * 5
MLA_SIZES = (MLA_Q_RANK, MLA_KV_RANK, MLA_ROPE)
RWKV_SIZES = (BRANCH_W, BRANCH_W, BRANCH_W, RWKV_DECAY_RANK, RWKV_DECAY_RANK,
              RWKV_A_RANK, RWKV_A_RANK, RWKV_GATE_RANK)
GROUP_SIZES = (sum(A_SIZES), sum(RET_SIZES), sum(MLA_SIZES), sum(RWKV_SIZES))
N_IN = sum(GROUP_SIZES)
F32 = jnp.float32


def split_sizes(x, sizes):
    out, o = [], 0
    for n in sizes:
        out.append(x[..., o:o + n])
        o += n
    return out


def heads(x, h):
    return x.reshape(x.shape[:-1] + (h, x.shape[-1] // h))


def rms_norm(x, g):
    xf = x.astype(F32)
    y = xf * lax.rsqrt(jnp.mean(xf * xf, axis=-1, keepdims=True) + NORM_EPS)
    return (y * g.astype(F32)).astype(x.dtype)


def head_norm(x, eps):
    xf = x.astype(F32)
    xc = xf - jnp.mean(xf, axis=-1, keepdims=True)
    return xc * lax.rsqrt(jnp.mean(xc * xc, axis=-1, keepdims=True) + eps)


def adaln(h, g, shift, scale):
    return rms_norm(h, g) * (1 + scale) + shift


def rope_tables(pos, dim):
    inv = ROPE_BASE ** (-jnp.arange(0, dim, 2, dtype=F32) / dim)
    ang = pos[:, None] * inv[None, :]
    return jnp.cos(ang), jnp.sin(ang)


def rope_rotate(x, cos, sin):
    m = x.shape[-1] // 2
    x1, x2 = x[..., :m], x[..., m:]
    c = cos[None, :, None, :]
    s = sin[None, :, None, :]
    return jnp.concatenate([x1 * c - x2 * s, x2 * c + x1 * s], axis=-1).astype(x.dtype)


def axial_tables(seq, dim):
    rows = seq // GRID_W
    row = jnp.repeat(jnp.arange(rows, dtype=F32), GRID_W)
    col = jnp.broadcast_to(jnp.arange(GRID_W, dtype=F32)[None, :], (rows, GRID_W)).reshape(-1)
    return rope_tables(row, dim // 2), rope_tables(col, dim // 2)


def axial_rope(x, tabs):
    (cos_r, sin_r), (cos_c, sin_c) = tabs
    half = x.shape[-1] // 2
    return jnp.concatenate([rope_rotate(x[..., :half], cos_r, sin_r),
                            rope_rotate(x[..., half:], cos_c, sin_c)], axis=-1)


def window_gqa(q, k, v, k_ctx, v_ctx, sink):
    b, s, hq, d = q.shape
    g = hq // ATT_KV_HEADS
    nb = s // BLOCK
    nw = 3 * BLOCK
    nc = k_ctx.shape[1]
    qb = (q * d ** -0.5).reshape(b, nb, BLOCK, ATT_KV_HEADS, g, d)
    pad = ((0, 0), (BLOCK, BLOCK), (0, 0), (0, 0))
    kp = jnp.pad(k, pad).reshape(b, nb + 2, BLOCK, ATT_KV_HEADS, d)
    vp = jnp.pad(v, pad).reshape(b, nb + 2, BLOCK, ATT_KV_HEADS, d)
    kw = jnp.concatenate([kp[:, :-2], kp[:, 1:-1], kp[:, 2:]], axis=2)
    vw = jnp.concatenate([vp[:, :-2], vp[:, 1:-1], vp[:, 2:]], axis=2)
    blk0 = jnp.arange(nb)[:, None, None] * BLOCK
    q_pos = blk0 + jnp.arange(BLOCK)[None, :, None]
    k_pos = blk0 - BLOCK + jnp.arange(nw)[None, None, :]
    valid = (jnp.abs(k_pos - q_pos) <= WINDOW) & (k_pos >= 0) & (k_pos < s)
    s_win = jnp.einsum('bnqhgd,bnkhd->bnhgqk', qb, kw).astype(F32)
    s_win = jnp.where(valid[None, :, None, None], s_win, NEG_INF)
    s_ctx = jnp.einsum('bnqhgd,bchd->bnhgqc', qb, k_ctx).astype(F32)
    s_sink = jnp.broadcast_to(sink.astype(F32).reshape(ATT_KV_HEADS, g, 1, 1), s_win.shape[:-1] + (1,))
    p = jax.nn.softmax(jnp.concatenate([s_win, s_ctx, s_sink], axis=-1), axis=-1).astype(v.dtype)
    o = (jnp.einsum('bnhgqk,bnkhd->bnqhgd', p[..., :nw], vw)
         + jnp.einsum('bnhgqc,bchd->bnqhgd', p[..., nw:nw + nc], v_ctx))
    return o.reshape(b, s, hq * d)


def ctx_gqa(q, k, v, sink):
    b, l, hq, d = q.shape
    g = hq // ATT_KV_HEADS
    qg = (q * d ** -0.5).reshape(b, l, ATT_KV_HEADS, g, d)
    sc = jnp.einsum('bqhgd,bkhd->bhgqk', qg, k).astype(F32)
    s_sink = jnp.broadcast_to(sink.astype(F32).reshape(ATT_KV_HEADS, g, 1, 1), sc.shape[:-1] + (1,))
    p = jax.nn.softmax(jnp.concatenate([sc, s_sink], axis=-1), axis=-1).astype(v.dtype)
    o = jnp.einsum('bhgqk,bkhd->bqhgd', p[..., :l], v)
    return o.reshape(b, l, hq * d)


def retention_log_gammas():
    lg = jnp.log(1.0 - jnp.exp(jnp.linspace(math.log(1.0 / 32), math.log(1.0 / 512), 2 * RET_HEADS, dtype=F32)))
    return lg[0::2], lg[1::2]


def retention_chunked(q, k, v, log_gamma, s0):
    b, t, h, d = q.shape
    n = t // RET_CHUNK
    cl = RET_CHUNK
    qc = q.reshape(b, n, cl, h, d).astype(F32)
    kc = k.reshape(b, n, cl, h, d).astype(F32)
    vc = v.reshape(b, n, cl, h, d).astype(F32)
    idx = jnp.arange(cl, dtype=F32)
    rel = idx[:, None] - idx[None, :]
    d_intra = jnp.where(rel >= 0, jnp.exp(jnp.maximum(rel, 0.0)[None] * log_gamma[:, None, None]), 0.0)
    a_int = jnp.einsum('bnihd,bnjhd->bnhij', qc, kc) * d_intra
    o = jnp.einsum('bnhij,bnjhe->bnihe', a_int, vc)
    k_dec = kc * jnp.exp((cl - 1 - idx)[:, None] * log_gamma[None, :])[..., None]
    kv = jnp.einsum('bnjhd,bnjhe->nbhde', k_dec, vc)
    g_chunk = jnp.exp(cl * log_gamma)[:, None, None]

    def step(state, kv_n):
        return state * g_chunk + kv_n, state

    s_fin, s_prev = lax.scan(step, s0, kv)
    q_dec = qc * jnp.exp((idx + 1)[:, None] * log_gamma[None, :])[..., None]
    o = o + jnp.einsum('bnihd,nbhde->bnihe', q_dec, s_prev)
    return o.reshape(b, t, h, d), s_fin


def retention_mix(pb, cos, sin, lg_f, lg_b, s0f, s0b):
    bsz, t = pb.shape[:2]
    q, k, v, gf, gb = split_sizes(pb, RET_SIZES)
    q = rope_rotate(heads(q, RET_HEADS), cos, sin)
    k = rope_rotate(heads(k, RET_HEADS), cos, sin) * HEAD_DIM ** -0.5
    v = heads(v, RET_HEADS)
    yf, sf = retention_chunked(q, k, v, lg_f, s0f)
    yb, sb = retention_chunked(q[:, ::-1], k[:, ::-1], v[:, ::-1], lg_b, s0b)
    out = (jax.nn.silu(gf) * head_norm(yf, RET_GN_EPS).reshape(bsz, t, -1)
           + jax.nn.silu(gb) * head_norm(yb[:, ::-1], RET_GN_EPS).reshape(bsz, t, -1))
    return out.astype(pb.dtype), sf, sb


def mla_project(pc, lp):
    cq, ckv, kr = split_sizes(pc, MLA_SIZES)
    q = heads(rms_norm(cq, lp['g_qnorm']) @ lp['w_uq'], MLA_HEADS)
    kv = heads(rms_norm(ckv, lp['g_kvnorm']) @ lp['w_ukv'], MLA_HEADS)
    return q[..., :MLA_NOPE], q[..., MLA_NOPE:], kv[..., :MLA_NOPE], kr[:, :, None, :], kv[..., MLA_NOPE:]


LANES = 128
MLA_Q_TILE = 1024
MLA_KEY_UNIT = 256
MLA_KEY_TILE_MAX = 768
MLA_VMEM_LIMIT = 40 * 1024 * 1024


def _mla_flash_kernel(q_ref, k_ref, v_ref, o_ref, m_scr, acc_scr, *, scale, tk, n_tiles):
    dv = o_ref.shape[-1]
    q = (q_ref[0, 0] * scale).astype(BF16)
    m_scr[...] = jnp.full_like(m_scr, NEG_INF)
    acc_scr[...] = jnp.zeros_like(acc_scr)

    def sweep(c, carry):
        keys = pl.ds(pl.multiple_of(c * tk, tk), tk)
        s = lax.dot_general(q, k_ref[0, 0, keys, :], (((1,), (1,)), ((), ())), preferred_element_type=F32)
        m_prev = m_scr[...]
        m_new = jnp.maximum(m_prev, jnp.max(s, axis=-1, keepdims=True))
        p = jnp.exp2(s - m_new).astype(BF16)
        acc_scr[...] = (jnp.exp2(m_prev - m_new) * acc_scr[...]
                        + jnp.dot(p, v_ref[0, 0, keys, :], preferred_element_type=F32))
        m_scr[...] = m_new
        return carry

    lax.fori_loop(0, n_tiles, sweep, 0)
    acc = acc_scr[...]
    o_ref[0, 0] = acc[:, :dv] / acc[:, dv:dv + 1]


def mla_attend(qn, qr, kn, kr, v):
    b, t, h, _ = qn.shape
    n = kn.shape[1]
    dv = v.shape[-1]
    scale = (MLA_NOPE + MLA_ROPE) ** -0.5 * math.log2(math.e)
    q = jnp.swapaxes(jnp.concatenate([qn, qr], axis=-1), 1, 2)
    kr_h = jnp.broadcast_to(kr, (b, n, h, MLA_ROPE))
    k = jnp.swapaxes(jnp.concatenate([kn, kr_h], axis=-1), 1, 2).astype(BF16)
    ones = jnp.ones((b, n, h, 1), v.dtype)
    zeros = jnp.zeros((b, n, h, LANES - dv - 1), v.dtype)
    vt = jnp.swapaxes(jnp.concatenate([v, ones, zeros], axis=-1), 1, 2).astype(BF16)
    dq = q.shape[-1]
    tq = min(MLA_Q_TILE, t)
    tk = max(d for d in range(MLA_KEY_UNIT, MLA_KEY_TILE_MAX + 1, MLA_KEY_UNIT) if n % d == 0)
    assert t % tq == 0
    o = pl.pallas_call(
        functools.partial(_mla_flash_kernel, scale=scale, tk=tk, n_tiles=n // tk),
        grid=(b, h, t // tq),
        in_specs=[pl.BlockSpec((1, 1, tq, dq), lambda bi, hi, qi: (bi, hi, qi, 0)),
                  pl.BlockSpec((1, 1, n, dq), lambda bi, hi, qi: (bi, hi, 0, 0)),
                  pl.BlockSpec((1, 1, n, LANES), lambda bi, hi, qi: (bi, hi, 0, 0))],
        out_specs=pl.BlockSpec((1, 1, tq, dv), lambda bi, hi, qi: (bi, hi, qi, 0)),
        out_shape=jax.ShapeDtypeStruct((b, h, t, dv), F32),
        scratch_shapes=[pltpu.VMEM((tq, 1), F32), pltpu.VMEM((tq, LANES), F32)],
        compiler_params=pltpu.CompilerParams(
            dimension_semantics=("parallel", "parallel", "arbitrary"),
            vmem_limit_bytes=MLA_VMEM_LIMIT),
        name="mla_flash",
    )(q, k, vt)
    return jnp.swapaxes(o, 1, 2).reshape(b, t, h * dv)


def centred_shift(p):
    prev = jnp.pad(p, ((0, 0), (1, 0), (0, 0)))[:, :-1]
    nxt = jnp.pad(p, ((0, 0), (0, 1), (0, 0)))[:, 1:]
    return 0.5 * (prev + nxt)


def rwkv_features(pd, lp):
    z = pd + (centred_shift(pd) - pd) * lp['rwkv_mu']
    zr, zk, zv, zwf, zwb, zaf, zab, zg = split_sizes(z, RWKV_SIZES)
    r = heads(zr, RWKV_HEADS)
    k = heads(zk, RWKV_HEADS)
    v = heads(zv, RWKV_HEADS)
    g = jax.nn.sigmoid(zg) @ lp['rwkv_g_up']
    kk = (k * heads(lp['rwkv_k_k'], RWKV_HEADS)).astype(F32)
    kk = kk / jnp.maximum(jnp.sqrt(jnp.sum(kk * kk, axis=-1, keepdims=True)), 1e-12)
    k_a = heads(lp['rwkv_k_a'], RWKV_HEADS)
    dirs = []
    for di, (zw, za) in enumerate(((zwf, zaf), (zwb, zab))):
        logw = -jax.nn.softplus(-(lp['rwkv_w0'][di] + jnp.tanh(zw) @ lp['rwkv_w_up'][di]).astype(F32)) - 0.5
        w = heads(-jnp.exp(logw), RWKV_HEADS)
        a = heads(jax.nn.sigmoid((lp['rwkv_a0'][di] + za @ lp['rwkv_a_up'][di]).astype(F32)), RWKV_HEADS)
        dirs.append((w, k * (1 + (a - 1) * k_a), kk * a))
    return r, v, g, kk, dirs


def rwkv_scan(r, w, k, v, kk, b_, s0, reverse):
    xs = tuple(jnp.moveaxis(z.astype(F32), 1, 0) for z in (r, w, k, v, kk, b_))

    def step(st, inp):
        r_t, w_t, k_t, v_t, kk_t, b_t = inp
        sa = jnp.einsum('bhvk,bhk->bhv', st, -kk_t)
        st = st * w_t[:, :, None, :] + sa[..., None] * b_t[:, :, None, :] + v_t[..., None] * k_t[:, :, None, :]
        return st, jnp.einsum('bhvk,bhk->bhv', st, r_t)

    s_fin, y = lax.scan(step, s0, xs, reverse=reverse)
    return jnp.moveaxis(y, 0, 1), s_fin


BF16 = jnp.bfloat16
RWKV_CHUNK = 64
RWKV_SUB = 16
RWKV_TOKENS_PER_STEP = 512
RWKV_PASSES = (1, 1)
RWKV_VMEM_LIMIT = 48 * 1024 * 1024
_NN = (((2,), (1,)), ((0,), (0,)))
_NT = (((2,), (2,)), ((0,), (0,)))
_TN = (((1,), (1,)), ((0,), (0,)))


def _bf16_parts(a, n):
    parts, rem = [], a
    for i in range(n):
        p = rem.astype(BF16)
        parts.append(p)
        if i + 1 < n:
            rem = rem - p.astype(F32)
    return parts


def _dot(a, b, dims, passes):
    if passes == 1:
        return lax.dot_general(a.astype(BF16), b.astype(BF16), dims, preferred_element_type=F32)
    a_hi, a_lo = _bf16_parts(a, 2)
    b_hi, b_lo = _bf16_parts(b, 2)
    out = lax.dot_general(a_hi, b_hi, dims, preferred_element_type=F32)
    out = out + lax.dot_general(a_hi, b_lo, dims, preferred_element_type=F32)
    return out + lax.dot_general(a_lo, b_hi, dims, preferred_element_type=F32)


def _unit_lower_inverse(l_mat, same_sub, passes):
    mm = lambda a, b: _dot(a, b, _NN, passes)
    ld = jnp.where(same_sub, l_mat, 0.0)
    lo = l_mat - ld
    p2 = mm(ld, ld)
    d = p2 - ld - mm(ld, p2)
    pw = p2
    span = 4
    while span < RWKV_SUB:
        pw = mm(pw, pw)
        d = d + pw + mm(d, pw)
        span *= 2
    n1 = lo + mm(d, lo)
    n2 = mm(n1, n1)
    x = n2 - n1 - mm(n1, n2)
    span = 4
    pw = n2
    while span < RWKV_CHUNK // RWKV_SUB:
        pw = mm(pw, pw)
        x = x + pw + mm(x, pw)
        span *= 2
    return x + d + mm(x, d)


def _rwkv_scan_kernel(r_ref, lw_ref, k_ref, v_ref, kk_ref, b_ref, s0_ref, y_ref, sfin_ref, s_scr,
                      *, reverse, n_chunks):
    cl = RWKV_CHUNK
    nh, tb, hd = r_ref.shape[1:]

    @pl.when(pl.program_id(1) == 0)
    def _():
        s_scr[...] = s0_ref[0]

    nb = nh * n_chunks
    shp = (nb, cl, hd)
    r = r_ref[0].reshape(shp)
    lw = lw_ref[0].reshape(shp)
    k = k_ref[0].reshape(shp)
    v = v_ref[0].reshape(shp)
    kk = kk_ref[0].reshape(shp)
    b = b_ref[0].reshape(shp)

    row = lax.broadcasted_iota(jnp.int32, (cl, cl), 0)
    col = lax.broadcasted_iota(jnp.int32, (cl, cl), 1)
    before = (col > row) if reverse else (col < row)
    upto = before | (col == row)
    same_sub = (row // RWKV_SUB) == (col // RWKV_SUB)

    tri = jnp.broadcast_to(jnp.where(upto, 1.0, 0.0).astype(BF16)[None], (nb, cl, cl))
    cum = sum(lax.dot_general(tri, p, _NN, preferred_element_type=F32) for p in _bf16_parts(lw, 3))
    last = 0 if reverse else cl - 1
    mid = cl // 2 if reverse else cl // 2 - 1
    tot = cum[:, last:last + 1, :]
    rho = cum[:, mid:mid + 1, :]
    cum_ex = cum - lw

    a_t = kk * jnp.exp(cum_ex - rho)
    r_t = r * jnp.exp(cum - rho)
    e_out = jnp.exp(rho - cum)
    b_t = b * e_out
    k_t = k * e_out
    a_0 = kk * jnp.exp(cum_ex)
    r_0 = r * jnp.exp(cum)
    e_end = jnp.exp(tot - cum)
    b_e = b * e_end
    k_e = k * e_end

    ps, pt = RWKV_PASSES
    l_mat = jnp.where(before, _dot(a_t, b_t, _NT, ps), 0.0)
    m_ak = jnp.where(before, _dot(a_t, k_t, _NT, ps), 0.0)
    m_rb = jnp.where(upto, _dot(r_t, b_t, _NT, ps), 0.0)
    m_rk = jnp.where(upto, _dot(r_t, k_t, _NT, ps), 0.0)
    t_m1 = _unit_lower_inverse(l_mat, same_sub, pt)

    mv = _dot(m_ak, v, _NN, ps)
    q = -(a_0 + _dot(t_m1, a_0, _NN, ps))
    w = -(mv + _dot(t_m1, mv, _NN, ps))
    r_h = r_0 + _dot(m_rb, q, _NN, ps)
    y_loc = _dot(m_rb, w, _NN, ps) + _dot(m_rk, v, _NN, ps)
    eye = lax.broadcasted_iota(jnp.int32, (hd, hd), 0) == lax.broadcasted_iota(jnp.int32, (hd, hd), 1)
    phi = (jnp.where(eye, jnp.exp(tot), 0.0) + _dot(b_e, q, _TN, ps)).reshape(nh, n_chunks, hd, hd)
    psi = (_dot(b_e, w, _TN, ps) + _dot(k_e, v, _TN, ps)).reshape(nh, n_chunks, hd, hd)

    s = s_scr[...]
    starts = [None] * n_chunks
    order = range(n_chunks - 1, -1, -1) if reverse else range(n_chunks)
    for n in order:
        starts[n] = s
        s = _dot(phi[:, n], s, _NN, 3) + psi[:, n]
    s_scr[...] = s
    sfin_ref[0] = s
    s_in = jnp.stack(starts, axis=1).reshape(nb, hd, hd)
    y = _dot(r_h, s_in, _NN, ps) + y_loc
    y_ref[0] = y.reshape(nh, tb, hd)


def rwkv_scan_blocked(r, lw, k, v, kk, b_, s0t, reverse):
    bsz, nh, t, hd = r.shape
    tb = min(RWKV_TOKENS_PER_STEP, t)
    nt = t // tb
    assert t == nt * tb and tb % RWKV_CHUNK == 0
    if reverse:
        tmap = lambda bi, j: (bi, 0, nt - 1 - j, 0)
    else:
        tmap = lambda bi, j: (bi, 0, j, 0)
    seq_spec = pl.BlockSpec((1, nh, tb, hd), tmap)
    st_spec = pl.BlockSpec((1, nh, hd, hd), lambda bi, j: (bi, 0, 0, 0))
    y, sfin = pl.pallas_call(
        functools.partial(_rwkv_scan_kernel, reverse=reverse, n_chunks=tb // RWKV_CHUNK),
        grid=(bsz, nt),
        in_specs=[seq_spec] * 6 + [st_spec],
        out_specs=[seq_spec, st_spec],
        out_shape=[jax.ShapeDtypeStruct((bsz, nh, t, hd), F32),
                   jax.ShapeDtypeStruct((bsz, nh, hd, hd), F32)],
        scratch_shapes=[pltpu.VMEM((nh, hd, hd), F32)],
        compiler_params=pltpu.CompilerParams(dimension_semantics=("parallel", "arbitrary"),
                                             vmem_limit_bytes=RWKV_VMEM_LIMIT),
        name="rwkv_scan_rev" if reverse else "rwkv_scan_fwd",
    )(r, lw, k, v, kk, b_, s0t)
    return y, sfin


def rwkv_mix(feat, lp, s0f, s0b):
    r, v, g, kk, dirs = feat
    (lwf, kf, bf), (lwb, kb, bb) = dirs
    tr = lambda z: jnp.swapaxes(z.astype(F32), 1, 2)
    rt, vt, kkt = tr(r), tr(v), tr(kk)
    yf, sf = rwkv_scan_blocked(rt, tr(lwf), tr(kf), vt, kkt, tr(bf), s0f, False)
    yb, sb = rwkv_scan_blocked(rt, tr(lwb), tr(kb), vt, kkt, tr(bb), s0b, True)
    yf = jnp.swapaxes(yf, 1, 2)
    yb = jnp.swapaxes(yb, 1, 2)
    y = head_norm(yf + yb, RWKV_GN_EPS) * heads(lp['rwkv_lnx_w'], RWKV_HEADS) + heads(lp['rwkv_lnx_b'], RWKV_HEADS)
    bonus = jnp.sum(r * (kf + kb) * lp['rwkv_r_k'], axis=-1, keepdims=True) * v
    out = (y + bonus).reshape(g.shape) * g
    return out.astype(g.dtype), sf, sb


def merge_branches(u, ys, lp):
    acc = jnp.zeros_like(u)
    for n in range(N_BRANCH):
        gate = jax.nn.sigmoid(u @ lp['w_gate'][n] + lp['b_gate'][n])
        acc = acc + gate * (ys[n].astype(u.dtype) @ lp['w_branch'][n])
    return acc @ lp['w_out']


def token_mixers(u_ctx, u_lat, lp, need_ctx):
    b, s, _ = u_lat.shape
    l = u_ctx.shape[1]
    pa_l, pb_l, pc_l, pd_l = split_sizes(u_lat @ lp['w_in'], GROUP_SIZES)
    pa_c, pb_c, pc_c, pd_c = split_sizes(u_ctx @ lp['w_in'], GROUP_SIZES)
    tab_att = axial_tables(s, HEAD_DIM)
    tab_mla = axial_tables(s, MLA_ROPE)

    aq_l, ak_l, av_l = split_sizes(pa_l, A_SIZES)
    aq_c, ak_c, av_c = split_sizes(pa_c, A_SIZES)
    k_c = heads(ak_c, ATT_KV_HEADS)
    v_c = heads(av_c, ATT_KV_HEADS)
    ya_l = window_gqa(axial_rope(heads(aq_l, ATT_Q_HEADS), tab_att),
                      axial_rope(heads(ak_l, ATT_KV_HEADS), tab_att),
                      heads(av_l, ATT_KV_HEADS), k_c, v_c, lp['sink'])

    lg_f, lg_b = retention_log_gammas()
    cos_c, sin_c = rope_tables(jnp.arange(l, dtype=F32), HEAD_DIM)
    cos_l, sin_l = rope_tables(l + jnp.arange(s, dtype=F32), HEAD_DIM)
    zero_ret = jnp.zeros((b, RET_HEADS, HEAD_DIM, HEAD_DIM), F32)
    yb_c, sbf, sbb = retention_mix(pb_c, cos_c, sin_c, lg_f, lg_b, zero_ret, zero_ret)
    yb_l, _, _ = retention_mix(pb_l, cos_l, sin_l, lg_f, lg_b, sbf, sbb)

    qn_l, qr_l, kn_l, kr_l, vm_l = mla_project(pc_l, lp)
    qn_c, qr_c, kn_c, kr_c, vm_c = mla_project(pc_c, lp)
    qr_l = axial_rope(qr_l, tab_mla)
    kr_l = axial_rope(kr_l, tab_mla)
    yc_l = mla_attend(qn_l, qr_l, jnp.concatenate([kn_c, kn_l], axis=1),
                      jnp.concatenate([kr_c, kr_l], axis=1), jnp.concatenate([vm_c, vm_l], axis=1))

    zero_wkv = jnp.zeros((b, RWKV_HEADS, HEAD_DIM, HEAD_DIM), F32)
    yd_c, sdf, sdb = rwkv_mix(rwkv_features(pd_c, lp), lp, zero_wkv, zero_wkv)
    yd_l, _, _ = rwkv_mix(rwkv_features(pd_l, lp), lp, sdf, sdb)

    m_lat = merge_branches(u_lat, (ya_l, yb_l, yc_l, yd_l), lp)
    if not need_ctx:
        return None, m_lat
    ya_c = ctx_gqa(heads(aq_c, ATT_Q_HEADS), k_c, v_c, lp['sink'])
    yc_c = mla_attend(qn_c, qr_c, kn_c, kr_c, vm_c)
    m_ctx = merge_branches(u_ctx, (ya_c, yb_c, yc_c, yd_c), lp)
    return m_ctx, m_lat


def moe_ffn(x, router_w, router_b, w_e_gate, w_e_up, w_e_down):
    n_tok, dm = x.shape
    scores = jax.nn.sigmoid(x.astype(F32) @ router_w.astype(F32))
    grp = (scores + router_b.astype(F32)).reshape(n_tok, N_GROUPS, EXPERTS_PER_GROUP)
    g_sel = jnp.argmax(jnp.sum(lax.top_k(grp, TOP_K)[0], axis=-1), axis=-1)
    in_grp = lax.top_k(jnp.take_along_axis(grp, g_sel[:, None, None], axis=1)[:, 0], TOP_K)[1]
    e_idx = g_sel[:, None] * EXPERTS_PER_GROUP + in_grp
    w_sel = jnp.take_along_axis(scores, e_idx, axis=1)
    w_sel = w_sel / jnp.sum(w_sel, axis=-1, keepdims=True)
    n_asg = n_tok * TOP_K
    flat_e = e_idx.reshape(-1).astype(jnp.int32)
    flat_w = w_sel.reshape(-1)
    order = jnp.argsort(flat_e).astype(jnp.int32)
    rank = jnp.argsort(order).astype(jnp.int32)
    onehot = flat_e[:, None] == jnp.arange(N_EXPERTS, dtype=jnp.int32)[None, :]
    counts = jnp.sum(onehot, axis=0, dtype=jnp.int32)
    padded = (counts + MOE_BLOCK - 1) // MOE_BLOCK * MOE_BLOCK
    pad_end = jnp.cumsum(padded)
    pad_start = pad_end - padded
    start = jnp.cumsum(counts) - counts
    n_blocks = -(-n_asg // MOE_BLOCK) + N_EXPERTS
    n_rows = n_blocks * MOE_BLOCK
    blk_row0 = jnp.arange(n_blocks, dtype=jnp.int32) * MOE_BLOCK
    blk_e = jnp.minimum(jnp.sum(pad_end[None, :] <= blk_row0[:, None], axis=1, dtype=jnp.int32), N_EXPERTS - 1)
    off = (blk_row0 - pad_start[blk_e])[:, None] + jnp.arange(MOE_BLOCK, dtype=jnp.int32)[None, :]
    valid = (off < counts[blk_e][:, None]).reshape(-1)
    src = order[jnp.clip(start[blk_e][:, None] + off, 0, n_asg - 1).reshape(-1)]
    row_tok = jnp.where(valid, src // TOP_K, 0)
    row_w = jnp.where(valid, flat_w[src], 0.0)
    xb = x.astype(BF16)[row_tok]
    yb = moe_expert_blocks(blk_e, xb, row_w[:, None], w_e_gate, w_e_up, w_e_down)
    shift = jnp.sum(jnp.where(onehot, (pad_start - start)[None, :], 0), axis=1, dtype=jnp.int32)
    slot = (rank + shift).reshape(n_tok, TOP_K)
    out = yb[slot[:, 0]]
    for j in range(1, TOP_K):
        out = out + yb[slot[:, j]]
    return out.astype(x.dtype)


MOE_VMEM_LIMIT = 40 * 1024 * 1024


def _moe_expert_kernel(blk_e_ref, x_ref, w_ref, wg_ref, wu_ref, wd_ref, o_ref):
    del blk_e_ref
    x = x_ref[...]
    g = jnp.dot(x, wg_ref[0].astype(BF16), preferred_element_type=F32)
    u = jnp.dot(x, wu_ref[0].astype(BF16), preferred_element_type=F32)
    hid = (g * jax.nn.sigmoid(g)) * u
    y = jnp.dot(hid.astype(BF16), wd_ref[0].astype(BF16), preferred_element_type=F32)
    o_ref[...] = y * w_ref[...]


def moe_expert_blocks(blk_e, xb, row_w, w_e_gate, w_e_up, w_e_down):
    n_rows, dm = xb.shape
    n_blocks = n_rows // MOE_BLOCK
    ff = w_e_gate.shape[-1]
    row_spec = lambda width: pl.BlockSpec((MOE_BLOCK, width), lambda i, be: (i, 0))
    return pl.pallas_call(
        _moe_expert_kernel,
        grid_spec=pltpu.PrefetchScalarGridSpec(
            num_scalar_prefetch=1,
            grid=(n_blocks,),
            in_specs=[row_spec(dm), row_spec(1),
                      pl.BlockSpec((1, dm, ff), lambda i, be: (be[i], 0, 0)),
                      pl.BlockSpec((1, dm, ff), lambda i, be: (be[i], 0, 0)),
                      pl.BlockSpec((1, ff, dm), lambda i, be: (be[i], 0, 0))],
            out_specs=row_spec(dm)),
        out_shape=jax.ShapeDtypeStruct((n_rows, dm), F32),
        compiler_params=pltpu.CompilerParams(dimension_semantics=("arbitrary",),
                                             vmem_limit_bytes=MOE_VMEM_LIMIT),
        name="moe_experts",
    )(blk_e, xb, row_w, w_e_gate, w_e_up, w_e_down)


def layer(h_ctx, h_lat, c, c_ctx, lp, router_w, router_b, need_ctx):
    b, s, d = h_lat.shape
    l = h_ctx.shape[1]
    m_lat = jnp.split((jax.nn.silu(c) @ lp['w_ada'] + lp['b_ada'])[:, None, :], 6, axis=-1)
    m_ctx = jnp.split(jax.nn.silu(c_ctx) @ lp['w_ada'] + lp['b_ada'], 6, axis=-1)
    u_lat = adaln(h_lat, lp['g_norm1'], m_lat[0], m_lat[1])
    u_ctx = adaln(h_ctx, lp['g_norm1'], m_ctx[0], m_ctx[1])
    mix_ctx, mix_lat = token_mixers(u_ctx, u_lat, lp, need_ctx)
    h_lat = h_lat + m_lat[2] * mix_lat
    f_lat = adaln(h_lat, lp['g_norm2'], m_lat[3], m_lat[4]).reshape(b * s, d)
    if not need_ctx:
        ffn = moe_ffn(f_lat, router_w, router_b, lp['w_e_gate'], lp['w_e_up'], lp['w_e_down'])
        return h_ctx, h_lat + m_lat[5] * ffn.reshape(b, s, d)
    h_ctx = h_ctx + m_ctx[2] * mix_ctx
    f_ctx = adaln(h_ctx, lp['g_norm2'], m_ctx[3], m_ctx[4]).reshape(b * l, d)
    ffn = moe_ffn(jnp.concatenate([f_ctx, f_lat], axis=0), router_w, router_b,
                  lp['w_e_gate'], lp['w_e_up'], lp['w_e_down'])
    h_ctx = h_ctx + m_ctx[5] * ffn[:b * l].reshape(b, l, d)
    h_lat = h_lat + m_lat[5] * ffn[b * l:].reshape(b, s, d)
    return h_ctx, h_lat


def _final_norm_kernel(x_ref, g_ref, o_ref):
    x = x_ref[...]
    y = x * lax.rsqrt(jnp.mean(x * x, axis=-1, keepdims=True) + NORM_EPS)
    o_ref[...] = y * g_ref[...]


def final_rms_norm(x, g):
    b, s, d = x.shape
    rows = b * s
    tile = 1024
    out = pl.pallas_call(
        _final_norm_kernel,
        grid=(rows // tile,),
        in_specs=[pl.BlockSpec((tile, d), lambda i: (i, 0)), pl.BlockSpec((1, d), lambda i: (0, 0))],
        out_specs=pl.BlockSpec((tile, d), lambda i: (i, 0)),
        out_shape=jax.ShapeDtypeStruct((rows, d), x.dtype),
        name="final_rms_norm",
    )(x.reshape(rows, d), g.reshape(1, d))
    return out.reshape(b, s, d)


_LAYER_PARAM_NAMES = (
    'w_ada', 'b_ada', 'g_norm1', 'g_norm2', 'w_in', 'sink', 'g_qnorm', 'g_kvnorm', 'w_uq',
    'w_ukv', 'rwkv_mu', 'rwkv_w0', 'rwkv_w_up', 'rwkv_a0', 'rwkv_a_up', 'rwkv_g_up',
    'rwkv_k_k', 'rwkv_k_a', 'rwkv_r_k', 'rwkv_lnx_w', 'rwkv_lnx_b', 'w_gate', 'b_gate',
    'w_branch', 'w_out', 'w_e_gate', 'w_e_up', 'w_e_down')


def kernel(x, c, ctx, c_ctx, w_ada, b_ada, g_norm1, g_norm2, w_in, sink, g_qnorm, g_kvnorm,
           w_uq, w_ukv, rwkv_mu, rwkv_w0, rwkv_w_up, rwkv_a0, rwkv_a_up, rwkv_g_up, rwkv_k_k,
           rwkv_k_a, rwkv_r_k, rwkv_lnx_w, rwkv_lnx_b, w_gate, b_gate, w_branch, w_out,
           router_w, router_b, w_e_gate, w_e_up, w_e_down, g_final):
    stacked = (w_ada, b_ada, g_norm1, g_norm2, w_in, sink, g_qnorm, g_kvnorm, w_uq, w_ukv, rwkv_mu,
               rwkv_w0, rwkv_w_up, rwkv_a0, rwkv_a_up, rwkv_g_up, rwkv_k_k, rwkv_k_a, rwkv_r_k,
               rwkv_lnx_w, rwkv_lnx_b, w_gate, b_gate, w_branch, w_out, w_e_gate, w_e_up, w_e_down)
    h_ctx, h_lat = ctx, x
    for i in range(DEPTH):
        lp = {n: a[i] for n, a in zip(_LAYER_PARAM_NAMES, stacked)}
        h_ctx, h_lat = layer(h_ctx, h_lat, c, c_ctx, lp, router_w, router_b, i < DEPTH - 1)
    return final_rms_norm(h_lat, g_final)
```

```python
import functools
import math

import jax
import jax.numpy as jnp
from jax import lax
from jax.experimental import pallas as pl
from jax.experimental.pallas import tpu as pltpu

D_MODEL = 1024
BATCH = 4
SEQ = 8192
DEPTH = 2

GRID_W = 64
CTX_LEN = 256
N_BRANCH = 4
BRANCH_W = D_MODEL // N_BRANCH
HEAD_DIM = 64
BRANCH_HEADS = BRANCH_W // HEAD_DIM
BLOCK = 128
ROPE_BASE = 10000.0
NORM_EPS = 1e-6
NEG_INF = -1e30
ATT_Q_HEADS = BRANCH_HEADS
ATT_KV_HEADS = BRANCH_HEADS // 2
WINDOW = 128
RET_HEADS = BRANCH_HEADS
RET_CHUNK = 128
RET_GN_EPS = 1e-5
MLA_HEADS = BRANCH_HEADS
MLA_Q_RANK = 256
MLA_KV_RANK = 128
MLA_NOPE = HEAD_DIM
MLA_ROPE = HEAD_DIM // 2
MLA_V = HEAD_DIM
RWKV_HEADS = BRANCH_HEADS
RWKV_DECAY_RANK = 64
RWKV_A_RANK = 64
RWKV_GATE_RANK = 128
RWKV_GN_EPS = 64e-5
N_EXPERTS = 64
N_GROUPS = 8
EXPERTS_PER_GROUP = N_EXPERTS // N_GROUPS
TOP_K = 2
EXPERT_FF = 512
MOE_BLOCK = 256
A_SIZES = (ATT_Q_HEADS * HEAD_DIM, ATT_KV_HEADS * HEAD_DIM, ATT_KV_HEADS * HEAD_DIM)
RET_SIZES = (BRANCH_W,) * 5
MLA_SIZES = (MLA_Q_RANK, MLA_KV_RANK, MLA_ROPE)
RWKV_SIZES = (BRANCH_W, BRANCH_W, BRANCH_W, RWKV_DECAY_RANK, RWKV_DECAY_RANK,
              RWKV_A_RANK, RWKV_A_RANK, RWKV_GATE_RANK)
GROUP_SIZES = (sum(A_SIZES), sum(RET_SIZES), sum(MLA_SIZES), sum(RWKV_SIZES))
N_IN = sum(GROUP_SIZES)
F32 = jnp.float32


def split_sizes(x, sizes):
    out, o = [], 0
    for n in sizes:
        out.append(x[..., o:o + n])
        o += n
    return out


def heads(x, h):
    return x.reshape(x.shape[:-1] + (h, x.shape[-1] // h))


def rms_norm(x, g):
    xf = x.astype(F32)
    y = xf * lax.rsqrt(jnp.mean(xf * xf, axis=-1, keepdims=True) + NORM_EPS)
    return (y * g.astype(F32)).astype(x.dtype)


def head_norm(x, eps):
    xf = x.astype(F32)
    xc = xf - jnp.mean(xf, axis=-1, keepdims=True)
    return xc * lax.rsqrt(jnp.mean(xc * xc, axis=-1, keepdims=True) + eps)


def adaln(h, g, shift, scale):
    return rms_norm(h, g) * (1 + scale) + shift


def rope_tables(pos, dim):
    inv = ROPE_BASE ** (-jnp.arange(0, dim, 2, dtype=F32) / dim)
    ang = pos[:, None] * inv[None, :]
    return jnp.cos(ang), jnp.sin(ang)


def rope_rotate(x, cos, sin):
    m = x.shape[-1] // 2
    x1, x2 = x[..., :m], x[..., m:]
    c = cos[None, :, None, :]
    s = sin[None, :, None, :]
    return jnp.concatenate([x1 * c - x2 * s, x2 * c + x1 * s], axis=-1).astype(x.dtype)


def axial_tables(seq, dim):
    rows = seq // GRID_W
    row = jnp.repeat(jnp.arange(rows, dtype=F32), GRID_W)
    col = jnp.broadcast_to(jnp.arange(GRID_W, dtype=F32)[None, :], (rows, GRID_W)).reshape(-1)
    return rope_tables(row, dim // 2), rope_tables(col, dim // 2)


def axial_rope(x, tabs):
    (cos_r, sin_r), (cos_c, sin_c) = tabs
    half = x.shape[-1] // 2
    return jnp.concatenate([rope_rotate(x[..., :half], cos_r, sin_r),
                            rope_rotate(x[..., half:], cos_c, sin_c)], axis=-1)


def window_gqa(q, k, v, k_ctx, v_ctx, sink):
    b, s, hq, d = q.shape
    g = hq // ATT_KV_HEADS
    nb = s // BLOCK
    nw = 3 * BLOCK
    nc = k_ctx.shape[1]
    qb = (q * d ** -0.5).reshape(b, nb, BLOCK, ATT_KV_HEADS, g, d)
    pad = ((0, 0), (BLOCK, BLOCK), (0, 0), (0, 0))
    kp = jnp.pad(k, pad).reshape(b, nb + 2, BLOCK, ATT_KV_HEADS, d)
    vp = jnp.pad(v, pad).reshape(b, nb + 2, BLOCK, ATT_KV_HEADS, d)
    kw = jnp.concatenate([kp[:, :-2], kp[:, 1:-1], kp[:, 2:]], axis=2)
    vw = jnp.concatenate([vp[:, :-2], vp[:, 1:-1], vp[:, 2:]], axis=2)
    blk0 = jnp.arange(nb)[:, None, None] * BLOCK
    q_pos = blk0 + jnp.arange(BLOCK)[None, :, None]
    k_pos = blk0 - BLOCK + jnp.arange(nw)[None, None, :]
    valid = (jnp.abs(k_pos - q_pos) <= WINDOW) & (k_pos >= 0) & (k_pos < s)
    s_win = jnp.einsum('bnqhgd,bnkhd->bnhgqk', qb, kw).astype(F32)
    s_win = jnp.where(valid[None, :, None, None], s_win, NEG_INF)
    s_ctx = jnp.einsum('bnqhgd,bchd->bnhgqc', qb, k_ctx).astype(F32)
    s_sink = jnp.broadcast_to(sink.astype(F32).reshape(ATT_KV_HEADS, g, 1, 1), s_win.shape[:-1] + (1,))
    p = jax.nn.softmax(jnp.concatenate([s_win, s_ctx, s_sink], axis=-1), axis=-1).astype(v.dtype)
    o = (jnp.einsum('bnhgqk,bnkhd->bnqhgd', p[..., :nw], vw)
         + jnp.einsum('bnhgqc,bchd->bnqhgd', p[..., nw:nw + nc], v_ctx))
    return o.reshape(b, s, hq * d)


def ctx_gqa(q, k, v, sink):
    b, l, hq, d = q.shape
    g = hq // ATT_KV_HEADS
    qg = (q * d ** -0.5).reshape(b, l, ATT_KV_HEADS, g, d)
    sc = jnp.einsum('bqhgd,bkhd->bhgqk', qg, k).astype(F32)
    s_sink = jnp.broadcast_to(sink.astype(F32).reshape(ATT_KV_HEADS, g, 1, 1), sc.shape[:-1] + (1,))
    p = jax.nn.softmax(jnp.concatenate([sc, s_sink], axis=-1), axis=-1).astype(v.dtype)
    o = jnp.einsum('bhgqk,bkhd->bqhgd', p[..., :l], v)
    return o.reshape(b, l, hq * d)


def retention_log_gammas():
    lg = jnp.log(1.0 - jnp.exp(jnp.linspace(math.log(1.0 / 32), math.log(1.0 / 512), 2 * RET_HEADS, dtype=F32)))
    return lg[0::2], lg[1::2]


def retention_chunked(q, k, v, log_gamma, s0):
    b, t, h, d = q.shape
    n = t // RET_CHUNK
    cl = RET_CHUNK
    qc = q.reshape(b, n, cl, h, d).astype(F32)
    kc = k.reshape(b, n, cl, h, d).astype(F32)
    vc = v.reshape(b, n, cl, h, d).astype(F32)
    idx = jnp.arange(cl, dtype=F32)
    rel = idx[:, None] - idx[None, :]
    d_intra = jnp.where(rel >= 0, jnp.exp(jnp.maximum(rel, 0.0)[None] * log_gamma[:, None, None]), 0.0)
    a_int = jnp.einsum('bnihd,bnjhd->bnhij', qc, kc) * d_intra
    o = jnp.einsum('bnhij,bnjhe->bnihe', a_int, vc)
    k_dec = kc * jnp.exp((cl - 1 - idx)[:, None] * log_gamma[None, :])[..., None]
    kv = jnp.einsum('bnjhd,bnjhe->nbhde', k_dec, vc)
    g_chunk = jnp.exp(cl * log_gamma)[:, None, None]

    def step(state, kv_n):
        return state * g_chunk + kv_n, state

    s_fin, s_prev = lax.scan(step, s0, kv)
    q_dec = qc * jnp.exp((idx + 1)[:, None] * log_gamma[None, :])[..., None]
    o = o + jnp.einsum('bnihd,nbhde->bnihe', q_dec, s_prev)
    return o.reshape(b, t, h, d), s_fin


def retention_mix(pb, cos, sin, lg_f, lg_b, s0f, s0b):
    bsz, t = pb.shape[:2]
    q, k, v, gf, gb = split_sizes(pb, RET_SIZES)
    q = rope_rotate(heads(q, RET_HEADS), cos, sin)
    k = rope_rotate(heads(k, RET_HEADS), cos, sin) * HEAD_DIM ** -0.5
    v = heads(v, RET_HEADS)
    yf, sf = retention_chunked(q, k, v, lg_f, s0f)
    yb, sb = retention_chunked(q[:, ::-1], k[:, ::-1], v[:, ::-1], lg_b, s0b)
    out = (jax.nn.silu(gf) * head_norm(yf, RET_GN_EPS).reshape(bsz, t, -1)
           + jax.nn.silu(gb) * head_norm(yb[:, ::-1], RET_GN_EPS).reshape(bsz, t, -1))
    return out.astype(pb.dtype), sf, sb


def mla_project(pc, lp):
    cq, ckv, kr = split_sizes(pc, MLA_SIZES)
    q = heads(rms_norm(cq, lp['g_qnorm']) @ lp['w_uq'], MLA_HEADS)
    kv = heads(rms_norm(ckv, lp['g_kvnorm']) @ lp['w_ukv'], MLA_HEADS)
    return q[..., :MLA_NOPE], q[..., MLA_NOPE:], kv[..., :MLA_NOPE], kr[:, :, None, :], kv[..., MLA_NOPE:]


LANES = 128
MLA_Q_TILE = 512
MLA_KEY_UNIT = 256
MLA_KEY_TILE_MAX = 8448
MLA_VMEM_LIMIT = 56 * 1024 * 1024


def _mla_flash_kernel(q_ref, k_ref, v_ref, o_ref, m_scr, acc_scr, *, scale, tk, n_tiles):
    dv = o_ref.shape[-1]
    q = (q_ref[0, 0] * scale).astype(BF16)
    m_scr[...] = jnp.full_like(m_scr, NEG_INF)
    acc_scr[...] = jnp.zeros_like(acc_scr)

    def sweep(c, carry):
        keys = pl.ds(pl.multiple_of(c * tk, tk), tk)
        s = lax.dot_general(q, k_ref[0, 0, keys, :], (((1,), (1,)), ((), ())), preferred_element_type=F32)
        m_prev = m_scr[...]
        m_new = jnp.maximum(m_prev, jnp.max(s, axis=-1, keepdims=True))
        p = jnp.exp2(s - m_new).astype(BF16)
        acc_scr[...] = (jnp.exp2(m_prev - m_new) * acc_scr[...]
                        + jnp.dot(p, v_ref[0, 0, keys, :], preferred_element_type=F32))
        m_scr[...] = m_new
        return carry

    lax.fori_loop(0, n_tiles, sweep, 0)
    acc = acc_scr[...]
    o_ref[0, 0] = acc[:, :dv] / acc[:, dv:dv + 1]


def mla_attend(qn, qr, kn, kr, v):
    b, t, h, _ = qn.shape
    n = kn.shape[1]
    dv = v.shape[-1]
    scale = (MLA_NOPE + MLA_ROPE) ** -0.5 * math.log2(math.e)
    q = jnp.swapaxes(jnp.concatenate([qn, qr], axis=-1), 1, 2)
    kr_h = jnp.broadcast_to(kr, (b, n, h, MLA_ROPE))
    k = jnp.swapaxes(jnp.concatenate([kn, kr_h], axis=-1), 1, 2).astype(BF16)
    ones = jnp.ones((b, n, h, 1), v.dtype)
    zeros = jnp.zeros((b, n, h, LANES - dv - 1), v.dtype)
    vt = jnp.swapaxes(jnp.concatenate([v, ones, zeros], axis=-1), 1, 2).astype(BF16)
    dq = q.shape[-1]
    tq = min(MLA_Q_TILE, t)
    tk = max(d for d in range(MLA_KEY_UNIT, MLA_KEY_TILE_MAX + 1, MLA_KEY_UNIT) if n % d == 0)
    assert t % tq == 0
    o = pl.pallas_call(
        functools.partial(_mla_flash_kernel, scale=scale, tk=tk, n_tiles=n // tk),
        grid=(b, h, t // tq),
        in_specs=[pl.BlockSpec((1, 1, tq, dq), lambda bi, hi, qi: (bi, hi, qi, 0)),
                  pl.BlockSpec((1, 1, n, dq), lambda bi, hi, qi: (bi, hi, 0, 0)),
                  pl.BlockSpec((1, 1, n, LANES), lambda bi, hi, qi: (bi, hi, 0, 0))],
        out_specs=pl.BlockSpec((1, 1, tq, dv), lambda bi, hi, qi: (bi, hi, qi, 0)),
        out_shape=jax.ShapeDtypeStruct((b, h, t, dv), F32),
        scratch_shapes=[pltpu.VMEM((tq, 1), F32), pltpu.VMEM((tq, LANES), F32)],
        compiler_params=pltpu.CompilerParams(
            dimension_semantics=("parallel", "parallel", "arbitrary"),
            vmem_limit_bytes=MLA_VMEM_LIMIT),
        name="mla_flash",
    )(q, k, vt)
    return jnp.swapaxes(o, 1, 2).reshape(b, t, h * dv)


def centred_shift(p):
    prev = jnp.pad(p, ((0, 0), (1, 0), (0, 0)))[:, :-1]
    nxt = jnp.pad(p, ((0, 0), (0, 1), (0, 0)))[:, 1:]
    return 0.5 * (prev + nxt)


def rwkv_features(pd, lp):
    z = pd + (centred_shift(pd) - pd) * lp['rwkv_mu']
    zr, zk, zv, zwf, zwb, zaf, zab, zg = split_sizes(z, RWKV_SIZES)
    r = heads(zr, RWKV_HEADS)
    k = heads(zk, RWKV_HEADS)
    v = heads(zv, RWKV_HEADS)
    g = jax.nn.sigmoid(zg) @ lp['rwkv_g_up']
    kk = (k * heads(lp['rwkv_k_k'], RWKV_HEADS)).astype(F32)
    kk = kk / jnp.maximum(jnp.sqrt(jnp.sum(kk * kk, axis=-1, keepdims=True)), 1e-12)
    k_a = heads(lp['rwkv_k_a'], RWKV_HEADS)
    dirs = []
    for di, (zw, za) in enumerate(((zwf, zaf), (zwb, zab))):
        logw = -jax.nn.softplus(-(lp['rwkv_w0'][di] + jnp.tanh(zw) @ lp['rwkv_w_up'][di]).astype(F32)) - 0.5
        w = heads(-jnp.exp(logw), RWKV_HEADS)
        a = heads(jax.nn.sigmoid((lp['rwkv_a0'][di] + za @ lp['rwkv_a_up'][di]).astype(F32)), RWKV_HEADS)
        dirs.append((w, k * (1 + (a - 1) * k_a), kk * a))
    return r, v, g, kk, dirs


def rwkv_scan(r, w, k, v, kk, b_, s0, reverse):
    xs = tuple(jnp.moveaxis(z.astype(F32), 1, 0) for z in (r, w, k, v, kk, b_))

    def step(st, inp):
        r_t, w_t, k_t, v_t, kk_t, b_t = inp
        sa = jnp.einsum('bhvk,bhk->bhv', st, -kk_t)
        st = st * w_t[:, :, None, :] + sa[..., None] * b_t[:, :, None, :] + v_t[..., None] * k_t[:, :, None, :]
        return st, jnp.einsum('bhvk,bhk->bhv', st, r_t)

    s_fin, y = lax.scan(step, s0, xs, reverse=reverse)
    return jnp.moveaxis(y, 0, 1), s_fin


BF16 = jnp.bfloat16
RWKV_CHUNK = 64
RWKV_SUB = 16
RWKV_TOKENS_PER_STEP = 512
RWKV_PASSES = (1, 1)
RWKV_VMEM_LIMIT = 48 * 1024 * 1024
_NN = (((2,), (1,)), ((0,), (0,)))
_NT = (((2,), (2,)), ((0,), (0,)))
_TN = (((1,), (1,)), ((0,), (0,)))


def _bf16_parts(a, n):
    parts, rem = [], a
    for i in range(n):
        p = rem.astype(BF16)
        parts.append(p)
        if i + 1 < n:
            rem = rem - p.astype(F32)
    return parts


def _dot(a, b, dims, passes):
    if passes == 1:
        return lax.dot_general(a.astype(BF16), b.astype(BF16), dims, preferred_element_type=F32)
    a_hi, a_lo = _bf16_parts(a, 2)
    b_hi, b_lo = _bf16_parts(b, 2)
    out = lax.dot_general(a_hi, b_hi, dims, preferred_element_type=F32)
    out = out + lax.dot_general(a_hi, b_lo, dims, preferred_element_type=F32)
    return out + lax.dot_general(a_lo, b_hi, dims, preferred_element_type=F32)


def _unit_lower_inverse(l_mat, same_sub, passes):
    mm = lambda a, b: _dot(a, b, _NN, passes)
    ld = jnp.where(same_sub, l_mat, 0.0)
    lo = l_mat - ld
    p2 = mm(ld, ld)
    d = p2 - ld - mm(ld, p2)
    pw = p2
    span = 4
    while span < RWKV_SUB:
        pw = mm(pw, pw)
        d = d + pw + mm(d, pw)
        span *= 2
    n1 = lo + mm(d, lo)
    n2 = mm(n1, n1)
    x = n2 - n1 - mm(n1, n2)
    span = 4
    pw = n2
    while span < RWKV_CHUNK // RWKV_SUB:
        pw = mm(pw, pw)
        x = x + pw + mm(x, pw)
        span *= 2
    return x + d + mm(x, d)


def _rwkv_scan_kernel(r_ref, lw_ref, k_ref, v_ref, kk_ref, b_ref, s0_ref, y_ref, sfin_ref, s_scr,
                      *, reverse, n_chunks):
    cl = RWKV_CHUNK
    nh, tb, hd = r_ref.shape[1:]

    @pl.when(pl.program_id(1) == 0)
    def _():
        s_scr[...] = s0_ref[0]

    nb = nh * n_chunks
    shp = (nb, cl, hd)
    r = r_ref[0].reshape(shp)
    lw = lw_ref[0].reshape(shp)
    k = k_ref[0].reshape(shp)
    v = v_ref[0].reshape(shp)
    kk = kk_ref[0].reshape(shp)
    b = b_ref[0].reshape(shp)

    row = lax.broadcasted_iota(jnp.int32, (cl, cl), 0)
    col = lax.broadcasted_iota(jnp.int32, (cl, cl), 1)
    before = (col > row) if reverse else (col < row)
    upto = before | (col == row)
    same_sub = (row // RWKV_SUB) == (col // RWKV_SUB)

    tri = jnp.broadcast_to(jnp.where(upto, 1.0, 0.0).astype(BF16)[None], (nb, cl, cl))
    cum = sum(lax.dot_general(tri, p, _NN, preferred_element_type=F32) for p in _bf16_parts(lw, 3))
    last = 0 if reverse else cl - 1
    mid = cl // 2 if reverse else cl // 2 - 1
    tot = cum[:, last:last + 1, :]
    rho = cum[:, mid:mid + 1, :]
    cum_ex = cum - lw

    a_t = kk * jnp.exp(cum_ex - rho)
    r_t = r * jnp.exp(cum - rho)
    e_out = jnp.exp(rho - cum)
    b_t = b * e_out
    k_t = k * e_out
    a_0 = kk * jnp.exp(cum_ex)
    r_0 = r * jnp.exp(cum)
    e_end = jnp.exp(tot - cum)
    b_e = b * e_end
    k_e = k * e_end

    ps, pt = RWKV_PASSES
    l_mat = jnp.where(before, _dot(a_t, b_t, _NT, ps), 0.0)
    m_ak = jnp.where(before, _dot(a_t, k_t, _NT, ps), 0.0)
    m_rb = jnp.where(upto, _dot(r_t, b_t, _NT, ps), 0.0)
    m_rk = jnp.where(upto, _dot(r_t, k_t, _NT, ps), 0.0)
    t_m1 = _unit_lower_inverse(l_mat, same_sub, pt)

    mv = _dot(m_ak, v, _NN, ps)
    q = -(a_0 + _dot(t_m1, a_0, _NN, ps))
    w = -(mv + _dot(t_m1, mv, _NN, ps))
    r_h = r_0 + _dot(m_rb, q, _NN, ps)
    y_loc = _dot(m_rb, w, _NN, ps) + _dot(m_rk, v, _NN, ps)
    eye = lax.broadcasted_iota(jnp.int32, (hd, hd), 0) == lax.broadcasted_iota(jnp.int32, (hd, hd), 1)
    phi = (jnp.where(eye, jnp.exp(tot), 0.0) + _dot(b_e, q, _TN, ps)).reshape(nh, n_chunks, hd, hd)
    psi = (_dot(b_e, w, _TN, ps) + _dot(k_e, v, _TN, ps)).reshape(nh, n_chunks, hd, hd)

    s = s_scr[...]
    starts = [None] * n_chunks
    order = range(n_chunks - 1, -1, -1) if reverse else range(n_chunks)
    for n in order:
        starts[n] = s
        s = _dot(phi[:, n], s, _NN, 3) + psi[:, n]
    s_scr[...] = s
    sfin_ref[0] = s
    s_in = jnp.stack(starts, axis=1).reshape(nb, hd, hd)
    y = _dot(r_h, s_in, _NN, ps) + y_loc
    y_ref[0] = y.reshape(nh, tb, hd)


def rwkv_scan_blocked(r, lw, k, v, kk, b_, s0t, reverse):
    bsz, nh, t, hd = r.shape
    tb = min(RWKV_TOKENS_PER_STEP, t)
    nt = t // tb
    assert t == nt * tb and tb % RWKV_CHUNK == 0
    if reverse:
        tmap = lambda bi, j: (bi, 0, nt - 1 - j, 0)
    else:
        tmap = lambda bi, j: (bi, 0, j, 0)
    seq_spec = pl.BlockSpec((1, nh, tb, hd), tmap)
    st_spec = pl.BlockSpec((1, nh, hd, hd), lambda bi, j: (bi, 0, 0, 0))
    y, sfin = pl.pallas_call(
        functools.partial(_rwkv_scan_kernel, reverse=reverse, n_chunks=tb // RWKV_CHUNK),
        grid=(bsz, nt),
        in_specs=[seq_spec] * 6 + [st_spec],
        out_specs=[seq_spec, st_spec],
        out_shape=[jax.ShapeDtypeStruct((bsz, nh, t, hd), F32),
                   jax.ShapeDtypeStruct((bsz, nh, hd, hd), F32)],
        scratch_shapes=[pltpu.VMEM((nh, hd, hd), F32)],
        compiler_params=pltpu.CompilerParams(dimension_semantics=("parallel", "arbitrary"),
                                             vmem_limit_bytes=RWKV_VMEM_LIMIT),
        name="rwkv_scan_rev" if reverse else "rwkv_scan_fwd",
    )(r, lw, k, v, kk, b_, s0t)
    return y, sfin


def rwkv_mix(feat, lp, s0f, s0b):
    r, v, g, kk, dirs = feat
    (lwf, kf, bf), (lwb, kb, bb) = dirs
    tr = lambda z: jnp.swapaxes(z.astype(F32), 1, 2)
    rt, vt, kkt = tr(r), tr(v), tr(kk)
    yf, sf = rwkv_scan_blocked(rt, tr(lwf), tr(kf), vt, kkt, tr(bf), s0f, False)
    yb, sb = rwkv_scan_blocked(rt, tr(lwb), tr(kb), vt, kkt, tr(bb), s0b, True)
    yf = jnp.swapaxes(yf, 1, 2)
    yb = jnp.swapaxes(yb, 1, 2)
    y = head_norm(yf + yb, RWKV_GN_EPS) * heads(lp['rwkv_lnx_w'], RWKV_HEADS) + heads(lp['rwkv_lnx_b'], RWKV_HEADS)
    bonus = jnp.sum(r * (kf + kb) * lp['rwkv_r_k'], axis=-1, keepdims=True) * v
    out = (y + bonus).reshape(g.shape) * g
    return out.astype(g.dtype), sf, sb


def merge_branches(u, ys, lp):
    acc = jnp.zeros_like(u)
    for n in range(N_BRANCH):
        gate = jax.nn.sigmoid(u @ lp['w_gate'][n] + lp['b_gate'][n])
        acc = acc + gate * (ys[n].astype(u.dtype) @ lp['w_branch'][n])
    return acc @ lp['w_out']


def token_mixers(u_ctx, u_lat, lp, need_ctx):
    b, s, _ = u_lat.shape
    l = u_ctx.shape[1]
    pa_l, pb_l, pc_l, pd_l = split_sizes(u_lat @ lp['w_in'], GROUP_SIZES)
    pa_c, pb_c, pc_c, pd_c = split_sizes(u_ctx @ lp['w_in'], GROUP_SIZES)
    tab_att = axial_tables(s, HEAD_DIM)
    tab_mla = axial_tables(s, MLA_ROPE)

    aq_l, ak_l, av_l = split_sizes(pa_l, A_SIZES)
    aq_c, ak_c, av_c = split_sizes(pa_c, A_SIZES)
    k_c = heads(ak_c, ATT_KV_HEADS)
    v_c = heads(av_c, ATT_KV_HEADS)
    ya_l = window_gqa(axial_rope(heads(aq_l, ATT_Q_HEADS), tab_att),
                      axial_rope(heads(ak_l, ATT_KV_HEADS), tab_att),
                      heads(av_l, ATT_KV_HEADS), k_c, v_c, lp['sink'])

    lg_f, lg_b = retention_log_gammas()
    cos_c, sin_c = rope_tables(jnp.arange(l, dtype=F32), HEAD_DIM)
    cos_l, sin_l = rope_tables(l + jnp.arange(s, dtype=F32), HEAD_DIM)
    zero_ret = jnp.zeros((b, RET_HEADS, HEAD_DIM, HEAD_DIM), F32)
    yb_c, sbf, sbb = retention_mix(pb_c, cos_c, sin_c, lg_f, lg_b, zero_ret, zero_ret)
    yb_l, _, _ = retention_mix(pb_l, cos_l, sin_l, lg_f, lg_b, sbf, sbb)

    qn_l, qr_l, kn_l, kr_l, vm_l = mla_project(pc_l, lp)
    qn_c, qr_c, kn_c, kr_c, vm_c = mla_project(pc_c, lp)
    qr_l = axial_rope(qr_l, tab_mla)
    kr_l = axial_rope(kr_l, tab_mla)
    yc_l = mla_attend(qn_l, qr_l, jnp.concatenate([kn_c, kn_l], axis=1),
                      jnp.concatenate([kr_c, kr_l], axis=1), jnp.concatenate([vm_c, vm_l], axis=1))

    zero_wkv = jnp.zeros((b, RWKV_HEADS, HEAD_DIM, HEAD_DIM), F32)
    yd_c, sdf, sdb = rwkv_mix(rwkv_features(pd_c, lp), lp, zero_wkv, zero_wkv)
    yd_l, _, _ = rwkv_mix(rwkv_features(pd_l, lp), lp, sdf, sdb)

    m_lat = merge_branches(u_lat, (ya_l, yb_l, yc_l, yd_l), lp)
    if not need_ctx:
        return None, m_lat
    ya_c = ctx_gqa(heads(aq_c, ATT_Q_HEADS), k_c, v_c, lp['sink'])
    yc_c = mla_attend(qn_c, qr_c, kn_c, kr_c, vm_c)
    m_ctx = merge_branches(u_ctx, (ya_c, yb_c, yc_c, yd_c), lp)
    return m_ctx, m_lat


def moe_ffn(x, router_w, router_b, layer_idx, w_e_gate, w_e_up, w_e_down):
    n_tok, dm = x.shape
    scores = jax.nn.sigmoid(x.astype(F32) @ router_w.astype(F32))
    grp = (scores + router_b.astype(F32)).reshape(n_tok, N_GROUPS, EXPERTS_PER_GROUP)
    g_sel = jnp.argmax(jnp.sum(lax.top_k(grp, TOP_K)[0], axis=-1), axis=-1)
    in_grp = lax.top_k(jnp.take_along_axis(grp, g_sel[:, None, None], axis=1)[:, 0], TOP_K)[1]
    e_idx = g_sel[:, None] * EXPERTS_PER_GROUP + in_grp
    w_sel = jnp.take_along_axis(scores, e_idx, axis=1)
    w_sel = w_sel / jnp.sum(w_sel, axis=-1, keepdims=True)
    n_asg = n_tok * TOP_K
    flat_e = e_idx.reshape(-1).astype(jnp.int32)
    flat_w = w_sel.reshape(-1)
    order = jnp.argsort(flat_e).astype(jnp.int32)
    rank = jnp.argsort(order).astype(jnp.int32)
    onehot = flat_e[:, None] == jnp.arange(N_EXPERTS, dtype=jnp.int32)[None, :]
    counts = jnp.sum(onehot, axis=0, dtype=jnp.int32)
    padded = (counts + MOE_BLOCK - 1) // MOE_BLOCK * MOE_BLOCK
    pad_end = jnp.cumsum(padded)
    pad_start = pad_end - padded
    start = jnp.cumsum(counts) - counts
    n_blocks = -(-n_asg // MOE_BLOCK) + N_EXPERTS
    n_rows = n_blocks * MOE_BLOCK
    blk_row0 = jnp.arange(n_blocks, dtype=jnp.int32) * MOE_BLOCK
    blk_e = jnp.minimum(jnp.sum(pad_end[None, :] <= blk_row0[:, None], axis=1, dtype=jnp.int32), N_EXPERTS - 1)
    off = (blk_row0 - pad_start[blk_e])[:, None] + jnp.arange(MOE_BLOCK, dtype=jnp.int32)[None, :]
    valid = (off < counts[blk_e][:, None]).reshape(-1)
    src = order[jnp.clip(start[blk_e][:, None] + off, 0, n_asg - 1).reshape(-1)]
    row_tok = jnp.where(valid, src // TOP_K, 0)
    row_w = jnp.where(valid, flat_w[src], 0.0)
    yb = moe_expert_blocks(blk_e, x[row_tok], row_w[:, None], layer_idx, w_e_gate, w_e_up, w_e_down)
    shift = jnp.sum(jnp.where(onehot, (pad_start - start)[None, :], 0), axis=1, dtype=jnp.int32)
    slot = (rank + shift).reshape(n_tok, TOP_K)
    out = yb[slot[:, 0]]
    for j in range(1, TOP_K):
        out = out + yb[slot[:, j]]
    return out.astype(x.dtype)


MOE_VMEM_LIMIT = 40 * 1024 * 1024


def _moe_expert_kernel(blk_e_ref, x_ref, w_ref, wg_ref, wu_ref, wd_ref, o_ref):
    del blk_e_ref
    x = x_ref[...].astype(BF16)
    g = jnp.dot(x, wg_ref[0, 0].astype(BF16), preferred_element_type=F32)
    u = jnp.dot(x, wu_ref[0, 0].astype(BF16), preferred_element_type=F32)
    hid = (g * jax.nn.sigmoid(g)) * u
    y = jnp.dot(hid.astype(BF16), wd_ref[0, 0].astype(BF16), preferred_element_type=F32)
    o_ref[...] = y * w_ref[...]


def moe_expert_blocks(blk_e, xb, row_w, layer_idx, w_e_gate, w_e_up, w_e_down):
    n_rows, dm = xb.shape
    n_blocks = n_rows // MOE_BLOCK
    ff = w_e_gate.shape[-1]
    row_spec = lambda width: pl.BlockSpec((MOE_BLOCK, width), lambda i, be: (i, 0))
    expert_spec = lambda rows, cols: pl.BlockSpec((1, 1, rows, cols), lambda i, be: (layer_idx, be[i], 0, 0))
    return pl.pallas_call(
        _moe_expert_kernel,
        grid_spec=pltpu.PrefetchScalarGridSpec(
            num_scalar_prefetch=1,
            grid=(n_blocks,),
            in_specs=[row_spec(dm), row_spec(1), expert_spec(dm, ff), expert_spec(dm, ff), expert_spec(ff, dm)],
            out_specs=row_spec(dm)),
        out_shape=jax.ShapeDtypeStruct((n_rows, dm), F32),
        compiler_params=pltpu.CompilerParams(dimension_semantics=("arbitrary",),
                                             vmem_limit_bytes=MOE_VMEM_LIMIT),
        name="moe_experts",
    )(blk_e, xb, row_w, w_e_gate, w_e_up, w_e_down)


def layer(h_ctx, h_lat, c, c_ctx, lp, router_w, router_b, need_ctx):
    b, s, d = h_lat.shape
    l = h_ctx.shape[1]
    m_lat = jnp.split((jax.nn.silu(c) @ lp['w_ada'] + lp['b_ada'])[:, None, :], 6, axis=-1)
    m_ctx = jnp.split(jax.nn.silu(c_ctx) @ lp['w_ada'] + lp['b_ada'], 6, axis=-1)
    u_lat = adaln(h_lat, lp['g_norm1'], m_lat[0], m_lat[1])
    u_ctx = adaln(h_ctx, lp['g_norm1'], m_ctx[0], m_ctx[1])
    mix_ctx, mix_lat = token_mixers(u_ctx, u_lat, lp, need_ctx)
    h_lat = h_lat + m_lat[2] * mix_lat
    f_lat = adaln(h_lat, lp['g_norm2'], m_lat[3], m_lat[4]).reshape(b * s, d)
    if not need_ctx:
        ffn = moe_ffn(f_lat, router_w, router_b, lp['layer_idx'], *lp['experts'])
        return h_ctx, h_lat + m_lat[5] * ffn.reshape(b, s, d)
    h_ctx = h_ctx + m_ctx[2] * mix_ctx
    f_ctx = adaln(h_ctx, lp['g_norm2'], m_ctx[3], m_ctx[4]).reshape(b * l, d)
    ffn = moe_ffn(jnp.concatenate([f_ctx, f_lat], axis=0), router_w, router_b, lp['layer_idx'], *lp['experts'])
    h_ctx = h_ctx + m_ctx[5] * ffn[:b * l].reshape(b, l, d)
    h_lat = h_lat + m_lat[5] * ffn[b * l:].reshape(b, s, d)
    return h_ctx, h_lat


def _final_norm_kernel(x_ref, g_ref, o_ref):
    x = x_ref[...]
    y = x * lax.rsqrt(jnp.mean(x * x, axis=-1, keepdims=True) + NORM_EPS)
    o_ref[...] = y * g_ref[...]


def final_rms_norm(x, g):
    b, s, d = x.shape
    rows = b * s
    tile = 1024
    out = pl.pallas_call(
        _final_norm_kernel,
        grid=(rows // tile,),
        in_specs=[pl.BlockSpec((tile, d), lambda i: (i, 0)), pl.BlockSpec((1, d), lambda i: (0, 0))],
        out_specs=pl.BlockSpec((tile, d), lambda i: (i, 0)),
        out_shape=jax.ShapeDtypeStruct((rows, d), x.dtype),
        name="final_rms_norm",
    )(x.reshape(rows, d), g.reshape(1, d))
    return out.reshape(b, s, d)


_LAYER_PARAM_NAMES = (
    'w_ada', 'b_ada', 'g_norm1', 'g_norm2', 'w_in', 'sink', 'g_qnorm', 'g_kvnorm', 'w_uq',
    'w_ukv', 'rwkv_mu', 'rwkv_w0', 'rwkv_w_up', 'rwkv_a0', 'rwkv_a_up', 'rwkv_g_up',
    'rwkv_k_k', 'rwkv_k_a', 'rwkv_r_k', 'rwkv_lnx_w', 'rwkv_lnx_b', 'w_gate', 'b_gate',
    'w_branch', 'w_out')


def kernel(x, c, ctx, c_ctx, w_ada, b_ada, g_norm1, g_norm2, w_in, sink, g_qnorm, g_kvnorm,
           w_uq, w_ukv, rwkv_mu, rwkv_w0, rwkv_w_up, rwkv_a0, rwkv_a_up, rwkv_g_up, rwkv_k_k,
           rwkv_k_a, rwkv_r_k, rwkv_lnx_w, rwkv_lnx_b, w_gate, b_gate, w_branch, w_out,
           router_w, router_b, w_e_gate, w_e_up, w_e_down, g_final):
    stacked = (w_ada, b_ada, g_norm1, g_norm2, w_in, sink, g_qnorm, g_kvnorm, w_uq, w_ukv, rwkv_mu,
               rwkv_w0, rwkv_w_up, rwkv_a0, rwkv_a_up, rwkv_g_up, rwkv_k_k, rwkv_k_a, rwkv_r_k,
               rwkv_lnx_w, rwkv_lnx_b, w_gate, b_gate, w_branch, w_out)
    h_ctx, h_lat = ctx, x
    for i in range(DEPTH):
        lp = {n: a[i] for n, a in zip(_LAYER_PARAM_NAMES, stacked)}
        lp['layer_idx'] = i
        lp['experts'] = (w_e_gate, w_e_up, w_e_down)
        h_ctx, h_lat = layer(h_ctx, h_lat, c, c_ctx, lp, router_w, router_b, i < DEPTH - 1)
    return final_rms_norm(h_lat, g_final)
```

```python
import functools
import math

import jax
import jax.numpy as jnp
from jax import lax
from jax.experimental import pallas as pl
from jax.experimental.pallas import tpu as pltpu

D_MODEL = 1024
BATCH = 4
SEQ = 8192
DEPTH = 2

GRID_W = 64
CTX_LEN = 256
N_BRANCH = 4
BRANCH_W = D_MODEL // N_BRANCH
HEAD_DIM = 64
BRANCH_HEADS = BRANCH_W // HEAD_DIM
BLOCK = 128
ROPE_BASE = 10000.0
NORM_EPS = 1e-6
NEG_INF = -1e30
ATT_Q_HEADS = BRANCH_HEADS
ATT_KV_HEADS = BRANCH_HEADS // 2
WINDOW = 128
RET_HEADS = BRANCH_HEADS
RET_CHUNK = 128
RET_GN_EPS = 1e-5
MLA_HEADS = BRANCH_HEADS
MLA_Q_RANK = 256
MLA_KV_RANK = 128
MLA_NOPE = HEAD_DIM
MLA_ROPE = HEAD_DIM // 2
MLA_V = HEAD_DIM
RWKV_HEADS = BRANCH_HEADS
RWKV_DECAY_RANK = 64
RWKV_A_RANK = 64
RWKV_GATE_RANK = 128
RWKV_GN_EPS = 64e-5
N_EXPERTS = 64
N_GROUPS = 8
EXPERTS_PER_GROUP = N_EXPERTS // N_GROUPS
TOP_K = 2
EXPERT_FF = 512
MOE_BLOCK = 256
A_SIZES = (ATT_Q_HEADS * HEAD_DIM, ATT_KV_HEADS * HEAD_DIM, ATT_KV_HEADS * HEAD_DIM)
RET_SIZES = (BRANCH_W,) * 5
MLA_SIZES = (MLA_Q_RANK, MLA_KV_RANK, MLA_ROPE)
RWKV_SIZES = (BRANCH_W, BRANCH_W, BRANCH_W, RWKV_DECAY_RANK, RWKV_DECAY_RANK,
              RWKV_A_RANK, RWKV_A_RANK, RWKV_GATE_RANK)
GROUP_SIZES = (sum(A_SIZES), sum(RET_SIZES), sum(MLA_SIZES), sum(RWKV_SIZES))
N_IN = sum(GROUP_SIZES)
F32 = jnp.float32


def split_sizes(x, sizes):
    out, o = [], 0
    for n in sizes:
        out.append(x[..., o:o + n])
        o += n
    return out


def heads(x, h):
    return x.reshape(x.shape[:-1] + (h, x.shape[-1] // h))


def rms_norm(x, g):
    xf = x.astype(F32)
    y = xf * lax.rsqrt(jnp.mean(xf * xf, axis=-1, keepdims=True) + NORM_EPS)
    return (y * g.astype(F32)).astype(x.dtype)


def head_norm(x, eps):
    xf = x.astype(F32)
    xc = xf - jnp.mean(xf, axis=-1, keepdims=True)
    return xc * lax.rsqrt(jnp.mean(xc * xc, axis=-1, keepdims=True) + eps)


def adaln(h, g, shift, scale):
    return rms_norm(h, g) * (1 + scale) + shift


def rope_tables(pos, dim):
    inv = ROPE_BASE ** (-jnp.arange(0, dim, 2, dtype=F32) / dim)
    ang = pos[:, None] * inv[None, :]
    return jnp.cos(ang), jnp.sin(ang)


def rope_rotate(x, cos, sin):
    m = x.shape[-1] // 2
    x1, x2 = x[..., :m], x[..., m:]
    c = cos[None, :, None, :]
    s = sin[None, :, None, :]
    return jnp.concatenate([x1 * c - x2 * s, x2 * c + x1 * s], axis=-1).astype(x.dtype)


def axial_tables(seq, dim):
    rows = seq // GRID_W
    row = jnp.repeat(jnp.arange(rows, dtype=F32), GRID_W)
    col = jnp.broadcast_to(jnp.arange(GRID_W, dtype=F32)[None, :], (rows, GRID_W)).reshape(-1)
    return rope_tables(row, dim // 2), rope_tables(col, dim // 2)


def axial_rope(x, tabs):
    (cos_r, sin_r), (cos_c, sin_c) = tabs
    half = x.shape[-1] // 2
    return jnp.concatenate([rope_rotate(x[..., :half], cos_r, sin_r),
                            rope_rotate(x[..., half:], cos_c, sin_c)], axis=-1)


def window_gqa(q, k, v, k_ctx, v_ctx, sink):
    b, s, hq, d = q.shape
    g = hq // ATT_KV_HEADS
    nb = s // BLOCK
    nw = 3 * BLOCK
    nc = k_ctx.shape[1]
    qb = (q * d ** -0.5).reshape(b, nb, BLOCK, ATT_KV_HEADS, g, d)
    pad = ((0, 0), (BLOCK, BLOCK), (0, 0), (0, 0))
    kp = jnp.pad(k, pad).reshape(b, nb + 2, BLOCK, ATT_KV_HEADS, d)
    vp = jnp.pad(v, pad).reshape(b, nb + 2, BLOCK, ATT_KV_HEADS, d)
    kw = jnp.concatenate([kp[:, :-2], kp[:, 1:-1], kp[:, 2:]], axis=2)
    vw = jnp.concatenate([vp[:, :-2], vp[:, 1:-1], vp[:, 2:]], axis=2)
    blk0 = jnp.arange(nb)[:, None, None] * BLOCK
    q_pos = blk0 + jnp.arange(BLOCK)[None, :, None]
    k_pos = blk0 - BLOCK + jnp.arange(nw)[None, None, :]
    valid = (jnp.abs(k_pos - q_pos) <= WINDOW) & (k_pos >= 0) & (k_pos < s)
    s_win = jnp.einsum('bnqhgd,bnkhd->bnhgqk', qb, kw).astype(F32)
    s_win = jnp.where(valid[None, :, None, None], s_win, NEG_INF)
    s_ctx = jnp.einsum('bnqhgd,bchd->bnhgqc', qb, k_ctx).astype(F32)
    s_sink = jnp.broadcast_to(sink.astype(F32).reshape(ATT_KV_HEADS, g, 1, 1), s_win.shape[:-1] + (1,))
    p = jax.nn.softmax(jnp.concatenate([s_win, s_ctx, s_sink], axis=-1), axis=-1).astype(v.dtype)
    o = (jnp.einsum('bnhgqk,bnkhd->bnqhgd', p[..., :nw], vw)
         + jnp.einsum('bnhgqc,bchd->bnqhgd', p[..., nw:nw + nc], v_ctx))
    return o.reshape(b, s, hq * d)


def ctx_gqa(q, k, v, sink):
    b, l, hq, d = q.shape
    g = hq // ATT_KV_HEADS
    qg = (q * d ** -0.5).reshape(b, l, ATT_KV_HEADS, g, d)
    sc = jnp.einsum('bqhgd,bkhd->bhgqk', qg, k).astype(F32)
    s_sink = jnp.broadcast_to(sink.astype(F32).reshape(ATT_KV_HEADS, g, 1, 1), sc.shape[:-1] + (1,))
    p = jax.nn.softmax(jnp.concatenate([sc, s_sink], axis=-1), axis=-1).astype(v.dtype)
    o = jnp.einsum('bhgqk,bkhd->bqhgd', p[..., :l], v)
    return o.reshape(b, l, hq * d)


def retention_log_gammas():
    lg = jnp.log(1.0 - jnp.exp(jnp.linspace(math.log(1.0 / 32), math.log(1.0 / 512), 2 * RET_HEADS, dtype=F32)))
    return lg[0::2], lg[1::2]


def retention_chunked(q, k, v, log_gamma, s0):
    b, t, h, d = q.shape
    n = t // RET_CHUNK
    cl = RET_CHUNK
    qc = q.reshape(b, n, cl, h, d).astype(F32)
    kc = k.reshape(b, n, cl, h, d).astype(F32)
    vc = v.reshape(b, n, cl, h, d).astype(F32)
    idx = jnp.arange(cl, dtype=F32)
    rel = idx[:, None] - idx[None, :]
    d_intra = jnp.where(rel >= 0, jnp.exp(jnp.maximum(rel, 0.0)[None] * log_gamma[:, None, None]), 0.0)
    a_int = jnp.einsum('bnihd,bnjhd->bnhij', qc, kc) * d_intra
    o = jnp.einsum('bnhij,bnjhe->bnihe', a_int, vc)
    k_dec = kc * jnp.exp((cl - 1 - idx)[:, None] * log_gamma[None, :])[..., None]
    kv = jnp.einsum('bnjhd,bnjhe->nbhde', k_dec, vc)
    g_chunk = jnp.exp(cl * log_gamma)[:, None, None]

    def step(state, kv_n):
        return state * g_chunk + kv_n, state

    s_fin, s_prev = lax.scan(step, s0, kv)
    q_dec = qc * jnp.exp((idx + 1)[:, None] * log_gamma[None, :])[..., None]
    o = o + jnp.einsum('bnihd,nbhde->bnihe', q_dec, s_prev)
    return o.reshape(b, t, h, d), s_fin


def retention_mix(pb, cos, sin, lg_f, lg_b, s0f, s0b):
    bsz, t = pb.shape[:2]
    q, k, v, gf, gb = split_sizes(pb, RET_SIZES)
    q = rope_rotate(heads(q, RET_HEADS), cos, sin)
    k = rope_rotate(heads(k, RET_HEADS), cos, sin) * HEAD_DIM ** -0.5
    v = heads(v, RET_HEADS)
    yf, sf = retention_chunked(q, k, v, lg_f, s0f)
    yb, sb = retention_chunked(q[:, ::-1], k[:, ::-1], v[:, ::-1], lg_b, s0b)
    out = (jax.nn.silu(gf) * head_norm(yf, RET_GN_EPS).reshape(bsz, t, -1)
           + jax.nn.silu(gb) * head_norm(yb[:, ::-1], RET_GN_EPS).reshape(bsz, t, -1))
    return out.astype(pb.dtype), sf, sb


def mla_project(pc, lp):
    cq, ckv, kr = split_sizes(pc, MLA_SIZES)
    q = heads(rms_norm(cq, lp['g_qnorm']) @ lp['w_uq'], MLA_HEADS)
    kv = heads(rms_norm(ckv, lp['g_kvnorm']) @ lp['w_ukv'], MLA_HEADS)
    return q[..., :MLA_NOPE], q[..., MLA_NOPE:], kv[..., :MLA_NOPE], kr[:, :, None, :], kv[..., MLA_NOPE:]


LANES = 128
MLA_Q_TILE = 512
MLA_KEY_UNIT = 256
MLA_KEY_TILE_MAX = 8448
MLA_VMEM_LIMIT = 56 * 1024 * 1024


def _mla_flash_kernel(q_ref, k_ref, v_ref, o_ref, m_scr, acc_scr, *, scale, tk, n_tiles):
    dv = o_ref.shape[-1]
    q = (q_ref[0, 0] * scale).astype(BF16)
    m_scr[...] = jnp.full_like(m_scr, NEG_INF)
    acc_scr[...] = jnp.zeros_like(acc_scr)

    def sweep(c, carry):
        keys = pl.ds(pl.multiple_of(c * tk, tk), tk)
        s = lax.dot_general(q, k_ref[0, 0, keys, :], (((1,), (1,)), ((), ())), preferred_element_type=F32)
        m_prev = m_scr[...]
        m_new = jnp.maximum(m_prev, jnp.max(s, axis=-1, keepdims=True))
        p = jnp.exp2(s - m_new).astype(BF16)
        acc_scr[...] = (jnp.exp2(m_prev - m_new) * acc_scr[...]
                        + jnp.dot(p, v_ref[0, 0, keys, :], preferred_element_type=F32))
        m_scr[...] = m_new
        return carry

    lax.fori_loop(0, n_tiles, sweep, 0)
    acc = acc_scr[...]
    o_ref[0, 0] = acc[:, :dv] / acc[:, dv:dv + 1]


def mla_attend(qn, qr, kn, kr, v):
    b, t, h, _ = qn.shape
    n = kn.shape[1]
    dv = v.shape[-1]
    scale = (MLA_NOPE + MLA_ROPE) ** -0.5 * math.log2(math.e)
    q = jnp.swapaxes(jnp.concatenate([qn, qr], axis=-1), 1, 2)
    kr_h = jnp.broadcast_to(kr, (b, n, h, MLA_ROPE))
    k = jnp.swapaxes(jnp.concatenate([kn, kr_h], axis=-1), 1, 2).astype(BF16)
    ones = jnp.ones((b, n, h, 1), v.dtype)
    zeros = jnp.zeros((b, n, h, LANES - dv - 1), v.dtype)
    vt = jnp.swapaxes(jnp.concatenate([v, ones, zeros], axis=-1), 1, 2).astype(BF16)
    dq = q.shape[-1]
    tq = min(MLA_Q_TILE, t)
    tk = max(d for d in range(MLA_KEY_UNIT, MLA_KEY_TILE_MAX + 1, MLA_KEY_UNIT) if n % d == 0)
    assert t % tq == 0
    o = pl.pallas_call(
        functools.partial(_mla_flash_kernel, scale=scale, tk=tk, n_tiles=n // tk),
        grid=(b, h, t // tq),
        in_specs=[pl.BlockSpec((1, 1, tq, dq), lambda bi, hi, qi: (bi, hi, qi, 0)),
                  pl.BlockSpec((1, 1, n, dq), lambda bi, hi, qi: (bi, hi, 0, 0)),
                  pl.BlockSpec((1, 1, n, LANES), lambda bi, hi, qi: (bi, hi, 0, 0))],
        out_specs=pl.BlockSpec((1, 1, tq, dv), lambda bi, hi, qi: (bi, hi, qi, 0)),
        out_shape=jax.ShapeDtypeStruct((b, h, t, dv), F32),
        scratch_shapes=[pltpu.VMEM((tq, 1), F32), pltpu.VMEM((tq, LANES), F32)],
        compiler_params=pltpu.CompilerParams(
            dimension_semantics=("parallel", "parallel", "arbitrary"),
            vmem_limit_bytes=MLA_VMEM_LIMIT),
        name="mla_flash",
    )(q, k, vt)
    return jnp.swapaxes(o, 1, 2).reshape(b, t, h * dv)


def centred_shift(p):
    prev = jnp.pad(p, ((0, 0), (1, 0), (0, 0)))[:, :-1]
    nxt = jnp.pad(p, ((0, 0), (0, 1), (0, 0)))[:, 1:]
    return 0.5 * (prev + nxt)


def rwkv_features(pd, lp):
    z = pd + (centred_shift(pd) - pd) * lp['rwkv_mu']
    zr, zk, zv, zwf, zwb, zaf, zab, zg = split_sizes(z, RWKV_SIZES)
    r = heads(zr, RWKV_HEADS)
    k = heads(zk, RWKV_HEADS)
    v = heads(zv, RWKV_HEADS)
    g = jax.nn.sigmoid(zg) @ lp['rwkv_g_up']
    kk = (k * heads(lp['rwkv_k_k'], RWKV_HEADS)).astype(F32)
    kk = kk / jnp.maximum(jnp.sqrt(jnp.sum(kk * kk, axis=-1, keepdims=True)), 1e-12)
    k_a = heads(lp['rwkv_k_a'], RWKV_HEADS)
    dirs = []
    for di, (zw, za) in enumerate(((zwf, zaf), (zwb, zab))):
        logw = -jax.nn.softplus(-(lp['rwkv_w0'][di] + jnp.tanh(zw) @ lp['rwkv_w_up'][di]).astype(F32)) - 0.5
        w = heads(-jnp.exp(logw), RWKV_HEADS)
        a = heads(jax.nn.sigmoid((lp['rwkv_a0'][di] + za @ lp['rwkv_a_up'][di]).astype(F32)), RWKV_HEADS)
        dirs.append((w, k * (1 + (a - 1) * k_a), kk * a))
    return r, v, g, kk, dirs


def rwkv_scan(r, w, k, v, kk, b_, s0, reverse):
    xs = tuple(jnp.moveaxis(z.astype(F32), 1, 0) for z in (r, w, k, v, kk, b_))

    def step(st, inp):
        r_t, w_t, k_t, v_t, kk_t, b_t = inp
        sa = jnp.einsum('bhvk,bhk->bhv', st, -kk_t)
        st = st * w_t[:, :, None, :] + sa[..., None] * b_t[:, :, None, :] + v_t[..., None] * k_t[:, :, None, :]
        return st, jnp.einsum('bhvk,bhk->bhv', st, r_t)

    s_fin, y = lax.scan(step, s0, xs, reverse=reverse)
    return jnp.moveaxis(y, 0, 1), s_fin


BF16 = jnp.bfloat16
RWKV_CHUNK = 64
RWKV_SUB = 16
RWKV_TOKENS_PER_STEP = 512
RWKV_PASSES = (1, 1)
RWKV_VMEM_LIMIT = 48 * 1024 * 1024
_NN = (((2,), (1,)), ((0,), (0,)))
_NT = (((2,), (2,)), ((0,), (0,)))
_TN = (((1,), (1,)), ((0,), (0,)))


def _bf16_parts(a, n):
    parts, rem = [], a
    for i in range(n):
        p = rem.astype(BF16)
        parts.append(p)
        if i + 1 < n:
            rem = rem - p.astype(F32)
    return parts


def _dot(a, b, dims, passes):
    if passes == 1:
        return lax.dot_general(a.astype(BF16), b.astype(BF16), dims, preferred_element_type=F32)
    a_hi, a_lo = _bf16_parts(a, 2)
    b_hi, b_lo = _bf16_parts(b, 2)
    out = lax.dot_general(a_hi, b_hi, dims, preferred_element_type=F32)
    out = out + lax.dot_general(a_hi, b_lo, dims, preferred_element_type=F32)
    return out + lax.dot_general(a_lo, b_hi, dims, preferred_element_type=F32)


def _unit_lower_inverse(l_mat, same_sub, passes):
    mm = lambda a, b: _dot(a, b, _NN, passes)
    ld = jnp.where(same_sub, l_mat, 0.0)
    lo = l_mat - ld
    p2 = mm(ld, ld)
    d = p2 - ld - mm(ld, p2)
    pw = p2
    span = 4
    while span < RWKV_SUB:
        pw = mm(pw, pw)
        d = d + pw + mm(d, pw)
        span *= 2
    n1 = lo + mm(d, lo)
    n2 = mm(n1, n1)
    x = n2 - n1 - mm(n1, n2)
    span = 4
    pw = n2
    while span < RWKV_CHUNK // RWKV_SUB:
        pw = mm(pw, pw)
        x = x + pw + mm(x, pw)
        span *= 2
    return x + d + mm(x, d)


def _rwkv_scan_kernel(r_ref, lw_ref, k_ref, v_ref, kk_ref, b_ref, s0_ref, y_ref, sfin_ref, s_scr,
                      *, reverse, n_chunks):
    cl = RWKV_CHUNK
    nh, tb, hd = r_ref.shape[1:]

    @pl.when(pl.program_id(1) == 0)
    def _():
        s_scr[...] = s0_ref[0]

    nb = nh * n_chunks
    shp = (nb, cl, hd)
    r = r_ref[0].reshape(shp)
    lw = lw_ref[0].reshape(shp)
    k = k_ref[0].reshape(shp)
    v = v_ref[0].reshape(shp)
    kk = kk_ref[0].reshape(shp)
    b = b_ref[0].reshape(shp)

    row = lax.broadcasted_iota(jnp.int32, (cl, cl), 0)
    col = lax.broadcasted_iota(jnp.int32, (cl, cl), 1)
    before = (col > row) if reverse else (col < row)
    upto = before | (col == row)
    same_sub = (row // RWKV_SUB) == (col // RWKV_SUB)

    tri = jnp.broadcast_to(jnp.where(upto, 1.0, 0.0).astype(BF16)[None], (nb, cl, cl))
    cum = sum(lax.dot_general(tri, p, _NN, preferred_element_type=F32) for p in _bf16_parts(lw, 3))
    last = 0 if reverse else cl - 1
    mid = cl // 2 if reverse else cl // 2 - 1
    tot = cum[:, last:last + 1, :]
    rho = cum[:, mid:mid + 1, :]
    cum_ex = cum - lw

    a_t = kk * jnp.exp(cum_ex - rho)
    r_t = r * jnp.exp(cum - rho)
    e_out = jnp.exp(rho - cum)
    b_t = b * e_out
    k_t = k * e_out
    a_0 = kk * jnp.exp(cum_ex)
    r_0 = r * jnp.exp(cum)
    e_end = jnp.exp(tot - cum)
    b_e = b * e_end
    k_e = k * e_end

    ps, pt = RWKV_PASSES
    l_mat = jnp.where(before, _dot(a_t, b_t, _NT, ps), 0.0)
    m_ak = jnp.where(before, _dot(a_t, k_t, _NT, ps), 0.0)
    m_rb = jnp.where(upto, _dot(r_t, b_t, _NT, ps), 0.0)
    m_rk = jnp.where(upto, _dot(r_t, k_t, _NT, ps), 0.0)
    t_m1 = _unit_lower_inverse(l_mat, same_sub, pt)

    mv = _dot(m_ak, v, _NN, ps)
    q = -(a_0 + _dot(t_m1, a_0, _NN, ps))
    w = -(mv + _dot(t_m1, mv, _NN, ps))
    r_h = r_0 + _dot(m_rb, q, _NN, ps)
    y_loc = _dot(m_rb, w, _NN, ps) + _dot(m_rk, v, _NN, ps)
    eye = lax.broadcasted_iota(jnp.int32, (hd, hd), 0) == lax.broadcasted_iota(jnp.int32, (hd, hd), 1)
    phi = (jnp.where(eye, jnp.exp(tot), 0.0) + _dot(b_e, q, _TN, ps)).reshape(nh, n_chunks, hd, hd)
    psi = (_dot(b_e, w, _TN, ps) + _dot(k_e, v, _TN, ps)).reshape(nh, n_chunks, hd, hd)

    s = s_scr[...]
    starts = [None] * n_chunks
    order = range(n_chunks - 1, -1, -1) if reverse else range(n_chunks)
    for n in order:
        starts[n] = s
        s = _dot(phi[:, n], s, _NN, 3) + psi[:, n]
    s_scr[...] = s
    sfin_ref[0] = s
    s_in = jnp.stack(starts, axis=1).reshape(nb, hd, hd)
    y = _dot(r_h, s_in, _NN, ps) + y_loc
    y_ref[0] = y.reshape(nh, tb, hd)


def rwkv_scan_blocked(r, lw, k, v, kk, b_, s0t, reverse):
    bsz, nh, t, hd = r.shape
    tb = min(RWKV_TOKENS_PER_STEP, t)
    nt = t // tb
    assert t == nt * tb and tb % RWKV_CHUNK == 0
    if reverse:
        tmap = lambda bi, j: (bi, 0, nt - 1 - j, 0)
    else:
        tmap = lambda bi, j: (bi, 0, j, 0)
    seq_spec = pl.BlockSpec((1, nh, tb, hd), tmap)
    st_spec = pl.BlockSpec((1, nh, hd, hd), lambda bi, j: (bi, 0, 0, 0))
    y, sfin = pl.pallas_call(
        functools.partial(_rwkv_scan_kernel, reverse=reverse, n_chunks=tb // RWKV_CHUNK),
        grid=(bsz, nt),
        in_specs=[seq_spec] * 6 + [st_spec],
        out_specs=[seq_spec, st_spec],
        out_shape=[jax.ShapeDtypeStruct((bsz, nh, t, hd), F32),
                   jax.ShapeDtypeStruct((bsz, nh, hd, hd), F32)],
        scratch_shapes=[pltpu.VMEM((nh, hd, hd), F32)],
        compiler_params=pltpu.CompilerParams(dimension_semantics=("parallel", "arbitrary"),
                                             vmem_limit_bytes=RWKV_VMEM_LIMIT),
        name="rwkv_scan_rev" if reverse else "rwkv_scan_fwd",
    )(r, lw, k, v, kk, b_, s0t)
    return y, sfin


def rwkv_mix(feat, lp, s0f, s0b):
    r, v, g, kk, dirs = feat
    (lwf, kf, bf), (lwb, kb, bb) = dirs
    tr = lambda z: jnp.swapaxes(z.astype(F32), 1, 2)
    rt, vt, kkt = tr(r), tr(v), tr(kk)
    yf, sf = rwkv_scan_blocked(rt, tr(lwf), tr(kf), vt, kkt, tr(bf), s0f, False)
    yb, sb = rwkv_scan_blocked(rt, tr(lwb), tr(kb), vt, kkt, tr(bb), s0b, True)
    yf = jnp.swapaxes(yf, 1, 2)
    yb = jnp.swapaxes(yb, 1, 2)
    y = head_norm(yf + yb, RWKV_GN_EPS) * heads(lp['rwkv_lnx_w'], RWKV_HEADS) + heads(lp['rwkv_lnx_b'], RWKV_HEADS)
    bonus = jnp.sum(r * (kf + kb) * lp['rwkv_r_k'], axis=-1, keepdims=True) * v
    out = (y + bonus).reshape(g.shape) * g
    return out.astype(g.dtype), sf, sb


def merge_branches(u, ys, lp):
    acc = jnp.zeros_like(u)
    for n in range(N_BRANCH):
        gate = jax.nn.sigmoid(u @ lp['w_gate'][n] + lp['b_gate'][n])
        acc = acc + gate * (ys[n].astype(u.dtype) @ lp['w_branch'][n])
    return acc @ lp['w_out']


def token_mixers(u_ctx, u_lat, lp, need_ctx):
    b, s, _ = u_lat.shape
    l = u_ctx.shape[1]
    pa_l, pb_l, pc_l, pd_l = split_sizes(u_lat @ lp['w_in'], GROUP_SIZES)
    pa_c, pb_c, pc_c, pd_c = split_sizes(u_ctx @ lp['w_in'], GROUP_SIZES)
    tab_att = axial_tables(s, HEAD_DIM)
    tab_mla = axial_tables(s, MLA_ROPE)

    aq_l, ak_l, av_l = split_sizes(pa_l, A_SIZES)
    aq_c, ak_c, av_c = split_sizes(pa_c, A_SIZES)
    k_c = heads(ak_c, ATT_KV_HEADS)
    v_c = heads(av_c, ATT_KV_HEADS)
    ya_l = window_gqa(axial_rope(heads(aq_l, ATT_Q_HEADS), tab_att),
                      axial_rope(heads(ak_l, ATT_KV_HEADS), tab_att),
                      heads(av_l, ATT_KV_HEADS), k_c, v_c, lp['sink'])

    lg_f, lg_b = retention_log_gammas()
    cos_c, sin_c = rope_tables(jnp.arange(l, dtype=F32), HEAD_DIM)
    cos_l, sin_l = rope_tables(l + jnp.arange(s, dtype=F32), HEAD_DIM)
    zero_ret = jnp.zeros((b, RET_HEADS, HEAD_DIM, HEAD_DIM), F32)
    yb_c, sbf, sbb = retention_mix(pb_c, cos_c, sin_c, lg_f, lg_b, zero_ret, zero_ret)
    yb_l, _, _ = retention_mix(pb_l, cos_l, sin_l, lg_f, lg_b, sbf, sbb)

    qn_l, qr_l, kn_l, kr_l, vm_l = mla_project(pc_l, lp)
    qn_c, qr_c, kn_c, kr_c, vm_c = mla_project(pc_c, lp)
    qr_l = axial_rope(qr_l, tab_mla)
    kr_l = axial_rope(kr_l, tab_mla)
    yc_l = mla_attend(qn_l, qr_l, jnp.concatenate([kn_c, kn_l], axis=1),
                      jnp.concatenate([kr_c, kr_l], axis=1), jnp.concatenate([vm_c, vm_l], axis=1))

    zero_wkv = jnp.zeros((b, RWKV_HEADS, HEAD_DIM, HEAD_DIM), F32)
    yd_c, sdf, sdb = rwkv_mix(rwkv_features(pd_c, lp), lp, zero_wkv, zero_wkv)
    yd_l, _, _ = rwkv_mix(rwkv_features(pd_l, lp), lp, sdf, sdb)

    m_lat = merge_branches(u_lat, (ya_l, yb_l, yc_l, yd_l), lp)
    if not need_ctx:
        return None, m_lat
    ya_c = ctx_gqa(heads(aq_c, ATT_Q_HEADS), k_c, v_c, lp['sink'])
    yc_c = mla_attend(qn_c, qr_c, kn_c, kr_c, vm_c)
    m_ctx = merge_branches(u_ctx, (ya_c, yb_c, yc_c, yd_c), lp)
    return m_ctx, m_lat


def moe_ffn(x, router_w, router_b, layer_idx, w_e_gate, w_e_up, w_e_down):
    n_tok, dm = x.shape
    scores = jax.nn.sigmoid(x.astype(F32) @ router_w.astype(F32))
    grp = (scores + router_b.astype(F32)).reshape(n_tok, N_GROUPS, EXPERTS_PER_GROUP)
    g_sel = jnp.argmax(jnp.sum(lax.top_k(grp, TOP_K)[0], axis=-1), axis=-1)
    in_grp = lax.top_k(jnp.take_along_axis(grp, g_sel[:, None, None], axis=1)[:, 0], TOP_K)[1]
    e_idx = g_sel[:, None] * EXPERTS_PER_GROUP + in_grp
    w_sel = jnp.take_along_axis(scores, e_idx, axis=1)
    w_sel = w_sel / jnp.sum(w_sel, axis=-1, keepdims=True)
    n_asg = n_tok * TOP_K
    flat_e = e_idx.reshape(-1).astype(jnp.int32)
    flat_w = w_sel.reshape(-1)
    order = jnp.argsort(flat_e).astype(jnp.int32)
    rank = jnp.argsort(order).astype(jnp.int32)
    onehot = flat_e[:, None] == jnp.arange(N_EXPERTS, dtype=jnp.int32)[None, :]
    counts = jnp.sum(onehot, axis=0, dtype=jnp.int32)
    padded = (counts + MOE_BLOCK - 1) // MOE_BLOCK * MOE_BLOCK
    pad_end = jnp.cumsum(padded)
    pad_start = pad_end - padded
    start = jnp.cumsum(counts) - counts
    n_blocks = -(-n_asg // MOE_BLOCK) + N_EXPERTS
    n_rows = n_blocks * MOE_BLOCK
    blk_row0 = jnp.arange(n_blocks, dtype=jnp.int32) * MOE_BLOCK
    blk_e = jnp.minimum(jnp.sum(pad_end[None, :] <= blk_row0[:, None], axis=1, dtype=jnp.int32), N_EXPERTS - 1)
    off = (blk_row0 - pad_start[blk_e])[:, None] + jnp.arange(MOE_BLOCK, dtype=jnp.int32)[None, :]
    valid = (off < counts[blk_e][:, None]).reshape(-1)
    src = order[jnp.clip(start[blk_e][:, None] + off, 0, n_asg - 1).reshape(-1)]
    row_tok = jnp.where(valid, src // TOP_K, 0)
    row_w = jnp.where(valid, flat_w[src], 0.0)
    yb = moe_expert_blocks(blk_e, row_tok, x, row_w[:, None], layer_idx, w_e_gate, w_e_up, w_e_down)
    shift = jnp.sum(jnp.where(onehot, (pad_start - start)[None, :], 0), axis=1, dtype=jnp.int32)
    slot = (rank + shift).reshape(n_tok, TOP_K)
    out = yb[slot[:, 0]]
    for j in range(1, TOP_K):
        out = out + yb[slot[:, j]]
    return out.astype(x.dtype)


MOE_VMEM_LIMIT = 40 * 1024 * 1024
MOE_GATHER_UNROLL = 8


def _moe_row_copy(x_hbm, xbuf, sem, tok, slot, r):
    return pltpu.make_async_copy(x_hbm.at[pl.ds(tok, 1)], xbuf.at[slot, pl.ds(r, 1)], sem.at[slot])


def _moe_expert_kernel(blk_e_ref, row_tok_ref, x_hbm, w_ref, wg_ref, wu_ref, wd_ref, o_ref, xbuf, sem):
    del blk_e_ref
    i = pl.program_id(0)
    slot = i % 2

    def issue(block, dst_slot):
        def body(r, carry):
            _moe_row_copy(x_hbm, xbuf, sem, row_tok_ref[block * MOE_BLOCK + r], dst_slot, r).start()
            return carry
        lax.fori_loop(0, MOE_BLOCK, body, 0, unroll=MOE_GATHER_UNROLL)

    @pl.when(i == 0)
    def _():
        issue(0, 0)

    @pl.when(i + 1 < pl.num_programs(0))
    def _():
        issue(i + 1, 1 - slot)

    def wait_row(r, carry):
        _moe_row_copy(x_hbm, xbuf, sem, 0, slot, r).wait()
        return carry
    lax.fori_loop(0, MOE_BLOCK, wait_row, 0, unroll=MOE_GATHER_UNROLL)

    x = xbuf[slot].astype(BF16)
    g = jnp.dot(x, wg_ref[0, 0].astype(BF16), preferred_element_type=F32)
    u = jnp.dot(x, wu_ref[0, 0].astype(BF16), preferred_element_type=F32)
    hid = (g * jax.nn.sigmoid(g)) * u
    y = jnp.dot(hid.astype(BF16), wd_ref[0, 0].astype(BF16), preferred_element_type=F32)
    o_ref[...] = y * w_ref[...]


def moe_expert_blocks(blk_e, row_tok, x, row_w, layer_idx, w_e_gate, w_e_up, w_e_down):
    n_rows = row_tok.shape[0]
    dm = x.shape[1]
    n_blocks = n_rows // MOE_BLOCK
    ff = w_e_gate.shape[-1]
    row_spec = lambda width: pl.BlockSpec((MOE_BLOCK, width), lambda i, be, rt: (i, 0))
    expert_spec = lambda rows, cols: pl.BlockSpec((1, 1, rows, cols), lambda i, be, rt: (layer_idx, be[i], 0, 0))
    return pl.pallas_call(
        _moe_expert_kernel,
        grid_spec=pltpu.PrefetchScalarGridSpec(
            num_scalar_prefetch=2,
            grid=(n_blocks,),
            in_specs=[pl.BlockSpec(memory_space=pl.ANY), row_spec(1),
                      expert_spec(dm, ff), expert_spec(dm, ff), expert_spec(ff, dm)],
            out_specs=row_spec(dm),
            scratch_shapes=[pltpu.VMEM((2, MOE_BLOCK, dm), x.dtype), pltpu.SemaphoreType.DMA((2,))]),
        out_shape=jax.ShapeDtypeStruct((n_rows, dm), F32),
        compiler_params=pltpu.CompilerParams(dimension_semantics=("arbitrary",),
                                             vmem_limit_bytes=MOE_VMEM_LIMIT),
        name="moe_experts",
    )(blk_e, row_tok, x, row_w, w_e_gate, w_e_up, w_e_down)


def layer(h_ctx, h_lat, c, c_ctx, lp, router_w, router_b, need_ctx):
    b, s, d = h_lat.shape
    l = h_ctx.shape[1]
    m_lat = jnp.split((jax.nn.silu(c) @ lp['w_ada'] + lp['b_ada'])[:, None, :], 6, axis=-1)
    m_ctx = jnp.split(jax.nn.silu(c_ctx) @ lp['w_ada'] + lp['b_ada'], 6, axis=-1)
    u_lat = adaln(h_lat, lp['g_norm1'], m_lat[0], m_lat[1])
    u_ctx = adaln(h_ctx, lp['g_norm1'], m_ctx[0], m_ctx[1])
    mix_ctx, mix_lat = token_mixers(u_ctx, u_lat, lp, need_ctx)
    h_lat = h_lat + m_lat[2] * mix_lat
    f_lat = adaln(h_lat, lp['g_norm2'], m_lat[3], m_lat[4]).reshape(b * s, d)
    if not need_ctx:
        ffn = moe_ffn(f_lat, router_w, router_b, lp['layer_idx'], *lp['experts'])
        return h_ctx, h_lat + m_lat[5] * ffn.reshape(b, s, d)
    h_ctx = h_ctx + m_ctx[2] * mix_ctx
    f_ctx = adaln(h_ctx, lp['g_norm2'], m_ctx[3], m_ctx[4]).reshape(b * l, d)
    ffn = moe_ffn(jnp.concatenate([f_ctx, f_lat], axis=0), router_w, router_b, lp['layer_idx'], *lp['experts'])
    h_ctx = h_ctx + m_ctx[5] * ffn[:b * l].reshape(b, l, d)
    h_lat = h_lat + m_lat[5] * ffn[b * l:].reshape(b, s, d)
    return h_ctx, h_lat


def _final_norm_kernel(x_ref, g_ref, o_ref):
    x = x_ref[...]
    y = x * lax.rsqrt(jnp.mean(x * x, axis=-1, keepdims=True) + NORM_EPS)
    o_ref[...] = y * g_ref[...]


def final_rms_norm(x, g):
    b, s, d = x.shape
    rows = b * s
    tile = 1024
    out = pl.pallas_call(
        _final_norm_kernel,
        grid=(rows // tile,),
        in_specs=[pl.BlockSpec((tile, d), lambda i: (i, 0)), pl.BlockSpec((1, d), lambda i: (0, 0))],
        out_specs=pl.BlockSpec((tile, d), lambda i: (i, 0)),
        out_shape=jax.ShapeDtypeStruct((rows, d), x.dtype),
        name="final_rms_norm",
    )(x.reshape(rows, d), g.reshape(1, d))
    return out.reshape(b, s, d)


_LAYER_PARAM_NAMES = (
    'w_ada', 'b_ada', 'g_norm1', 'g_norm2', 'w_in', 'sink', 'g_qnorm', 'g_kvnorm', 'w_uq',
    'w_ukv', 'rwkv_mu', 'rwkv_w0', 'rwkv_w_up', 'rwkv_a0', 'rwkv_a_up', 'rwkv_g_up',
    'rwkv_k_k', 'rwkv_k_a', 'rwkv_r_k', 'rwkv_lnx_w', 'rwkv_lnx_b', 'w_gate', 'b_gate',
    'w_branch', 'w_out')


def kernel(x, c, ctx, c_ctx, w_ada, b_ada, g_norm1, g_norm2, w_in, sink, g_qnorm, g_kvnorm,
           w_uq, w_ukv, rwkv_mu, rwkv_w0, rwkv_w_up, rwkv_a0, rwkv_a_up, rwkv_g_up, rwkv_k_k,
           rwkv_k_a, rwkv_r_k, rwkv_lnx_w, rwkv_lnx_b, w_gate, b_gate, w_branch, w_out,
           router_w, router_b, w_e_gate, w_e_up, w_e_down, g_final):
    stacked = (w_ada, b_ada, g_norm1, g_norm2, w_in, sink, g_qnorm, g_kvnorm, w_uq, w_ukv, rwkv_mu,
               rwkv_w0, rwkv_w_up, rwkv_a0, rwkv_a_up, rwkv_g_up, rwkv_k_k, rwkv_k_a, rwkv_r_k,
               rwkv_lnx_w, rwkv_lnx_b, w_gate, b_gate, w_branch, w_out)
    h_ctx, h_lat = ctx, x
    for i in range(DEPTH):
        lp = {n: a[i] for n, a in zip(_LAYER_PARAM_NAMES, stacked)}
        lp['layer_idx'] = i
        lp['experts'] = (w_e_gate, w_e_up, w_e_down)
        h_ctx, h_lat = layer(h_ctx, h_lat, c, c_ctx, lp, router_w, router_b, i < DEPTH - 1)
    return final_rms_norm(h_lat, g_final)
```

```python
import functools
import math

import jax
import jax.numpy as jnp
from jax import lax
from jax.experimental import pallas as pl
from jax.experimental.pallas import tpu as pltpu

D_MODEL = 1024
BATCH = 4
SEQ = 8192
DEPTH = 2

GRID_W = 64
CTX_LEN = 256
N_BRANCH = 4
BRANCH_W = D_MODEL // N_BRANCH
HEAD_DIM = 64
BRANCH_HEADS = BRANCH_W // HEAD_DIM
BLOCK = 128
ROPE_BASE = 10000.0
NORM_EPS = 1e-6
NEG_INF = -1e30
ATT_Q_HEADS = BRANCH_HEADS
ATT_KV_HEADS = BRANCH_HEADS // 2
WINDOW = 128
RET_HEADS = BRANCH_HEADS
RET_CHUNK = 128
RET_GN_EPS = 1e-5
MLA_HEADS = BRANCH_HEADS
MLA_Q_RANK = 256
MLA_KV_RANK = 128
MLA_NOPE = HEAD_DIM
MLA_ROPE = HEAD_DIM // 2
MLA_V = HEAD_DIM
RWKV_HEADS = BRANCH_HEADS
RWKV_DECAY_RANK = 64
RWKV_A_RANK = 64
RWKV_GATE_RANK = 128
RWKV_GN_EPS = 64e-5
N_EXPERTS = 64
N_GROUPS = 8
EXPERTS_PER_GROUP = N_EXPERTS // N_GROUPS
TOP_K = 2
EXPERT_FF = 512
MOE_BLOCK = 256
A_SIZES = (ATT_Q_HEADS * HEAD_DIM, ATT_KV_HEADS * HEAD_DIM, ATT_KV_HEADS * HEAD_DIM)
RET_SIZES = (BRANCH_W,) * 5
MLA_SIZES = (MLA_Q_RANK, MLA_KV_RANK, MLA_ROPE)
RWKV_SIZES = (BRANCH_W, BRANCH_W, BRANCH_W, RWKV_DECAY_RANK, RWKV_DECAY_RANK,
              RWKV_A_RANK, RWKV_A_RANK, RWKV_GATE_RANK)
GROUP_SIZES = (sum(A_SIZES), sum(RET_SIZES), sum(MLA_SIZES), sum(RWKV_SIZES))
N_IN = sum(GROUP_SIZES)
F32 = jnp.float32


def split_sizes(x, sizes):
    out, o = [], 0
    for n in sizes:
        out.append(x[..., o:o + n])
        o += n
    return out


def heads(x, h):
    return x.reshape(x.shape[:-1] + (h, x.shape[-1] // h))


def rms_norm(x, g):
    xf = x.astype(F32)
    y = xf * lax.rsqrt(jnp.mean(xf * xf, axis=-1, keepdims=True) + NORM_EPS)
    return (y * g.astype(F32)).astype(x.dtype)


def head_norm(x, eps):
    xf = x.astype(F32)
    xc = xf - jnp.mean(xf, axis=-1, keepdims=True)
    return xc * lax.rsqrt(jnp.mean(xc * xc, axis=-1, keepdims=True) + eps)


def adaln(h, g, shift, scale):
    return rms_norm(h, g) * (1 + scale) + shift


def rope_tables(pos, dim):
    inv = ROPE_BASE ** (-jnp.arange(0, dim, 2, dtype=F32) / dim)
    ang = pos[:, None] * inv[None, :]
    return jnp.cos(ang), jnp.sin(ang)


def rope_rotate(x, cos, sin):
    m = x.shape[-1] // 2
    x1, x2 = x[..., :m], x[..., m:]
    c = cos[None, :, None, :]
    s = sin[None, :, None, :]
    return jnp.concatenate([x1 * c - x2 * s, x2 * c + x1 * s], axis=-1).astype(x.dtype)


def axial_tables(seq, dim):
    rows = seq // GRID_W
    row = jnp.repeat(jnp.arange(rows, dtype=F32), GRID_W)
    col = jnp.broadcast_to(jnp.arange(GRID_W, dtype=F32)[None, :], (rows, GRID_W)).reshape(-1)
    return rope_tables(row, dim // 2), rope_tables(col, dim // 2)


def axial_rope(x, tabs):
    (cos_r, sin_r), (cos_c, sin_c) = tabs
    half = x.shape[-1] // 2
    return jnp.concatenate([rope_rotate(x[..., :half], cos_r, sin_r),
                            rope_rotate(x[..., half:], cos_c, sin_c)], axis=-1)


def window_gqa(q, k, v, k_ctx, v_ctx, sink):
    b, s, hq, d = q.shape
    g = hq // ATT_KV_HEADS
    nb = s // BLOCK
    nw = 3 * BLOCK
    nc = k_ctx.shape[1]
    qb = (q * d ** -0.5).reshape(b, nb, BLOCK, ATT_KV_HEADS, g, d)
    pad = ((0, 0), (BLOCK, BLOCK), (0, 0), (0, 0))
    kp = jnp.pad(k, pad).reshape(b, nb + 2, BLOCK, ATT_KV_HEADS, d)
    vp = jnp.pad(v, pad).reshape(b, nb + 2, BLOCK, ATT_KV_HEADS, d)
    kw = jnp.concatenate([kp[:, :-2], kp[:, 1:-1], kp[:, 2:]], axis=2)
    vw = jnp.concatenate([vp[:, :-2], vp[:, 1:-1], vp[:, 2:]], axis=2)
    blk0 = jnp.arange(nb)[:, None, None] * BLOCK
    q_pos = blk0 + jnp.arange(BLOCK)[None, :, None]
    k_pos = blk0 - BLOCK + jnp.arange(nw)[None, None, :]
    valid = (jnp.abs(k_pos - q_pos) <= WINDOW) & (k_pos >= 0) & (k_pos < s)
    s_win = jnp.einsum('bnqhgd,bnkhd->bnhgqk', qb, kw).astype(F32)
    s_win = jnp.where(valid[None, :, None, None], s_win, NEG_INF)
    s_ctx = jnp.einsum('bnqhgd,bchd->bnhgqc', qb, k_ctx).astype(F32)
    s_sink = jnp.broadcast_to(sink.astype(F32).reshape(ATT_KV_HEADS, g, 1, 1), s_win.shape[:-1] + (1,))
    p = jax.nn.softmax(jnp.concatenate([s_win, s_ctx, s_sink], axis=-1), axis=-1).astype(v.dtype)
    o = (jnp.einsum('bnhgqk,bnkhd->bnqhgd', p[..., :nw], vw)
         + jnp.einsum('bnhgqc,bchd->bnqhgd', p[..., nw:nw + nc], v_ctx))
    return o.reshape(b, s, hq * d)


def ctx_gqa(q, k, v, sink):
    b, l, hq, d = q.shape
    g = hq // ATT_KV_HEADS
    qg = (q * d ** -0.5).reshape(b, l, ATT_KV_HEADS, g, d)
    sc = jnp.einsum('bqhgd,bkhd->bhgqk', qg, k).astype(F32)
    s_sink = jnp.broadcast_to(sink.astype(F32).reshape(ATT_KV_HEADS, g, 1, 1), sc.shape[:-1] + (1,))
    p = jax.nn.softmax(jnp.concatenate([sc, s_sink], axis=-1), axis=-1).astype(v.dtype)
    o = jnp.einsum('bhgqk,bkhd->bqhgd', p[..., :l], v)
    return o.reshape(b, l, hq * d)


def retention_log_gammas():
    lg = jnp.log(1.0 - jnp.exp(jnp.linspace(math.log(1.0 / 32), math.log(1.0 / 512), 2 * RET_HEADS, dtype=F32)))
    return lg[0::2], lg[1::2]


def retention_chunked(q, k, v, log_gamma, s0):
    b, t, h, d = q.shape
    n = t // RET_CHUNK
    cl = RET_CHUNK
    qc = q.reshape(b, n, cl, h, d).astype(F32)
    kc = k.reshape(b, n, cl, h, d).astype(F32)
    vc = v.reshape(b, n, cl, h, d).astype(F32)
    idx = jnp.arange(cl, dtype=F32)
    rel = idx[:, None] - idx[None, :]
    d_intra = jnp.where(rel >= 0, jnp.exp(jnp.maximum(rel, 0.0)[None] * log_gamma[:, None, None]), 0.0)
    a_int = jnp.einsum('bnihd,bnjhd->bnhij', qc, kc) * d_intra
    o = jnp.einsum('bnhij,bnjhe->bnihe', a_int, vc)
    k_dec = kc * jnp.exp((cl - 1 - idx)[:, None] * log_gamma[None, :])[..., None]
    kv = jnp.einsum('bnjhd,bnjhe->nbhde', k_dec, vc)
    g_chunk = jnp.exp(cl * log_gamma)[:, None, None]

    def step(state, kv_n):
        return state * g_chunk + kv_n, state

    s_fin, s_prev = lax.scan(step, s0, kv)
    q_dec = qc * jnp.exp((idx + 1)[:, None] * log_gamma[None, :])[..., None]
    o = o + jnp.einsum('bnihd,nbhde->bnihe', q_dec, s_prev)
    return o.reshape(b, t, h, d), s_fin


RET_TOKENS_PER_STEP = 1024


def _retention_kernel(q_ref, k_ref, v_ref, g_ref, d_ref, kdec_ref, qdec_ref, gch_ref, s0_ref,
                      y_ref, sfin_ref, s_scr, *, reverse, n_chunks):
    cl = RET_CHUNK
    nh, tb, hd = q_ref.shape[1:]

    @pl.when(pl.program_id(1) == 0)
    def _():
        s_scr[...] = s0_ref[0]

    nb = nh * n_chunks
    shp = (nb, cl, hd)
    per_chunk = lambda z: jnp.broadcast_to(z[:, None], (nh, n_chunks) + z.shape[1:]).reshape((nb,) + z.shape[1:])
    q = q_ref[0].reshape(shp)
    k = k_ref[0].reshape(shp)
    v = v_ref[0].reshape(shp)
    a = _dot(q, k, _NT, 1) * per_chunk(d_ref[...])
    o = _dot(a, v, _NN, 1)
    kv = _dot(k * per_chunk(kdec_ref[...]), v, _TN, 1).reshape(nh, n_chunks, hd, hd)

    s = s_scr[...]
    gch = gch_ref[...]
    starts = [None] * n_chunks
    order = range(n_chunks - 1, -1, -1) if reverse else range(n_chunks)
    for n in order:
        starts[n] = s
        s = s * gch + kv[:, n]
    s_scr[...] = s
    sfin_ref[0] = s
    s_in = jnp.stack(starts, axis=1).reshape(nb, hd, hd)
    o = o + _dot(q * per_chunk(qdec_ref[...]), s_in, _NN, 1)

    oc = o - jnp.mean(o, axis=-1, keepdims=True)
    normed = oc * lax.rsqrt(jnp.mean(oc * oc, axis=-1, keepdims=True) + RET_GN_EPS)
    g = g_ref[0].reshape(shp)
    y_ref[0] = ((g * jax.nn.sigmoid(g)) * normed).reshape(nh, tb, hd)


def retention_direction(q, k, v, gate, log_gamma, s0, reverse):
    bsz, nh, t, hd = q.shape
    tb = min(RET_TOKENS_PER_STEP, t)
    nt = t // tb
    assert t == nt * tb and tb % RET_CHUNK == 0
    cl = RET_CHUNK
    idx = jnp.arange(cl, dtype=F32)
    rel = (idx[None, :] - idx[:, None]) if reverse else (idx[:, None] - idx[None, :])
    lg = log_gamma[:, None, None]
    d_intra = jnp.where(rel >= 0, jnp.exp(jnp.maximum(rel, 0.0)[None] * lg), 0.0)
    kdec = jnp.exp((idx if reverse else cl - 1 - idx)[None, :, None] * lg)
    qdec = jnp.exp((cl - idx if reverse else idx + 1)[None, :, None] * lg)
    gch = jnp.exp(cl * lg)
    if reverse:
        tmap = lambda bi, j: (bi, 0, nt - 1 - j, 0)
    else:
        tmap = lambda bi, j: (bi, 0, j, 0)
    seq_spec = pl.BlockSpec((1, nh, tb, hd), tmap)
    st_spec = pl.BlockSpec((1, nh, hd, hd), lambda bi, j: (bi, 0, 0, 0))
    const = lambda shape: pl.BlockSpec(shape, lambda bi, j: (0,) * len(shape))
    y, sfin = pl.pallas_call(
        functools.partial(_retention_kernel, reverse=reverse, n_chunks=tb // cl),
        grid=(bsz, nt),
        in_specs=[seq_spec] * 4 + [const((nh, cl, cl)), const((nh, cl, 1)), const((nh, cl, 1)),
                                   const((nh, 1, 1)), st_spec],
        out_specs=[seq_spec, st_spec],
        out_shape=[jax.ShapeDtypeStruct((bsz, nh, t, hd), F32),
                   jax.ShapeDtypeStruct((bsz, nh, hd, hd), F32)],
        scratch_shapes=[pltpu.VMEM((nh, hd, hd), F32)],
        compiler_params=pltpu.CompilerParams(dimension_semantics=("parallel", "arbitrary"),
                                             vmem_limit_bytes=RWKV_VMEM_LIMIT),
        name="retention_rev" if reverse else "retention_fwd",
    )(q, k, v, gate, d_intra, kdec, qdec, gch, s0)
    return y, sfin


def retention_mix(pb, cos, sin, lg_f, lg_b, s0f, s0b):
    bsz, t = pb.shape[:2]
    q, k, v, gf, gb = split_sizes(pb, RET_SIZES)
    tr = lambda z: jnp.swapaxes(z, 1, 2)
    q = tr(rope_rotate(heads(q, RET_HEADS), cos, sin))
    k = tr(rope_rotate(heads(k, RET_HEADS), cos, sin) * HEAD_DIM ** -0.5)
    v = tr(heads(v, RET_HEADS))
    yf, sf = retention_direction(q, k, v, tr(heads(gf, RET_HEADS)), lg_f, s0f, False)
    yb, sb = retention_direction(q, k, v, tr(heads(gb, RET_HEADS)), lg_b, s0b, True)
    out = tr(yf + yb).reshape(bsz, t, -1)
    return out.astype(pb.dtype), sf, sb


def mla_project(pc, lp):
    cq, ckv, kr = split_sizes(pc, MLA_SIZES)
    q = heads(rms_norm(cq, lp['g_qnorm']) @ lp['w_uq'], MLA_HEADS)
    kv = heads(rms_norm(ckv, lp['g_kvnorm']) @ lp['w_ukv'], MLA_HEADS)
    return q[..., :MLA_NOPE], q[..., MLA_NOPE:], kv[..., :MLA_NOPE], kr[:, :, None, :], kv[..., MLA_NOPE:]


LANES = 128
MLA_Q_TILE = 512
MLA_KEY_UNIT = 256
MLA_KEY_TILE_MAX = 8448
MLA_VMEM_LIMIT = 56 * 1024 * 1024


def _mla_flash_kernel(q_ref, k_ref, v_ref, o_ref, m_scr, acc_scr, *, scale, tk, n_tiles):
    dv = o_ref.shape[-1]
    q = (q_ref[0, 0] * scale).astype(BF16)
    m_scr[...] = jnp.full_like(m_scr, NEG_INF)
    acc_scr[...] = jnp.zeros_like(acc_scr)

    def sweep(c, carry):
        keys = pl.ds(pl.multiple_of(c * tk, tk), tk)
        s = lax.dot_general(q, k_ref[0, 0, keys, :], (((1,), (1,)), ((), ())), preferred_element_type=F32)
        m_prev = m_scr[...]
        m_new = jnp.maximum(m_prev, jnp.max(s, axis=-1, keepdims=True))
        p = jnp.exp2(s - m_new).astype(BF16)
        acc_scr[...] = (jnp.exp2(m_prev - m_new) * acc_scr[...]
                        + jnp.dot(p, v_ref[0, 0, keys, :], preferred_element_type=F32))
        m_scr[...] = m_new
        return carry

    lax.fori_loop(0, n_tiles, sweep, 0)
    acc = acc_scr[...]
    o_ref[0, 0] = acc[:, :dv] / acc[:, dv:dv + 1]


def mla_attend(qn, qr, kn, kr, v):
    b, t, h, _ = qn.shape
    n = kn.shape[1]
    dv = v.shape[-1]
    scale = (MLA_NOPE + MLA_ROPE) ** -0.5 * math.log2(math.e)
    q = jnp.swapaxes(jnp.concatenate([qn, qr], axis=-1), 1, 2)
    kr_h = jnp.broadcast_to(kr, (b, n, h, MLA_ROPE))
    k = jnp.swapaxes(jnp.concatenate([kn, kr_h], axis=-1), 1, 2).astype(BF16)
    ones = jnp.ones((b, n, h, 1), v.dtype)
    zeros = jnp.zeros((b, n, h, LANES - dv - 1), v.dtype)
    vt = jnp.swapaxes(jnp.concatenate([v, ones, zeros], axis=-1), 1, 2).astype(BF16)
    dq = q.shape[-1]
    tq = min(MLA_Q_TILE, t)
    tk = max(d for d in range(MLA_KEY_UNIT, MLA_KEY_TILE_MAX + 1, MLA_KEY_UNIT) if n % d == 0)
    assert t % tq == 0
    o = pl.pallas_call(
        functools.partial(_mla_flash_kernel, scale=scale, tk=tk, n_tiles=n // tk),
        grid=(b, h, t // tq),
        in_specs=[pl.BlockSpec((1, 1, tq, dq), lambda bi, hi, qi: (bi, hi, qi, 0)),
                  pl.BlockSpec((1, 1, n, dq), lambda bi, hi, qi: (bi, hi, 0, 0)),
                  pl.BlockSpec((1, 1, n, LANES), lambda bi, hi, qi: (bi, hi, 0, 0))],
        out_specs=pl.BlockSpec((1, 1, tq, dv), lambda bi, hi, qi: (bi, hi, qi, 0)),
        out_shape=jax.ShapeDtypeStruct((b, h, t, dv), F32),
        scratch_shapes=[pltpu.VMEM((tq, 1), F32), pltpu.VMEM((tq, LANES), F32)],
        compiler_params=pltpu.CompilerParams(
            dimension_semantics=("parallel", "parallel", "arbitrary"),
            vmem_limit_bytes=MLA_VMEM_LIMIT),
        name="mla_flash",
    )(q, k, vt)
    return jnp.swapaxes(o, 1, 2).reshape(b, t, h * dv)


def centred_shift(p):
    prev = jnp.pad(p, ((0, 0), (1, 0), (0, 0)))[:, :-1]
    nxt = jnp.pad(p, ((0, 0), (0, 1), (0, 0)))[:, 1:]
    return 0.5 * (prev + nxt)


def rwkv_features(pd, lp):
    z = pd + (centred_shift(pd) - pd) * lp['rwkv_mu']
    zr, zk, zv, zwf, zwb, zaf, zab, zg = split_sizes(z, RWKV_SIZES)
    r = heads(zr, RWKV_HEADS)
    k = heads(zk, RWKV_HEADS)
    v = heads(zv, RWKV_HEADS)
    g = jax.nn.sigmoid(zg) @ lp['rwkv_g_up']
    kk = (k * heads(lp['rwkv_k_k'], RWKV_HEADS)).astype(F32)
    kk = kk / jnp.maximum(jnp.sqrt(jnp.sum(kk * kk, axis=-1, keepdims=True)), 1e-12)
    k_a = heads(lp['rwkv_k_a'], RWKV_HEADS)
    dirs = []
    for di, (zw, za) in enumerate(((zwf, zaf), (zwb, zab))):
        logw = -jax.nn.softplus(-(lp['rwkv_w0'][di] + jnp.tanh(zw) @ lp['rwkv_w_up'][di]).astype(F32)) - 0.5
        w = heads(-jnp.exp(logw), RWKV_HEADS)
        a = heads(jax.nn.sigmoid((lp['rwkv_a0'][di] + za @ lp['rwkv_a_up'][di]).astype(F32)), RWKV_HEADS)
        dirs.append((w, k * (1 + (a - 1) * k_a), kk * a))
    return r, v, g, kk, dirs


def rwkv_scan(r, w, k, v, kk, b_, s0, reverse):
    xs = tuple(jnp.moveaxis(z.astype(F32), 1, 0) for z in (r, w, k, v, kk, b_))

    def step(st, inp):
        r_t, w_t, k_t, v_t, kk_t, b_t = inp
        sa = jnp.einsum('bhvk,bhk->bhv', st, -kk_t)
        st = st * w_t[:, :, None, :] + sa[..., None] * b_t[:, :, None, :] + v_t[..., None] * k_t[:, :, None, :]
        return st, jnp.einsum('bhvk,bhk->bhv', st, r_t)

    s_fin, y = lax.scan(step, s0, xs, reverse=reverse)
    return jnp.moveaxis(y, 0, 1), s_fin


BF16 = jnp.bfloat16
RWKV_CHUNK = 64
RWKV_SUB = 16
RWKV_TOKENS_PER_STEP = 512
RWKV_PASSES = (1, 1)
RWKV_VMEM_LIMIT = 48 * 1024 * 1024
_NN = (((2,), (1,)), ((0,), (0,)))
_NT = (((2,), (2,)), ((0,), (0,)))
_TN = (((1,), (1,)), ((0,), (0,)))


def _bf16_parts(a, n):
    parts, rem = [], a
    for i in range(n):
        p = rem.astype(BF16)
        parts.append(p)
        if i + 1 < n:
            rem = rem - p.astype(F32)
    return parts


def _dot(a, b, dims, passes):
    if passes == 1:
        return lax.dot_general(a.astype(BF16), b.astype(BF16), dims, preferred_element_type=F32)
    a_hi, a_lo = _bf16_parts(a, 2)
    b_hi, b_lo = _bf16_parts(b, 2)
    out = lax.dot_general(a_hi, b_hi, dims, preferred_element_type=F32)
    out = out + lax.dot_general(a_hi, b_lo, dims, preferred_element_type=F32)
    return out + lax.dot_general(a_lo, b_hi, dims, preferred_element_type=F32)


def _unit_lower_inverse(l_mat, same_sub, passes):
    mm = lambda a, b: _dot(a, b, _NN, passes)
    ld = jnp.where(same_sub, l_mat, 0.0)
    lo = l_mat - ld
    p2 = mm(ld, ld)
    d = p2 - ld - mm(ld, p2)
    pw = p2
    span = 4
    while span < RWKV_SUB:
        pw = mm(pw, pw)
        d = d + pw + mm(d, pw)
        span *= 2
    n1 = lo + mm(d, lo)
    n2 = mm(n1, n1)
    x = n2 - n1 - mm(n1, n2)
    span = 4
    pw = n2
    while span < RWKV_CHUNK // RWKV_SUB:
        pw = mm(pw, pw)
        x = x + pw + mm(x, pw)
        span *= 2
    return x + d + mm(x, d)


def _rwkv_scan_kernel(r_ref, lw_ref, k_ref, v_ref, kk_ref, b_ref, s0_ref, y_ref, sfin_ref, s_scr,
                      *, reverse, n_chunks):
    cl = RWKV_CHUNK
    nh, tb, hd = r_ref.shape[1:]

    @pl.when(pl.program_id(1) == 0)
    def _():
        s_scr[...] = s0_ref[0]

    nb = nh * n_chunks
    shp = (nb, cl, hd)
    r = r_ref[0].reshape(shp)
    lw = lw_ref[0].reshape(shp)
    k = k_ref[0].reshape(shp)
    v = v_ref[0].reshape(shp)
    kk = kk_ref[0].reshape(shp)
    b = b_ref[0].reshape(shp)

    row = lax.broadcasted_iota(jnp.int32, (cl, cl), 0)
    col = lax.broadcasted_iota(jnp.int32, (cl, cl), 1)
    before = (col > row) if reverse else (col < row)
    upto = before | (col == row)
    same_sub = (row // RWKV_SUB) == (col // RWKV_SUB)

    tri = jnp.broadcast_to(jnp.where(upto, 1.0, 0.0).astype(BF16)[None], (nb, cl, cl))
    cum = sum(lax.dot_general(tri, p, _NN, preferred_element_type=F32) for p in _bf16_parts(lw, 3))
    last = 0 if reverse else cl - 1
    mid = cl // 2 if reverse else cl // 2 - 1
    tot = cum[:, last:last + 1, :]
    rho = cum[:, mid:mid + 1, :]
    cum_ex = cum - lw

    a_t = kk * jnp.exp(cum_ex - rho)
    r_t = r * jnp.exp(cum - rho)
    e_out = jnp.exp(rho - cum)
    b_t = b * e_out
    k_t = k * e_out
    a_0 = kk * jnp.exp(cum_ex)
    r_0 = r * jnp.exp(cum)
    e_end = jnp.exp(tot - cum)
    b_e = b * e_end
    k_e = k * e_end

    ps, pt = RWKV_PASSES
    l_mat = jnp.where(before, _dot(a_t, b_t, _NT, ps), 0.0)
    m_ak = jnp.where(before, _dot(a_t, k_t, _NT, ps), 0.0)
    m_rb = jnp.where(upto, _dot(r_t, b_t, _NT, ps), 0.0)
    m_rk = jnp.where(upto, _dot(r_t, k_t, _NT, ps), 0.0)
    t_m1 = _unit_lower_inverse(l_mat, same_sub, pt)

    mv = _dot(m_ak, v, _NN, ps)
    q = -(a_0 + _dot(t_m1, a_0, _NN, ps))
    w = -(mv + _dot(t_m1, mv, _NN, ps))
    r_h = r_0 + _dot(m_rb, q, _NN, ps)
    y_loc = _dot(m_rb, w, _NN, ps) + _dot(m_rk, v, _NN, ps)
    eye = lax.broadcasted_iota(jnp.int32, (hd, hd), 0) == lax.broadcasted_iota(jnp.int32, (hd, hd), 1)
    phi = (jnp.where(eye, jnp.exp(tot), 0.0) + _dot(b_e, q, _TN, ps)).reshape(nh, n_chunks, hd, hd)
    psi = (_dot(b_e, w, _TN, ps) + _dot(k_e, v, _TN, ps)).reshape(nh, n_chunks, hd, hd)

    s = s_scr[...]
    starts = [None] * n_chunks
    order = range(n_chunks - 1, -1, -1) if reverse else range(n_chunks)
    for n in order:
        starts[n] = s
        s = _dot(phi[:, n], s, _NN, 3) + psi[:, n]
    s_scr[...] = s
    sfin_ref[0] = s
    s_in = jnp.stack(starts, axis=1).reshape(nb, hd, hd)
    y = _dot(r_h, s_in, _NN, ps) + y_loc
    y_ref[0] = y.reshape(nh, tb, hd)


def rwkv_scan_blocked(r, lw, k, v, kk, b_, s0t, reverse):
    bsz, nh, t, hd = r.shape
    tb = min(RWKV_TOKENS_PER_STEP, t)
    nt = t // tb
    assert t == nt * tb and tb % RWKV_CHUNK == 0
    if reverse:
        tmap = lambda bi, j: (bi, 0, nt - 1 - j, 0)
    else:
        tmap = lambda bi, j: (bi, 0, j, 0)
    seq_spec = pl.BlockSpec((1, nh, tb, hd), tmap)
    st_spec = pl.BlockSpec((1, nh, hd, hd), lambda bi, j: (bi, 0, 0, 0))
    y, sfin = pl.pallas_call(
        functools.partial(_rwkv_scan_kernel, reverse=reverse, n_chunks=tb // RWKV_CHUNK),
        grid=(bsz, nt),
        in_specs=[seq_spec] * 6 + [st_spec],
        out_specs=[seq_spec, st_spec],
        out_shape=[jax.ShapeDtypeStruct((bsz, nh, t, hd), F32),
                   jax.ShapeDtypeStruct((bsz, nh, hd, hd), F32)],
        scratch_shapes=[pltpu.VMEM((nh, hd, hd), F32)],
        compiler_params=pltpu.CompilerParams(dimension_semantics=("parallel", "arbitrary"),
                                             vmem_limit_bytes=RWKV_VMEM_LIMIT),
        name="rwkv_scan_rev" if reverse else "rwkv_scan_fwd",
    )(r, lw, k, v, kk, b_, s0t)
    return y, sfin


def rwkv_mix(feat, lp, s0f, s0b):
    r, v, g, kk, dirs = feat
    (lwf, kf, bf), (lwb, kb, bb) = dirs
    tr = lambda z: jnp.swapaxes(z.astype(F32), 1, 2)
    rt, vt, kkt = tr(r), tr(v), tr(kk)
    yf, sf = rwkv_scan_blocked(rt, tr(lwf), tr(kf), vt, kkt, tr(bf), s0f, False)
    yb, sb = rwkv_scan_blocked(rt, tr(lwb), tr(kb), vt, kkt, tr(bb), s0b, True)
    yf = jnp.swapaxes(yf, 1, 2)
    yb = jnp.swapaxes(yb, 1, 2)
    y = head_norm(yf + yb, RWKV_GN_EPS) * heads(lp['rwkv_lnx_w'], RWKV_HEADS) + heads(lp['rwkv_lnx_b'], RWKV_HEADS)
    bonus = jnp.sum(r * (kf + kb) * lp['rwkv_r_k'], axis=-1, keepdims=True) * v
    out = (y + bonus).reshape(g.shape) * g
    return out.astype(g.dtype), sf, sb


MERGE_ROW_TILE = 512
MERGE_VMEM_LIMIT = 52 * 1024 * 1024


def _merge_kernel(u_ref, ya_ref, yb_ref, yc_ref, yd_ref, wg_ref, bg_ref, wb_ref, wo_ref, o_ref):
    u = u_ref[...].astype(BF16)
    acc = None
    for n, y_ref in enumerate((ya_ref, yb_ref, yc_ref, yd_ref)):
        gate = jax.nn.sigmoid(jnp.dot(u, wg_ref[n], preferred_element_type=F32) + bg_ref[n])
        term = gate * jnp.dot(y_ref[...].astype(BF16), wb_ref[n], preferred_element_type=F32)
        acc = term if acc is None else acc + term
    o_ref[...] = jnp.dot(acc.astype(BF16), wo_ref[...], preferred_element_type=F32)


def merge_branches(u, ys, lp):
    lead, d = u.shape[:-1], u.shape[-1]
    rows = math.prod(lead)
    tm = min(MERGE_ROW_TILE, rows)
    assert rows % tm == 0
    bw = ys[0].shape[-1]
    row_spec = lambda width: pl.BlockSpec((tm, width), lambda i: (i, 0))
    whole = lambda shape: pl.BlockSpec(shape, lambda i: (0,) * len(shape))
    out = pl.pallas_call(
        _merge_kernel,
        grid=(rows // tm,),
        in_specs=[row_spec(d)] + [row_spec(bw)] * N_BRANCH
                 + [whole((N_BRANCH, d, d)), whole((N_BRANCH, 1, d)), whole((N_BRANCH, bw, d)), whole((d, d))],
        out_specs=row_spec(d),
        out_shape=jax.ShapeDtypeStruct((rows, d), F32),
        compiler_params=pltpu.CompilerParams(dimension_semantics=("parallel",),
                                             vmem_limit_bytes=MERGE_VMEM_LIMIT),
        name="merge_branches",
    )(u.reshape(rows, d), *[y.reshape(rows, bw) for y in ys],
      lp['w_gate'].astype(BF16), lp['b_gate'][:, None, :], lp['w_branch'].astype(BF16), lp['w_out'].astype(BF16))
    return out.reshape(lead + (d,))


def token_mixers(u_ctx, u_lat, lp, need_ctx):
    b, s, _ = u_lat.shape
    l = u_ctx.shape[1]
    w_groups = split_sizes(lp['w_in'], GROUP_SIZES)
    pa_l, pb_l, pc_l, pd_l = [u_lat @ w for w in w_groups]
    pa_c, pb_c, pc_c, pd_c = [u_ctx @ w for w in w_groups]
    tab_att = axial_tables(s, HEAD_DIM)
    tab_mla = axial_tables(s, MLA_ROPE)

    aq_l, ak_l, av_l = split_sizes(pa_l, A_SIZES)
    aq_c, ak_c, av_c = split_sizes(pa_c, A_SIZES)
    k_c = heads(ak_c, ATT_KV_HEADS)
    v_c = heads(av_c, ATT_KV_HEADS)
    ya_l = window_gqa(axial_rope(heads(aq_l, ATT_Q_HEADS), tab_att),
                      axial_rope(heads(ak_l, ATT_KV_HEADS), tab_att),
                      heads(av_l, ATT_KV_HEADS), k_c, v_c, lp['sink'])

    lg_f, lg_b = retention_log_gammas()
    cos_c, sin_c = rope_tables(jnp.arange(l, dtype=F32), HEAD_DIM)
    cos_l, sin_l = rope_tables(l + jnp.arange(s, dtype=F32), HEAD_DIM)
    zero_ret = jnp.zeros((b, RET_HEADS, HEAD_DIM, HEAD_DIM), F32)
    yb_c, sbf, sbb = retention_mix(pb_c, cos_c, sin_c, lg_f, lg_b, zero_ret, zero_ret)
    yb_l, _, _ = retention_mix(pb_l, cos_l, sin_l, lg_f, lg_b, sbf, sbb)

    qn_l, qr_l, kn_l, kr_l, vm_l = mla_project(pc_l, lp)
    qn_c, qr_c, kn_c, kr_c, vm_c = mla_project(pc_c, lp)
    qr_l = axial_rope(qr_l, tab_mla)
    kr_l = axial_rope(kr_l, tab_mla)
    yc_l = mla_attend(qn_l, qr_l, jnp.concatenate([kn_c, kn_l], axis=1),
                      jnp.concatenate([kr_c, kr_l], axis=1), jnp.concatenate([vm_c, vm_l], axis=1))

    zero_wkv = jnp.zeros((b, RWKV_HEADS, HEAD_DIM, HEAD_DIM), F32)
    yd_c, sdf, sdb = rwkv_mix(rwkv_features(pd_c, lp), lp, zero_wkv, zero_wkv)
    yd_l, _, _ = rwkv_mix(rwkv_features(pd_l, lp), lp, sdf, sdb)

    m_lat = merge_branches(u_lat, (ya_l, yb_l, yc_l, yd_l), lp)
    if not need_ctx:
        return None, m_lat
    ya_c = ctx_gqa(heads(aq_c, ATT_Q_HEADS), k_c, v_c, lp['sink'])
    yc_c = mla_attend(qn_c, qr_c, kn_c, kr_c, vm_c)
    m_ctx = merge_branches(u_ctx, (ya_c, yb_c, yc_c, yd_c), lp)
    return m_ctx, m_lat


def moe_ffn(x, router_w, router_b, layer_idx, w_e_gate, w_e_up, w_e_down):
    n_tok, dm = x.shape
    scores = jax.nn.sigmoid(x.astype(F32) @ router_w.astype(F32))
    grp = (scores + router_b.astype(F32)).reshape(n_tok, N_GROUPS, EXPERTS_PER_GROUP)
    g_sel = jnp.argmax(jnp.sum(lax.top_k(grp, TOP_K)[0], axis=-1), axis=-1)
    in_grp = lax.top_k(jnp.take_along_axis(grp, g_sel[:, None, None], axis=1)[:, 0], TOP_K)[1]
    e_idx = g_sel[:, None] * EXPERTS_PER_GROUP + in_grp
    w_sel = jnp.take_along_axis(scores, e_idx, axis=1)
    w_sel = w_sel / jnp.sum(w_sel, axis=-1, keepdims=True)
    n_asg = n_tok * TOP_K
    flat_e = e_idx.reshape(-1).astype(jnp.int32)
    flat_w = w_sel.reshape(-1)
    order = jnp.argsort(flat_e).astype(jnp.int32)
    rank = jnp.argsort(order).astype(jnp.int32)
    onehot = flat_e[:, None] == jnp.arange(N_EXPERTS, dtype=jnp.int32)[None, :]
    counts = jnp.sum(onehot, axis=0, dtype=jnp.int32)
    padded = (counts + MOE_BLOCK - 1) // MOE_BLOCK * MOE_BLOCK
    pad_end = jnp.cumsum(padded)
    pad_start = pad_end - padded
    start = jnp.cumsum(counts) - counts
    n_blocks = -(-n_asg // MOE_BLOCK) + N_EXPERTS
    n_rows = n_blocks * MOE_BLOCK
    blk_row0 = jnp.arange(n_blocks, dtype=jnp.int32) * MOE_BLOCK
    blk_e = jnp.minimum(jnp.sum(pad_end[None, :] <= blk_row0[:, None], axis=1, dtype=jnp.int32), N_EXPERTS - 1)
    off = (blk_row0 - pad_start[blk_e])[:, None] + jnp.arange(MOE_BLOCK, dtype=jnp.int32)[None, :]
    valid = (off < counts[blk_e][:, None]).reshape(-1)
    src = order[jnp.clip(start[blk_e][:, None] + off, 0, n_asg - 1).reshape(-1)]
    row_tok = jnp.where(valid, src // TOP_K, 0)
    row_w = jnp.where(valid, flat_w[src], 0.0)
    yb = moe_expert_blocks(blk_e, row_tok, x, row_w[:, None], layer_idx, w_e_gate, w_e_up, w_e_down)
    shift = jnp.sum(jnp.where(onehot, (pad_start - start)[None, :], 0), axis=1, dtype=jnp.int32)
    slot = (rank + shift).reshape(n_tok, TOP_K)
    out = yb[slot[:, 0]]
    for j in range(1, TOP_K):
        out = out + yb[slot[:, j]]
    return out.astype(x.dtype)


MOE_VMEM_LIMIT = 40 * 1024 * 1024
MOE_GATHER_UNROLL = 8


def _moe_row_copy(x_hbm, xbuf, sem, tok, slot, r):
    return pltpu.make_async_copy(x_hbm.at[pl.ds(tok, 1)], xbuf.at[slot, pl.ds(r, 1)], sem.at[slot])


def _moe_expert_kernel(blk_e_ref, row_tok_ref, x_hbm, w_ref, wg_ref, wu_ref, wd_ref, o_ref, xbuf, sem):
    del blk_e_ref
    i = pl.program_id(0)
    slot = i % 2

    def issue(block, dst_slot):
        def body(r, carry):
            _moe_row_copy(x_hbm, xbuf, sem, row_tok_ref[block * MOE_BLOCK + r], dst_slot, r).start()
            return carry
        lax.fori_loop(0, MOE_BLOCK, body, 0, unroll=MOE_GATHER_UNROLL)

    @pl.when(i == 0)
    def _():
        issue(0, 0)

    @pl.when(i + 1 < pl.num_programs(0))
    def _():
        issue(i + 1, 1 - slot)

    def wait_row(r, carry):
        _moe_row_copy(x_hbm, xbuf, sem, 0, slot, r).wait()
        return carry
    lax.fori_loop(0, MOE_BLOCK, wait_row, 0, unroll=MOE_GATHER_UNROLL)

    x = xbuf[slot].astype(BF16)
    g = jnp.dot(x, wg_ref[0, 0].astype(BF16), preferred_element_type=F32)
    u = jnp.dot(x, wu_ref[0, 0].astype(BF16), preferred_element_type=F32)
    hid = (g * jax.nn.sigmoid(g)) * u
    y = jnp.dot(hid.astype(BF16), wd_ref[0, 0].astype(BF16), preferred_element_type=F32)
    o_ref[...] = y * w_ref[...]


def moe_expert_blocks(blk_e, row_tok, x, row_w, layer_idx, w_e_gate, w_e_up, w_e_down):
    n_rows = row_tok.shape[0]
    dm = x.shape[1]
    n_blocks = n_rows // MOE_BLOCK
    ff = w_e_gate.shape[-1]
    row_spec = lambda width: pl.BlockSpec((MOE_BLOCK, width), lambda i, be, rt: (i, 0))
    expert_spec = lambda rows, cols: pl.BlockSpec((1, 1, rows, cols), lambda i, be, rt: (layer_idx, be[i], 0, 0))
    return pl.pallas_call(
        _moe_expert_kernel,
        grid_spec=pltpu.PrefetchScalarGridSpec(
            num_scalar_prefetch=2,
            grid=(n_blocks,),
            in_specs=[pl.BlockSpec(memory_space=pl.ANY), row_spec(1),
                      expert_spec(dm, ff), expert_spec(dm, ff), expert_spec(ff, dm)],
            out_specs=row_spec(dm),
            scratch_shapes=[pltpu.VMEM((2, MOE_BLOCK, dm), x.dtype), pltpu.SemaphoreType.DMA((2,))]),
        out_shape=jax.ShapeDtypeStruct((n_rows, dm), F32),
        compiler_params=pltpu.CompilerParams(dimension_semantics=("arbitrary",),
                                             vmem_limit_bytes=MOE_VMEM_LIMIT),
        name="moe_experts",
    )(blk_e, row_tok, x, row_w, w_e_gate, w_e_up, w_e_down)


def layer(h_ctx, h_lat, c, c_ctx, lp, router_w, router_b, need_ctx):
    b, s, d = h_lat.shape
    l = h_ctx.shape[1]
    m_lat = jnp.split((jax.nn.silu(c) @ lp['w_ada'] + lp['b_ada'])[:, None, :], 6, axis=-1)
    m_ctx = jnp.split(jax.nn.silu(c_ctx) @ lp['w_ada'] + lp['b_ada'], 6, axis=-1)
    u_lat = adaln(h_lat, lp['g_norm1'], m_lat[0], m_lat[1])
    u_ctx = adaln(h_ctx, lp['g_norm1'], m_ctx[0], m_ctx[1])
    mix_ctx, mix_lat = token_mixers(u_ctx, u_lat, lp, need_ctx)
    h_lat = h_lat + m_lat[2] * mix_lat
    f_lat = adaln(h_lat, lp['g_norm2'], m_lat[3], m_lat[4]).reshape(b * s, d)
    if not need_ctx:
        ffn = moe_ffn(f_lat, router_w, router_b, lp['layer_idx'], *lp['experts'])
        return h_ctx, h_lat + m_lat[5] * ffn.reshape(b, s, d)
    h_ctx = h_ctx + m_ctx[2] * mix_ctx
    f_ctx = adaln(h_ctx, lp['g_norm2'], m_ctx[3], m_ctx[4]).reshape(b * l, d)
    ffn = moe_ffn(jnp.concatenate([f_ctx, f_lat], axis=0), router_w, router_b, lp['layer_idx'], *lp['experts'])
    h_ctx = h_ctx + m_ctx[5] * ffn[:b * l].reshape(b, l, d)
    h_lat = h_lat + m_lat[5] * ffn[b * l:].reshape(b, s, d)
    return h_ctx, h_lat


def _final_norm_kernel(x_ref, g_ref, o_ref):
    x = x_ref[...]
    y = x * lax.rsqrt(jnp.mean(x * x, axis=-1, keepdims=True) + NORM_EPS)
    o_ref[...] = y * g_ref[...]


def final_rms_norm(x, g):
    b, s, d = x.shape
    rows = b * s
    tile = 1024
    out = pl.pallas_call(
        _final_norm_kernel,
        grid=(rows // tile,),
        in_specs=[pl.BlockSpec((tile, d), lambda i: (i, 0)), pl.BlockSpec((1, d), lambda i: (0, 0))],
        out_specs=pl.BlockSpec((tile, d), lambda i: (i, 0)),
        out_shape=jax.ShapeDtypeStruct((rows, d), x.dtype),
        name="final_rms_norm",
    )(x.reshape(rows, d), g.reshape(1, d))
    return out.reshape(b, s, d)


_LAYER_PARAM_NAMES = (
    'w_ada', 'b_ada', 'g_norm1', 'g_norm2', 'w_in', 'sink', 'g_qnorm', 'g_kvnorm', 'w_uq',
    'w_ukv', 'rwkv_mu', 'rwkv_w0', 'rwkv_w_up', 'rwkv_a0', 'rwkv_a_up', 'rwkv_g_up',
    'rwkv_k_k', 'rwkv_k_a', 'rwkv_r_k', 'rwkv_lnx_w', 'rwkv_lnx_b', 'w_gate', 'b_gate',
    'w_branch', 'w_out')


def kernel(x, c, ctx, c_ctx, w_ada, b_ada, g_norm1, g_norm2, w_in, sink, g_qnorm, g_kvnorm,
           w_uq, w_ukv, rwkv_mu, rwkv_w0, rwkv_w_up, rwkv_a0, rwkv_a_up, rwkv_g_up, rwkv_k_k,
           rwkv_k_a, rwkv_r_k, rwkv_lnx_w, rwkv_lnx_b, w_gate, b_gate, w_branch, w_out,
           router_w, router_b, w_e_gate, w_e_up, w_e_down, g_final):
    stacked = (w_ada, b_ada, g_norm1, g_norm2, w_in, sink, g_qnorm, g_kvnorm, w_uq, w_ukv, rwkv_mu,
               rwkv_w0, rwkv_w_up, rwkv_a0, rwkv_a_up, rwkv_g_up, rwkv_k_k, rwkv_k_a, rwkv_r_k,
               rwkv_lnx_w, rwkv_lnx_b, w_gate, b_gate, w_branch, w_out)
    h_ctx, h_lat = ctx, x
    for i in range(DEPTH):
        lp = {n: a[i] for n, a in zip(_LAYER_PARAM_NAMES, stacked)}
        lp['layer_idx'] = i
        lp['experts'] = (w_e_gate, w_e_up, w_e_down)
        h_ctx, h_lat = layer(h_ctx, h_lat, c, c_ctx, lp, router_w, router_b, i < DEPTH - 1)
    return final_rms_norm(h_lat, g_final)
```

```python
import functools
import math

import jax
import jax.numpy as jnp
from jax import lax
from jax.experimental import pallas as pl
from jax.experimental.pallas import tpu as pltpu

D_MODEL = 1024
BATCH = 4
SEQ = 8192
DEPTH = 2

GRID_W = 64
CTX_LEN = 256
N_BRANCH = 4
BRANCH_W = D_MODEL // N_BRANCH
HEAD_DIM = 64
BRANCH_HEADS = BRANCH_W // HEAD_DIM
BLOCK = 128
ROPE_BASE = 10000.0
NORM_EPS = 1e-6
NEG_INF = -1e30
ATT_Q_HEADS = BRANCH_HEADS
ATT_KV_HEADS = BRANCH_HEADS // 2
WINDOW = 128
RET_HEADS = BRANCH_HEADS
RET_CHUNK = 128
RET_GN_EPS = 1e-5
MLA_HEADS = BRANCH_HEADS
MLA_Q_RANK = 256
MLA_KV_RANK = 128
MLA_NOPE = HEAD_DIM
MLA_ROPE = HEAD_DIM // 2
MLA_V = HEAD_DIM
RWKV_HEADS = BRANCH_HEADS
RWKV_DECAY_RANK = 64
RWKV_A_RANK = 64
RWKV_GATE_RANK = 128
RWKV_GN_EPS = 64e-5
N_EXPERTS = 64
N_GROUPS = 8
EXPERTS_PER_GROUP = N_EXPERTS // N_GROUPS
TOP_K = 2
EXPERT_FF = 512
MOE_BLOCK = 256
A_SIZES = (ATT_Q_HEADS * HEAD_DIM, ATT_KV_HEADS * HEAD_DIM, ATT_KV_HEADS * HEAD_DIM)
RET_SIZES = (BRANCH_W,) * 5
MLA_SIZES = (MLA_Q_RANK, MLA_KV_RANK, MLA_ROPE)
RWKV_SIZES = (BRANCH_W, BRANCH_W, BRANCH_W, RWKV_DECAY_RANK, RWKV_DECAY_RANK,
              RWKV_A_RANK, RWKV_A_RANK, RWKV_GATE_RANK)
GROUP_SIZES = (sum(A_SIZES), sum(RET_SIZES), sum(MLA_SIZES), sum(RWKV_SIZES))
N_IN = sum(GROUP_SIZES)
F32 = jnp.float32


def split_sizes(x, sizes):
    out, o = [], 0
    for n in sizes:
        out.append(x[..., o:o + n])
        o += n
    return out


def heads(x, h):
    return x.reshape(x.shape[:-1] + (h, x.shape[-1] // h))


def rms_norm(x, g):
    xf = x.astype(F32)
    y = xf * lax.rsqrt(jnp.mean(xf * xf, axis=-1, keepdims=True) + NORM_EPS)
    return (y * g.astype(F32)).astype(x.dtype)


def head_norm(x, eps):
    xf = x.astype(F32)
    xc = xf - jnp.mean(xf, axis=-1, keepdims=True)
    return xc * lax.rsqrt(jnp.mean(xc * xc, axis=-1, keepdims=True) + eps)


def adaln(h, g, shift, scale):
    return rms_norm(h, g) * (1 + scale) + shift


def rope_tables(pos, dim):
    inv = ROPE_BASE ** (-jnp.arange(0, dim, 2, dtype=F32) / dim)
    ang = pos[:, None] * inv[None, :]
    return jnp.cos(ang), jnp.sin(ang)


def rope_rotate(x, cos, sin):
    m = x.shape[-1] // 2
    x1, x2 = x[..., :m], x[..., m:]
    c = cos[None, :, None, :]
    s = sin[None, :, None, :]
    return jnp.concatenate([x1 * c - x2 * s, x2 * c + x1 * s], axis=-1).astype(x.dtype)


def axial_tables(seq, dim):
    rows = seq // GRID_W
    row = jnp.repeat(jnp.arange(rows, dtype=F32), GRID_W)
    col = jnp.broadcast_to(jnp.arange(GRID_W, dtype=F32)[None, :], (rows, GRID_W)).reshape(-1)
    return rope_tables(row, dim // 2), rope_tables(col, dim // 2)


def axial_rope(x, tabs):
    (cos_r, sin_r), (cos_c, sin_c) = tabs
    half = x.shape[-1] // 2
    return jnp.concatenate([rope_rotate(x[..., :half], cos_r, sin_r),
                            rope_rotate(x[..., half:], cos_c, sin_c)], axis=-1)


def window_gqa(q, k, v, k_ctx, v_ctx, sink):
    b, s, hq, d = q.shape
    g = hq // ATT_KV_HEADS
    nb = s // BLOCK
    nw = 3 * BLOCK
    nc = k_ctx.shape[1]
    qb = (q * d ** -0.5).reshape(b, nb, BLOCK, ATT_KV_HEADS, g, d)
    pad = ((0, 0), (BLOCK, BLOCK), (0, 0), (0, 0))
    kp = jnp.pad(k, pad).reshape(b, nb + 2, BLOCK, ATT_KV_HEADS, d)
    vp = jnp.pad(v, pad).reshape(b, nb + 2, BLOCK, ATT_KV_HEADS, d)
    kw = jnp.concatenate([kp[:, :-2], kp[:, 1:-1], kp[:, 2:]], axis=2)
    vw = jnp.concatenate([vp[:, :-2], vp[:, 1:-1], vp[:, 2:]], axis=2)
    blk0 = jnp.arange(nb)[:, None, None] * BLOCK
    q_pos = blk0 + jnp.arange(BLOCK)[None, :, None]
    k_pos = blk0 - BLOCK + jnp.arange(nw)[None, None, :]
    valid = (jnp.abs(k_pos - q_pos) <= WINDOW) & (k_pos >= 0) & (k_pos < s)
    s_win = jnp.einsum('bnqhgd,bnkhd->bnhgqk', qb, kw).astype(F32)
    s_win = jnp.where(valid[None, :, None, None], s_win, NEG_INF)
    s_ctx = jnp.einsum('bnqhgd,bchd->bnhgqc', qb, k_ctx).astype(F32)
    s_sink = jnp.broadcast_to(sink.astype(F32).reshape(ATT_KV_HEADS, g, 1, 1), s_win.shape[:-1] + (1,))
    p = jax.nn.softmax(jnp.concatenate([s_win, s_ctx, s_sink], axis=-1), axis=-1).astype(v.dtype)
    o = (jnp.einsum('bnhgqk,bnkhd->bnqhgd', p[..., :nw], vw)
         + jnp.einsum('bnhgqc,bchd->bnqhgd', p[..., nw:nw + nc], v_ctx))
    return o.reshape(b, s, hq * d)


def ctx_gqa(q, k, v, sink):
    b, l, hq, d = q.shape
    g = hq // ATT_KV_HEADS
    qg = (q * d ** -0.5).reshape(b, l, ATT_KV_HEADS, g, d)
    sc = jnp.einsum('bqhgd,bkhd->bhgqk', qg, k).astype(F32)
    s_sink = jnp.broadcast_to(sink.astype(F32).reshape(ATT_KV_HEADS, g, 1, 1), sc.shape[:-1] + (1,))
    p = jax.nn.softmax(jnp.concatenate([sc, s_sink], axis=-1), axis=-1).astype(v.dtype)
    o = jnp.einsum('bhgqk,bkhd->bqhgd', p[..., :l], v)
    return o.reshape(b, l, hq * d)


def retention_log_gammas():
    lg = jnp.log(1.0 - jnp.exp(jnp.linspace(math.log(1.0 / 32), math.log(1.0 / 512), 2 * RET_HEADS, dtype=F32)))
    return lg[0::2], lg[1::2]


def retention_chunked(q, k, v, log_gamma, s0):
    b, t, h, d = q.shape
    n = t // RET_CHUNK
    cl = RET_CHUNK
    qc = q.reshape(b, n, cl, h, d).astype(F32)
    kc = k.reshape(b, n, cl, h, d).astype(F32)
    vc = v.reshape(b, n, cl, h, d).astype(F32)
    idx = jnp.arange(cl, dtype=F32)
    rel = idx[:, None] - idx[None, :]
    d_intra = jnp.where(rel >= 0, jnp.exp(jnp.maximum(rel, 0.0)[None] * log_gamma[:, None, None]), 0.0)
    a_int = jnp.einsum('bnihd,bnjhd->bnhij', qc, kc) * d_intra
    o = jnp.einsum('bnhij,bnjhe->bnihe', a_int, vc)
    k_dec = kc * jnp.exp((cl - 1 - idx)[:, None] * log_gamma[None, :])[..., None]
    kv = jnp.einsum('bnjhd,bnjhe->nbhde', k_dec, vc)
    g_chunk = jnp.exp(cl * log_gamma)[:, None, None]

    def step(state, kv_n):
        return state * g_chunk + kv_n, state

    s_fin, s_prev = lax.scan(step, s0, kv)
    q_dec = qc * jnp.exp((idx + 1)[:, None] * log_gamma[None, :])[..., None]
    o = o + jnp.einsum('bnihd,nbhde->bnihe', q_dec, s_prev)
    return o.reshape(b, t, h, d), s_fin


RET_TOKENS_PER_STEP = 1024


def _retention_kernel(q_ref, k_ref, v_ref, g_ref, d_ref, kdec_ref, qdec_ref, gch_ref, s0_ref,
                      y_ref, sfin_ref, s_scr, *, reverse, n_chunks):
    cl = RET_CHUNK
    nh, tb, hd = q_ref.shape[1:]

    @pl.when(pl.program_id(1) == 0)
    def _():
        s_scr[...] = s0_ref[0]

    nb = nh * n_chunks
    shp = (nb, cl, hd)
    per_chunk = lambda z: jnp.broadcast_to(z[:, None], (nh, n_chunks) + z.shape[1:]).reshape((nb,) + z.shape[1:])
    q = q_ref[0].reshape(shp)
    k = k_ref[0].reshape(shp)
    v = v_ref[0].reshape(shp)
    a = _dot(q, k, _NT, 1) * per_chunk(d_ref[...])
    o = _dot(a, v, _NN, 1)
    kv = _dot(k * per_chunk(kdec_ref[...]), v, _TN, 1).reshape(nh, n_chunks, hd, hd)

    s = s_scr[...]
    gch = gch_ref[...]
    starts = [None] * n_chunks
    order = range(n_chunks - 1, -1, -1) if reverse else range(n_chunks)
    for n in order:
        starts[n] = s
        s = s * gch + kv[:, n]
    s_scr[...] = s
    sfin_ref[0] = s
    s_in = jnp.stack(starts, axis=1).reshape(nb, hd, hd)
    o = o + _dot(q * per_chunk(qdec_ref[...]), s_in, _NN, 1)

    oc = o - jnp.mean(o, axis=-1, keepdims=True)
    normed = oc * lax.rsqrt(jnp.mean(oc * oc, axis=-1, keepdims=True) + RET_GN_EPS)
    g = g_ref[0].reshape(shp)
    y_ref[0] = ((g * jax.nn.sigmoid(g)) * normed).reshape(nh, tb, hd)


def retention_direction(q, k, v, gate, log_gamma, s0, reverse):
    bsz, nh, t, hd = q.shape
    tb = min(RET_TOKENS_PER_STEP, t)
    nt = t // tb
    assert t == nt * tb and tb % RET_CHUNK == 0
    cl = RET_CHUNK
    idx = jnp.arange(cl, dtype=F32)
    rel = (idx[None, :] - idx[:, None]) if reverse else (idx[:, None] - idx[None, :])
    lg = log_gamma[:, None, None]
    d_intra = jnp.where(rel >= 0, jnp.exp(jnp.maximum(rel, 0.0)[None] * lg), 0.0)
    kdec = jnp.exp((idx if reverse else cl - 1 - idx)[None, :, None] * lg)
    qdec = jnp.exp((cl - idx if reverse else idx + 1)[None, :, None] * lg)
    gch = jnp.exp(cl * lg)
    if reverse:
        tmap = lambda bi, j: (bi, 0, nt - 1 - j, 0)
    else:
        tmap = lambda bi, j: (bi, 0, j, 0)
    seq_spec = pl.BlockSpec((1, nh, tb, hd), tmap)
    st_spec = pl.BlockSpec((1, nh, hd, hd), lambda bi, j: (bi, 0, 0, 0))
    const = lambda shape: pl.BlockSpec(shape, lambda bi, j: (0,) * len(shape))
    y, sfin = pl.pallas_call(
        functools.partial(_retention_kernel, reverse=reverse, n_chunks=tb // cl),
        grid=(bsz, nt),
        in_specs=[seq_spec] * 4 + [const((nh, cl, cl)), const((nh, cl, 1)), const((nh, cl, 1)),
                                   const((nh, 1, 1)), st_spec],
        out_specs=[seq_spec, st_spec],
        out_shape=[jax.ShapeDtypeStruct((bsz, nh, t, hd), F32),
                   jax.ShapeDtypeStruct((bsz, nh, hd, hd), F32)],
        scratch_shapes=[pltpu.VMEM((nh, hd, hd), F32)],
        compiler_params=pltpu.CompilerParams(dimension_semantics=("parallel", "arbitrary"),
                                             vmem_limit_bytes=RWKV_VMEM_LIMIT),
        name="retention_rev" if reverse else "retention_fwd",
    )(q, k, v, gate, d_intra, kdec, qdec, gch, s0)
    return y, sfin


def retention_mix(pb, cos, sin, lg_f, lg_b, s0f, s0b):
    bsz, t = pb.shape[:2]
    q, k, v, gf, gb = split_sizes(pb, RET_SIZES)
    tr = lambda z: jnp.swapaxes(z, 1, 2)
    q = tr(rope_rotate(heads(q, RET_HEADS), cos, sin))
    k = tr(rope_rotate(heads(k, RET_HEADS), cos, sin) * HEAD_DIM ** -0.5)
    v = tr(heads(v, RET_HEADS))
    yf, sf = retention_direction(q, k, v, tr(heads(gf, RET_HEADS)), lg_f, s0f, False)
    yb, sb = retention_direction(q, k, v, tr(heads(gb, RET_HEADS)), lg_b, s0b, True)
    out = tr(yf + yb).reshape(bsz, t, -1)
    return out.astype(pb.dtype), sf, sb


def mla_project(pc, lp):
    cq, ckv, kr = split_sizes(pc, MLA_SIZES)
    q = heads(rms_norm(cq, lp['g_qnorm']) @ lp['w_uq'], MLA_HEADS)
    kv = heads(rms_norm(ckv, lp['g_kvnorm']) @ lp['w_ukv'], MLA_HEADS)
    return q[..., :MLA_NOPE], q[..., MLA_NOPE:], kv[..., :MLA_NOPE], kr[:, :, None, :], kv[..., MLA_NOPE:]


LANES = 128
SUBLANES = 8
MLA_Q_TILE = 512
MLA_KEY_UNIT = 256
MLA_KEY_TILE_MAX = 8448
MLA_VMEM_LIMIT = 56 * 1024 * 1024


def _mla_flash_kernel(q_ref, k_ref, v_ref, o_ref, m_scr, acc_scr, *, scale, tk, n_tiles):
    dv = o_ref.shape[-1]
    q = (q_ref[0, 0] * scale).astype(BF16)
    m_scr[...] = jnp.full_like(m_scr, NEG_INF)
    acc_scr[...] = jnp.zeros_like(acc_scr)

    def sweep(c, carry):
        keys = pl.ds(pl.multiple_of(c * tk, tk), tk)
        s = lax.dot_general(q, k_ref[0, 0, keys, :], (((1,), (1,)), ((), ())), preferred_element_type=F32)
        m_prev = m_scr[...]
        m_new = jnp.maximum(m_prev, jnp.max(s, axis=-1, keepdims=True))
        p = jnp.exp2(s - m_new).astype(BF16)
        acc_scr[...] = (jnp.exp2(m_prev - m_new) * acc_scr[...]
                        + jnp.dot(p, v_ref[0, 0, keys, :], preferred_element_type=F32))
        m_scr[...] = m_new
        return carry

    lax.fori_loop(0, n_tiles, sweep, 0)
    acc = acc_scr[...]
    o_ref[0, 0] = acc[:, :dv] / acc[:, dv:dv + 1]


def mla_attend(qn, qr, kn, kr, v):
    b, t, h, _ = qn.shape
    n = kn.shape[1]
    dv = v.shape[-1]
    scale = (MLA_NOPE + MLA_ROPE) ** -0.5 * math.log2(math.e)
    q = jnp.swapaxes(jnp.concatenate([qn, qr], axis=-1), 1, 2)
    kr_h = jnp.broadcast_to(kr, (b, n, h, MLA_ROPE))
    k = jnp.swapaxes(jnp.concatenate([kn, kr_h], axis=-1), 1, 2).astype(BF16)
    ones = jnp.ones((b, n, h, 1), v.dtype)
    zeros = jnp.zeros((b, n, h, LANES - dv - 1), v.dtype)
    vt = jnp.swapaxes(jnp.concatenate([v, ones, zeros], axis=-1), 1, 2).astype(BF16)
    dq = q.shape[-1]
    tq = min(MLA_Q_TILE, t)
    tk = max(d for d in range(MLA_KEY_UNIT, MLA_KEY_TILE_MAX + 1, MLA_KEY_UNIT) if n % d == 0)
    assert t % tq == 0
    o = pl.pallas_call(
        functools.partial(_mla_flash_kernel, scale=scale, tk=tk, n_tiles=n // tk),
        grid=(b, h, t // tq),
        in_specs=[pl.BlockSpec((1, 1, tq, dq), lambda bi, hi, qi: (bi, hi, qi, 0)),
                  pl.BlockSpec((1, 1, n, dq), lambda bi, hi, qi: (bi, hi, 0, 0)),
                  pl.BlockSpec((1, 1, n, LANES), lambda bi, hi, qi: (bi, hi, 0, 0))],
        out_specs=pl.BlockSpec((1, 1, tq, dv), lambda bi, hi, qi: (bi, hi, qi, 0)),
        out_shape=jax.ShapeDtypeStruct((b, h, t, dv), F32),
        scratch_shapes=[pltpu.VMEM((tq, 1), F32), pltpu.VMEM((tq, LANES), F32)],
        compiler_params=pltpu.CompilerParams(
            dimension_semantics=("parallel", "parallel", "arbitrary"),
            vmem_limit_bytes=MLA_VMEM_LIMIT),
        name="mla_flash",
    )(q, k, vt)
    return jnp.swapaxes(o, 1, 2).reshape(b, t, h * dv)


def centred_shift(p):
    prev = jnp.pad(p, ((0, 0), (1, 0), (0, 0)))[:, :-1]
    nxt = jnp.pad(p, ((0, 0), (0, 1), (0, 0)))[:, 1:]
    return 0.5 * (prev + nxt)


def rwkv_features(pd, lp):
    z = pd + (centred_shift(pd) - pd) * lp['rwkv_mu']
    zr, zk, zv, zwf, zwb, zaf, zab, zg = split_sizes(z, RWKV_SIZES)
    r = heads(zr, RWKV_HEADS)
    k = heads(zk, RWKV_HEADS)
    v = heads(zv, RWKV_HEADS)
    g = jax.nn.sigmoid(zg) @ lp['rwkv_g_up']
    kk = (k * heads(lp['rwkv_k_k'], RWKV_HEADS)).astype(F32)
    kk = kk / jnp.maximum(jnp.sqrt(jnp.sum(kk * kk, axis=-1, keepdims=True)), 1e-12)
    k_a = heads(lp['rwkv_k_a'], RWKV_HEADS)
    dirs = []
    for di, (zw, za) in enumerate(((zwf, zaf), (zwb, zab))):
        logw = -jax.nn.softplus(-(lp['rwkv_w0'][di] + jnp.tanh(zw) @ lp['rwkv_w_up'][di]).astype(F32)) - 0.5
        w = heads(-jnp.exp(logw), RWKV_HEADS)
        a = heads(jax.nn.sigmoid((lp['rwkv_a0'][di] + za @ lp['rwkv_a_up'][di]).astype(F32)), RWKV_HEADS)
        dirs.append((w, k * (1 + (a - 1) * k_a), kk * a))
    return r, v, g, kk, dirs


def rwkv_scan(r, w, k, v, kk, b_, s0, reverse):
    xs = tuple(jnp.moveaxis(z.astype(F32), 1, 0) for z in (r, w, k, v, kk, b_))

    def step(st, inp):
        r_t, w_t, k_t, v_t, kk_t, b_t = inp
        sa = jnp.einsum('bhvk,bhk->bhv', st, -kk_t)
        st = st * w_t[:, :, None, :] + sa[..., None] * b_t[:, :, None, :] + v_t[..., None] * k_t[:, :, None, :]
        return st, jnp.einsum('bhvk,bhk->bhv', st, r_t)

    s_fin, y = lax.scan(step, s0, xs, reverse=reverse)
    return jnp.moveaxis(y, 0, 1), s_fin


BF16 = jnp.bfloat16
RWKV_CHUNK = 64
RWKV_SUB = 16
RWKV_TOKENS_PER_STEP = 512
RWKV_PASSES = (1, 1)
RWKV_VMEM_LIMIT = 48 * 1024 * 1024
_NN = (((2,), (1,)), ((0,), (0,)))
_NT = (((2,), (2,)), ((0,), (0,)))
_TN = (((1,), (1,)), ((0,), (0,)))


def _bf16_parts(a, n):
    parts, rem = [], a
    for i in range(n):
        p = rem.astype(BF16)
        parts.append(p)
        if i + 1 < n:
            rem = rem - p.astype(F32)
    return parts


def _dot(a, b, dims, passes):
    if passes == 1:
        return lax.dot_general(a.astype(BF16), b.astype(BF16), dims, preferred_element_type=F32)
    a_hi, a_lo = _bf16_parts(a, 2)
    b_hi, b_lo = _bf16_parts(b, 2)
    out = lax.dot_general(a_hi, b_hi, dims, preferred_element_type=F32)
    out = out + lax.dot_general(a_hi, b_lo, dims, preferred_element_type=F32)
    return out + lax.dot_general(a_lo, b_hi, dims, preferred_element_type=F32)


def _unit_lower_inverse(l_mat, same_sub, passes):
    mm = lambda a, b: _dot(a, b, _NN, passes)
    ld = jnp.where(same_sub, l_mat, 0.0)
    lo = l_mat - ld
    p2 = mm(ld, ld)
    d = p2 - ld - mm(ld, p2)
    pw = p2
    span = 4
    while span < RWKV_SUB:
        pw = mm(pw, pw)
        d = d + pw + mm(d, pw)
        span *= 2
    n1 = lo + mm(d, lo)
    n2 = mm(n1, n1)
    x = n2 - n1 - mm(n1, n2)
    span = 4
    pw = n2
    while span < RWKV_CHUNK // RWKV_SUB:
        pw = mm(pw, pw)
        x = x + pw + mm(x, pw)
        span *= 2
    return x + d + mm(x, d)


def _rwkv_scan_kernel(r_ref, lw_ref, k_ref, v_ref, kk_ref, b_ref, s0_ref, y_ref, sfin_ref, s_scr,
                      *, reverse, n_chunks):
    cl = RWKV_CHUNK
    nh, tb, hd = r_ref.shape[1:]

    @pl.when(pl.program_id(1) == 0)
    def _():
        s_scr[...] = s0_ref[0]

    nb = nh * n_chunks
    shp = (nb, cl, hd)
    r = r_ref[0].reshape(shp)
    lw = lw_ref[0].reshape(shp)
    k = k_ref[0].reshape(shp)
    v = v_ref[0].reshape(shp)
    kk = kk_ref[0].reshape(shp)
    b = b_ref[0].reshape(shp)

    row = lax.broadcasted_iota(jnp.int32, (cl, cl), 0)
    col = lax.broadcasted_iota(jnp.int32, (cl, cl), 1)
    before = (col > row) if reverse else (col < row)
    upto = before | (col == row)
    same_sub = (row // RWKV_SUB) == (col // RWKV_SUB)

    tri = jnp.broadcast_to(jnp.where(upto, 1.0, 0.0).astype(BF16)[None], (nb, cl, cl))
    cum = sum(lax.dot_general(tri, p, _NN, preferred_element_type=F32) for p in _bf16_parts(lw, 3))
    last = 0 if reverse else cl - 1
    mid = cl // 2 if reverse else cl // 2 - 1
    tot = cum[:, last:last + 1, :]
    rho = cum[:, mid:mid + 1, :]
    cum_ex = cum - lw

    a_t = kk * jnp.exp(cum_ex - rho)
    r_t = r * jnp.exp(cum - rho)
    e_out = jnp.exp(rho - cum)
    b_t = b * e_out
    k_t = k * e_out
    a_0 = kk * jnp.exp(cum_ex)
    r_0 = r * jnp.exp(cum)
    e_end = jnp.exp(tot - cum)
    b_e = b * e_end
    k_e = k * e_end

    ps, pt = RWKV_PASSES
    l_mat = jnp.where(before, _dot(a_t, b_t, _NT, ps), 0.0)
    m_ak = jnp.where(before, _dot(a_t, k_t, _NT, ps), 0.0)
    m_rb = jnp.where(upto, _dot(r_t, b_t, _NT, ps), 0.0)
    m_rk = jnp.where(upto, _dot(r_t, k_t, _NT, ps), 0.0)
    t_m1 = _unit_lower_inverse(l_mat, same_sub, pt)

    mv = _dot(m_ak, v, _NN, ps)
    q = -(a_0 + _dot(t_m1, a_0, _NN, ps))
    w = -(mv + _dot(t_m1, mv, _NN, ps))
    r_h = r_0 + _dot(m_rb, q, _NN, ps)
    y_loc = _dot(m_rb, w, _NN, ps) + _dot(m_rk, v, _NN, ps)
    eye = lax.broadcasted_iota(jnp.int32, (hd, hd), 0) == lax.broadcasted_iota(jnp.int32, (hd, hd), 1)
    phi = (jnp.where(eye, jnp.exp(tot), 0.0) + _dot(b_e, q, _TN, ps)).reshape(nh, n_chunks, hd, hd)
    psi = (_dot(b_e, w, _TN, ps) + _dot(k_e, v, _TN, ps)).reshape(nh, n_chunks, hd, hd)

    s = s_scr[...]
    starts = [None] * n_chunks
    order = range(n_chunks - 1, -1, -1) if reverse else range(n_chunks)
    for n in order:
        starts[n] = s
        s = _dot(phi[:, n], s, _NN, 3) + psi[:, n]
    s_scr[...] = s
    sfin_ref[0] = s
    s_in = jnp.stack(starts, axis=1).reshape(nb, hd, hd)
    y = _dot(r_h, s_in, _NN, ps) + y_loc
    y_ref[0] = y.reshape(nh, tb, hd)


def rwkv_scan_blocked(r, lw, k, v, kk, b_, s0t, reverse):
    bsz, nh, t, hd = r.shape
    tb = min(RWKV_TOKENS_PER_STEP, t)
    nt = t // tb
    assert t == nt * tb and tb % RWKV_CHUNK == 0
    if reverse:
        tmap = lambda bi, j: (bi, 0, nt - 1 - j, 0)
    else:
        tmap = lambda bi, j: (bi, 0, j, 0)
    seq_spec = pl.BlockSpec((1, nh, tb, hd), tmap)
    st_spec = pl.BlockSpec((1, nh, hd, hd), lambda bi, j: (bi, 0, 0, 0))
    y, sfin = pl.pallas_call(
        functools.partial(_rwkv_scan_kernel, reverse=reverse, n_chunks=tb // RWKV_CHUNK),
        grid=(bsz, nt),
        in_specs=[seq_spec] * 6 + [st_spec],
        out_specs=[seq_spec, st_spec],
        out_shape=[jax.ShapeDtypeStruct((bsz, nh, t, hd), F32),
                   jax.ShapeDtypeStruct((bsz, nh, hd, hd), F32)],
        scratch_shapes=[pltpu.VMEM((nh, hd, hd), F32)],
        compiler_params=pltpu.CompilerParams(dimension_semantics=("parallel", "arbitrary"),
                                             vmem_limit_bytes=RWKV_VMEM_LIMIT),
        name="rwkv_scan_rev" if reverse else "rwkv_scan_fwd",
    )(r, lw, k, v, kk, b_, s0t)
    return y, sfin


def rwkv_mix(feat, lp, s0f, s0b):
    r, v, g, kk, dirs = feat
    (lwf, kf, bf), (lwb, kb, bb) = dirs
    tr = lambda z: jnp.swapaxes(z.astype(F32), 1, 2)
    rt, vt, kkt = tr(r), tr(v), tr(kk)
    yf, sf = rwkv_scan_blocked(rt, tr(lwf), tr(kf), vt, kkt, tr(bf), s0f, False)
    yb, sb = rwkv_scan_blocked(rt, tr(lwb), tr(kb), vt, kkt, tr(bb), s0b, True)
    yf = jnp.swapaxes(yf, 1, 2)
    yb = jnp.swapaxes(yb, 1, 2)
    y = head_norm(yf + yb, RWKV_GN_EPS) * heads(lp['rwkv_lnx_w'], RWKV_HEADS) + heads(lp['rwkv_lnx_b'], RWKV_HEADS)
    bonus = jnp.sum(r * (kf + kb) * lp['rwkv_r_k'], axis=-1, keepdims=True) * v
    out = (y + bonus).reshape(g.shape) * g
    return out.astype(g.dtype), sf, sb


MERGE_ROW_TILE = 512
MERGE_VMEM_LIMIT = 52 * 1024 * 1024


def _merge_kernel(u_ref, ya_ref, yb_ref, yc_ref, yd_ref, wg_ref, bg_ref, wb_ref, wo_ref, o_ref):
    u = u_ref[...].astype(BF16)
    acc = None
    for n, y_ref in enumerate((ya_ref, yb_ref, yc_ref, yd_ref)):
        gate = jax.nn.sigmoid(jnp.dot(u, wg_ref[n], preferred_element_type=F32) + bg_ref[n])
        term = gate * jnp.dot(y_ref[...].astype(BF16), wb_ref[n], preferred_element_type=F32)
        acc = term if acc is None else acc + term
    o_ref[...] = jnp.dot(acc.astype(BF16), wo_ref[...], preferred_element_type=F32)


def merge_branches(u, ys, lp):
    lead, d = u.shape[:-1], u.shape[-1]
    rows = math.prod(lead)
    tm = min(MERGE_ROW_TILE, rows)
    assert rows % tm == 0
    bw = ys[0].shape[-1]
    row_spec = lambda width: pl.BlockSpec((tm, width), lambda i: (i, 0))
    whole = lambda shape: pl.BlockSpec(shape, lambda i: (0,) * len(shape))
    out = pl.pallas_call(
        _merge_kernel,
        grid=(rows // tm,),
        in_specs=[row_spec(d)] + [row_spec(bw)] * N_BRANCH
                 + [whole((N_BRANCH, d, d)), whole((N_BRANCH, 1, d)), whole((N_BRANCH, bw, d)), whole((d, d))],
        out_specs=row_spec(d),
        out_shape=jax.ShapeDtypeStruct((rows, d), F32),
        compiler_params=pltpu.CompilerParams(dimension_semantics=("parallel",),
                                             vmem_limit_bytes=MERGE_VMEM_LIMIT),
        name="merge_branches",
    )(u.reshape(rows, d), *[y.reshape(rows, bw) for y in ys],
      lp['w_gate'].astype(BF16), lp['b_gate'][:, None, :], lp['w_branch'].astype(BF16), lp['w_out'].astype(BF16))
    return out.reshape(lead + (d,))


def token_mixers(u_ctx, u_lat, lp, need_ctx):
    b, s, _ = u_lat.shape
    l = u_ctx.shape[1]
    w_groups = split_sizes(lp['w_in'], GROUP_SIZES)
    pa_l, pb_l, pc_l, pd_l = [u_lat @ w for w in w_groups]
    pa_c, pb_c, pc_c, pd_c = [u_ctx @ w for w in w_groups]
    tab_att = axial_tables(s, HEAD_DIM)
    tab_mla = axial_tables(s, MLA_ROPE)

    aq_l, ak_l, av_l = split_sizes(pa_l, A_SIZES)
    aq_c, ak_c, av_c = split_sizes(pa_c, A_SIZES)
    k_c = heads(ak_c, ATT_KV_HEADS)
    v_c = heads(av_c, ATT_KV_HEADS)
    ya_l = window_gqa(axial_rope(heads(aq_l, ATT_Q_HEADS), tab_att),
                      axial_rope(heads(ak_l, ATT_KV_HEADS), tab_att),
                      heads(av_l, ATT_KV_HEADS), k_c, v_c, lp['sink'])

    lg_f, lg_b = retention_log_gammas()
    cos_c, sin_c = rope_tables(jnp.arange(l, dtype=F32), HEAD_DIM)
    cos_l, sin_l = rope_tables(l + jnp.arange(s, dtype=F32), HEAD_DIM)
    zero_ret = jnp.zeros((b, RET_HEADS, HEAD_DIM, HEAD_DIM), F32)
    yb_c, sbf, sbb = retention_mix(pb_c, cos_c, sin_c, lg_f, lg_b, zero_ret, zero_ret)
    yb_l, _, _ = retention_mix(pb_l, cos_l, sin_l, lg_f, lg_b, sbf, sbb)

    qn_l, qr_l, kn_l, kr_l, vm_l = mla_project(pc_l, lp)
    qn_c, qr_c, kn_c, kr_c, vm_c = mla_project(pc_c, lp)
    qr_l = axial_rope(qr_l, tab_mla)
    kr_l = axial_rope(kr_l, tab_mla)
    yc_l = mla_attend(qn_l, qr_l, jnp.concatenate([kn_c, kn_l], axis=1),
                      jnp.concatenate([kr_c, kr_l], axis=1), jnp.concatenate([vm_c, vm_l], axis=1))

    zero_wkv = jnp.zeros((b, RWKV_HEADS, HEAD_DIM, HEAD_DIM), F32)
    yd_c, sdf, sdb = rwkv_mix(rwkv_features(pd_c, lp), lp, zero_wkv, zero_wkv)
    yd_l, _, _ = rwkv_mix(rwkv_features(pd_l, lp), lp, sdf, sdb)

    m_lat = merge_branches(u_lat, (ya_l, yb_l, yc_l, yd_l), lp)
    if not need_ctx:
        return None, m_lat
    ya_c = ctx_gqa(heads(aq_c, ATT_Q_HEADS), k_c, v_c, lp['sink'])
    yc_c = mla_attend(qn_c, qr_c, kn_c, kr_c, vm_c)
    m_ctx = merge_branches(u_ctx, (ya_c, yb_c, yc_c, yd_c), lp)
    return m_ctx, m_lat


def moe_ffn(x, router_w, router_b, layer_idx, w_e_gate, w_e_up, w_e_down):
    n_tok, dm = x.shape
    scores = jax.nn.sigmoid(x.astype(F32) @ router_w.astype(F32))
    grp = (scores + router_b.astype(F32)).reshape(n_tok, N_GROUPS, EXPERTS_PER_GROUP)
    g_sel = jnp.argmax(jnp.sum(lax.top_k(grp, TOP_K)[0], axis=-1), axis=-1)
    in_grp = lax.top_k(jnp.take_along_axis(grp, g_sel[:, None, None], axis=1)[:, 0], TOP_K)[1]
    e_idx = g_sel[:, None] * EXPERTS_PER_GROUP + in_grp
    w_sel = jnp.take_along_axis(scores, e_idx, axis=1)
    w_sel = w_sel / jnp.sum(w_sel, axis=-1, keepdims=True)
    n_asg = n_tok * TOP_K
    flat_e = e_idx.reshape(-1).astype(jnp.int32)
    flat_w = w_sel.reshape(-1)
    order = jnp.argsort(flat_e).astype(jnp.int32)
    rank = jnp.argsort(order).astype(jnp.int32)
    onehot = flat_e[:, None] == jnp.arange(N_EXPERTS, dtype=jnp.int32)[None, :]
    counts = jnp.sum(onehot, axis=0, dtype=jnp.int32)
    padded = (counts + MOE_BLOCK - 1) // MOE_BLOCK * MOE_BLOCK
    pad_end = jnp.cumsum(padded)
    pad_start = pad_end - padded
    start = jnp.cumsum(counts) - counts
    n_blocks = -(-n_asg // MOE_BLOCK) + N_EXPERTS
    n_rows = n_blocks * MOE_BLOCK
    blk_row0 = jnp.arange(n_blocks, dtype=jnp.int32) * MOE_BLOCK
    blk_e = jnp.minimum(jnp.sum(pad_end[None, :] <= blk_row0[:, None], axis=1, dtype=jnp.int32), N_EXPERTS - 1)
    off = (blk_row0 - pad_start[blk_e])[:, None] + jnp.arange(MOE_BLOCK, dtype=jnp.int32)[None, :]
    valid = (off < counts[blk_e][:, None]).reshape(-1)
    src = order[jnp.clip(start[blk_e][:, None] + off, 0, n_asg - 1).reshape(-1)]
    row_tok = jnp.where(valid, src // TOP_K, 0)
    row_w = jnp.where(valid, flat_w[src], 0.0)
    yb = moe_expert_blocks(blk_e, row_tok, x, row_w[:, None], layer_idx, w_e_gate, w_e_up, w_e_down)
    shift = jnp.sum(jnp.where(onehot, (pad_start - start)[None, :], 0), axis=1, dtype=jnp.int32)
    slot = (rank + shift).reshape(n_tok, TOP_K)
    out = yb[slot[:, 0]]
    for j in range(1, TOP_K):
        out = out + yb[slot[:, j]]
    return out.astype(x.dtype)


MOE_VMEM_LIMIT = 40 * 1024 * 1024
MOE_GATHER_UNROLL = 8


def _moe_row_copy(x_hbm, xbuf, sem, tok, slot, r):
    return pltpu.make_async_copy(x_hbm.at[pl.ds(tok, 1)], xbuf.at[slot, pl.ds(r, 1)], sem.at[slot])


def _moe_expert_kernel(blk_e_ref, row_tok_ref, x_hbm, w_ref, wg_ref, wu_ref, wd_ref, o_ref, xbuf, sem):
    del blk_e_ref
    i = pl.program_id(0)
    slot = i % 2

    def issue(block, dst_slot):
        def body(r, carry):
            _moe_row_copy(x_hbm, xbuf, sem, row_tok_ref[block * MOE_BLOCK + r], dst_slot, r).start()
            return carry
        lax.fori_loop(0, MOE_BLOCK, body, 0, unroll=MOE_GATHER_UNROLL)

    @pl.when(i == 0)
    def _():
        issue(0, 0)

    @pl.when(i + 1 < pl.num_programs(0))
    def _():
        issue(i + 1, 1 - slot)

    def wait_row(r, carry):
        _moe_row_copy(x_hbm, xbuf, sem, 0, slot, r).wait()
        return carry
    lax.fori_loop(0, MOE_BLOCK, wait_row, 0, unroll=MOE_GATHER_UNROLL)

    x = jnp.concatenate([xbuf[slot, :, j, :] for j in range(xbuf.shape[2])], axis=-1).astype(BF16)
    g = jnp.dot(x, wg_ref[0, 0].astype(BF16), preferred_element_type=F32)
    u = jnp.dot(x, wu_ref[0, 0].astype(BF16), preferred_element_type=F32)
    hid = (g * jax.nn.sigmoid(g)) * u
    y = jnp.dot(hid.astype(BF16), wd_ref[0, 0].astype(BF16), preferred_element_type=F32)
    o_ref[...] = y * w_ref[...]


def moe_expert_blocks(blk_e, row_tok, x, row_w, layer_idx, w_e_gate, w_e_up, w_e_down):
    n_rows = row_tok.shape[0]
    n_tok, dm = x.shape
    n_blocks = n_rows // MOE_BLOCK
    ff = w_e_gate.shape[-1]
    assert dm == SUBLANES * LANES
    x = x.reshape(n_tok, SUBLANES, LANES)
    row_spec = lambda width: pl.BlockSpec((MOE_BLOCK, width), lambda i, be, rt: (i, 0))
    expert_spec = lambda rows, cols: pl.BlockSpec((1, 1, rows, cols), lambda i, be, rt: (layer_idx, be[i], 0, 0))
    return pl.pallas_call(
        _moe_expert_kernel,
        grid_spec=pltpu.PrefetchScalarGridSpec(
            num_scalar_prefetch=2,
            grid=(n_blocks,),
            in_specs=[pl.BlockSpec(memory_space=pl.ANY), row_spec(1),
                      expert_spec(dm, ff), expert_spec(dm, ff), expert_spec(ff, dm)],
            out_specs=row_spec(dm),
            scratch_shapes=[pltpu.VMEM((2, MOE_BLOCK, SUBLANES, LANES), x.dtype), pltpu.SemaphoreType.DMA((2,))]),
        out_shape=jax.ShapeDtypeStruct((n_rows, dm), F32),
        compiler_params=pltpu.CompilerParams(dimension_semantics=("arbitrary",),
                                             vmem_limit_bytes=MOE_VMEM_LIMIT),
        name="moe_experts",
    )(blk_e, row_tok, x, row_w, w_e_gate, w_e_up, w_e_down)


def layer(h_ctx, h_lat, c, c_ctx, lp, router_w, router_b, need_ctx):
    b, s, d = h_lat.shape
    l = h_ctx.shape[1]
    m_lat = jnp.split((jax.nn.silu(c) @ lp['w_ada'] + lp['b_ada'])[:, None, :], 6, axis=-1)
    m_ctx = jnp.split(jax.nn.silu(c_ctx) @ lp['w_ada'] + lp['b_ada'], 6, axis=-1)
    u_lat = adaln(h_lat, lp['g_norm1'], m_lat[0], m_lat[1])
    u_ctx = adaln(h_ctx, lp['g_norm1'], m_ctx[0], m_ctx[1])
    mix_ctx, mix_lat = token_mixers(u_ctx, u_lat, lp, need_ctx)
    h_lat = h_lat + m_lat[2] * mix_lat
    f_lat = adaln(h_lat, lp['g_norm2'], m_lat[3], m_lat[4]).reshape(b * s, d)
    if not need_ctx:
        ffn = moe_ffn(f_lat, router_w, router_b, lp['layer_idx'], *lp['experts'])
        return h_ctx, h_lat + m_lat[5] * ffn.reshape(b, s, d)
    h_ctx = h_ctx + m_ctx[2] * mix_ctx
    f_ctx = adaln(h_ctx, lp['g_norm2'], m_ctx[3], m_ctx[4]).reshape(b * l, d)
    ffn = moe_ffn(jnp.concatenate([f_ctx, f_lat], axis=0), router_w, router_b, lp['layer_idx'], *lp['experts'])
    h_ctx = h_ctx + m_ctx[5] * ffn[:b * l].reshape(b, l, d)
    h_lat = h_lat + m_lat[5] * ffn[b * l:].reshape(b, s, d)
    return h_ctx, h_lat


def _final_norm_kernel(x_ref, g_ref, o_ref):
    x = x_ref[...]
    y = x * lax.rsqrt(jnp.mean(x * x, axis=-1, keepdims=True) + NORM_EPS)
    o_ref[...] = y * g_ref[...]


def final_rms_norm(x, g):
    b, s, d = x.shape
    rows = b * s
    tile = 1024
    out = pl.pallas_call(
        _final_norm_kernel,
        grid=(rows // tile,),
        in_specs=[pl.BlockSpec((tile, d), lambda i: (i, 0)), pl.BlockSpec((1, d), lambda i: (0, 0))],
        out_specs=pl.BlockSpec((tile, d), lambda i: (i, 0)),
        out_shape=jax.ShapeDtypeStruct((rows, d), x.dtype),
        name="final_rms_norm",
    )(x.reshape(rows, d), g.reshape(1, d))
    return out.reshape(b, s, d)


_LAYER_PARAM_NAMES = (
    'w_ada', 'b_ada', 'g_norm1', 'g_norm2', 'w_in', 'sink', 'g_qnorm', 'g_kvnorm', 'w_uq',
    'w_ukv', 'rwkv_mu', 'rwkv_w0', 'rwkv_w_up', 'rwkv_a0', 'rwkv_a_up', 'rwkv_g_up',
    'rwkv_k_k', 'rwkv_k_a', 'rwkv_r_k', 'rwkv_lnx_w', 'rwkv_lnx_b', 'w_gate', 'b_gate',
    'w_branch', 'w_out')


def kernel(x, c, ctx, c_ctx, w_ada, b_ada, g_norm1, g_norm2, w_in, sink, g_qnorm, g_kvnorm,
           w_uq, w_ukv, rwkv_mu, rwkv_w0, rwkv_w_up, rwkv_a0, rwkv_a_up, rwkv_g_up, rwkv_k_k,
           rwkv_k_a, rwkv_r_k, rwkv_lnx_w, rwkv_lnx_b, w_gate, b_gate, w_branch, w_out,
           router_w, router_b, w_e_gate, w_e_up, w_e_down, g_final):
    stacked = (w_ada, b_ada, g_norm1, g_norm2, w_in, sink, g_qnorm, g_kvnorm, w_uq, w_ukv, rwkv_mu,
               rwkv_w0, rwkv_w_up, rwkv_a0, rwkv_a_up, rwkv_g_up, rwkv_k_k, rwkv_k_a, rwkv_r_k,
               rwkv_lnx_w, rwkv_lnx_b, w_gate, b_gate, w_branch, w_out)
    h_ctx, h_lat = ctx, x
    for i in range(DEPTH):
        lp = {n: a[i] for n, a in zip(_LAYER_PARAM_NAMES, stacked)}
        lp['layer_idx'] = i
        lp['experts'] = (w_e_gate, w_e_up, w_e_down)
        h_ctx, h_lat = layer(h_ctx, h_lat, c, c_ctx, lp, router_w, router_b, i < DEPTH - 1)
    return final_rms_norm(h_lat, g_final)
```

```python
import functools
import math

import jax
import jax.numpy as jnp
from jax import lax
from jax.experimental import pallas as pl
from jax.experimental.pallas import tpu as pltpu

D_MODEL = 1024
BATCH = 4
SEQ = 8192
DEPTH = 2

GRID_W = 64
CTX_LEN = 256
N_BRANCH = 4
BRANCH_W = D_MODEL // N_BRANCH
HEAD_DIM = 64
BRANCH_HEADS = BRANCH_W // HEAD_DIM
BLOCK = 128
ROPE_BASE = 10000.0
NORM_EPS = 1e-6
NEG_INF = -1e30
ATT_Q_HEADS = BRANCH_HEADS
ATT_KV_HEADS = BRANCH_HEADS // 2
WINDOW = 128
RET_HEADS = BRANCH_HEADS
RET_CHUNK = 128
RET_GN_EPS = 1e-5
MLA_HEADS = BRANCH_HEADS
MLA_Q_RANK = 256
MLA_KV_RANK = 128
MLA_NOPE = HEAD_DIM
MLA_ROPE = HEAD_DIM // 2
MLA_V = HEAD_DIM
RWKV_HEADS = BRANCH_HEADS
RWKV_DECAY_RANK = 64
RWKV_A_RANK = 64
RWKV_GATE_RANK = 128
RWKV_GN_EPS = 64e-5
N_EXPERTS = 64
N_GROUPS = 8
EXPERTS_PER_GROUP = N_EXPERTS // N_GROUPS
TOP_K = 2
EXPERT_FF = 512
MOE_BLOCK = 256
A_SIZES = (ATT_Q_HEADS * HEAD_DIM, ATT_KV_HEADS * HEAD_DIM, ATT_KV_HEADS * HEAD_DIM)
RET_SIZES = (BRANCH_W,) * 5
MLA_SIZES = (MLA_Q_RANK, MLA_KV_RANK, MLA_ROPE)
RWKV_SIZES = (BRANCH_W, BRANCH_W, BRANCH_W, RWKV_DECAY_RANK, RWKV_DECAY_RANK,
              RWKV_A_RANK, RWKV_A_RANK, RWKV_GATE_RANK)
GROUP_SIZES = (sum(A_SIZES), sum(RET_SIZES), sum(MLA_SIZES), sum(RWKV_SIZES))
N_IN = sum(GROUP_SIZES)
F32 = jnp.float32


def split_sizes(x, sizes):
    out, o = [], 0
    for n in sizes:
        out.append(x[..., o:o + n])
        o += n
    return out


def heads(x, h):
    return x.reshape(x.shape[:-1] + (h, x.shape[-1] // h))


def rms_norm(x, g):
    xf = x.astype(F32)
    y = xf * lax.rsqrt(jnp.mean(xf * xf, axis=-1, keepdims=True) + NORM_EPS)
    return (y * g.astype(F32)).astype(x.dtype)


def head_norm(x, eps):
    xf = x.astype(F32)
    xc = xf - jnp.mean(xf, axis=-1, keepdims=True)
    return xc * lax.rsqrt(jnp.mean(xc * xc, axis=-1, keepdims=True) + eps)


def adaln(h, g, shift, scale):
    return rms_norm(h, g) * (1 + scale) + shift


def rope_tables(pos, dim):
    inv = ROPE_BASE ** (-jnp.arange(0, dim, 2, dtype=F32) / dim)
    ang = pos[:, None] * inv[None, :]
    return jnp.cos(ang), jnp.sin(ang)


def rope_rotate(x, cos, sin):
    m = x.shape[-1] // 2
    x1, x2 = x[..., :m], x[..., m:]
    c = cos[None, :, None, :]
    s = sin[None, :, None, :]
    return jnp.concatenate([x1 * c - x2 * s, x2 * c + x1 * s], axis=-1).astype(x.dtype)


def axial_tables(seq, dim):
    rows = seq // GRID_W
    row = jnp.repeat(jnp.arange(rows, dtype=F32), GRID_W)
    col = jnp.broadcast_to(jnp.arange(GRID_W, dtype=F32)[None, :], (rows, GRID_W)).reshape(-1)
    return rope_tables(row, dim // 2), rope_tables(col, dim // 2)


def axial_rope(x, tabs):
    (cos_r, sin_r), (cos_c, sin_c) = tabs
    half = x.shape[-1] // 2
    return jnp.concatenate([rope_rotate(x[..., :half], cos_r, sin_r),
                            rope_rotate(x[..., half:], cos_c, sin_c)], axis=-1)


def window_gqa(q, k, v, k_ctx, v_ctx, sink):
    b, s, hq, d = q.shape
    g = hq // ATT_KV_HEADS
    nb = s // BLOCK
    nw = 3 * BLOCK
    nc = k_ctx.shape[1]
    qb = (q * d ** -0.5).reshape(b, nb, BLOCK, ATT_KV_HEADS, g, d)
    pad = ((0, 0), (BLOCK, BLOCK), (0, 0), (0, 0))
    kp = jnp.pad(k, pad).reshape(b, nb + 2, BLOCK, ATT_KV_HEADS, d)
    vp = jnp.pad(v, pad).reshape(b, nb + 2, BLOCK, ATT_KV_HEADS, d)
    kw = jnp.concatenate([kp[:, :-2], kp[:, 1:-1], kp[:, 2:]], axis=2)
    vw = jnp.concatenate([vp[:, :-2], vp[:, 1:-1], vp[:, 2:]], axis=2)
    blk0 = jnp.arange(nb)[:, None, None] * BLOCK
    q_pos = blk0 + jnp.arange(BLOCK)[None, :, None]
    k_pos = blk0 - BLOCK + jnp.arange(nw)[None, None, :]
    valid = (jnp.abs(k_pos - q_pos) <= WINDOW) & (k_pos >= 0) & (k_pos < s)
    s_win = jnp.einsum('bnqhgd,bnkhd->bnhgqk', qb, kw).astype(F32)
    s_win = jnp.where(valid[None, :, None, None], s_win, NEG_INF)
    s_ctx = jnp.einsum('bnqhgd,bchd->bnhgqc', qb, k_ctx).astype(F32)
    s_sink = jnp.broadcast_to(sink.astype(F32).reshape(ATT_KV_HEADS, g, 1, 1), s_win.shape[:-1] + (1,))
    p = jax.nn.softmax(jnp.concatenate([s_win, s_ctx, s_sink], axis=-1), axis=-1).astype(v.dtype)
    o = (jnp.einsum('bnhgqk,bnkhd->bnqhgd', p[..., :nw], vw)
         + jnp.einsum('bnhgqc,bchd->bnqhgd', p[..., nw:nw + nc], v_ctx))
    return o.reshape(b, s, hq * d)


def ctx_gqa(q, k, v, sink):
    b, l, hq, d = q.shape
    g = hq // ATT_KV_HEADS
    qg = (q * d ** -0.5).reshape(b, l, ATT_KV_HEADS, g, d)
    sc = jnp.einsum('bqhgd,bkhd->bhgqk', qg, k).astype(F32)
    s_sink = jnp.broadcast_to(sink.astype(F32).reshape(ATT_KV_HEADS, g, 1, 1), sc.shape[:-1] + (1,))
    p = jax.nn.softmax(jnp.concatenate([sc, s_sink], axis=-1), axis=-1).astype(v.dtype)
    o = jnp.einsum('bhgqk,bkhd->bqhgd', p[..., :l], v)
    return o.reshape(b, l, hq * d)


def retention_log_gammas():
    lg = jnp.log(1.0 - jnp.exp(jnp.linspace(math.log(1.0 / 32), math.log(1.0 / 512), 2 * RET_HEADS, dtype=F32)))
    return lg[0::2], lg[1::2]


def retention_chunked(q, k, v, log_gamma, s0):
    b, t, h, d = q.shape
    n = t // RET_CHUNK
    cl = RET_CHUNK
    qc = q.reshape(b, n, cl, h, d).astype(F32)
    kc = k.reshape(b, n, cl, h, d).astype(F32)
    vc = v.reshape(b, n, cl, h, d).astype(F32)
    idx = jnp.arange(cl, dtype=F32)
    rel = idx[:, None] - idx[None, :]
    d_intra = jnp.where(rel >= 0, jnp.exp(jnp.maximum(rel, 0.0)[None] * log_gamma[:, None, None]), 0.0)
    a_int = jnp.einsum('bnihd,bnjhd->bnhij', qc, kc) * d_intra
    o = jnp.einsum('bnhij,bnjhe->bnihe', a_int, vc)
    k_dec = kc * jnp.exp((cl - 1 - idx)[:, None] * log_gamma[None, :])[..., None]
    kv = jnp.einsum('bnjhd,bnjhe->nbhde', k_dec, vc)
    g_chunk = jnp.exp(cl * log_gamma)[:, None, None]

    def step(state, kv_n):
        return state * g_chunk + kv_n, state

    s_fin, s_prev = lax.scan(step, s0, kv)
    q_dec = qc * jnp.exp((idx + 1)[:, None] * log_gamma[None, :])[..., None]
    o = o + jnp.einsum('bnihd,nbhde->bnihe', q_dec, s_prev)
    return o.reshape(b, t, h, d), s_fin


RET_TOKENS_PER_STEP = 1024


def _retention_kernel(q_ref, k_ref, v_ref, g_ref, d_ref, kdec_ref, qdec_ref, gch_ref, s0_ref,
                      y_ref, sfin_ref, s_scr, *, reverse, n_chunks):
    cl = RET_CHUNK
    nh, tb, hd = q_ref.shape[1:]

    @pl.when(pl.program_id(1) == 0)
    def _():
        s_scr[...] = s0_ref[0]

    nb = nh * n_chunks
    shp = (nb, cl, hd)
    per_chunk = lambda z: jnp.broadcast_to(z[:, None], (nh, n_chunks) + z.shape[1:]).reshape((nb,) + z.shape[1:])
    q = q_ref[0].reshape(shp)
    k = k_ref[0].reshape(shp)
    v = v_ref[0].reshape(shp)
    a = _dot(q, k, _NT, 1) * per_chunk(d_ref[...])
    o = _dot(a, v, _NN, 1)
    kv = _dot(k * per_chunk(kdec_ref[...]), v, _TN, 1).reshape(nh, n_chunks, hd, hd)

    s = s_scr[...]
    gch = gch_ref[...]
    starts = [None] * n_chunks
    order = range(n_chunks - 1, -1, -1) if reverse else range(n_chunks)
    for n in order:
        starts[n] = s
        s = s * gch + kv[:, n]
    s_scr[...] = s
    sfin_ref[0] = s
    s_in = jnp.stack(starts, axis=1).reshape(nb, hd, hd)
    o = o + _dot(q * per_chunk(qdec_ref[...]), s_in, _NN, 1)

    oc = o - jnp.mean(o, axis=-1, keepdims=True)
    normed = oc * lax.rsqrt(jnp.mean(oc * oc, axis=-1, keepdims=True) + RET_GN_EPS)
    g = g_ref[0].reshape(shp)
    y_ref[0] = ((g * jax.nn.sigmoid(g)) * normed).reshape(nh, tb, hd)


def retention_direction(q, k, v, gate, log_gamma, s0, reverse):
    bsz, nh, t, hd = q.shape
    tb = min(RET_TOKENS_PER_STEP, t)
    nt = t // tb
    assert t == nt * tb and tb % RET_CHUNK == 0
    cl = RET_CHUNK
    idx = jnp.arange(cl, dtype=F32)
    rel = (idx[None, :] - idx[:, None]) if reverse else (idx[:, None] - idx[None, :])
    lg = log_gamma[:, None, None]
    d_intra = jnp.where(rel >= 0, jnp.exp(jnp.maximum(rel, 0.0)[None] * lg), 0.0)
    kdec = jnp.exp((idx if reverse else cl - 1 - idx)[None, :, None] * lg)
    qdec = jnp.exp((cl - idx if reverse else idx + 1)[None, :, None] * lg)
    gch = jnp.exp(cl * lg)
    if reverse:
        tmap = lambda bi, j: (bi, 0, nt - 1 - j, 0)
    else:
        tmap = lambda bi, j: (bi, 0, j, 0)
    seq_spec = pl.BlockSpec((1, nh, tb, hd), tmap)
    st_spec = pl.BlockSpec((1, nh, hd, hd), lambda bi, j: (bi, 0, 0, 0))
    const = lambda shape: pl.BlockSpec(shape, lambda bi, j: (0,) * len(shape))
    y, sfin = pl.pallas_call(
        functools.partial(_retention_kernel, reverse=reverse, n_chunks=tb // cl),
        grid=(bsz, nt),
        in_specs=[seq_spec] * 4 + [const((nh, cl, cl)), const((nh, cl, 1)), const((nh, cl, 1)),
                                   const((nh, 1, 1)), st_spec],
        out_specs=[seq_spec, st_spec],
        out_shape=[jax.ShapeDtypeStruct((bsz, nh, t, hd), F32),
                   jax.ShapeDtypeStruct((bsz, nh, hd, hd), F32)],
        scratch_shapes=[pltpu.VMEM((nh, hd, hd), F32)],
        compiler_params=pltpu.CompilerParams(dimension_semantics=("parallel", "arbitrary"),
                                             vmem_limit_bytes=RWKV_VMEM_LIMIT),
        name="retention_rev" if reverse else "retention_fwd",
    )(q, k, v, gate, d_intra, kdec, qdec, gch, s0)
    return y, sfin


def retention_mix(pb, cos, sin, lg_f, lg_b, s0f, s0b):
    bsz, t = pb.shape[:2]
    q, k, v, gf, gb = split_sizes(pb, RET_SIZES)
    tr = lambda z: jnp.swapaxes(z, 1, 2)
    q = tr(rope_rotate(heads(q, RET_HEADS), cos, sin))
    k = tr(rope_rotate(heads(k, RET_HEADS), cos, sin) * HEAD_DIM ** -0.5)
    v = tr(heads(v, RET_HEADS))
    yf, sf = retention_direction(q, k, v, tr(heads(gf, RET_HEADS)), lg_f, s0f, False)
    yb, sb = retention_direction(q, k, v, tr(heads(gb, RET_HEADS)), lg_b, s0b, True)
    out = tr(yf + yb).reshape(bsz, t, -1)
    return out.astype(pb.dtype), sf, sb


def mla_project(pc, lp):
    cq, ckv, kr = split_sizes(pc, MLA_SIZES)
    q = heads(rms_norm(cq, lp['g_qnorm']) @ lp['w_uq'], MLA_HEADS)
    kv = heads(rms_norm(ckv, lp['g_kvnorm']) @ lp['w_ukv'], MLA_HEADS)
    return q[..., :MLA_NOPE], q[..., MLA_NOPE:], kv[..., :MLA_NOPE], kr[:, :, None, :], kv[..., MLA_NOPE:]


LANES = 128
SUBLANES = 8
MLA_Q_TILE = 512
MLA_KEY_UNIT = 256
MLA_KEY_TILE_MAX = 8448
MLA_VMEM_LIMIT = 56 * 1024 * 1024


def _mla_flash_kernel(q_ref, k_ref, v_ref, o_ref, m_scr, acc_scr, *, scale, tk, n_tiles):
    dv = o_ref.shape[-1]
    q = (q_ref[0, 0] * scale).astype(BF16)
    m_scr[...] = jnp.full_like(m_scr, NEG_INF)
    acc_scr[...] = jnp.zeros_like(acc_scr)

    def sweep(c, carry):
        keys = pl.ds(pl.multiple_of(c * tk, tk), tk)
        s = lax.dot_general(q, k_ref[0, 0, keys, :], (((1,), (1,)), ((), ())), preferred_element_type=F32)
        m_prev = m_scr[...]
        m_new = jnp.maximum(m_prev, jnp.max(s, axis=-1, keepdims=True))
        p = jnp.exp2(s - m_new).astype(BF16)
        acc_scr[...] = (jnp.exp2(m_prev - m_new) * acc_scr[...]
                        + jnp.dot(p, v_ref[0, 0, keys, :], preferred_element_type=F32))
        m_scr[...] = m_new
        return carry

    lax.fori_loop(0, n_tiles, sweep, 0)
    acc = acc_scr[...]
    o_ref[0, 0] = acc[:, :dv] / acc[:, dv:dv + 1]


def mla_attend(qn, qr, kn, kr, v):
    b, t, h, _ = qn.shape
    n = kn.shape[1]
    dv = v.shape[-1]
    scale = (MLA_NOPE + MLA_ROPE) ** -0.5 * math.log2(math.e)
    q = jnp.swapaxes(jnp.concatenate([qn, qr], axis=-1), 1, 2)
    kr_h = jnp.broadcast_to(kr, (b, n, h, MLA_ROPE))
    k = jnp.swapaxes(jnp.concatenate([kn, kr_h], axis=-1), 1, 2).astype(BF16)
    ones = jnp.ones((b, n, h, 1), v.dtype)
    zeros = jnp.zeros((b, n, h, LANES - dv - 1), v.dtype)
    vt = jnp.swapaxes(jnp.concatenate([v, ones, zeros], axis=-1), 1, 2).astype(BF16)
    dq = q.shape[-1]
    tq = min(MLA_Q_TILE, t)
    tk = max(d for d in range(MLA_KEY_UNIT, MLA_KEY_TILE_MAX + 1, MLA_KEY_UNIT) if n % d == 0)
    assert t % tq == 0
    o = pl.pallas_call(
        functools.partial(_mla_flash_kernel, scale=scale, tk=tk, n_tiles=n // tk),
        grid=(b, h, t // tq),
        in_specs=[pl.BlockSpec((1, 1, tq, dq), lambda bi, hi, qi: (bi, hi, qi, 0)),
                  pl.BlockSpec((1, 1, n, dq), lambda bi, hi, qi: (bi, hi, 0, 0)),
                  pl.BlockSpec((1, 1, n, LANES), lambda bi, hi, qi: (bi, hi, 0, 0))],
        out_specs=pl.BlockSpec((1, 1, tq, dv), lambda bi, hi, qi: (bi, hi, qi, 0)),
        out_shape=jax.ShapeDtypeStruct((b, h, t, dv), F32),
        scratch_shapes=[pltpu.VMEM((tq, 1), F32), pltpu.VMEM((tq, LANES), F32)],
        compiler_params=pltpu.CompilerParams(
            dimension_semantics=("parallel", "parallel", "arbitrary"),
            vmem_limit_bytes=MLA_VMEM_LIMIT),
        name="mla_flash",
    )(q, k, vt)
    return jnp.swapaxes(o, 1, 2).reshape(b, t, h * dv)


def centred_shift(p):
    prev = jnp.pad(p, ((0, 0), (1, 0), (0, 0)))[:, :-1]
    nxt = jnp.pad(p, ((0, 0), (0, 1), (0, 0)))[:, 1:]
    return 0.5 * (prev + nxt)


def rwkv_features(pd, lp):
    z = pd + (centred_shift(pd) - pd) * lp['rwkv_mu']
    zr, zk, zv, zwf, zwb, zaf, zab, zg = split_sizes(z, RWKV_SIZES)
    r = heads(zr, RWKV_HEADS)
    k = heads(zk, RWKV_HEADS)
    v = heads(zv, RWKV_HEADS)
    g = jax.nn.sigmoid(zg) @ lp['rwkv_g_up']
    kk = (k * heads(lp['rwkv_k_k'], RWKV_HEADS)).astype(F32)
    kk = kk / jnp.maximum(jnp.sqrt(jnp.sum(kk * kk, axis=-1, keepdims=True)), 1e-12)
    k_a = heads(lp['rwkv_k_a'], RWKV_HEADS)
    dirs = []
    for di, (zw, za) in enumerate(((zwf, zaf), (zwb, zab))):
        logw = -jax.nn.softplus(-(lp['rwkv_w0'][di] + jnp.tanh(zw) @ lp['rwkv_w_up'][di]).astype(F32)) - 0.5
        w = heads(-jnp.exp(logw), RWKV_HEADS)
        a = heads(jax.nn.sigmoid((lp['rwkv_a0'][di] + za @ lp['rwkv_a_up'][di]).astype(F32)), RWKV_HEADS)
        dirs.append((w, k * (1 + (a - 1) * k_a), kk * a))
    return r, v, g, kk, dirs


def rwkv_scan(r, w, k, v, kk, b_, s0, reverse):
    xs = tuple(jnp.moveaxis(z.astype(F32), 1, 0) for z in (r, w, k, v, kk, b_))

    def step(st, inp):
        r_t, w_t, k_t, v_t, kk_t, b_t = inp
        sa = jnp.einsum('bhvk,bhk->bhv', st, -kk_t)
        st = st * w_t[:, :, None, :] + sa[..., None] * b_t[:, :, None, :] + v_t[..., None] * k_t[:, :, None, :]
        return st, jnp.einsum('bhvk,bhk->bhv', st, r_t)

    s_fin, y = lax.scan(step, s0, xs, reverse=reverse)
    return jnp.moveaxis(y, 0, 1), s_fin


BF16 = jnp.bfloat16
RWKV_CHUNK = 64
RWKV_SUB = 16
RWKV_TOKENS_PER_STEP = 512
RWKV_PASSES = (1, 1)
RWKV_VMEM_LIMIT = 48 * 1024 * 1024
_NN = (((2,), (1,)), ((0,), (0,)))
_NT = (((2,), (2,)), ((0,), (0,)))
_TN = (((1,), (1,)), ((0,), (0,)))


def _bf16_parts(a, n):
    parts, rem = [], a
    for i in range(n):
        p = rem.astype(BF16)
        parts.append(p)
        if i + 1 < n:
            rem = rem - p.astype(F32)
    return parts


def _dot(a, b, dims, passes):
    if passes == 1:
        return lax.dot_general(a.astype(BF16), b.astype(BF16), dims, preferred_element_type=F32)
    a_hi, a_lo = _bf16_parts(a, 2)
    b_hi, b_lo = _bf16_parts(b, 2)
    out = lax.dot_general(a_hi, b_hi, dims, preferred_element_type=F32)
    out = out + lax.dot_general(a_hi, b_lo, dims, preferred_element_type=F32)
    return out + lax.dot_general(a_lo, b_hi, dims, preferred_element_type=F32)


def _unit_lower_inverse(l_mat, same_sub, passes):
    mm = lambda a, b: _dot(a, b, _NN, passes)
    ld = jnp.where(same_sub, l_mat, 0.0)
    lo = l_mat - ld
    p2 = mm(ld, ld)
    d = p2 - ld - mm(ld, p2)
    pw = p2
    span = 4
    while span < RWKV_SUB:
        pw = mm(pw, pw)
        d = d + pw + mm(d, pw)
        span *= 2
    n1 = lo + mm(d, lo)
    n2 = mm(n1, n1)
    x = n2 - n1 - mm(n1, n2)
    span = 4
    pw = n2
    while span < RWKV_CHUNK // RWKV_SUB:
        pw = mm(pw, pw)
        x = x + pw + mm(x, pw)
        span *= 2
    return x + d + mm(x, d)


def _rwkv_scan_kernel(r_ref, lw_ref, k_ref, v_ref, kk_ref, b_ref, s0_ref, y_ref, sfin_ref, s_scr,
                      *, reverse, n_chunks):
    cl = RWKV_CHUNK
    nh, tb, hd = r_ref.shape[1:]

    @pl.when(pl.program_id(1) == 0)
    def _():
        s_scr[...] = s0_ref[0]

    nb = nh * n_chunks
    shp = (nb, cl, hd)
    r = r_ref[0].reshape(shp)
    lw = lw_ref[0].reshape(shp)
    k = k_ref[0].reshape(shp)
    v = v_ref[0].reshape(shp)
    kk = kk_ref[0].reshape(shp)
    b = b_ref[0].reshape(shp)

    row = lax.broadcasted_iota(jnp.int32, (cl, cl), 0)
    col = lax.broadcasted_iota(jnp.int32, (cl, cl), 1)
    before = (col > row) if reverse else (col < row)
    upto = before | (col == row)
    same_sub = (row // RWKV_SUB) == (col // RWKV_SUB)

    tri = jnp.broadcast_to(jnp.where(upto, 1.0, 0.0).astype(BF16)[None], (nb, cl, cl))
    cum = sum(lax.dot_general(tri, p, _NN, preferred_element_type=F32) for p in _bf16_parts(lw, 3))
    last = 0 if reverse else cl - 1
    mid = cl // 2 if reverse else cl // 2 - 1
    tot = cum[:, last:last + 1, :]
    rho = cum[:, mid:mid + 1, :]
    cum_ex = cum - lw

    a_t = kk * jnp.exp(cum_ex - rho)
    r_t = r * jnp.exp(cum - rho)
    e_out = jnp.exp(rho - cum)
    b_t = b * e_out
    k_t = k * e_out
    a_0 = kk * jnp.exp(cum_ex)
    r_0 = r * jnp.exp(cum)
    e_end = jnp.exp(tot - cum)
    b_e = b * e_end
    k_e = k * e_end

    ps, pt = RWKV_PASSES
    l_mat = jnp.where(before, _dot(a_t, b_t, _NT, ps), 0.0)
    m_ak = jnp.where(before, _dot(a_t, k_t, _NT, ps), 0.0)
    m_rb = jnp.where(upto, _dot(r_t, b_t, _NT, ps), 0.0)
    m_rk = jnp.where(upto, _dot(r_t, k_t, _NT, ps), 0.0)
    t_m1 = _unit_lower_inverse(l_mat, same_sub, pt)

    mv = _dot(m_ak, v, _NN, ps)
    q = -(a_0 + _dot(t_m1, a_0, _NN, ps))
    w = -(mv + _dot(t_m1, mv, _NN, ps))
    r_h = r_0 + _dot(m_rb, q, _NN, ps)
    y_loc = _dot(m_rb, w, _NN, ps) + _dot(m_rk, v, _NN, ps)
    eye = lax.broadcasted_iota(jnp.int32, (hd, hd), 0) == lax.broadcasted_iota(jnp.int32, (hd, hd), 1)
    phi = (jnp.where(eye, jnp.exp(tot), 0.0) + _dot(b_e, q, _TN, ps)).reshape(nh, n_chunks, hd, hd)
    psi = (_dot(b_e, w, _TN, ps) + _dot(k_e, v, _TN, ps)).reshape(nh, n_chunks, hd, hd)

    s = s_scr[...]
    starts = [None] * n_chunks
    order = range(n_chunks - 1, -1, -1) if reverse else range(n_chunks)
    for n in order:
        starts[n] = s
        s = _dot(phi[:, n], s, _NN, 3) + psi[:, n]
    s_scr[...] = s
    sfin_ref[0] = s
    s_in = jnp.stack(starts, axis=1).reshape(nb, hd, hd)
    y = _dot(r_h, s_in, _NN, ps) + y_loc
    y_ref[0] = y.reshape(nh, tb, hd)


def rwkv_scan_blocked(r, lw, k, v, kk, b_, s0t, reverse):
    bsz, nh, t, hd = r.shape
    tb = min(RWKV_TOKENS_PER_STEP, t)
    nt = t // tb
    assert t == nt * tb and tb % RWKV_CHUNK == 0
    if reverse:
        tmap = lambda bi, j: (bi, 0, nt - 1 - j, 0)
    else:
        tmap = lambda bi, j: (bi, 0, j, 0)
    seq_spec = pl.BlockSpec((1, nh, tb, hd), tmap)
    st_spec = pl.BlockSpec((1, nh, hd, hd), lambda bi, j: (bi, 0, 0, 0))
    y, sfin = pl.pallas_call(
        functools.partial(_rwkv_scan_kernel, reverse=reverse, n_chunks=tb // RWKV_CHUNK),
        grid=(bsz, nt),
        in_specs=[seq_spec] * 6 + [st_spec],
        out_specs=[seq_spec, st_spec],
        out_shape=[jax.ShapeDtypeStruct((bsz, nh, t, hd), F32),
                   jax.ShapeDtypeStruct((bsz, nh, hd, hd), F32)],
        scratch_shapes=[pltpu.VMEM((nh, hd, hd), F32)],
        compiler_params=pltpu.CompilerParams(dimension_semantics=("parallel", "arbitrary"),
                                             vmem_limit_bytes=RWKV_VMEM_LIMIT),
        name="rwkv_scan_rev" if reverse else "rwkv_scan_fwd",
    )(r, lw, k, v, kk, b_, s0t)
    return y, sfin


def rwkv_mix(feat, lp, s0f, s0b):
    r, v, g, kk, dirs = feat
    (lwf, kf, bf), (lwb, kb, bb) = dirs
    tr = lambda z: jnp.swapaxes(z.astype(F32), 1, 2)
    rt, vt, kkt = tr(r), tr(v), tr(kk)
    yf, sf = rwkv_scan_blocked(rt, tr(lwf), tr(kf), vt, kkt, tr(bf), s0f, False)
    yb, sb = rwkv_scan_blocked(rt, tr(lwb), tr(kb), vt, kkt, tr(bb), s0b, True)
    yf = jnp.swapaxes(yf, 1, 2)
    yb = jnp.swapaxes(yb, 1, 2)
    y = head_norm(yf + yb, RWKV_GN_EPS) * heads(lp['rwkv_lnx_w'], RWKV_HEADS) + heads(lp['rwkv_lnx_b'], RWKV_HEADS)
    bonus = jnp.sum(r * (kf + kb) * lp['rwkv_r_k'], axis=-1, keepdims=True) * v
    out = (y + bonus).reshape(g.shape) * g
    return out.astype(g.dtype), sf, sb


MERGE_ROW_TILE = 512
MERGE_VMEM_LIMIT = 52 * 1024 * 1024


def _merge_kernel(u_ref, ya_ref, yb_ref, yc_ref, yd_ref, wg_ref, bg_ref, wb_ref, wo_ref, o_ref):
    u = u_ref[...].astype(BF16)
    acc = None
    for n, y_ref in enumerate((ya_ref, yb_ref, yc_ref, yd_ref)):
        gate = jax.nn.sigmoid(jnp.dot(u, wg_ref[n], preferred_element_type=F32) + bg_ref[n])
        term = gate * jnp.dot(y_ref[...].astype(BF16), wb_ref[n], preferred_element_type=F32)
        acc = term if acc is None else acc + term
    o_ref[...] = jnp.dot(acc.astype(BF16), wo_ref[...], preferred_element_type=F32)


def merge_branches(u, ys, lp):
    lead, d = u.shape[:-1], u.shape[-1]
    rows = math.prod(lead)
    tm = min(MERGE_ROW_TILE, rows)
    assert rows % tm == 0
    bw = ys[0].shape[-1]
    row_spec = lambda width: pl.BlockSpec((tm, width), lambda i: (i, 0))
    whole = lambda shape: pl.BlockSpec(shape, lambda i: (0,) * len(shape))
    out = pl.pallas_call(
        _merge_kernel,
        grid=(rows // tm,),
        in_specs=[row_spec(d)] + [row_spec(bw)] * N_BRANCH
                 + [whole((N_BRANCH, d, d)), whole((N_BRANCH, 1, d)), whole((N_BRANCH, bw, d)), whole((d, d))],
        out_specs=row_spec(d),
        out_shape=jax.ShapeDtypeStruct((rows, d), F32),
        compiler_params=pltpu.CompilerParams(dimension_semantics=("parallel",),
                                             vmem_limit_bytes=MERGE_VMEM_LIMIT),
        name="merge_branches",
    )(u.reshape(rows, d), *[y.reshape(rows, bw) for y in ys],
      lp['w_gate'].astype(BF16), lp['b_gate'][:, None, :], lp['w_branch'].astype(BF16), lp['w_out'].astype(BF16))
    return out.reshape(lead + (d,))


def token_mixers(u_ctx, u_lat, lp, need_ctx):
    b, s, _ = u_lat.shape
    l = u_ctx.shape[1]
    w_groups = split_sizes(lp['w_in'], GROUP_SIZES)
    pa_l, pb_l, pc_l, pd_l = [u_lat @ w for w in w_groups]
    pa_c, pb_c, pc_c, pd_c = [u_ctx @ w for w in w_groups]
    tab_att = axial_tables(s, HEAD_DIM)
    tab_mla = axial_tables(s, MLA_ROPE)

    aq_l, ak_l, av_l = split_sizes(pa_l, A_SIZES)
    aq_c, ak_c, av_c = split_sizes(pa_c, A_SIZES)
    k_c = heads(ak_c, ATT_KV_HEADS)
    v_c = heads(av_c, ATT_KV_HEADS)
    ya_l = window_gqa(axial_rope(heads(aq_l, ATT_Q_HEADS), tab_att),
                      axial_rope(heads(ak_l, ATT_KV_HEADS), tab_att),
                      heads(av_l, ATT_KV_HEADS), k_c, v_c, lp['sink'])

    lg_f, lg_b = retention_log_gammas()
    cos_c, sin_c = rope_tables(jnp.arange(l, dtype=F32), HEAD_DIM)
    cos_l, sin_l = rope_tables(l + jnp.arange(s, dtype=F32), HEAD_DIM)
    zero_ret = jnp.zeros((b, RET_HEADS, HEAD_DIM, HEAD_DIM), F32)
    yb_c, sbf, sbb = retention_mix(pb_c, cos_c, sin_c, lg_f, lg_b, zero_ret, zero_ret)
    yb_l, _, _ = retention_mix(pb_l, cos_l, sin_l, lg_f, lg_b, sbf, sbb)

    qn_l, qr_l, kn_l, kr_l, vm_l = mla_project(pc_l, lp)
    qn_c, qr_c, kn_c, kr_c, vm_c = mla_project(pc_c, lp)
    qr_l = axial_rope(qr_l, tab_mla)
    kr_l = axial_rope(kr_l, tab_mla)
    yc_l = mla_attend(qn_l, qr_l, jnp.concatenate([kn_c, kn_l], axis=1),
                      jnp.concatenate([kr_c, kr_l], axis=1), jnp.concatenate([vm_c, vm_l], axis=1))

    zero_wkv = jnp.zeros((b, RWKV_HEADS, HEAD_DIM, HEAD_DIM), F32)
    yd_c, sdf, sdb = rwkv_mix(rwkv_features(pd_c, lp), lp, zero_wkv, zero_wkv)
    yd_l, _, _ = rwkv_mix(rwkv_features(pd_l, lp), lp, sdf, sdb)

    m_lat = merge_branches(u_lat, (ya_l, yb_l, yc_l, yd_l), lp)
    if not need_ctx:
        return None, m_lat
    ya_c = ctx_gqa(heads(aq_c, ATT_Q_HEADS), k_c, v_c, lp['sink'])
    yc_c = mla_attend(qn_c, qr_c, kn_c, kr_c, vm_c)
    m_ctx = merge_branches(u_ctx, (ya_c, yb_c, yc_c, yd_c), lp)
    return m_ctx, m_lat


def moe_ffn(x, router_w, router_b, layer_idx, w_e_gate, w_e_up, w_e_down):
    n_tok, dm = x.shape
    scores = jax.nn.sigmoid(x.astype(F32) @ router_w.astype(F32))
    grp = (scores + router_b.astype(F32)).reshape(n_tok, N_GROUPS, EXPERTS_PER_GROUP)
    g_sel = jnp.argmax(jnp.sum(lax.top_k(grp, TOP_K)[0], axis=-1), axis=-1)
    in_grp = lax.top_k(jnp.take_along_axis(grp, g_sel[:, None, None], axis=1)[:, 0], TOP_K)[1]
    e_idx = g_sel[:, None] * EXPERTS_PER_GROUP + in_grp
    w_sel = jnp.take_along_axis(scores, e_idx, axis=1)
    w_sel = w_sel / jnp.sum(w_sel, axis=-1, keepdims=True)
    n_asg = n_tok * TOP_K
    flat_e = e_idx.reshape(-1).astype(jnp.int32)
    flat_w = w_sel.reshape(-1)
    order = jnp.argsort(flat_e).astype(jnp.int32)
    rank = jnp.argsort(order).astype(jnp.int32)
    onehot = flat_e[:, None] == jnp.arange(N_EXPERTS, dtype=jnp.int32)[None, :]
    counts = jnp.sum(onehot, axis=0, dtype=jnp.int32)
    padded = (counts + MOE_BLOCK - 1) // MOE_BLOCK * MOE_BLOCK
    pad_end = jnp.cumsum(padded)
    pad_start = pad_end - padded
    start = jnp.cumsum(counts) - counts
    n_blocks = -(-n_asg // MOE_BLOCK) + N_EXPERTS
    n_rows = n_blocks * MOE_BLOCK
    blk_row0 = jnp.arange(n_blocks, dtype=jnp.int32) * MOE_BLOCK
    blk_e = jnp.minimum(jnp.sum(pad_end[None, :] <= blk_row0[:, None], axis=1, dtype=jnp.int32), N_EXPERTS - 1)
    off = (blk_row0 - pad_start[blk_e])[:, None] + jnp.arange(MOE_BLOCK, dtype=jnp.int32)[None, :]
    valid = (off < counts[blk_e][:, None]).reshape(-1)
    src = order[jnp.clip(start[blk_e][:, None] + off, 0, n_asg - 1).reshape(-1)]
    row_tok = jnp.where(valid, src // TOP_K, 0)
    row_w = jnp.where(valid, flat_w[src], 0.0)
    blk_valid = jnp.clip(counts[blk_e] - (blk_row0 - pad_start[blk_e]), 0, MOE_BLOCK)
    blk_groups = (blk_valid + MOE_GATHER_UNROLL - 1) // MOE_GATHER_UNROLL
    yb = moe_expert_blocks(blk_e, blk_groups.astype(jnp.int32), row_tok, x, row_w[:, None], layer_idx,
                           w_e_gate, w_e_up, w_e_down)
    shift = jnp.sum(jnp.where(onehot, (pad_start - start)[None, :], 0), axis=1, dtype=jnp.int32)
    slot = (rank + shift).reshape(n_tok, TOP_K)
    out = yb[slot[:, 0]]
    for j in range(1, TOP_K):
        out = out + yb[slot[:, j]]
    return out.astype(x.dtype)


MOE_VMEM_LIMIT = 40 * 1024 * 1024
MOE_GATHER_UNROLL = 8


def _moe_row_copy(x_hbm, xbuf, sem, tok, slot, r):
    return pltpu.make_async_copy(x_hbm.at[pl.ds(tok, 1)], xbuf.at[slot, pl.ds(r, 1)], sem.at[slot])


def _moe_expert_kernel(blk_e_ref, blk_groups_ref, row_tok_ref, x_hbm, w_ref, wg_ref, wu_ref, wd_ref, o_ref,
                       xbuf, sem):
    del blk_e_ref
    i = pl.program_id(0)
    slot = i % 2

    def for_each_row(block, fn):
        def body(gi, carry):
            for q in range(MOE_GATHER_UNROLL):
                fn(gi * MOE_GATHER_UNROLL + q, q)
            return carry
        lax.fori_loop(0, blk_groups_ref[block], body, 0)

    def issue(block, dst_slot):
        for_each_row(block, lambda r, q: _moe_row_copy(
            x_hbm, xbuf, sem, row_tok_ref[block * MOE_BLOCK + r], dst_slot, r).start(priority=q % 2))

    @pl.when(i == 0)
    def _():
        xbuf[...] = jnp.zeros_like(xbuf)
        issue(0, 0)

    @pl.when(i + 1 < pl.num_programs(0))
    def _():
        issue(i + 1, 1 - slot)

    for_each_row(i, lambda r, q: _moe_row_copy(x_hbm, xbuf, sem, 0, slot, r).wait())

    @pl.when(blk_groups_ref[i] == 0)
    def _():
        o_ref[...] = jnp.zeros_like(o_ref)

    @pl.when(blk_groups_ref[i] > 0)
    def _():
        x = jnp.concatenate([xbuf[slot, :, j, :] for j in range(xbuf.shape[2])], axis=-1).astype(BF16)
        g = jnp.dot(x, wg_ref[0, 0].astype(BF16), preferred_element_type=F32)
        u = jnp.dot(x, wu_ref[0, 0].astype(BF16), preferred_element_type=F32)
        hid = (g * jax.nn.sigmoid(g)) * u
        y = jnp.dot(hid.astype(BF16), wd_ref[0, 0].astype(BF16), preferred_element_type=F32)
        o_ref[...] = y * w_ref[...]


def moe_expert_blocks(blk_e, blk_groups, row_tok, x, row_w, layer_idx, w_e_gate, w_e_up, w_e_down):
    n_rows = row_tok.shape[0]
    n_tok, dm = x.shape
    n_blocks = n_rows // MOE_BLOCK
    ff = w_e_gate.shape[-1]
    assert dm == SUBLANES * LANES
    x = x.reshape(n_tok, SUBLANES, LANES)
    row_spec = lambda width: pl.BlockSpec((MOE_BLOCK, width), lambda i, be, bg, rt: (i, 0))
    expert_spec = lambda rows, cols: pl.BlockSpec((1, 1, rows, cols),
                                                  lambda i, be, bg, rt: (layer_idx, be[i], 0, 0))
    return pl.pallas_call(
        _moe_expert_kernel,
        grid_spec=pltpu.PrefetchScalarGridSpec(
            num_scalar_prefetch=3,
            grid=(n_blocks,),
            in_specs=[pl.BlockSpec(memory_space=pl.ANY), row_spec(1),
                      expert_spec(dm, ff), expert_spec(dm, ff), expert_spec(ff, dm)],
            out_specs=row_spec(dm),
            scratch_shapes=[pltpu.VMEM((2, MOE_BLOCK, SUBLANES, LANES), x.dtype), pltpu.SemaphoreType.DMA((2,))]),
        out_shape=jax.ShapeDtypeStruct((n_rows, dm), F32),
        compiler_params=pltpu.CompilerParams(dimension_semantics=("arbitrary",),
                                             vmem_limit_bytes=MOE_VMEM_LIMIT),
        name="moe_experts",
    )(blk_e, blk_groups, row_tok, x, row_w, w_e_gate, w_e_up, w_e_down)


def layer(h_ctx, h_lat, c, c_ctx, lp, router_w, router_b, need_ctx):
    b, s, d = h_lat.shape
    l = h_ctx.shape[1]
    m_lat = jnp.split((jax.nn.silu(c) @ lp['w_ada'] + lp['b_ada'])[:, None, :], 6, axis=-1)
    m_ctx = jnp.split(jax.nn.silu(c_ctx) @ lp['w_ada'] + lp['b_ada'], 6, axis=-1)
    u_lat = adaln(h_lat, lp['g_norm1'], m_lat[0], m_lat[1])
    u_ctx = adaln(h_ctx, lp['g_norm1'], m_ctx[0], m_ctx[1])
    mix_ctx, mix_lat = token_mixers(u_ctx, u_lat, lp, need_ctx)
    h_lat = h_lat + m_lat[2] * mix_lat
    f_lat = adaln(h_lat, lp['g_norm2'], m_lat[3], m_lat[4]).reshape(b * s, d)
    if not need_ctx:
        ffn = moe_ffn(f_lat, router_w, router_b, lp['layer_idx'], *lp['experts'])
        return h_ctx, h_lat + m_lat[5] * ffn.reshape(b, s, d)
    h_ctx = h_ctx + m_ctx[2] * mix_ctx
    f_ctx = adaln(h_ctx, lp['g_norm2'], m_ctx[3], m_ctx[4]).reshape(b * l, d)
    ffn = moe_ffn(jnp.concatenate([f_ctx, f_lat], axis=0), router_w, router_b, lp['layer_idx'], *lp['experts'])
    h_ctx = h_ctx + m_ctx[5] * ffn[:b * l].reshape(b, l, d)
    h_lat = h_lat + m_lat[5] * ffn[b * l:].reshape(b, s, d)
    return h_ctx, h_lat


def _final_norm_kernel(x_ref, g_ref, o_ref):
    x = x_ref[...]
    y = x * lax.rsqrt(jnp.mean(x * x, axis=-1, keepdims=True) + NORM_EPS)
    o_ref[...] = y * g_ref[...]


def final_rms_norm(x, g):
    b, s, d = x.shape
    rows = b * s
    tile = 1024
    out = pl.pallas_call(
        _final_norm_kernel,
        grid=(rows // tile,),
        in_specs=[pl.BlockSpec((tile, d), lambda i: (i, 0)), pl.BlockSpec((1, d), lambda i: (0, 0))],
        out_specs=pl.BlockSpec((tile, d), lambda i: (i, 0)),
        out_shape=jax.ShapeDtypeStruct((rows, d), x.dtype),
        name="final_rms_norm",
    )(x.reshape(rows, d), g.reshape(1, d))
    return out.reshape(b, s, d)


_LAYER_PARAM_NAMES = (
    'w_ada', 'b_ada', 'g_norm1', 'g_norm2', 'w_in', 'sink', 'g_qnorm', 'g_kvnorm', 'w_uq',
    'w_ukv', 'rwkv_mu', 'rwkv_w0', 'rwkv_w_up', 'rwkv_a0', 'rwkv_a_up', 'rwkv_g_up',
    'rwkv_k_k', 'rwkv_k_a', 'rwkv_r_k', 'rwkv_lnx_w', 'rwkv_lnx_b', 'w_gate', 'b_gate',
    'w_branch', 'w_out')


def kernel(x, c, ctx, c_ctx, w_ada, b_ada, g_norm1, g_norm2, w_in, sink, g_qnorm, g_kvnorm,
           w_uq, w_ukv, rwkv_mu, rwkv_w0, rwkv_w_up, rwkv_a0, rwkv_a_up, rwkv_g_up, rwkv_k_k,
           rwkv_k_a, rwkv_r_k, rwkv_lnx_w, rwkv_lnx_b, w_gate, b_gate, w_branch, w_out,
           router_w, router_b, w_e_gate, w_e_up, w_e_down, g_final):
    stacked = (w_ada, b_ada, g_norm1, g_norm2, w_in, sink, g_qnorm, g_kvnorm, w_uq, w_ukv, rwkv_mu,
               rwkv_w0, rwkv_w_up, rwkv_a0, rwkv_a_up, rwkv_g_up, rwkv_k_k, rwkv_k_a, rwkv_r_k,
               rwkv_lnx_w, rwkv_lnx_b, w_gate, b_gate, w_branch, w_out)
    h_ctx, h_lat = ctx, x
    for i in range(DEPTH):
        lp = {n: a[i] for n, a in zip(_LAYER_PARAM_NAMES, stacked)}
        lp['layer_idx'] = i
        lp['experts'] = (w_e_gate, w_e_up, w_e_down)
        h_ctx, h_lat = layer(h_ctx, h_lat, c, c_ctx, lp, router_w, router_b, i < DEPTH - 1)
    return final_rms_norm(h_lat, g_final)
```

```python
import functools
import math

import jax
import jax.numpy as jnp
from jax import lax
from jax.experimental import pallas as pl
from jax.experimental.pallas import tpu as pltpu

D_MODEL = 1024
BATCH = 4
SEQ = 8192
DEPTH = 2

GRID_W = 64
CTX_LEN = 256
N_BRANCH = 4
BRANCH_W = D_MODEL // N_BRANCH
HEAD_DIM = 64
BRANCH_HEADS = BRANCH_W // HEAD_DIM
BLOCK = 128
ROPE_BASE = 10000.0
NORM_EPS = 1e-6
NEG_INF = -1e30
ATT_Q_HEADS = BRANCH_HEADS
ATT_KV_HEADS = BRANCH_HEADS // 2
WINDOW = 128
RET_HEADS = BRANCH_HEADS
RET_CHUNK = 128
RET_GN_EPS = 1e-5
MLA_HEADS = BRANCH_HEADS
MLA_Q_RANK = 256
MLA_KV_RANK = 128
MLA_NOPE = HEAD_DIM
MLA_ROPE = HEAD_DIM // 2
MLA_V = HEAD_DIM
RWKV_HEADS = BRANCH_HEADS
RWKV_DECAY_RANK = 64
RWKV_A_RANK = 64
RWKV_GATE_RANK = 128
RWKV_GN_EPS = 64e-5
N_EXPERTS = 64
N_GROUPS = 8
EXPERTS_PER_GROUP = N_EXPERTS // N_GROUPS
TOP_K = 2
EXPERT_FF = 512
MOE_BLOCK = 256
A_SIZES = (ATT_Q_HEADS * HEAD_DIM, ATT_KV_HEADS * HEAD_DIM, ATT_KV_HEADS * HEAD_DIM)
RET_SIZES = (BRANCH_W,) * 5
MLA_SIZES = (MLA_Q_RANK, MLA_KV_RANK, MLA_ROPE)
RWKV_SIZES = (BRANCH_W, BRANCH_W, BRANCH_W, RWKV_DECAY_RANK, RWKV_DECAY_RANK,
              RWKV_A_RANK, RWKV_A_RANK, RWKV_GATE_RANK)
GROUP_SIZES = (sum(A_SIZES), sum(RET_SIZES), sum(MLA_SIZES), sum(RWKV_SIZES))
N_IN = sum(GROUP_SIZES)
F32 = jnp.float32


def split_sizes(x, sizes):
    out, o = [], 0
    for n in sizes:
        out.append(x[..., o:o + n])
        o += n
    return out


def heads(x, h):
    return x.reshape(x.shape[:-1] + (h, x.shape[-1] // h))


def rms_norm(x, g):
    xf = x.astype(F32)
    y = xf * lax.rsqrt(jnp.mean(xf * xf, axis=-1, keepdims=True) + NORM_EPS)
    return (y * g.astype(F32)).astype(x.dtype)


def head_norm(x, eps):
    xf = x.astype(F32)
    xc = xf - jnp.mean(xf, axis=-1, keepdims=True)
    return xc * lax.rsqrt(jnp.mean(xc * xc, axis=-1, keepdims=True) + eps)


def adaln(h, g, shift, scale):
    return rms_norm(h, g) * (1 + scale) + shift


def rope_tables(pos, dim):
    inv = ROPE_BASE ** (-jnp.arange(0, dim, 2, dtype=F32) / dim)
    ang = pos[:, None] * inv[None, :]
    return jnp.cos(ang), jnp.sin(ang)


def rope_rotate(x, cos, sin):
    m = x.shape[-1] // 2
    x1, x2 = x[..., :m], x[..., m:]
    c = cos[None, :, None, :]
    s = sin[None, :, None, :]
    return jnp.concatenate([x1 * c - x2 * s, x2 * c + x1 * s], axis=-1).astype(x.dtype)


def axial_tables(seq, dim):
    rows = seq // GRID_W
    row = jnp.repeat(jnp.arange(rows, dtype=F32), GRID_W)
    col = jnp.broadcast_to(jnp.arange(GRID_W, dtype=F32)[None, :], (rows, GRID_W)).reshape(-1)
    return rope_tables(row, dim // 2), rope_tables(col, dim // 2)


def axial_rope(x, tabs):
    (cos_r, sin_r), (cos_c, sin_c) = tabs
    half = x.shape[-1] // 2
    return jnp.concatenate([rope_rotate(x[..., :half], cos_r, sin_r),
                            rope_rotate(x[..., half:], cos_c, sin_c)], axis=-1)


def window_gqa(q, k, v, k_ctx, v_ctx, sink):
    b, s, hq, d = q.shape
    g = hq // ATT_KV_HEADS
    nb = s // BLOCK
    nw = 3 * BLOCK
    nc = k_ctx.shape[1]
    qb = (q * d ** -0.5).reshape(b, nb, BLOCK, ATT_KV_HEADS, g, d)
    pad = ((0, 0), (BLOCK, BLOCK), (0, 0), (0, 0))
    kp = jnp.pad(k, pad).reshape(b, nb + 2, BLOCK, ATT_KV_HEADS, d)
    vp = jnp.pad(v, pad).reshape(b, nb + 2, BLOCK, ATT_KV_HEADS, d)
    kw = jnp.concatenate([kp[:, :-2], kp[:, 1:-1], kp[:, 2:]], axis=2)
    vw = jnp.concatenate([vp[:, :-2], vp[:, 1:-1], vp[:, 2:]], axis=2)
    blk0 = jnp.arange(nb)[:, None, None] * BLOCK
    q_pos = blk0 + jnp.arange(BLOCK)[None, :, None]
    k_pos = blk0 - BLOCK + jnp.arange(nw)[None, None, :]
    valid = (jnp.abs(k_pos - q_pos) <= WINDOW) & (k_pos >= 0) & (k_pos < s)
    s_win = jnp.einsum('bnqhgd,bnkhd->bnhgqk', qb, kw).astype(F32)
    s_win = jnp.where(valid[None, :, None, None], s_win, NEG_INF)
    s_ctx = jnp.einsum('bnqhgd,bchd->bnhgqc', qb, k_ctx).astype(F32)
    s_sink = jnp.broadcast_to(sink.astype(F32).reshape(ATT_KV_HEADS, g, 1, 1), s_win.shape[:-1] + (1,))
    p = jax.nn.softmax(jnp.concatenate([s_win, s_ctx, s_sink], axis=-1), axis=-1).astype(v.dtype)
    o = (jnp.einsum('bnhgqk,bnkhd->bnqhgd', p[..., :nw], vw)
         + jnp.einsum('bnhgqc,bchd->bnqhgd', p[..., nw:nw + nc], v_ctx))
    return o.reshape(b, s, hq * d)


def ctx_gqa(q, k, v, sink):
    b, l, hq, d = q.shape
    g = hq // ATT_KV_HEADS
    qg = (q * d ** -0.5).reshape(b, l, ATT_KV_HEADS, g, d)
    sc = jnp.einsum('bqhgd,bkhd->bhgqk', qg, k).astype(F32)
    s_sink = jnp.broadcast_to(sink.astype(F32).reshape(ATT_KV_HEADS, g, 1, 1), sc.shape[:-1] + (1,))
    p = jax.nn.softmax(jnp.concatenate([sc, s_sink], axis=-1), axis=-1).astype(v.dtype)
    o = jnp.einsum('bhgqk,bkhd->bqhgd', p[..., :l], v)
    return o.reshape(b, l, hq * d)


def retention_log_gammas():
    lg = jnp.log(1.0 - jnp.exp(jnp.linspace(math.log(1.0 / 32), math.log(1.0 / 512), 2 * RET_HEADS, dtype=F32)))
    return lg[0::2], lg[1::2]


def retention_chunked(q, k, v, log_gamma, s0):
    b, t, h, d = q.shape
    n = t // RET_CHUNK
    cl = RET_CHUNK
    qc = q.reshape(b, n, cl, h, d).astype(F32)
    kc = k.reshape(b, n, cl, h, d).astype(F32)
    vc = v.reshape(b, n, cl, h, d).astype(F32)
    idx = jnp.arange(cl, dtype=F32)
    rel = idx[:, None] - idx[None, :]
    d_intra = jnp.where(rel >= 0, jnp.exp(jnp.maximum(rel, 0.0)[None] * log_gamma[:, None, None]), 0.0)
    a_int = jnp.einsum('bnihd,bnjhd->bnhij', qc, kc) * d_intra
    o = jnp.einsum('bnhij,bnjhe->bnihe', a_int, vc)
    k_dec = kc * jnp.exp((cl - 1 - idx)[:, None] * log_gamma[None, :])[..., None]
    kv = jnp.einsum('bnjhd,bnjhe->nbhde', k_dec, vc)
    g_chunk = jnp.exp(cl * log_gamma)[:, None, None]

    def step(state, kv_n):
        return state * g_chunk + kv_n, state

    s_fin, s_prev = lax.scan(step, s0, kv)
    q_dec = qc * jnp.exp((idx + 1)[:, None] * log_gamma[None, :])[..., None]
    o = o + jnp.einsum('bnihd,nbhde->bnihe', q_dec, s_prev)
    return o.reshape(b, t, h, d), s_fin


RET_TOKENS_PER_STEP = 1024


def _retention_kernel(q_ref, k_ref, v_ref, g_ref, d_ref, kdec_ref, qdec_ref, gch_ref, s0_ref,
                      y_ref, sfin_ref, s_scr, *, reverse, n_chunks):
    cl = RET_CHUNK
    nh, hd = s_scr.shape[:2]
    tb = q_ref.shape[1]

    @pl.when(pl.program_id(1) == 0)
    def _():
        s_scr[...] = s0_ref[0]

    nb = nh * n_chunks
    shp = (nb, cl, hd)
    per_chunk = lambda z: jnp.broadcast_to(z[:, None], (nh, n_chunks) + z.shape[1:]).reshape((nb,) + z.shape[1:])
    q = _split_heads(q_ref[0], nh).reshape(shp)
    k = _split_heads(k_ref[0], nh).reshape(shp)
    v = _split_heads(v_ref[0], nh).reshape(shp)
    a = _dot(q, k, _NT, 1) * per_chunk(d_ref[...])
    o = _dot(a, v, _NN, 1)
    kv = _dot(k * per_chunk(kdec_ref[...]), v, _TN, 1).reshape(nh, n_chunks, hd, hd)

    s = s_scr[...]
    gch = gch_ref[...]
    starts = [None] * n_chunks
    order = range(n_chunks - 1, -1, -1) if reverse else range(n_chunks)
    for n in order:
        starts[n] = s
        s = s * gch + kv[:, n]
    s_scr[...] = s
    sfin_ref[0] = s
    s_in = jnp.stack(starts, axis=1).reshape(nb, hd, hd)
    o = o + _dot(q * per_chunk(qdec_ref[...]), s_in, _NN, 1)

    oc = o - jnp.mean(o, axis=-1, keepdims=True)
    normed = oc * lax.rsqrt(jnp.mean(oc * oc, axis=-1, keepdims=True) + RET_GN_EPS)
    g = g_ref[0]
    y_ref[0] = (g * jax.nn.sigmoid(g)) * _join_heads(normed.reshape(nh, tb, hd))


def _split_heads(x, nh):
    hd = x.shape[-1] // nh
    return jnp.stack([x[:, h * hd:(h + 1) * hd] for h in range(nh)], axis=0)


def _join_heads(y):
    return jnp.concatenate([y[h] for h in range(y.shape[0])], axis=-1)


def retention_direction(q, k, v, gate, log_gamma, s0, reverse):
    bsz, t, width = q.shape
    nh, hd = s0.shape[1:3]
    tb = min(RET_TOKENS_PER_STEP, t)
    nt = t // tb
    assert t == nt * tb and tb % RET_CHUNK == 0
    cl = RET_CHUNK
    idx = jnp.arange(cl, dtype=F32)
    rel = (idx[None, :] - idx[:, None]) if reverse else (idx[:, None] - idx[None, :])
    lg = log_gamma[:, None, None]
    d_intra = jnp.where(rel >= 0, jnp.exp(jnp.maximum(rel, 0.0)[None] * lg), 0.0)
    kdec = jnp.exp((idx if reverse else cl - 1 - idx)[None, :, None] * lg)
    qdec = jnp.exp((cl - idx if reverse else idx + 1)[None, :, None] * lg)
    gch = jnp.exp(cl * lg)
    if reverse:
        tmap = lambda bi, j: (bi, nt - 1 - j, 0)
    else:
        tmap = lambda bi, j: (bi, j, 0)
    seq_spec = pl.BlockSpec((1, tb, width), tmap)
    st_spec = pl.BlockSpec((1, nh, hd, hd), lambda bi, j: (bi, 0, 0, 0))
    const = lambda shape: pl.BlockSpec(shape, lambda bi, j: (0,) * len(shape))
    y, sfin = pl.pallas_call(
        functools.partial(_retention_kernel, reverse=reverse, n_chunks=tb // cl),
        grid=(bsz, nt),
        in_specs=[seq_spec] * 4 + [const((nh, cl, cl)), const((nh, cl, 1)), const((nh, cl, 1)),
                                   const((nh, 1, 1)), st_spec],
        out_specs=[seq_spec, st_spec],
        out_shape=[jax.ShapeDtypeStruct((bsz, t, width), F32),
                   jax.ShapeDtypeStruct((bsz, nh, hd, hd), F32)],
        scratch_shapes=[pltpu.VMEM((nh, hd, hd), F32)],
        compiler_params=pltpu.CompilerParams(dimension_semantics=("parallel", "arbitrary"),
                                             vmem_limit_bytes=RWKV_VMEM_LIMIT),
        name="retention_rev" if reverse else "retention_fwd",
    )(q, k, v, gate, d_intra, kdec, qdec, gch, s0)
    return y, sfin


def retention_mix(pb, cos, sin, lg_f, lg_b, s0f, s0b):
    bsz, t = pb.shape[:2]
    q, k, v, gf, gb = split_sizes(pb, RET_SIZES)
    q = rope_rotate(heads(q, RET_HEADS), cos, sin).reshape(bsz, t, -1)
    k = (rope_rotate(heads(k, RET_HEADS), cos, sin) * HEAD_DIM ** -0.5).reshape(bsz, t, -1)
    yf, sf = retention_direction(q, k, v, gf, lg_f, s0f, False)
    yb, sb = retention_direction(q, k, v, gb, lg_b, s0b, True)
    return (yf + yb).astype(pb.dtype), sf, sb


def mla_project(pc, lp):
    cq, ckv, kr = split_sizes(pc, MLA_SIZES)
    q = heads(rms_norm(cq, lp['g_qnorm']) @ lp['w_uq'], MLA_HEADS)
    kv = heads(rms_norm(ckv, lp['g_kvnorm']) @ lp['w_ukv'], MLA_HEADS)
    return q[..., :MLA_NOPE], q[..., MLA_NOPE:], kv[..., :MLA_NOPE], kr[:, :, None, :], kv[..., MLA_NOPE:]


LANES = 128
SUBLANES = 8
MLA_Q_TILE = 512
MLA_KEY_UNIT = 256
MLA_KEY_TILE_MAX = 8448
MLA_VMEM_LIMIT = 56 * 1024 * 1024


def _mla_flash_kernel(q_ref, k_ref, v_ref, o_ref, m_scr, acc_scr, *, scale, tk, n_tiles):
    dv = o_ref.shape[-1]
    q = (q_ref[0, 0] * scale).astype(BF16)
    m_scr[...] = jnp.full_like(m_scr, NEG_INF)
    acc_scr[...] = jnp.zeros_like(acc_scr)

    def sweep(c, carry):
        keys = pl.ds(pl.multiple_of(c * tk, tk), tk)
        s = lax.dot_general(q, k_ref[0, 0, keys, :], (((1,), (1,)), ((), ())), preferred_element_type=F32)
        m_prev = m_scr[...]
        m_new = jnp.maximum(m_prev, jnp.max(s, axis=-1, keepdims=True))
        p = jnp.exp2(s - m_new).astype(BF16)
        acc_scr[...] = (jnp.exp2(m_prev - m_new) * acc_scr[...]
                        + jnp.dot(p, v_ref[0, 0, keys, :], preferred_element_type=F32))
        m_scr[...] = m_new
        return carry

    lax.fori_loop(0, n_tiles, sweep, 0)
    acc = acc_scr[...]
    o_ref[0, 0] = acc[:, :dv] / acc[:, dv:dv + 1]


def mla_attend(qn, qr, kn, kr, v):
    b, t, h, _ = qn.shape
    n = kn.shape[1]
    dv = v.shape[-1]
    scale = (MLA_NOPE + MLA_ROPE) ** -0.5 * math.log2(math.e)
    q = jnp.swapaxes(jnp.concatenate([qn, qr], axis=-1), 1, 2)
    kr_h = jnp.broadcast_to(kr, (b, n, h, MLA_ROPE))
    k = jnp.swapaxes(jnp.concatenate([kn, kr_h], axis=-1), 1, 2).astype(BF16)
    ones = jnp.ones((b, n, h, 1), v.dtype)
    zeros = jnp.zeros((b, n, h, LANES - dv - 1), v.dtype)
    vt = jnp.swapaxes(jnp.concatenate([v, ones, zeros], axis=-1), 1, 2).astype(BF16)
    dq = q.shape[-1]
    tq = min(MLA_Q_TILE, t)
    tk = max(d for d in range(MLA_KEY_UNIT, MLA_KEY_TILE_MAX + 1, MLA_KEY_UNIT) if n % d == 0)
    assert t % tq == 0
    o = pl.pallas_call(
        functools.partial(_mla_flash_kernel, scale=scale, tk=tk, n_tiles=n // tk),
        grid=(b, h, t // tq),
        in_specs=[pl.BlockSpec((1, 1, tq, dq), lambda bi, hi, qi: (bi, hi, qi, 0)),
                  pl.BlockSpec((1, 1, n, dq), lambda bi, hi, qi: (bi, hi, 0, 0)),
                  pl.BlockSpec((1, 1, n, LANES), lambda bi, hi, qi: (bi, hi, 0, 0))],
        out_specs=pl.BlockSpec((1, 1, tq, dv), lambda bi, hi, qi: (bi, hi, qi, 0)),
        out_shape=jax.ShapeDtypeStruct((b, h, t, dv), F32),
        scratch_shapes=[pltpu.VMEM((tq, 1), F32), pltpu.VMEM((tq, LANES), F32)],
        compiler_params=pltpu.CompilerParams(
            dimension_semantics=("parallel", "parallel", "arbitrary"),
            vmem_limit_bytes=MLA_VMEM_LIMIT),
        name="mla_flash",
    )(q, k, vt)
    return jnp.swapaxes(o, 1, 2).reshape(b, t, h * dv)


def centred_shift(p):
    prev = jnp.pad(p, ((0, 0), (1, 0), (0, 0)))[:, :-1]
    nxt = jnp.pad(p, ((0, 0), (0, 1), (0, 0)))[:, 1:]
    return 0.5 * (prev + nxt)


def rwkv_features(pd, lp):
    z = pd + (centred_shift(pd) - pd) * lp['rwkv_mu']
    zr, zk, zv, zwf, zwb, zaf, zab, zg = split_sizes(z, RWKV_SIZES)
    r = heads(zr, RWKV_HEADS)
    k = heads(zk, RWKV_HEADS)
    v = heads(zv, RWKV_HEADS)
    g = jax.nn.sigmoid(zg) @ lp['rwkv_g_up']
    kk = (k * heads(lp['rwkv_k_k'], RWKV_HEADS)).astype(F32)
    kk = kk / jnp.maximum(jnp.sqrt(jnp.sum(kk * kk, axis=-1, keepdims=True)), 1e-12)
    k_a = heads(lp['rwkv_k_a'], RWKV_HEADS)
    dirs = []
    for di, (zw, za) in enumerate(((zwf, zaf), (zwb, zab))):
        logw = -jax.nn.softplus(-(lp['rwkv_w0'][di] + jnp.tanh(zw) @ lp['rwkv_w_up'][di]).astype(F32)) - 0.5
        w = heads(-jnp.exp(logw), RWKV_HEADS)
        a = heads(jax.nn.sigmoid((lp['rwkv_a0'][di] + za @ lp['rwkv_a_up'][di]).astype(F32)), RWKV_HEADS)
        dirs.append((w, k * (1 + (a - 1) * k_a), kk * a))
    return r, v, g, kk, dirs


def rwkv_scan(r, w, k, v, kk, b_, s0, reverse):
    xs = tuple(jnp.moveaxis(z.astype(F32), 1, 0) for z in (r, w, k, v, kk, b_))

    def step(st, inp):
        r_t, w_t, k_t, v_t, kk_t, b_t = inp
        sa = jnp.einsum('bhvk,bhk->bhv', st, -kk_t)
        st = st * w_t[:, :, None, :] + sa[..., None] * b_t[:, :, None, :] + v_t[..., None] * k_t[:, :, None, :]
        return st, jnp.einsum('bhvk,bhk->bhv', st, r_t)

    s_fin, y = lax.scan(step, s0, xs, reverse=reverse)
    return jnp.moveaxis(y, 0, 1), s_fin


BF16 = jnp.bfloat16
RWKV_CHUNK = 64
RWKV_SUB = 16
RWKV_TOKENS_PER_STEP = 512
RWKV_PASSES = (1, 1)
RWKV_VMEM_LIMIT = 48 * 1024 * 1024
_NN = (((2,), (1,)), ((0,), (0,)))
_NT = (((2,), (2,)), ((0,), (0,)))
_TN = (((1,), (1,)), ((0,), (0,)))


def _bf16_parts(a, n):
    parts, rem = [], a
    for i in range(n):
        p = rem.astype(BF16)
        parts.append(p)
        if i + 1 < n:
            rem = rem - p.astype(F32)
    return parts


def _dot(a, b, dims, passes):
    if passes == 1:
        return lax.dot_general(a.astype(BF16), b.astype(BF16), dims, preferred_element_type=F32)
    a_hi, a_lo = _bf16_parts(a, 2)
    b_hi, b_lo = _bf16_parts(b, 2)
    out = lax.dot_general(a_hi, b_hi, dims, preferred_element_type=F32)
    out = out + lax.dot_general(a_hi, b_lo, dims, preferred_element_type=F32)
    return out + lax.dot_general(a_lo, b_hi, dims, preferred_element_type=F32)


def _unit_lower_inverse(l_mat, same_sub, passes):
    mm = lambda a, b: _dot(a, b, _NN, passes)
    ld = jnp.where(same_sub, l_mat, 0.0)
    lo = l_mat - ld
    p2 = mm(ld, ld)
    d = p2 - ld - mm(ld, p2)
    pw = p2
    span = 4
    while span < RWKV_SUB:
        pw = mm(pw, pw)
        d = d + pw + mm(d, pw)
        span *= 2
    n1 = lo + mm(d, lo)
    n2 = mm(n1, n1)
    x = n2 - n1 - mm(n1, n2)
    span = 4
    pw = n2
    while span < RWKV_CHUNK // RWKV_SUB:
        pw = mm(pw, pw)
        x = x + pw + mm(x, pw)
        span *= 2
    return x + d + mm(x, d)


def _rwkv_scan_kernel(r_ref, lw_ref, k_ref, v_ref, kk_ref, b_ref, s0_ref, y_ref, sfin_ref, s_scr,
                      *, reverse, n_chunks):
    cl = RWKV_CHUNK
    nh, hd = s_scr.shape[:2]
    tb = r_ref.shape[1]

    @pl.when(pl.program_id(1) == 0)
    def _():
        s_scr[...] = s0_ref[0]

    nb = nh * n_chunks
    shp = (nb, cl, hd)
    r = _split_heads(r_ref[0], nh).reshape(shp)
    lw = _split_heads(lw_ref[0], nh).reshape(shp)
    k = _split_heads(k_ref[0], nh).reshape(shp)
    v = _split_heads(v_ref[0], nh).reshape(shp)
    kk = _split_heads(kk_ref[0], nh).reshape(shp)
    b = _split_heads(b_ref[0], nh).reshape(shp)

    row = lax.broadcasted_iota(jnp.int32, (cl, cl), 0)
    col = lax.broadcasted_iota(jnp.int32, (cl, cl), 1)
    before = (col > row) if reverse else (col < row)
    upto = before | (col == row)
    same_sub = (row // RWKV_SUB) == (col // RWKV_SUB)

    tri = jnp.broadcast_to(jnp.where(upto, 1.0, 0.0).astype(BF16)[None], (nb, cl, cl))
    cum = sum(lax.dot_general(tri, p, _NN, preferred_element_type=F32) for p in _bf16_parts(lw, 3))
    last = 0 if reverse else cl - 1
    mid = cl // 2 if reverse else cl // 2 - 1
    tot = cum[:, last:last + 1, :]
    rho = cum[:, mid:mid + 1, :]
    cum_ex = cum - lw

    a_t = kk * jnp.exp(cum_ex - rho)
    r_t = r * jnp.exp(cum - rho)
    e_out = jnp.exp(rho - cum)
    b_t = b * e_out
    k_t = k * e_out
    a_0 = kk * jnp.exp(cum_ex)
    r_0 = r * jnp.exp(cum)
    e_end = jnp.exp(tot - cum)
    b_e = b * e_end
    k_e = k * e_end

    ps, pt = RWKV_PASSES
    l_mat = jnp.where(before, _dot(a_t, b_t, _NT, ps), 0.0)
    m_ak = jnp.where(before, _dot(a_t, k_t, _NT, ps), 0.0)
    m_rb = jnp.where(upto, _dot(r_t, b_t, _NT, ps), 0.0)
    m_rk = jnp.where(upto, _dot(r_t, k_t, _NT, ps), 0.0)
    t_m1 = _unit_lower_inverse(l_mat, same_sub, pt)

    mv = _dot(m_ak, v, _NN, ps)
    q = -(a_0 + _dot(t_m1, a_0, _NN, ps))
    w = -(mv + _dot(t_m1, mv, _NN, ps))
    r_h = r_0 + _dot(m_rb, q, _NN, ps)
    y_loc = _dot(m_rb, w, _NN, ps) + _dot(m_rk, v, _NN, ps)
    eye = lax.broadcasted_iota(jnp.int32, (hd, hd), 0) == lax.broadcasted_iota(jnp.int32, (hd, hd), 1)
    phi = (jnp.where(eye, jnp.exp(tot), 0.0) + _dot(b_e, q, _TN, ps)).reshape(nh, n_chunks, hd, hd)
    psi = (_dot(b_e, w, _TN, ps) + _dot(k_e, v, _TN, ps)).reshape(nh, n_chunks, hd, hd)

    s = s_scr[...]
    starts = [None] * n_chunks
    order = range(n_chunks - 1, -1, -1) if reverse else range(n_chunks)
    for n in order:
        starts[n] = s
        s = _dot(phi[:, n], s, _NN, 3) + psi[:, n]
    s_scr[...] = s
    sfin_ref[0] = s
    s_in = jnp.stack(starts, axis=1).reshape(nb, hd, hd)
    y = _dot(r_h, s_in, _NN, ps) + y_loc
    y_ref[0] = _join_heads(y.reshape(nh, tb, hd))


def rwkv_scan_blocked(r, lw, k, v, kk, b_, s0t, reverse):
    bsz, t, width = r.shape
    nh, hd = s0t.shape[1:3]
    tb = min(RWKV_TOKENS_PER_STEP, t)
    nt = t // tb
    assert t == nt * tb and tb % RWKV_CHUNK == 0
    if reverse:
        tmap = lambda bi, j: (bi, nt - 1 - j, 0)
    else:
        tmap = lambda bi, j: (bi, j, 0)
    seq_spec = pl.BlockSpec((1, tb, width), tmap)
    st_spec = pl.BlockSpec((1, nh, hd, hd), lambda bi, j: (bi, 0, 0, 0))
    y, sfin = pl.pallas_call(
        functools.partial(_rwkv_scan_kernel, reverse=reverse, n_chunks=tb // RWKV_CHUNK),
        grid=(bsz, nt),
        in_specs=[seq_spec] * 6 + [st_spec],
        out_specs=[seq_spec, st_spec],
        out_shape=[jax.ShapeDtypeStruct((bsz, t, width), F32),
                   jax.ShapeDtypeStruct((bsz, nh, hd, hd), F32)],
        scratch_shapes=[pltpu.VMEM((nh, hd, hd), F32)],
        compiler_params=pltpu.CompilerParams(dimension_semantics=("parallel", "arbitrary"),
                                             vmem_limit_bytes=RWKV_VMEM_LIMIT),
        name="rwkv_scan_rev" if reverse else "rwkv_scan_fwd",
    )(r, lw, k, v, kk, b_, s0t)
    return y, sfin


def rwkv_mix(feat, lp, s0f, s0b):
    r, v, g, kk, dirs = feat
    (lwf, kf, bf), (lwb, kb, bb) = dirs
    flat = lambda z: z.astype(F32).reshape(z.shape[0], z.shape[1], -1)
    rt, vt, kkt = flat(r), flat(v), flat(kk)
    yf, sf = rwkv_scan_blocked(rt, flat(lwf), flat(kf), vt, kkt, flat(bf), s0f, False)
    yb, sb = rwkv_scan_blocked(rt, flat(lwb), flat(kb), vt, kkt, flat(bb), s0b, True)
    y = head_norm(heads(yf + yb, RWKV_HEADS), RWKV_GN_EPS)
    y = y * heads(lp['rwkv_lnx_w'], RWKV_HEADS) + heads(lp['rwkv_lnx_b'], RWKV_HEADS)
    bonus = jnp.sum(r * (kf + kb) * lp['rwkv_r_k'], axis=-1, keepdims=True) * v
    out = (y + bonus).reshape(g.shape) * g
    return out.astype(g.dtype), sf, sb


MERGE_ROW_TILE = 512
MERGE_VMEM_LIMIT = 52 * 1024 * 1024


def _merge_kernel(u_ref, ya_ref, yb_ref, yc_ref, yd_ref, wg_ref, bg_ref, wb_ref, wo_ref, o_ref):
    u = u_ref[...].astype(BF16)
    acc = None
    for n, y_ref in enumerate((ya_ref, yb_ref, yc_ref, yd_ref)):
        gate = jax.nn.sigmoid(jnp.dot(u, wg_ref[n], preferred_element_type=F32) + bg_ref[n])
        term = gate * jnp.dot(y_ref[...].astype(BF16), wb_ref[n], preferred_element_type=F32)
        acc = term if acc is None else acc + term
    o_ref[...] = jnp.dot(acc.astype(BF16), wo_ref[...], preferred_element_type=F32)


def merge_branches(u, ys, lp):
    lead, d = u.shape[:-1], u.shape[-1]
    rows = math.prod(lead)
    tm = min(MERGE_ROW_TILE, rows)
    assert rows % tm == 0
    bw = ys[0].shape[-1]
    row_spec = lambda width: pl.BlockSpec((tm, width), lambda i: (i, 0))
    whole = lambda shape: pl.BlockSpec(shape, lambda i: (0,) * len(shape))
    out = pl.pallas_call(
        _merge_kernel,
        grid=(rows // tm,),
        in_specs=[row_spec(d)] + [row_spec(bw)] * N_BRANCH
                 + [whole((N_BRANCH, d, d)), whole((N_BRANCH, 1, d)), whole((N_BRANCH, bw, d)), whole((d, d))],
        out_specs=row_spec(d),
        out_shape=jax.ShapeDtypeStruct((rows, d), F32),
        compiler_params=pltpu.CompilerParams(dimension_semantics=("parallel",),
                                             vmem_limit_bytes=MERGE_VMEM_LIMIT),
        name="merge_branches",
    )(u.reshape(rows, d), *[y.reshape(rows, bw) for y in ys],
      lp['w_gate'].astype(BF16), lp['b_gate'][:, None, :], lp['w_branch'].astype(BF16), lp['w_out'].astype(BF16))
    return out.reshape(lead + (d,))


def token_mixers(u_ctx, u_lat, lp, need_ctx):
    b, s, _ = u_lat.shape
    l = u_ctx.shape[1]
    w_groups = split_sizes(lp['w_in'], GROUP_SIZES)
    pa_l, pb_l, pc_l, pd_l = [u_lat @ w for w in w_groups]
    pa_c, pb_c, pc_c, pd_c = [u_ctx @ w for w in w_groups]
    tab_att = axial_tables(s, HEAD_DIM)
    tab_mla = axial_tables(s, MLA_ROPE)

    aq_l, ak_l, av_l = split_sizes(pa_l, A_SIZES)
    aq_c, ak_c, av_c = split_sizes(pa_c, A_SIZES)
    k_c = heads(ak_c, ATT_KV_HEADS)
    v_c = heads(av_c, ATT_KV_HEADS)
    ya_l = window_gqa(axial_rope(heads(aq_l, ATT_Q_HEADS), tab_att),
                      axial_rope(heads(ak_l, ATT_KV_HEADS), tab_att),
                      heads(av_l, ATT_KV_HEADS), k_c, v_c, lp['sink'])

    lg_f, lg_b = retention_log_gammas()
    cos_c, sin_c = rope_tables(jnp.arange(l, dtype=F32), HEAD_DIM)
    cos_l, sin_l = rope_tables(l + jnp.arange(s, dtype=F32), HEAD_DIM)
    zero_ret = jnp.zeros((b, RET_HEADS, HEAD_DIM, HEAD_DIM), F32)
    yb_c, sbf, sbb = retention_mix(pb_c, cos_c, sin_c, lg_f, lg_b, zero_ret, zero_ret)
    yb_l, _, _ = retention_mix(pb_l, cos_l, sin_l, lg_f, lg_b, sbf, sbb)

    qn_l, qr_l, kn_l, kr_l, vm_l = mla_project(pc_l, lp)
    qn_c, qr_c, kn_c, kr_c, vm_c = mla_project(pc_c, lp)
    qr_l = axial_rope(qr_l, tab_mla)
    kr_l = axial_rope(kr_l, tab_mla)
    yc_l = mla_attend(qn_l, qr_l, jnp.concatenate([kn_c, kn_l], axis=1),
                      jnp.concatenate([kr_c, kr_l], axis=1), jnp.concatenate([vm_c, vm_l], axis=1))

    zero_wkv = jnp.zeros((b, RWKV_HEADS, HEAD_DIM, HEAD_DIM), F32)
    yd_c, sdf, sdb = rwkv_mix(rwkv_features(pd_c, lp), lp, zero_wkv, zero_wkv)
    yd_l, _, _ = rwkv_mix(rwkv_features(pd_l, lp), lp, sdf, sdb)

    m_lat = merge_branches(u_lat, (ya_l, yb_l, yc_l, yd_l), lp)
    if not need_ctx:
        return None, m_lat
    ya_c = ctx_gqa(heads(aq_c, ATT_Q_HEADS), k_c, v_c, lp['sink'])
    yc_c = mla_attend(qn_c, qr_c, kn_c, kr_c, vm_c)
    m_ctx = merge_branches(u_ctx, (ya_c, yb_c, yc_c, yd_c), lp)
    return m_ctx, m_lat


def moe_ffn(x, router_w, router_b, layer_idx, w_e_gate, w_e_up, w_e_down):
    n_tok, dm = x.shape
    scores = jax.nn.sigmoid(x.astype(F32) @ router_w.astype(F32))
    grp = (scores + router_b.astype(F32)).reshape(n_tok, N_GROUPS, EXPERTS_PER_GROUP)
    g_sel = jnp.argmax(jnp.sum(lax.top_k(grp, TOP_K)[0], axis=-1), axis=-1)
    in_grp = lax.top_k(jnp.take_along_axis(grp, g_sel[:, None, None], axis=1)[:, 0], TOP_K)[1]
    e_idx = g_sel[:, None] * EXPERTS_PER_GROUP + in_grp
    w_sel = jnp.take_along_axis(scores, e_idx, axis=1)
    w_sel = w_sel / jnp.sum(w_sel, axis=-1, keepdims=True)
    n_asg = n_tok * TOP_K
    flat_e = e_idx.reshape(-1).astype(jnp.int32)
    flat_w = w_sel.reshape(-1)
    order = jnp.argsort(flat_e).astype(jnp.int32)
    rank = jnp.argsort(order).astype(jnp.int32)
    onehot = flat_e[:, None] == jnp.arange(N_EXPERTS, dtype=jnp.int32)[None, :]
    counts = jnp.sum(onehot, axis=0, dtype=jnp.int32)
    padded = (counts + MOE_BLOCK - 1) // MOE_BLOCK * MOE_BLOCK
    pad_end = jnp.cumsum(padded)
    pad_start = pad_end - padded
    start = jnp.cumsum(counts) - counts
    n_blocks = -(-n_asg // MOE_BLOCK) + N_EXPERTS
    n_rows = n_blocks * MOE_BLOCK
    blk_row0 = jnp.arange(n_blocks, dtype=jnp.int32) * MOE_BLOCK
    blk_e = jnp.minimum(jnp.sum(pad_end[None, :] <= blk_row0[:, None], axis=1, dtype=jnp.int32), N_EXPERTS - 1)
    off = (blk_row0 - pad_start[blk_e])[:, None] + jnp.arange(MOE_BLOCK, dtype=jnp.int32)[None, :]
    valid = (off < counts[blk_e][:, None]).reshape(-1)
    src = order[jnp.clip(start[blk_e][:, None] + off, 0, n_asg - 1).reshape(-1)]
    row_tok = jnp.where(valid, src // TOP_K, 0)
    row_w = jnp.where(valid, flat_w[src], 0.0)
    blk_valid = jnp.clip(counts[blk_e] - (blk_row0 - pad_start[blk_e]), 0, MOE_BLOCK)
    blk_groups = (blk_valid + MOE_GATHER_UNROLL - 1) // MOE_GATHER_UNROLL
    yb = moe_expert_blocks(blk_e, blk_groups.astype(jnp.int32), row_tok, x, row_w[:, None], layer_idx,
                           w_e_gate, w_e_up, w_e_down)
    shift = jnp.sum(jnp.where(onehot, (pad_start - start)[None, :], 0), axis=1, dtype=jnp.int32)
    slot = (rank + shift).reshape(n_tok, TOP_K)
    out = yb[slot[:, 0]]
    for j in range(1, TOP_K):
        out = out + yb[slot[:, j]]
    return out.astype(x.dtype)


MOE_VMEM_LIMIT = 40 * 1024 * 1024
MOE_GATHER_UNROLL = 8


def _moe_row_copy(x_hbm, xbuf, sem, tok, slot, r):
    return pltpu.make_async_copy(x_hbm.at[pl.ds(tok, 1)], xbuf.at[slot, pl.ds(r, 1)], sem.at[slot])


def _moe_expert_kernel(blk_e_ref, blk_groups_ref, row_tok_ref, x_hbm, w_ref, wg_ref, wu_ref, wd_ref, o_ref,
                       xbuf, sem):
    del blk_e_ref
    i = pl.program_id(0)
    slot = i % 2

    def for_each_row(block, fn):
        def body(gi, carry):
            for q in range(MOE_GATHER_UNROLL):
                fn(gi * MOE_GATHER_UNROLL + q, q)
            return carry
        lax.fori_loop(0, blk_groups_ref[block], body, 0)

    def issue(block, dst_slot):
        for_each_row(block, lambda r, q: _moe_row_copy(
            x_hbm, xbuf, sem, row_tok_ref[block * MOE_BLOCK + r], dst_slot, r).start(priority=q % 2))

    @pl.when(i == 0)
    def _():
        xbuf[...] = jnp.zeros_like(xbuf)
        issue(0, 0)

    @pl.when(i + 1 < pl.num_programs(0))
    def _():
        issue(i + 1, 1 - slot)

    for_each_row(i, lambda r, q: _moe_row_copy(x_hbm, xbuf, sem, 0, slot, r).wait())

    @pl.when(blk_groups_ref[i] == 0)
    def _():
        o_ref[...] = jnp.zeros_like(o_ref)

    @pl.when(blk_groups_ref[i] > 0)
    def _():
        x = jnp.concatenate([xbuf[slot, :, j, :] for j in range(xbuf.shape[2])], axis=-1).astype(BF16)
        g = jnp.dot(x, wg_ref[0, 0].astype(BF16), preferred_element_type=F32)
        u = jnp.dot(x, wu_ref[0, 0].astype(BF16), preferred_element_type=F32)
        hid = (g * jax.nn.sigmoid(g)) * u
        y = jnp.dot(hid.astype(BF16), wd_ref[0, 0].astype(BF16), preferred_element_type=F32)
        o_ref[...] = y * w_ref[...]


def moe_expert_blocks(blk_e, blk_groups, row_tok, x, row_w, layer_idx, w_e_gate, w_e_up, w_e_down):
    n_rows = row_tok.shape[0]
    n_tok, dm = x.shape
    n_blocks = n_rows // MOE_BLOCK
    ff = w_e_gate.shape[-1]
    assert dm == SUBLANES * LANES
    x = x.reshape(n_tok, SUBLANES, LANES)
    row_spec = lambda width: pl.BlockSpec((MOE_BLOCK, width), lambda i, be, bg, rt: (i, 0))
    expert_spec = lambda rows, cols: pl.BlockSpec((1, 1, rows, cols),
                                                  lambda i, be, bg, rt: (layer_idx, be[i], 0, 0))
    return pl.pallas_call(
        _moe_expert_kernel,
        grid_spec=pltpu.PrefetchScalarGridSpec(
            num_scalar_prefetch=3,
            grid=(n_blocks,),
            in_specs=[pl.BlockSpec(memory_space=pl.ANY), row_spec(1),
                      expert_spec(dm, ff), expert_spec(dm, ff), expert_spec(ff, dm)],
            out_specs=row_spec(dm),
            scratch_shapes=[pltpu.VMEM((2, MOE_BLOCK, SUBLANES, LANES), x.dtype), pltpu.SemaphoreType.DMA((2,))]),
        out_shape=jax.ShapeDtypeStruct((n_rows, dm), F32),
        compiler_params=pltpu.CompilerParams(dimension_semantics=("arbitrary",),
                                             vmem_limit_bytes=MOE_VMEM_LIMIT),
        name="moe_experts",
    )(blk_e, blk_groups, row_tok, x, row_w, w_e_gate, w_e_up, w_e_down)


def layer(h_ctx, h_lat, c, c_ctx, lp, router_w, router_b, need_ctx):
    b, s, d = h_lat.shape
    l = h_ctx.shape[1]
    m_lat = jnp.split((jax.nn.silu(c) @ lp['w_ada'] + lp['b_ada'])[:, None, :], 6, axis=-1)
    m_ctx = jnp.split(jax.nn.silu(c_ctx) @ lp['w_ada'] + lp['b_ada'], 6, axis=-1)
    u_lat = adaln(h_lat, lp['g_norm1'], m_lat[0], m_lat[1])
    u_ctx = adaln(h_ctx, lp['g_norm1'], m_ctx[0], m_ctx[1])
    mix_ctx, mix_lat = token_mixers(u_ctx, u_lat, lp, need_ctx)
    h_lat = h_lat + m_lat[2] * mix_lat
    f_lat = adaln(h_lat, lp['g_norm2'], m_lat[3], m_lat[4]).reshape(b * s, d)
    if not need_ctx:
        ffn = moe_ffn(f_lat, router_w, router_b, lp['layer_idx'], *lp['experts'])
        return h_ctx, h_lat + m_lat[5] * ffn.reshape(b, s, d)
    h_ctx = h_ctx + m_ctx[2] * mix_ctx
    f_ctx = adaln(h_ctx, lp['g_norm2'], m_ctx[3], m_ctx[4]).reshape(b * l, d)
    ffn = moe_ffn(jnp.concatenate([f_ctx, f_lat], axis=0), router_w, router_b, lp['layer_idx'], *lp['experts'])
    h_ctx = h_ctx + m_ctx[5] * ffn[:b * l].reshape(b, l, d)
    h_lat = h_lat + m_lat[5] * ffn[b * l:].reshape(b, s, d)
    return h_ctx, h_lat


def _final_norm_kernel(x_ref, g_ref, o_ref):
    x = x_ref[...]
    y = x * lax.rsqrt(jnp.mean(x * x, axis=-1, keepdims=True) + NORM_EPS)
    o_ref[...] = y * g_ref[...]


def final_rms_norm(x, g):
    b, s, d = x.shape
    rows = b * s
    tile = 1024
    out = pl.pallas_call(
        _final_norm_kernel,
        grid=(rows // tile,),
        in_specs=[pl.BlockSpec((tile, d), lambda i: (i, 0)), pl.BlockSpec((1, d), lambda i: (0, 0))],
        out_specs=pl.BlockSpec((tile, d), lambda i: (i, 0)),
        out_shape=jax.ShapeDtypeStruct((rows, d), x.dtype),
        name="final_rms_norm",
    )(x.reshape(rows, d), g.reshape(1, d))
    return out.reshape(b, s, d)


_LAYER_PARAM_NAMES = (
    'w_ada', 'b_ada', 'g_norm1', 'g_norm2', 'w_in', 'sink', 'g_qnorm', 'g_kvnorm', 'w_uq',
    'w_ukv', 'rwkv_mu', 'rwkv_w0', 'rwkv_w_up', 'rwkv_a0', 'rwkv_a_up', 'rwkv_g_up',
    'rwkv_k_k', 'rwkv_k_a', 'rwkv_r_k', 'rwkv_lnx_w', 'rwkv_lnx_b', 'w_gate', 'b_gate',
    'w_branch', 'w_out')


def kernel(x, c, ctx, c_ctx, w_ada, b_ada, g_norm1, g_norm2, w_in, sink, g_qnorm, g_kvnorm,
           w_uq, w_ukv, rwkv_mu, rwkv_w0, rwkv_w_up, rwkv_a0, rwkv_a_up, rwkv_g_up, rwkv_k_k,
           rwkv_k_a, rwkv_r_k, rwkv_lnx_w, rwkv_lnx_b, w_gate, b_gate, w_branch, w_out,
           router_w, router_b, w_e_gate, w_e_up, w_e_down, g_final):
    stacked = (w_ada, b_ada, g_norm1, g_norm2, w_in, sink, g_qnorm, g_kvnorm, w_uq, w_ukv, rwkv_mu,
               rwkv_w0, rwkv_w_up, rwkv_a0, rwkv_a_up, rwkv_g_up, rwkv_k_k, rwkv_k_a, rwkv_r_k,
               rwkv_lnx_w, rwkv_lnx_b, w_gate, b_gate, w_branch, w_out)
    h_ctx, h_lat = ctx, x
    for i in range(DEPTH):
        lp = {n: a[i] for n, a in zip(_LAYER_PARAM_NAMES, stacked)}
        lp['layer_idx'] = i
        lp['experts'] = (w_e_gate, w_e_up, w_e_down)
        h_ctx, h_lat = layer(h_ctx, h_lat, c, c_ctx, lp, router_w, router_b, i < DEPTH - 1)
    return final_rms_norm(h_lat, g_final)
```

```python
import functools
import math

import jax
import jax.numpy as jnp
from jax import lax
from jax.experimental import pallas as pl
from jax.experimental.pallas import tpu as pltpu

D_MODEL = 1024
BATCH = 4
SEQ = 8192
DEPTH = 2

GRID_W = 64
CTX_LEN = 256
N_BRANCH = 4
BRANCH_W = D_MODEL // N_BRANCH
HEAD_DIM = 64
BRANCH_HEADS = BRANCH_W // HEAD_DIM
BLOCK = 128
ROPE_BASE = 10000.0
NORM_EPS = 1e-6
NEG_INF = -1e30
ATT_Q_HEADS = BRANCH_HEADS
ATT_KV_HEADS = BRANCH_HEADS // 2
WINDOW = 128
RET_HEADS = BRANCH_HEADS
RET_CHUNK = 128
RET_GN_EPS = 1e-5
MLA_HEADS = BRANCH_HEADS
MLA_Q_RANK = 256
MLA_KV_RANK = 128
MLA_NOPE = HEAD_DIM
MLA_ROPE = HEAD_DIM // 2
MLA_V = HEAD_DIM
RWKV_HEADS = BRANCH_HEADS
RWKV_DECAY_RANK = 64
RWKV_A_RANK = 64
RWKV_GATE_RANK = 128
RWKV_GN_EPS = 64e-5
N_EXPERTS = 64
N_GROUPS = 8
EXPERTS_PER_GROUP = N_EXPERTS // N_GROUPS
TOP_K = 2
EXPERT_FF = 512
MOE_BLOCK = 256
A_SIZES = (ATT_Q_HEADS * HEAD_DIM, ATT_KV_HEADS * HEAD_DIM, ATT_KV_HEADS * HEAD_DIM)
RET_SIZES = (BRANCH_W,) * 5
MLA_SIZES = (MLA_Q_RANK, MLA_KV_RANK, MLA_ROPE)
RWKV_SIZES = (BRANCH_W, BRANCH_W, BRANCH_W, RWKV_DECAY_RANK, RWKV_DECAY_RANK,
              RWKV_A_RANK, RWKV_A_RANK, RWKV_GATE_RANK)
GROUP_SIZES = (sum(A_SIZES), sum(RET_SIZES), sum(MLA_SIZES), sum(RWKV_SIZES))
N_IN = sum(GROUP_SIZES)
F32 = jnp.float32


def split_sizes(x, sizes):
    out, o = [], 0
    for n in sizes:
        out.append(x[..., o:o + n])
        o += n
    return out


def heads(x, h):
    return x.reshape(x.shape[:-1] + (h, x.shape[-1] // h))


def rms_norm(x, g):
    xf = x.astype(F32)
    y = xf * lax.rsqrt(jnp.mean(xf * xf, axis=-1, keepdims=True) + NORM_EPS)
    return (y * g.astype(F32)).astype(x.dtype)


def head_norm(x, eps):
    xf = x.astype(F32)
    xc = xf - jnp.mean(xf, axis=-1, keepdims=True)
    return xc * lax.rsqrt(jnp.mean(xc * xc, axis=-1, keepdims=True) + eps)


def adaln(h, g, shift, scale):
    return rms_norm(h, g) * (1 + scale) + shift


def rope_tables(pos, dim):
    inv = ROPE_BASE ** (-jnp.arange(0, dim, 2, dtype=F32) / dim)
    ang = pos[:, None] * inv[None, :]
    return jnp.cos(ang), jnp.sin(ang)


def rope_rotate(x, cos, sin):
    m = x.shape[-1] // 2
    x1, x2 = x[..., :m], x[..., m:]
    c = cos[None, :, None, :]
    s = sin[None, :, None, :]
    return jnp.concatenate([x1 * c - x2 * s, x2 * c + x1 * s], axis=-1).astype(x.dtype)


def axial_tables(seq, dim):
    rows = seq // GRID_W
    row = jnp.repeat(jnp.arange(rows, dtype=F32), GRID_W)
    col = jnp.broadcast_to(jnp.arange(GRID_W, dtype=F32)[None, :], (rows, GRID_W)).reshape(-1)
    return rope_tables(row, dim // 2), rope_tables(col, dim // 2)


def axial_rope(x, tabs):
    (cos_r, sin_r), (cos_c, sin_c) = tabs
    half = x.shape[-1] // 2
    return jnp.concatenate([rope_rotate(x[..., :half], cos_r, sin_r),
                            rope_rotate(x[..., half:], cos_c, sin_c)], axis=-1)


def window_gqa(q, k, v, k_ctx, v_ctx, sink):
    b, s, hq, d = q.shape
    g = hq // ATT_KV_HEADS
    nb = s // BLOCK
    nw = 3 * BLOCK
    nc = k_ctx.shape[1]
    qb = (q * d ** -0.5).reshape(b, nb, BLOCK, ATT_KV_HEADS, g, d)
    pad = ((0, 0), (BLOCK, BLOCK), (0, 0), (0, 0))
    kp = jnp.pad(k, pad).reshape(b, nb + 2, BLOCK, ATT_KV_HEADS, d)
    vp = jnp.pad(v, pad).reshape(b, nb + 2, BLOCK, ATT_KV_HEADS, d)
    kw = jnp.concatenate([kp[:, :-2], kp[:, 1:-1], kp[:, 2:]], axis=2)
    vw = jnp.concatenate([vp[:, :-2], vp[:, 1:-1], vp[:, 2:]], axis=2)
    blk0 = jnp.arange(nb)[:, None, None] * BLOCK
    q_pos = blk0 + jnp.arange(BLOCK)[None, :, None]
    k_pos = blk0 - BLOCK + jnp.arange(nw)[None, None, :]
    valid = (jnp.abs(k_pos - q_pos) <= WINDOW) & (k_pos >= 0) & (k_pos < s)
    s_win = jnp.einsum('bnqhgd,bnkhd->bnhgqk', qb, kw).astype(F32)
    s_win = jnp.where(valid[None, :, None, None], s_win, NEG_INF)
    s_ctx = jnp.einsum('bnqhgd,bchd->bnhgqc', qb, k_ctx).astype(F32)
    s_sink = jnp.broadcast_to(sink.astype(F32).reshape(ATT_KV_HEADS, g, 1, 1), s_win.shape[:-1] + (1,))
    p = jax.nn.softmax(jnp.concatenate([s_win, s_ctx, s_sink], axis=-1), axis=-1).astype(v.dtype)
    o = (jnp.einsum('bnhgqk,bnkhd->bnqhgd', p[..., :nw], vw)
         + jnp.einsum('bnhgqc,bchd->bnqhgd', p[..., nw:nw + nc], v_ctx))
    return o.reshape(b, s, hq * d)


def ctx_gqa(q, k, v, sink):
    b, l, hq, d = q.shape
    g = hq // ATT_KV_HEADS
    qg = (q * d ** -0.5).reshape(b, l, ATT_KV_HEADS, g, d)
    sc = jnp.einsum('bqhgd,bkhd->bhgqk', qg, k).astype(F32)
    s_sink = jnp.broadcast_to(sink.astype(F32).reshape(ATT_KV_HEADS, g, 1, 1), sc.shape[:-1] + (1,))
    p = jax.nn.softmax(jnp.concatenate([sc, s_sink], axis=-1), axis=-1).astype(v.dtype)
    o = jnp.einsum('bhgqk,bkhd->bqhgd', p[..., :l], v)
    return o.reshape(b, l, hq * d)


def retention_log_gammas():
    lg = jnp.log(1.0 - jnp.exp(jnp.linspace(math.log(1.0 / 32), math.log(1.0 / 512), 2 * RET_HEADS, dtype=F32)))
    return lg[0::2], lg[1::2]


def retention_chunked(q, k, v, log_gamma, s0):
    b, t, h, d = q.shape
    n = t // RET_CHUNK
    cl = RET_CHUNK
    qc = q.reshape(b, n, cl, h, d).astype(F32)
    kc = k.reshape(b, n, cl, h, d).astype(F32)
    vc = v.reshape(b, n, cl, h, d).astype(F32)
    idx = jnp.arange(cl, dtype=F32)
    rel = idx[:, None] - idx[None, :]
    d_intra = jnp.where(rel >= 0, jnp.exp(jnp.maximum(rel, 0.0)[None] * log_gamma[:, None, None]), 0.0)
    a_int = jnp.einsum('bnihd,bnjhd->bnhij', qc, kc) * d_intra
    o = jnp.einsum('bnhij,bnjhe->bnihe', a_int, vc)
    k_dec = kc * jnp.exp((cl - 1 - idx)[:, None] * log_gamma[None, :])[..., None]
    kv = jnp.einsum('bnjhd,bnjhe->nbhde', k_dec, vc)
    g_chunk = jnp.exp(cl * log_gamma)[:, None, None]

    def step(state, kv_n):
        return state * g_chunk + kv_n, state

    s_fin, s_prev = lax.scan(step, s0, kv)
    q_dec = qc * jnp.exp((idx + 1)[:, None] * log_gamma[None, :])[..., None]
    o = o + jnp.einsum('bnihd,nbhde->bnihe', q_dec, s_prev)
    return o.reshape(b, t, h, d), s_fin


RET_TOKENS_PER_STEP = 1024


def _retention_kernel(q_ref, k_ref, v_ref, g_ref, d_ref, kdec_ref, qdec_ref, gch_ref, s0_ref,
                      y_ref, sfin_ref, s_scr, *, reverse, n_chunks):
    cl = RET_CHUNK
    nh, hd = s_scr.shape[:2]
    tb = q_ref.shape[1]

    @pl.when(pl.program_id(1) == 0)
    def _():
        s_scr[...] = s0_ref[0]

    nb = nh * n_chunks
    shp = (nb, cl, hd)
    per_chunk = lambda z: jnp.broadcast_to(z[:, None], (nh, n_chunks) + z.shape[1:]).reshape((nb,) + z.shape[1:])
    q = _split_heads(q_ref[0], nh).reshape(shp)
    k = _split_heads(k_ref[0], nh).reshape(shp)
    v = _split_heads(v_ref[0], nh).reshape(shp)
    a = _dot(q, k, _NT, 1) * per_chunk(d_ref[...])
    o = _dot(a, v, _NN, 1)
    kv = _dot(k * per_chunk(kdec_ref[...]), v, _TN, 1).reshape(nh, n_chunks, hd, hd)

    s = s_scr[...]
    gch = gch_ref[...]
    starts = [None] * n_chunks
    order = range(n_chunks - 1, -1, -1) if reverse else range(n_chunks)
    for n in order:
        starts[n] = s
        s = s * gch + kv[:, n]
    s_scr[...] = s
    sfin_ref[0] = s
    s_in = jnp.stack(starts, axis=1).reshape(nb, hd, hd)
    o = o + _dot(q * per_chunk(qdec_ref[...]), s_in, _NN, 1)

    oc = o - jnp.mean(o, axis=-1, keepdims=True)
    normed = oc * lax.rsqrt(jnp.mean(oc * oc, axis=-1, keepdims=True) + RET_GN_EPS)
    g = g_ref[0]
    y_ref[0] = (g * jax.nn.sigmoid(g)) * _join_heads(normed.reshape(nh, tb, hd))


def _split_heads(x, nh):
    hd = x.shape[-1] // nh
    return jnp.stack([x[:, h * hd:(h + 1) * hd] for h in range(nh)], axis=0)


def _join_heads(y):
    return jnp.concatenate([y[h] for h in range(y.shape[0])], axis=-1)


def retention_direction(q, k, v, gate, log_gamma, s0, reverse):
    bsz, t, width = q.shape
    nh, hd = s0.shape[1:3]
    tb = min(RET_TOKENS_PER_STEP, t)
    nt = t // tb
    assert t == nt * tb and tb % RET_CHUNK == 0
    cl = RET_CHUNK
    idx = jnp.arange(cl, dtype=F32)
    rel = (idx[None, :] - idx[:, None]) if reverse else (idx[:, None] - idx[None, :])
    lg = log_gamma[:, None, None]
    d_intra = jnp.where(rel >= 0, jnp.exp(jnp.maximum(rel, 0.0)[None] * lg), 0.0)
    kdec = jnp.exp((idx if reverse else cl - 1 - idx)[None, :, None] * lg)
    qdec = jnp.exp((cl - idx if reverse else idx + 1)[None, :, None] * lg)
    gch = jnp.exp(cl * lg)
    if reverse:
        tmap = lambda bi, j: (bi, nt - 1 - j, 0)
    else:
        tmap = lambda bi, j: (bi, j, 0)
    seq_spec = pl.BlockSpec((1, tb, width), tmap)
    st_spec = pl.BlockSpec((1, nh, hd, hd), lambda bi, j: (bi, 0, 0, 0))
    const = lambda shape: pl.BlockSpec(shape, lambda bi, j: (0,) * len(shape))
    y, sfin = pl.pallas_call(
        functools.partial(_retention_kernel, reverse=reverse, n_chunks=tb // cl),
        grid=(bsz, nt),
        in_specs=[seq_spec] * 4 + [const((nh, cl, cl)), const((nh, cl, 1)), const((nh, cl, 1)),
                                   const((nh, 1, 1)), st_spec],
        out_specs=[seq_spec, st_spec],
        out_shape=[jax.ShapeDtypeStruct((bsz, t, width), F32),
                   jax.ShapeDtypeStruct((bsz, nh, hd, hd), F32)],
        scratch_shapes=[pltpu.VMEM((nh, hd, hd), F32)],
        compiler_params=pltpu.CompilerParams(dimension_semantics=("parallel", "arbitrary"),
                                             vmem_limit_bytes=RWKV_VMEM_LIMIT),
        name="retention_rev" if reverse else "retention_fwd",
    )(q, k, v, gate, d_intra, kdec, qdec, gch, s0)
    return y, sfin


def retention_mix(pb, cos, sin, lg_f, lg_b, s0f, s0b):
    bsz, t = pb.shape[:2]
    q, k, v, gf, gb = split_sizes(pb, RET_SIZES)
    q = rope_rotate(heads(q, RET_HEADS), cos, sin).reshape(bsz, t, -1)
    k = (rope_rotate(heads(k, RET_HEADS), cos, sin) * HEAD_DIM ** -0.5).reshape(bsz, t, -1)
    yf, sf = retention_direction(q, k, v, gf, lg_f, s0f, False)
    yb, sb = retention_direction(q, k, v, gb, lg_b, s0b, True)
    return (yf + yb).astype(pb.dtype), sf, sb


def mla_project(pc, lp):
    cq, ckv, kr = split_sizes(pc, MLA_SIZES)
    q = heads(rms_norm(cq, lp['g_qnorm']) @ lp['w_uq'], MLA_HEADS)
    kv = heads(rms_norm(ckv, lp['g_kvnorm']) @ lp['w_ukv'], MLA_HEADS)
    return q[..., :MLA_NOPE], q[..., MLA_NOPE:], kv[..., :MLA_NOPE], kr[:, :, None, :], kv[..., MLA_NOPE:]


LANES = 128
SUBLANES = 8
MLA_Q_TILE = 512
MLA_KEY_UNIT = 256
MLA_KEY_TILE_MAX = 8448
MLA_VMEM_LIMIT = 56 * 1024 * 1024


def _mla_flash_kernel(q_ref, k_ref, v_ref, o_ref, m_scr, acc_scr, *, scale, tk, n_tiles):
    dv = o_ref.shape[-1]
    q = (q_ref[0, 0] * scale).astype(BF16)
    m_scr[...] = jnp.full_like(m_scr, NEG_INF)
    acc_scr[...] = jnp.zeros_like(acc_scr)

    def sweep(c, carry):
        keys = pl.ds(pl.multiple_of(c * tk, tk), tk)
        s = lax.dot_general(q, k_ref[0, 0, keys, :], (((1,), (1,)), ((), ())), preferred_element_type=F32)
        m_prev = m_scr[...]
        m_new = jnp.maximum(m_prev, jnp.max(s, axis=-1, keepdims=True))
        p = jnp.exp2(s - m_new).astype(BF16)
        acc_scr[...] = (jnp.exp2(m_prev - m_new) * acc_scr[...]
                        + jnp.dot(p, v_ref[0, 0, keys, :], preferred_element_type=F32))
        m_scr[...] = m_new
        return carry

    lax.fori_loop(0, n_tiles, sweep, 0)
    acc = acc_scr[...]
    o_ref[0, 0] = acc[:, :dv] / acc[:, dv:dv + 1]


def mla_attend(qn, qr, kn, kr, v):
    b, t, h, _ = qn.shape
    n = kn.shape[1]
    dv = v.shape[-1]
    scale = (MLA_NOPE + MLA_ROPE) ** -0.5 * math.log2(math.e)
    q = jnp.swapaxes(jnp.concatenate([qn, qr], axis=-1), 1, 2)
    kr_h = jnp.broadcast_to(kr, (b, n, h, MLA_ROPE))
    k = jnp.swapaxes(jnp.concatenate([kn, kr_h], axis=-1), 1, 2).astype(BF16)
    ones = jnp.ones((b, n, h, 1), v.dtype)
    zeros = jnp.zeros((b, n, h, LANES - dv - 1), v.dtype)
    vt = jnp.swapaxes(jnp.concatenate([v, ones, zeros], axis=-1), 1, 2).astype(BF16)
    dq = q.shape[-1]
    tq = min(MLA_Q_TILE, t)
    tk = max(d for d in range(MLA_KEY_UNIT, MLA_KEY_TILE_MAX + 1, MLA_KEY_UNIT) if n % d == 0)
    assert t % tq == 0
    o = pl.pallas_call(
        functools.partial(_mla_flash_kernel, scale=scale, tk=tk, n_tiles=n // tk),
        grid=(b, h, t // tq),
        in_specs=[pl.BlockSpec((1, 1, tq, dq), lambda bi, hi, qi: (bi, hi, qi, 0)),
                  pl.BlockSpec((1, 1, n, dq), lambda bi, hi, qi: (bi, hi, 0, 0)),
                  pl.BlockSpec((1, 1, n, LANES), lambda bi, hi, qi: (bi, hi, 0, 0))],
        out_specs=pl.BlockSpec((1, 1, tq, dv), lambda bi, hi, qi: (bi, hi, qi, 0)),
        out_shape=jax.ShapeDtypeStruct((b, h, t, dv), F32),
        scratch_shapes=[pltpu.VMEM((tq, 1), F32), pltpu.VMEM((tq, LANES), F32)],
        compiler_params=pltpu.CompilerParams(
            dimension_semantics=("parallel", "parallel", "arbitrary"),
            vmem_limit_bytes=MLA_VMEM_LIMIT),
        name="mla_flash",
    )(q, k, vt)
    return jnp.swapaxes(o, 1, 2).reshape(b, t, h * dv)


def centred_shift(p):
    prev = jnp.pad(p, ((0, 0), (1, 0), (0, 0)))[:, :-1]
    nxt = jnp.pad(p, ((0, 0), (0, 1), (0, 0)))[:, 1:]
    return 0.5 * (prev + nxt)


def rwkv_features(pd, lp):
    z = pd + (centred_shift(pd) - pd) * lp['rwkv_mu']
    zr, zk, zv, zwf, zwb, zaf, zab, zg = split_sizes(z, RWKV_SIZES)
    r = heads(zr, RWKV_HEADS)
    k = heads(zk, RWKV_HEADS)
    v = heads(zv, RWKV_HEADS)
    g = jax.nn.sigmoid(zg) @ lp['rwkv_g_up']
    kk = (k * heads(lp['rwkv_k_k'], RWKV_HEADS)).astype(F32)
    kk = kk / jnp.maximum(jnp.sqrt(jnp.sum(kk * kk, axis=-1, keepdims=True)), 1e-12)
    k_a = heads(lp['rwkv_k_a'], RWKV_HEADS)
    dirs = []
    for di, (zw, za) in enumerate(((zwf, zaf), (zwb, zab))):
        logw = -jax.nn.softplus(-(lp['rwkv_w0'][di] + jnp.tanh(zw) @ lp['rwkv_w_up'][di]).astype(F32)) - 0.5
        w = heads(-jnp.exp(logw), RWKV_HEADS)
        a = heads(jax.nn.sigmoid((lp['rwkv_a0'][di] + za @ lp['rwkv_a_up'][di]).astype(F32)), RWKV_HEADS)
        dirs.append((w, k * (1 + (a - 1) * k_a), kk * a))
    return r, v, g, kk, dirs


def rwkv_scan(r, w, k, v, kk, b_, s0, reverse):
    xs = tuple(jnp.moveaxis(z.astype(F32), 1, 0) for z in (r, w, k, v, kk, b_))

    def step(st, inp):
        r_t, w_t, k_t, v_t, kk_t, b_t = inp
        sa = jnp.einsum('bhvk,bhk->bhv', st, -kk_t)
        st = st * w_t[:, :, None, :] + sa[..., None] * b_t[:, :, None, :] + v_t[..., None] * k_t[:, :, None, :]
        return st, jnp.einsum('bhvk,bhk->bhv', st, r_t)

    s_fin, y = lax.scan(step, s0, xs, reverse=reverse)
    return jnp.moveaxis(y, 0, 1), s_fin


BF16 = jnp.bfloat16
RWKV_CHUNK = 64
RWKV_SUB = 16
RWKV_TOKENS_PER_STEP = 512
RWKV_PASSES = (1, 1)
RWKV_VMEM_LIMIT = 48 * 1024 * 1024
_NN = (((2,), (1,)), ((0,), (0,)))
_NT = (((2,), (2,)), ((0,), (0,)))
_TN = (((1,), (1,)), ((0,), (0,)))


def _bf16_parts(a, n):
    parts, rem = [], a
    for i in range(n):
        p = rem.astype(BF16)
        parts.append(p)
        if i + 1 < n:
            rem = rem - p.astype(F32)
    return parts


def _dot(a, b, dims, passes):
    if passes == 1:
        return lax.dot_general(a.astype(BF16), b.astype(BF16), dims, preferred_element_type=F32)
    a_hi, a_lo = _bf16_parts(a, 2)
    b_hi, b_lo = _bf16_parts(b, 2)
    out = lax.dot_general(a_hi, b_hi, dims, preferred_element_type=F32)
    out = out + lax.dot_general(a_hi, b_lo, dims, preferred_element_type=F32)
    return out + lax.dot_general(a_lo, b_hi, dims, preferred_element_type=F32)


def _unit_lower_inverse(l_mat, same_sub, passes):
    mm = lambda a, b: _dot(a, b, _NN, passes)
    ld = jnp.where(same_sub, l_mat, 0.0)
    lo = l_mat - ld
    p2 = mm(ld, ld)
    d = p2 - ld - mm(ld, p2)
    pw = p2
    span = 4
    while span < RWKV_SUB:
        pw = mm(pw, pw)
        d = d + pw + mm(d, pw)
        span *= 2
    n1 = lo + mm(d, lo)
    n2 = mm(n1, n1)
    x = n2 - n1 - mm(n1, n2)
    span = 4
    pw = n2
    while span < RWKV_CHUNK // RWKV_SUB:
        pw = mm(pw, pw)
        x = x + pw + mm(x, pw)
        span *= 2
    return x + d + mm(x, d)


def _rwkv_scan_kernel(r_ref, lw_ref, k_ref, v_ref, kk_ref, b_ref, s0_ref, y_ref, sfin_ref, s_scr,
                      *, reverse, n_chunks):
    cl = RWKV_CHUNK
    nh, hd = s_scr.shape[:2]
    tb = r_ref.shape[1]

    @pl.when(pl.program_id(1) == 0)
    def _():
        s_scr[...] = s0_ref[0]

    nb = nh * n_chunks
    shp = (nb, cl, hd)
    r = _split_heads(r_ref[0], nh).reshape(shp)
    lw = _split_heads(lw_ref[0], nh).reshape(shp)
    k = _split_heads(k_ref[0], nh).reshape(shp)
    v = _split_heads(v_ref[0], nh).reshape(shp)
    kk = _split_heads(kk_ref[0], nh).reshape(shp)
    b = _split_heads(b_ref[0], nh).reshape(shp)

    row = lax.broadcasted_iota(jnp.int32, (cl, cl), 0)
    col = lax.broadcasted_iota(jnp.int32, (cl, cl), 1)
    before = (col > row) if reverse else (col < row)
    upto = before | (col == row)
    same_sub = (row // RWKV_SUB) == (col // RWKV_SUB)

    tri = jnp.broadcast_to(jnp.where(upto, 1.0, 0.0).astype(BF16)[None], (nb, cl, cl))
    cum = sum(lax.dot_general(tri, p, _NN, preferred_element_type=F32) for p in _bf16_parts(lw, 3))
    last = 0 if reverse else cl - 1
    mid = cl // 2 if reverse else cl // 2 - 1
    tot = cum[:, last:last + 1, :]
    rho = cum[:, mid:mid + 1, :]
    cum_ex = cum - lw

    a_t = kk * jnp.exp(cum_ex - rho)
    r_t = r * jnp.exp(cum - rho)
    e_out = jnp.exp(rho - cum)
    b_t = b * e_out
    k_t = k * e_out
    a_0 = kk * jnp.exp(cum_ex)
    r_0 = r * jnp.exp(cum)
    e_end = jnp.exp(tot - cum)
    b_e = b * e_end
    k_e = k * e_end

    ps, pt = RWKV_PASSES
    l_mat = jnp.where(before, _dot(a_t, b_t, _NT, ps), 0.0)
    m_ak = jnp.where(before, _dot(a_t, k_t, _NT, ps), 0.0)
    m_rb = jnp.where(upto, _dot(r_t, b_t, _NT, ps), 0.0)
    m_rk = jnp.where(upto, _dot(r_t, k_t, _NT, ps), 0.0)
    t_m1 = _unit_lower_inverse(l_mat, same_sub, pt)

    mv = _dot(m_ak, v, _NN, ps)
    q = -(a_0 + _dot(t_m1, a_0, _NN, ps))
    w = -(mv + _dot(t_m1, mv, _NN, ps))
    r_h = r_0 + _dot(m_rb, q, _NN, ps)
    y_loc = _dot(m_rb, w, _NN, ps) + _dot(m_rk, v, _NN, ps)
    eye = lax.broadcasted_iota(jnp.int32, (hd, hd), 0) == lax.broadcasted_iota(jnp.int32, (hd, hd), 1)
    phi = (jnp.where(eye, jnp.exp(tot), 0.0) + _dot(b_e, q, _TN, ps)).reshape(nh, n_chunks, hd, hd)
    psi = (_dot(b_e, w, _TN, ps) + _dot(k_e, v, _TN, ps)).reshape(nh, n_chunks, hd, hd)

    s = s_scr[...]
    starts = [None] * n_chunks
    order = range(n_chunks - 1, -1, -1) if reverse else range(n_chunks)
    for n in order:
        starts[n] = s
        s = _dot(phi[:, n], s, _NN, 3) + psi[:, n]
    s_scr[...] = s
    sfin_ref[0] = s
    s_in = jnp.stack(starts, axis=1).reshape(nb, hd, hd)
    y = _dot(r_h, s_in, _NN, ps) + y_loc
    y_ref[0] = _join_heads(y.reshape(nh, tb, hd))


def rwkv_scan_blocked(r, lw, k, v, kk, b_, s0t, reverse):
    bsz, t, width = r.shape
    nh, hd = s0t.shape[1:3]
    tb = min(RWKV_TOKENS_PER_STEP, t)
    nt = t // tb
    assert t == nt * tb and tb % RWKV_CHUNK == 0
    if reverse:
        tmap = lambda bi, j: (bi, nt - 1 - j, 0)
    else:
        tmap = lambda bi, j: (bi, j, 0)
    seq_spec = pl.BlockSpec((1, tb, width), tmap)
    st_spec = pl.BlockSpec((1, nh, hd, hd), lambda bi, j: (bi, 0, 0, 0))
    y, sfin = pl.pallas_call(
        functools.partial(_rwkv_scan_kernel, reverse=reverse, n_chunks=tb // RWKV_CHUNK),
        grid=(bsz, nt),
        in_specs=[seq_spec] * 6 + [st_spec],
        out_specs=[seq_spec, st_spec],
        out_shape=[jax.ShapeDtypeStruct((bsz, t, width), F32),
                   jax.ShapeDtypeStruct((bsz, nh, hd, hd), F32)],
        scratch_shapes=[pltpu.VMEM((nh, hd, hd), F32)],
        compiler_params=pltpu.CompilerParams(dimension_semantics=("parallel", "arbitrary"),
                                             vmem_limit_bytes=RWKV_VMEM_LIMIT),
        name="rwkv_scan_rev" if reverse else "rwkv_scan_fwd",
    )(r, lw, k, v, kk, b_, s0t)
    return y, sfin


def rwkv_mix(feat, lp, s0f, s0b):
    r, v, g, kk, dirs = feat
    (lwf, kf, bf), (lwb, kb, bb) = dirs
    flat = lambda z: z.astype(F32).reshape(z.shape[0], z.shape[1], -1)
    rt, vt, kkt = flat(r), flat(v), flat(kk)
    yf, sf = rwkv_scan_blocked(rt, flat(lwf), flat(kf), vt, kkt, flat(bf), s0f, False)
    yb, sb = rwkv_scan_blocked(rt, flat(lwb), flat(kb), vt, kkt, flat(bb), s0b, True)
    y = head_norm(heads(yf + yb, RWKV_HEADS), RWKV_GN_EPS)
    y = y * heads(lp['rwkv_lnx_w'], RWKV_HEADS) + heads(lp['rwkv_lnx_b'], RWKV_HEADS)
    bonus = jnp.sum(r * (kf + kb) * lp['rwkv_r_k'], axis=-1, keepdims=True) * v
    out = (y + bonus).reshape(g.shape) * g
    return out.astype(g.dtype), sf, sb


MERGE_ROW_TILE = 512
MERGE_VMEM_LIMIT = 52 * 1024 * 1024


def _merge_kernel(u_ref, ya_ref, yb_ref, yc_ref, yd_ref, h_ref, mod_ref, g2_ref, wg_ref, bg_ref, wb_ref, wo_ref,
                  hn_ref, f_ref):
    u = u_ref[...].astype(BF16)
    acc = None
    for n, y_ref in enumerate((ya_ref, yb_ref, yc_ref, yd_ref)):
        gate = jax.nn.sigmoid(jnp.dot(u, wg_ref[n], preferred_element_type=F32) + bg_ref[n])
        term = gate * jnp.dot(y_ref[...].astype(BF16), wb_ref[n], preferred_element_type=F32)
        acc = term if acc is None else acc + term
    mix = jnp.dot(acc.astype(BF16), wo_ref[...], preferred_element_type=F32)
    mod = mod_ref[0]
    hn = h_ref[...] + mod[0:1] * mix
    hn_ref[...] = hn
    normed = hn * lax.rsqrt(jnp.mean(hn * hn, axis=-1, keepdims=True) + NORM_EPS) * g2_ref[...]
    f_ref[...] = normed * (1 + mod[2:3]) + mod[1:2]


def merge_branches(u, ys, lp, h, mod):
    lead, d = u.shape[:-1], u.shape[-1]
    rows = math.prod(lead)
    tm = min(MERGE_ROW_TILE, rows)
    rows_per_mod = rows // mod.shape[0]
    assert rows % tm == 0 and (mod.shape[0] == 1 or rows_per_mod % tm == 0)
    bw = ys[0].shape[-1]
    row_spec = lambda width: pl.BlockSpec((tm, width), lambda i: (i, 0))
    whole = lambda shape: pl.BlockSpec(shape, lambda i: (0,) * len(shape))
    hn, f = pl.pallas_call(
        _merge_kernel,
        grid=(rows // tm,),
        in_specs=[row_spec(d)] + [row_spec(bw)] * N_BRANCH
                 + [row_spec(d), pl.BlockSpec((1, 3, d), lambda i: ((i * tm) // rows_per_mod, 0, 0)), whole((1, d)),
                    whole((N_BRANCH, d, d)), whole((N_BRANCH, 1, d)), whole((N_BRANCH, bw, d)), whole((d, d))],
        out_specs=[row_spec(d), row_spec(d)],
        out_shape=[jax.ShapeDtypeStruct((rows, d), F32), jax.ShapeDtypeStruct((rows, d), F32)],
        compiler_params=pltpu.CompilerParams(dimension_semantics=("parallel",),
                                             vmem_limit_bytes=MERGE_VMEM_LIMIT),
        name="merge_branches",
    )(u.reshape(rows, d), *[y.reshape(rows, bw) for y in ys], h.reshape(rows, d), mod, lp['g_norm2'][None, :],
      lp['w_gate'].astype(BF16), lp['b_gate'][:, None, :], lp['w_branch'].astype(BF16), lp['w_out'].astype(BF16))
    return hn.reshape(lead + (d,)), f.reshape(lead + (d,))


def token_mixers(u_ctx, u_lat, lp, need_ctx, h_ctx, h_lat, mod_ctx, mod_lat):
    b, s, _ = u_lat.shape
    l = u_ctx.shape[1]
    w_groups = split_sizes(lp['w_in'], GROUP_SIZES)
    pa_l, pb_l, pc_l, pd_l = [u_lat @ w for w in w_groups]
    pa_c, pb_c, pc_c, pd_c = [u_ctx @ w for w in w_groups]
    tab_att = axial_tables(s, HEAD_DIM)
    tab_mla = axial_tables(s, MLA_ROPE)

    aq_l, ak_l, av_l = split_sizes(pa_l, A_SIZES)
    aq_c, ak_c, av_c = split_sizes(pa_c, A_SIZES)
    k_c = heads(ak_c, ATT_KV_HEADS)
    v_c = heads(av_c, ATT_KV_HEADS)
    ya_l = window_gqa(axial_rope(heads(aq_l, ATT_Q_HEADS), tab_att),
                      axial_rope(heads(ak_l, ATT_KV_HEADS), tab_att),
                      heads(av_l, ATT_KV_HEADS), k_c, v_c, lp['sink'])

    lg_f, lg_b = retention_log_gammas()
    cos_c, sin_c = rope_tables(jnp.arange(l, dtype=F32), HEAD_DIM)
    cos_l, sin_l = rope_tables(l + jnp.arange(s, dtype=F32), HEAD_DIM)
    zero_ret = jnp.zeros((b, RET_HEADS, HEAD_DIM, HEAD_DIM), F32)
    yb_c, sbf, sbb = retention_mix(pb_c, cos_c, sin_c, lg_f, lg_b, zero_ret, zero_ret)
    yb_l, _, _ = retention_mix(pb_l, cos_l, sin_l, lg_f, lg_b, sbf, sbb)

    qn_l, qr_l, kn_l, kr_l, vm_l = mla_project(pc_l, lp)
    qn_c, qr_c, kn_c, kr_c, vm_c = mla_project(pc_c, lp)
    qr_l = axial_rope(qr_l, tab_mla)
    kr_l = axial_rope(kr_l, tab_mla)
    yc_l = mla_attend(qn_l, qr_l, jnp.concatenate([kn_c, kn_l], axis=1),
                      jnp.concatenate([kr_c, kr_l], axis=1), jnp.concatenate([vm_c, vm_l], axis=1))

    zero_wkv = jnp.zeros((b, RWKV_HEADS, HEAD_DIM, HEAD_DIM), F32)
    yd_c, sdf, sdb = rwkv_mix(rwkv_features(pd_c, lp), lp, zero_wkv, zero_wkv)
    yd_l, _, _ = rwkv_mix(rwkv_features(pd_l, lp), lp, sdf, sdb)

    out_lat = merge_branches(u_lat, (ya_l, yb_l, yc_l, yd_l), lp, h_lat, mod_lat)
    if not need_ctx:
        return None, out_lat
    ya_c = ctx_gqa(heads(aq_c, ATT_Q_HEADS), k_c, v_c, lp['sink'])
    yc_c = mla_attend(qn_c, qr_c, kn_c, kr_c, vm_c)
    out_ctx = merge_branches(u_ctx, (ya_c, yb_c, yc_c, yd_c), lp, h_ctx, mod_ctx)
    return out_ctx, out_lat


def moe_ffn(x, router_w, router_b, layer_idx, w_e_gate, w_e_up, w_e_down):
    n_tok, dm = x.shape
    scores = jax.nn.sigmoid(x.astype(F32) @ router_w.astype(F32))
    grp = (scores + router_b.astype(F32)).reshape(n_tok, N_GROUPS, EXPERTS_PER_GROUP)
    g_sel = jnp.argmax(jnp.sum(lax.top_k(grp, TOP_K)[0], axis=-1), axis=-1)
    in_grp = lax.top_k(jnp.take_along_axis(grp, g_sel[:, None, None], axis=1)[:, 0], TOP_K)[1]
    e_idx = g_sel[:, None] * EXPERTS_PER_GROUP + in_grp
    w_sel = jnp.take_along_axis(scores, e_idx, axis=1)
    w_sel = w_sel / jnp.sum(w_sel, axis=-1, keepdims=True)
    n_asg = n_tok * TOP_K
    flat_e = e_idx.reshape(-1).astype(jnp.int32)
    flat_w = w_sel.reshape(-1)
    order = jnp.argsort(flat_e).astype(jnp.int32)
    rank = jnp.argsort(order).astype(jnp.int32)
    onehot = flat_e[:, None] == jnp.arange(N_EXPERTS, dtype=jnp.int32)[None, :]
    counts = jnp.sum(onehot, axis=0, dtype=jnp.int32)
    padded = (counts + MOE_BLOCK - 1) // MOE_BLOCK * MOE_BLOCK
    pad_end = jnp.cumsum(padded)
    pad_start = pad_end - padded
    start = jnp.cumsum(counts) - counts
    n_blocks = -(-n_asg // MOE_BLOCK) + N_EXPERTS
    n_rows = n_blocks * MOE_BLOCK
    blk_row0 = jnp.arange(n_blocks, dtype=jnp.int32) * MOE_BLOCK
    blk_e = jnp.minimum(jnp.sum(pad_end[None, :] <= blk_row0[:, None], axis=1, dtype=jnp.int32), N_EXPERTS - 1)
    off = (blk_row0 - pad_start[blk_e])[:, None] + jnp.arange(MOE_BLOCK, dtype=jnp.int32)[None, :]
    valid = (off < counts[blk_e][:, None]).reshape(-1)
    src = order[jnp.clip(start[blk_e][:, None] + off, 0, n_asg - 1).reshape(-1)]
    row_tok = jnp.where(valid, src // TOP_K, 0)
    row_w = jnp.where(valid, flat_w[src], 0.0)
    blk_valid = jnp.clip(counts[blk_e] - (blk_row0 - pad_start[blk_e]), 0, MOE_BLOCK)
    blk_groups = (blk_valid + MOE_GATHER_UNROLL - 1) // MOE_GATHER_UNROLL
    yb = moe_expert_blocks(blk_e, blk_groups.astype(jnp.int32), row_tok, x, row_w[:, None], layer_idx,
                           w_e_gate, w_e_up, w_e_down)
    shift = jnp.sum(jnp.where(onehot, (pad_start - start)[None, :], 0), axis=1, dtype=jnp.int32)
    slot = (rank + shift).reshape(n_tok, TOP_K)
    out = yb[slot[:, 0]]
    for j in range(1, TOP_K):
        out = out + yb[slot[:, j]]
    return out.astype(x.dtype)


MOE_VMEM_LIMIT = 40 * 1024 * 1024
MOE_GATHER_UNROLL = 8


def _moe_row_copy(x_hbm, xbuf, sem, tok, slot, r):
    return pltpu.make_async_copy(x_hbm.at[pl.ds(tok, 1)], xbuf.at[slot, pl.ds(r, 1)], sem.at[slot])


def _moe_expert_kernel(blk_e_ref, blk_groups_ref, row_tok_ref, x_hbm, w_ref, wg_ref, wu_ref, wd_ref, o_ref,
                       xbuf, sem):
    del blk_e_ref
    i = pl.program_id(0)
    slot = i % 2

    def for_each_row(block, fn):
        def body(gi, carry):
            for q in range(MOE_GATHER_UNROLL):
                fn(gi * MOE_GATHER_UNROLL + q, q)
            return carry
        lax.fori_loop(0, blk_groups_ref[block], body, 0)

    def issue(block, dst_slot):
        for_each_row(block, lambda r, q: _moe_row_copy(
            x_hbm, xbuf, sem, row_tok_ref[block * MOE_BLOCK + r], dst_slot, r).start(priority=q % 2))

    @pl.when(i == 0)
    def _():
        xbuf[...] = jnp.zeros_like(xbuf)
        issue(0, 0)

    @pl.when(i + 1 < pl.num_programs(0))
    def _():
        issue(i + 1, 1 - slot)

    for_each_row(i, lambda r, q: _moe_row_copy(x_hbm, xbuf, sem, 0, slot, r).wait())

    @pl.when(blk_groups_ref[i] == 0)
    def _():
        o_ref[...] = jnp.zeros_like(o_ref)

    @pl.when(blk_groups_ref[i] > 0)
    def _():
        x = jnp.concatenate([xbuf[slot, :, j, :] for j in range(xbuf.shape[2])], axis=-1).astype(BF16)
        g = jnp.dot(x, wg_ref[0, 0].astype(BF16), preferred_element_type=F32)
        u = jnp.dot(x, wu_ref[0, 0].astype(BF16), preferred_element_type=F32)
        hid = (g * jax.nn.sigmoid(g)) * u
        y = jnp.dot(hid.astype(BF16), wd_ref[0, 0].astype(BF16), preferred_element_type=F32)
        o_ref[...] = y * w_ref[...]


def moe_expert_blocks(blk_e, blk_groups, row_tok, x, row_w, layer_idx, w_e_gate, w_e_up, w_e_down):
    n_rows = row_tok.shape[0]
    n_tok, dm = x.shape
    n_blocks = n_rows // MOE_BLOCK
    ff = w_e_gate.shape[-1]
    assert dm == SUBLANES * LANES
    x = x.reshape(n_tok, SUBLANES, LANES)
    row_spec = lambda width: pl.BlockSpec((MOE_BLOCK, width), lambda i, be, bg, rt: (i, 0))
    expert_spec = lambda rows, cols: pl.BlockSpec((1, 1, rows, cols),
                                                  lambda i, be, bg, rt: (layer_idx, be[i], 0, 0))
    return pl.pallas_call(
        _moe_expert_kernel,
        grid_spec=pltpu.PrefetchScalarGridSpec(
            num_scalar_prefetch=3,
            grid=(n_blocks,),
            in_specs=[pl.BlockSpec(memory_space=pl.ANY), row_spec(1),
                      expert_spec(dm, ff), expert_spec(dm, ff), expert_spec(ff, dm)],
            out_specs=row_spec(dm),
            scratch_shapes=[pltpu.VMEM((2, MOE_BLOCK, SUBLANES, LANES), x.dtype), pltpu.SemaphoreType.DMA((2,))]),
        out_shape=jax.ShapeDtypeStruct((n_rows, dm), F32),
        compiler_params=pltpu.CompilerParams(dimension_semantics=("arbitrary",),
                                             vmem_limit_bytes=MOE_VMEM_LIMIT),
        name="moe_experts",
    )(blk_e, blk_groups, row_tok, x, row_w, w_e_gate, w_e_up, w_e_down)


def layer(h_ctx, h_lat, c, c_ctx, lp, router_w, router_b, need_ctx):
    b, s, d = h_lat.shape
    l = h_ctx.shape[1]
    m_lat = jnp.split((jax.nn.silu(c) @ lp['w_ada'] + lp['b_ada'])[:, None, :], 6, axis=-1)
    m_ctx = jnp.split(jax.nn.silu(c_ctx) @ lp['w_ada'] + lp['b_ada'], 6, axis=-1)
    u_lat = adaln(h_lat, lp['g_norm1'], m_lat[0], m_lat[1])
    u_ctx = adaln(h_ctx, lp['g_norm1'], m_ctx[0], m_ctx[1])
    mod_lat = jnp.concatenate(m_lat[2:5], axis=1)
    mod_ctx = jnp.stack(m_ctx[2:5], axis=0)[None]
    out_ctx, (h_lat, f_lat) = token_mixers(u_ctx, u_lat, lp, need_ctx, h_ctx, h_lat, mod_ctx, mod_lat)
    f_lat = f_lat.reshape(b * s, d)
    if not need_ctx:
        ffn = moe_ffn(f_lat, router_w, router_b, lp['layer_idx'], *lp['experts'])
        return h_ctx, h_lat + m_lat[5] * ffn.reshape(b, s, d)
    h_ctx, f_ctx = out_ctx
    f_ctx = f_ctx.reshape(b * l, d)
    ffn = moe_ffn(jnp.concatenate([f_ctx, f_lat], axis=0), router_w, router_b, lp['layer_idx'], *lp['experts'])
    h_ctx = h_ctx + m_ctx[5] * ffn[:b * l].reshape(b, l, d)
    h_lat = h_lat + m_lat[5] * ffn[b * l:].reshape(b, s, d)
    return h_ctx, h_lat


def _final_norm_kernel(x_ref, g_ref, o_ref):
    x = x_ref[...]
    y = x * lax.rsqrt(jnp.mean(x * x, axis=-1, keepdims=True) + NORM_EPS)
    o_ref[...] = y * g_ref[...]


def final_rms_norm(x, g):
    b, s, d = x.shape
    rows = b * s
    tile = 1024
    out = pl.pallas_call(
        _final_norm_kernel,
        grid=(rows // tile,),
        in_specs=[pl.BlockSpec((tile, d), lambda i: (i, 0)), pl.BlockSpec((1, d), lambda i: (0, 0))],
        out_specs=pl.BlockSpec((tile, d), lambda i: (i, 0)),
        out_shape=jax.ShapeDtypeStruct((rows, d), x.dtype),
        name="final_rms_norm",
    )(x.reshape(rows, d), g.reshape(1, d))
    return out.reshape(b, s, d)


_LAYER_PARAM_NAMES = (
    'w_ada', 'b_ada', 'g_norm1', 'g_norm2', 'w_in', 'sink', 'g_qnorm', 'g_kvnorm', 'w_uq',
    'w_ukv', 'rwkv_mu', 'rwkv_w0', 'rwkv_w_up', 'rwkv_a0', 'rwkv_a_up', 'rwkv_g_up',
    'rwkv_k_k', 'rwkv_k_a', 'rwkv_r_k', 'rwkv_lnx_w', 'rwkv_lnx_b', 'w_gate', 'b_gate',
    'w_branch', 'w_out')


def kernel(x, c, ctx, c_ctx, w_ada, b_ada, g_norm1, g_norm2, w_in, sink, g_qnorm, g_kvnorm,
           w_uq, w_ukv, rwkv_mu, rwkv_w0, rwkv_w_up, rwkv_a0, rwkv_a_up, rwkv_g_up, rwkv_k_k,
           rwkv_k_a, rwkv_r_k, rwkv_lnx_w, rwkv_lnx_b, w_gate, b_gate, w_branch, w_out,
           router_w, router_b, w_e_gate, w_e_up, w_e_down, g_final):
    stacked = (w_ada, b_ada, g_norm1, g_norm2, w_in, sink, g_qnorm, g_kvnorm, w_uq, w_ukv, rwkv_mu,
               rwkv_w0, rwkv_w_up, rwkv_a0, rwkv_a_up, rwkv_g_up, rwkv_k_k, rwkv_k_a, rwkv_r_k,
               rwkv_lnx_w, rwkv_lnx_b, w_gate, b_gate, w_branch, w_out)
    h_ctx, h_lat = ctx, x
    for i in range(DEPTH):
        lp = {n: a[i] for n, a in zip(_LAYER_PARAM_NAMES, stacked)}
        lp['layer_idx'] = i
        lp['experts'] = (w_e_gate, w_e_up, w_e_down)
        h_ctx, h_lat = layer(h_ctx, h_lat, c, c_ctx, lp, router_w, router_b, i < DEPTH - 1)
    return final_rms_norm(h_lat, g_final)
```

```python
import functools
import math

import jax
import jax.numpy as jnp
from jax import lax
from jax.experimental import pallas as pl
from jax.experimental.pallas import tpu as pltpu

D_MODEL = 1024
BATCH = 4
SEQ = 8192
DEPTH = 2

GRID_W = 64
CTX_LEN = 256
N_BRANCH = 4
BRANCH_W = D_MODEL // N_BRANCH
HEAD_DIM = 64
BRANCH_HEADS = BRANCH_W // HEAD_DIM
BLOCK = 128
ROPE_BASE = 10000.0
NORM_EPS = 1e-6
NEG_INF = -1e30
ATT_Q_HEADS = BRANCH_HEADS
ATT_KV_HEADS = BRANCH_HEADS // 2
WINDOW = 128
RET_HEADS = BRANCH_HEADS
RET_CHUNK = 128
RET_GN_EPS = 1e-5
MLA_HEADS = BRANCH_HEADS
MLA_Q_RANK = 256
MLA_KV_RANK = 128
MLA_NOPE = HEAD_DIM
MLA_ROPE = HEAD_DIM // 2
MLA_V = HEAD_DIM
RWKV_HEADS = BRANCH_HEADS
RWKV_DECAY_RANK = 64
RWKV_A_RANK = 64
RWKV_GATE_RANK = 128
RWKV_GN_EPS = 64e-5
N_EXPERTS = 64
N_GROUPS = 8
EXPERTS_PER_GROUP = N_EXPERTS // N_GROUPS
TOP_K = 2
EXPERT_FF = 512
MOE_BLOCK = 256
A_SIZES = (ATT_Q_HEADS * HEAD_DIM, ATT_KV_HEADS * HEAD_DIM, ATT_KV_HEADS * HEAD_DIM)
RET_SIZES = (BRANCH_W,) * 5
MLA_SIZES = (MLA_Q_RANK, MLA_KV_RANK, MLA_ROPE)
RWKV_SIZES = (BRANCH_W, BRANCH_W, BRANCH_W, RWKV_DECAY_RANK, RWKV_DECAY_RANK,
              RWKV_A_RANK, RWKV_A_RANK, RWKV_GATE_RANK)
GROUP_SIZES = (sum(A_SIZES), sum(RET_SIZES), sum(MLA_SIZES), sum(RWKV_SIZES))
N_IN = sum(GROUP_SIZES)
F32 = jnp.float32


def split_sizes(x, sizes):
    out, o = [], 0
    for n in sizes:
        out.append(x[..., o:o + n])
        o += n
    return out


def heads(x, h):
    return x.reshape(x.shape[:-1] + (h, x.shape[-1] // h))


def rms_norm(x, g):
    xf = x.astype(F32)
    y = xf * lax.rsqrt(jnp.mean(xf * xf, axis=-1, keepdims=True) + NORM_EPS)
    return (y * g.astype(F32)).astype(x.dtype)


def head_norm(x, eps):
    xf = x.astype(F32)
    xc = xf - jnp.mean(xf, axis=-1, keepdims=True)
    return xc * lax.rsqrt(jnp.mean(xc * xc, axis=-1, keepdims=True) + eps)


def adaln(h, g, shift, scale):
    return rms_norm(h, g) * (1 + scale) + shift


def rope_tables(pos, dim):
    inv = ROPE_BASE ** (-jnp.arange(0, dim, 2, dtype=F32) / dim)
    ang = pos[:, None] * inv[None, :]
    return jnp.cos(ang), jnp.sin(ang)


def rope_rotate(x, cos, sin):
    m = x.shape[-1] // 2
    x1, x2 = x[..., :m], x[..., m:]
    c = cos[None, :, None, :]
    s = sin[None, :, None, :]
    return jnp.concatenate([x1 * c - x2 * s, x2 * c + x1 * s], axis=-1).astype(x.dtype)


def axial_tables(seq, dim):
    rows = seq // GRID_W
    row = jnp.repeat(jnp.arange(rows, dtype=F32), GRID_W)
    col = jnp.broadcast_to(jnp.arange(GRID_W, dtype=F32)[None, :], (rows, GRID_W)).reshape(-1)
    return rope_tables(row, dim // 2), rope_tables(col, dim // 2)


def axial_rope(x, tabs):
    (cos_r, sin_r), (cos_c, sin_c) = tabs
    half = x.shape[-1] // 2
    return jnp.concatenate([rope_rotate(x[..., :half], cos_r, sin_r),
                            rope_rotate(x[..., half:], cos_c, sin_c)], axis=-1)


def window_gqa(q, k, v, k_ctx, v_ctx, sink):
    b, s, hq, d = q.shape
    g = hq // ATT_KV_HEADS
    nb = s // BLOCK
    nw = 3 * BLOCK
    nc = k_ctx.shape[1]
    qb = (q * d ** -0.5).reshape(b, nb, BLOCK, ATT_KV_HEADS, g, d)
    pad = ((0, 0), (BLOCK, BLOCK), (0, 0), (0, 0))
    kp = jnp.pad(k, pad).reshape(b, nb + 2, BLOCK, ATT_KV_HEADS, d)
    vp = jnp.pad(v, pad).reshape(b, nb + 2, BLOCK, ATT_KV_HEADS, d)
    kw = jnp.concatenate([kp[:, :-2], kp[:, 1:-1], kp[:, 2:]], axis=2)
    vw = jnp.concatenate([vp[:, :-2], vp[:, 1:-1], vp[:, 2:]], axis=2)
    blk0 = jnp.arange(nb)[:, None, None] * BLOCK
    q_pos = blk0 + jnp.arange(BLOCK)[None, :, None]
    k_pos = blk0 - BLOCK + jnp.arange(nw)[None, None, :]
    valid = (jnp.abs(k_pos - q_pos) <= WINDOW) & (k_pos >= 0) & (k_pos < s)
    s_win = jnp.einsum('bnqhgd,bnkhd->bnhgqk', qb, kw).astype(F32)
    s_win = jnp.where(valid[None, :, None, None], s_win, NEG_INF)
    s_ctx = jnp.einsum('bnqhgd,bchd->bnhgqc', qb, k_ctx).astype(F32)
    s_sink = jnp.broadcast_to(sink.astype(F32).reshape(ATT_KV_HEADS, g, 1, 1), s_win.shape[:-1] + (1,))
    p = jax.nn.softmax(jnp.concatenate([s_win, s_ctx, s_sink], axis=-1), axis=-1).astype(v.dtype)
    o = (jnp.einsum('bnhgqk,bnkhd->bnqhgd', p[..., :nw], vw)
         + jnp.einsum('bnhgqc,bchd->bnqhgd', p[..., nw:nw + nc], v_ctx))
    return o.reshape(b, s, hq * d)


def ctx_gqa(q, k, v, sink):
    b, l, hq, d = q.shape
    g = hq // ATT_KV_HEADS
    qg = (q * d ** -0.5).reshape(b, l, ATT_KV_HEADS, g, d)
    sc = jnp.einsum('bqhgd,bkhd->bhgqk', qg, k).astype(F32)
    s_sink = jnp.broadcast_to(sink.astype(F32).reshape(ATT_KV_HEADS, g, 1, 1), sc.shape[:-1] + (1,))
    p = jax.nn.softmax(jnp.concatenate([sc, s_sink], axis=-1), axis=-1).astype(v.dtype)
    o = jnp.einsum('bhgqk,bkhd->bqhgd', p[..., :l], v)
    return o.reshape(b, l, hq * d)


def retention_log_gammas():
    lg = jnp.log(1.0 - jnp.exp(jnp.linspace(math.log(1.0 / 32), math.log(1.0 / 512), 2 * RET_HEADS, dtype=F32)))
    return lg[0::2], lg[1::2]


def retention_chunked(q, k, v, log_gamma, s0):
    b, t, h, d = q.shape
    n = t // RET_CHUNK
    cl = RET_CHUNK
    qc = q.reshape(b, n, cl, h, d).astype(F32)
    kc = k.reshape(b, n, cl, h, d).astype(F32)
    vc = v.reshape(b, n, cl, h, d).astype(F32)
    idx = jnp.arange(cl, dtype=F32)
    rel = idx[:, None] - idx[None, :]
    d_intra = jnp.where(rel >= 0, jnp.exp(jnp.maximum(rel, 0.0)[None] * log_gamma[:, None, None]), 0.0)
    a_int = jnp.einsum('bnihd,bnjhd->bnhij', qc, kc) * d_intra
    o = jnp.einsum('bnhij,bnjhe->bnihe', a_int, vc)
    k_dec = kc * jnp.exp((cl - 1 - idx)[:, None] * log_gamma[None, :])[..., None]
    kv = jnp.einsum('bnjhd,bnjhe->nbhde', k_dec, vc)
    g_chunk = jnp.exp(cl * log_gamma)[:, None, None]

    def step(state, kv_n):
        return state * g_chunk + kv_n, state

    s_fin, s_prev = lax.scan(step, s0, kv)
    q_dec = qc * jnp.exp((idx + 1)[:, None] * log_gamma[None, :])[..., None]
    o = o + jnp.einsum('bnihd,nbhde->bnihe', q_dec, s_prev)
    return o.reshape(b, t, h, d), s_fin


RET_TOKENS_PER_STEP = 1024


def _retention_kernel(q_ref, k_ref, v_ref, g_ref, d_ref, kdec_ref, qdec_ref, gch_ref, s0_ref,
                      y_ref, sfin_ref, s_scr, *, reverse, n_chunks):
    cl = RET_CHUNK
    nh, hd = s_scr.shape[:2]
    tb = q_ref.shape[1]

    @pl.when(pl.program_id(1) == 0)
    def _():
        s_scr[...] = s0_ref[0]

    nb = nh * n_chunks
    shp = (nb, cl, hd)
    per_chunk = lambda z: jnp.broadcast_to(z[:, None], (nh, n_chunks) + z.shape[1:]).reshape((nb,) + z.shape[1:])
    q = _split_heads(q_ref[0], nh).reshape(shp)
    k = _split_heads(k_ref[0], nh).reshape(shp)
    v = _split_heads(v_ref[0], nh).reshape(shp)
    a = _dot(q, k, _NT, 1) * per_chunk(d_ref[...])
    o = _dot(a, v, _NN, 1)
    kv = _dot(k * per_chunk(kdec_ref[...]), v, _TN, 1).reshape(nh, n_chunks, hd, hd)

    s = s_scr[...]
    gch = gch_ref[...]
    starts = [None] * n_chunks
    order = range(n_chunks - 1, -1, -1) if reverse else range(n_chunks)
    for n in order:
        starts[n] = s
        s = s * gch + kv[:, n]
    s_scr[...] = s
    sfin_ref[0] = s
    s_in = jnp.stack(starts, axis=1).reshape(nb, hd, hd)
    o = o + _dot(q * per_chunk(qdec_ref[...]), s_in, _NN, 1)

    oc = o - jnp.mean(o, axis=-1, keepdims=True)
    normed = oc * lax.rsqrt(jnp.mean(oc * oc, axis=-1, keepdims=True) + RET_GN_EPS)
    g = g_ref[0]
    y_ref[0] = (g * jax.nn.sigmoid(g)) * _join_heads(normed.reshape(nh, tb, hd))


def _split_heads(x, nh):
    hd = x.shape[-1] // nh
    return jnp.stack([x[:, h * hd:(h + 1) * hd] for h in range(nh)], axis=0)


def _join_heads(y):
    return jnp.concatenate([y[h] for h in range(y.shape[0])], axis=-1)


def retention_direction(q, k, v, gate, log_gamma, s0, reverse):
    bsz, t, width = q.shape
    nh, hd = s0.shape[1:3]
    tb = min(RET_TOKENS_PER_STEP, t)
    nt = t // tb
    assert t == nt * tb and tb % RET_CHUNK == 0
    cl = RET_CHUNK
    idx = jnp.arange(cl, dtype=F32)
    rel = (idx[None, :] - idx[:, None]) if reverse else (idx[:, None] - idx[None, :])
    lg = log_gamma[:, None, None]
    d_intra = jnp.where(rel >= 0, jnp.exp(jnp.maximum(rel, 0.0)[None] * lg), 0.0)
    kdec = jnp.exp((idx if reverse else cl - 1 - idx)[None, :, None] * lg)
    qdec = jnp.exp((cl - idx if reverse else idx + 1)[None, :, None] * lg)
    gch = jnp.exp(cl * lg)
    if reverse:
        tmap = lambda bi, j: (bi, nt - 1 - j, 0)
    else:
        tmap = lambda bi, j: (bi, j, 0)
    seq_spec = pl.BlockSpec((1, tb, width), tmap)
    st_spec = pl.BlockSpec((1, nh, hd, hd), lambda bi, j: (bi, 0, 0, 0))
    const = lambda shape: pl.BlockSpec(shape, lambda bi, j: (0,) * len(shape))
    y, sfin = pl.pallas_call(
        functools.partial(_retention_kernel, reverse=reverse, n_chunks=tb // cl),
        grid=(bsz, nt),
        in_specs=[seq_spec] * 4 + [const((nh, cl, cl)), const((nh, cl, 1)), const((nh, cl, 1)),
                                   const((nh, 1, 1)), st_spec],
        out_specs=[seq_spec, st_spec],
        out_shape=[jax.ShapeDtypeStruct((bsz, t, width), F32),
                   jax.ShapeDtypeStruct((bsz, nh, hd, hd), F32)],
        scratch_shapes=[pltpu.VMEM((nh, hd, hd), F32)],
        compiler_params=pltpu.CompilerParams(dimension_semantics=("parallel", "arbitrary"),
                                             vmem_limit_bytes=RWKV_VMEM_LIMIT),
        name="retention_rev" if reverse else "retention_fwd",
    )(q, k, v, gate, d_intra, kdec, qdec, gch, s0)
    return y, sfin


def retention_mix(pb, cos, sin, lg_f, lg_b, s0f, s0b):
    bsz, t = pb.shape[:2]
    q, k, v, gf, gb = split_sizes(pb, RET_SIZES)
    q = rope_rotate(heads(q, RET_HEADS), cos, sin).reshape(bsz, t, -1)
    k = (rope_rotate(heads(k, RET_HEADS), cos, sin) * HEAD_DIM ** -0.5).reshape(bsz, t, -1)
    yf, sf = retention_direction(q, k, v, gf, lg_f, s0f, False)
    yb, sb = retention_direction(q, k, v, gb, lg_b, s0b, True)
    return (yf + yb).astype(pb.dtype), sf, sb


def mla_project(pc, lp):
    cq, ckv, kr = split_sizes(pc, MLA_SIZES)
    q = heads(rms_norm(cq, lp['g_qnorm']) @ lp['w_uq'], MLA_HEADS)
    kv = heads(rms_norm(ckv, lp['g_kvnorm']) @ lp['w_ukv'], MLA_HEADS)
    return q[..., :MLA_NOPE], q[..., MLA_NOPE:], kv[..., :MLA_NOPE], kr[:, :, None, :], kv[..., MLA_NOPE:]


LANES = 128
SUBLANES = 8
MLA_Q_TILE = 512
MLA_KEY_UNIT = 256
MLA_KEY_TILE_MAX = 8448
MLA_VMEM_LIMIT = 56 * 1024 * 1024


def _mla_flash_kernel(q_ref, k_ref, v_ref, o_ref, m_scr, acc_scr, *, scale, tk, n_tiles):
    dv = o_ref.shape[-1]
    q = (q_ref[0, 0] * scale).astype(BF16)
    m_scr[...] = jnp.full_like(m_scr, NEG_INF)
    acc_scr[...] = jnp.zeros_like(acc_scr)

    def sweep(c, carry):
        keys = pl.ds(pl.multiple_of(c * tk, tk), tk)
        s = lax.dot_general(q, k_ref[0, 0, keys, :], (((1,), (1,)), ((), ())), preferred_element_type=F32)
        m_prev = m_scr[...]
        m_new = jnp.maximum(m_prev, jnp.max(s, axis=-1, keepdims=True))
        p = jnp.exp2(s - m_new).astype(BF16)
        acc_scr[...] = (jnp.exp2(m_prev - m_new) * acc_scr[...]
                        + jnp.dot(p, v_ref[0, 0, keys, :], preferred_element_type=F32))
        m_scr[...] = m_new
        return carry

    lax.fori_loop(0, n_tiles, sweep, 0)
    acc = acc_scr[...]
    o_ref[0, 0] = acc[:, :dv] / acc[:, dv:dv + 1]


def mla_attend(qn, qr, kn, kr, v):
    b, t, h, _ = qn.shape
    n = kn.shape[1]
    dv = v.shape[-1]
    scale = (MLA_NOPE + MLA_ROPE) ** -0.5 * math.log2(math.e)
    q = jnp.swapaxes(jnp.concatenate([qn, qr], axis=-1), 1, 2)
    kr_h = jnp.broadcast_to(kr, (b, n, h, MLA_ROPE))
    k = jnp.swapaxes(jnp.concatenate([kn, kr_h], axis=-1), 1, 2).astype(BF16)
    ones = jnp.ones((b, n, h, 1), v.dtype)
    zeros = jnp.zeros((b, n, h, LANES - dv - 1), v.dtype)
    vt = jnp.swapaxes(jnp.concatenate([v, ones, zeros], axis=-1), 1, 2).astype(BF16)
    dq = q.shape[-1]
    tq = min(MLA_Q_TILE, t)
    tk = max(d for d in range(MLA_KEY_UNIT, MLA_KEY_TILE_MAX + 1, MLA_KEY_UNIT) if n % d == 0)
    assert t % tq == 0
    o = pl.pallas_call(
        functools.partial(_mla_flash_kernel, scale=scale, tk=tk, n_tiles=n // tk),
        grid=(b, h, t // tq),
        in_specs=[pl.BlockSpec((1, 1, tq, dq), lambda bi, hi, qi: (bi, hi, qi, 0)),
                  pl.BlockSpec((1, 1, n, dq), lambda bi, hi, qi: (bi, hi, 0, 0)),
                  pl.BlockSpec((1, 1, n, LANES), lambda bi, hi, qi: (bi, hi, 0, 0))],
        out_specs=pl.BlockSpec((1, 1, tq, dv), lambda bi, hi, qi: (bi, hi, qi, 0)),
        out_shape=jax.ShapeDtypeStruct((b, h, t, dv), F32),
        scratch_shapes=[pltpu.VMEM((tq, 1), F32), pltpu.VMEM((tq, LANES), F32)],
        compiler_params=pltpu.CompilerParams(
            dimension_semantics=("parallel", "parallel", "arbitrary"),
            vmem_limit_bytes=MLA_VMEM_LIMIT),
        name="mla_flash",
    )(q, k, vt)
    return jnp.swapaxes(o, 1, 2).reshape(b, t, h * dv)


def centred_shift(p):
    prev = jnp.pad(p, ((0, 0), (1, 0), (0, 0)))[:, :-1]
    nxt = jnp.pad(p, ((0, 0), (0, 1), (0, 0)))[:, 1:]
    return 0.5 * (prev + nxt)


def rwkv_features(pd, lp):
    z = pd + (centred_shift(pd) - pd) * lp['rwkv_mu']
    zr, zk, zv, zwf, zwb, zaf, zab, zg = split_sizes(z, RWKV_SIZES)
    r = heads(zr, RWKV_HEADS)
    k = heads(zk, RWKV_HEADS)
    v = heads(zv, RWKV_HEADS)
    g = jax.nn.sigmoid(zg) @ lp['rwkv_g_up']
    kk = (k * heads(lp['rwkv_k_k'], RWKV_HEADS)).astype(F32)
    kk = kk / jnp.maximum(jnp.sqrt(jnp.sum(kk * kk, axis=-1, keepdims=True)), 1e-12)
    k_a = heads(lp['rwkv_k_a'], RWKV_HEADS)
    dirs = []
    for di, (zw, za) in enumerate(((zwf, zaf), (zwb, zab))):
        logw = -jax.nn.softplus(-(lp['rwkv_w0'][di] + jnp.tanh(zw) @ lp['rwkv_w_up'][di]).astype(F32)) - 0.5
        w = heads(-jnp.exp(logw), RWKV_HEADS)
        a = heads(jax.nn.sigmoid((lp['rwkv_a0'][di] + za @ lp['rwkv_a_up'][di]).astype(F32)), RWKV_HEADS)
        dirs.append((w, k * (1 + (a - 1) * k_a), kk * a))
    return r, v, g, kk, dirs


def rwkv_scan(r, w, k, v, kk, b_, s0, reverse):
    xs = tuple(jnp.moveaxis(z.astype(F32), 1, 0) for z in (r, w, k, v, kk, b_))

    def step(st, inp):
        r_t, w_t, k_t, v_t, kk_t, b_t = inp
        sa = jnp.einsum('bhvk,bhk->bhv', st, -kk_t)
        st = st * w_t[:, :, None, :] + sa[..., None] * b_t[:, :, None, :] + v_t[..., None] * k_t[:, :, None, :]
        return st, jnp.einsum('bhvk,bhk->bhv', st, r_t)

    s_fin, y = lax.scan(step, s0, xs, reverse=reverse)
    return jnp.moveaxis(y, 0, 1), s_fin


BF16 = jnp.bfloat16
RWKV_CHUNK = 64
RWKV_SUB = 16
RWKV_TOKENS_PER_STEP = 512
RWKV_PASSES = (1, 1)
RWKV_VMEM_LIMIT = 48 * 1024 * 1024
_NN = (((2,), (1,)), ((0,), (0,)))
_NT = (((2,), (2,)), ((0,), (0,)))
_TN = (((1,), (1,)), ((0,), (0,)))


def _bf16_parts(a, n):
    parts, rem = [], a
    for i in range(n):
        p = rem.astype(BF16)
        parts.append(p)
        if i + 1 < n:
            rem = rem - p.astype(F32)
    return parts


def _dot(a, b, dims, passes):
    if passes == 1:
        return lax.dot_general(a.astype(BF16), b.astype(BF16), dims, preferred_element_type=F32)
    a_hi, a_lo = _bf16_parts(a, 2)
    b_hi, b_lo = _bf16_parts(b, 2)
    out = lax.dot_general(a_hi, b_hi, dims, preferred_element_type=F32)
    out = out + lax.dot_general(a_hi, b_lo, dims, preferred_element_type=F32)
    return out + lax.dot_general(a_lo, b_hi, dims, preferred_element_type=F32)


def _unit_lower_inverse(l_mat, same_sub, passes):
    mm = lambda a, b: _dot(a, b, _NN, passes)
    ld = jnp.where(same_sub, l_mat, 0.0)
    lo = l_mat - ld
    p2 = mm(ld, ld)
    d = p2 - ld - mm(ld, p2)
    pw = p2
    span = 4
    while span < RWKV_SUB:
        pw = mm(pw, pw)
        d = d + pw + mm(d, pw)
        span *= 2
    n1 = lo + mm(d, lo)
    n2 = mm(n1, n1)
    x = n2 - n1 - mm(n1, n2)
    span = 4
    pw = n2
    while span < RWKV_CHUNK // RWKV_SUB:
        pw = mm(pw, pw)
        x = x + pw + mm(x, pw)
        span *= 2
    return x + d + mm(x, d)


def _rwkv_scan_kernel(r_ref, lw_ref, k_ref, v_ref, kk_ref, b_ref, s0_ref, y_ref, sfin_ref, s_scr,
                      *, reverse, n_chunks):
    cl = RWKV_CHUNK
    nh, hd = s_scr.shape[:2]
    tb = r_ref.shape[1]

    @pl.when(pl.program_id(1) == 0)
    def _():
        s_scr[...] = s0_ref[0]

    nb = nh * n_chunks
    shp = (nb, cl, hd)
    r = _split_heads(r_ref[0], nh).reshape(shp)
    lw = _split_heads(lw_ref[0], nh).reshape(shp)
    k = _split_heads(k_ref[0], nh).reshape(shp)
    v = _split_heads(v_ref[0], nh).reshape(shp)
    kk = _split_heads(kk_ref[0], nh).reshape(shp)
    b = _split_heads(b_ref[0], nh).reshape(shp)

    row = lax.broadcasted_iota(jnp.int32, (cl, cl), 0)
    col = lax.broadcasted_iota(jnp.int32, (cl, cl), 1)
    before = (col > row) if reverse else (col < row)
    upto = before | (col == row)
    same_sub = (row // RWKV_SUB) == (col // RWKV_SUB)

    tri = jnp.broadcast_to(jnp.where(upto, 1.0, 0.0).astype(BF16)[None], (nb, cl, cl))
    cum = sum(lax.dot_general(tri, p, _NN, preferred_element_type=F32) for p in _bf16_parts(lw, 3))
    last = 0 if reverse else cl - 1
    mid = cl // 2 if reverse else cl // 2 - 1
    tot = cum[:, last:last + 1, :]
    rho = cum[:, mid:mid + 1, :]
    cum_ex = cum - lw

    a_t = kk * jnp.exp(cum_ex - rho)
    r_t = r * jnp.exp(cum - rho)
    e_out = jnp.exp(rho - cum)
    b_t = b * e_out
    k_t = k * e_out
    a_0 = kk * jnp.exp(cum_ex)
    r_0 = r * jnp.exp(cum)
    e_end = jnp.exp(tot - cum)
    b_e = b * e_end
    k_e = k * e_end

    ps, pt = RWKV_PASSES
    sc = _dot(jnp.concatenate([a_t, r_t], axis=1), jnp.concatenate([b_t, k_t], axis=1), _NT, ps)
    l_mat = jnp.where(before, sc[:, :cl, :cl], 0.0)
    m_ak = jnp.where(before, sc[:, :cl, cl:], 0.0)
    m_rb = jnp.where(upto, sc[:, cl:, :cl], 0.0)
    m_rk = jnp.where(upto, sc[:, cl:, cl:], 0.0)
    t_m1 = _unit_lower_inverse(l_mat, same_sub, pt)

    mv = _dot(m_ak, v, _NN, ps)
    am = jnp.concatenate([a_0, mv], axis=2)
    qw = -(am + _dot(t_m1, am, _NN, ps))
    rq = _dot(m_rb, qw, _NN, ps)
    r_h = r_0 + rq[:, :, :hd]
    y_loc = rq[:, :, hd:] + _dot(m_rk, v, _NN, ps)
    eye = lax.broadcasted_iota(jnp.int32, (hd, hd), 0) == lax.broadcasted_iota(jnp.int32, (hd, hd), 1)
    pq = _dot(b_e, qw, _TN, ps)
    phi = (jnp.where(eye, jnp.exp(tot), 0.0) + pq[:, :, :hd]).reshape(nh, n_chunks, hd, hd)
    psi = (pq[:, :, hd:] + _dot(k_e, v, _TN, ps)).reshape(nh, n_chunks, hd, hd)

    s = s_scr[...]
    starts = [None] * n_chunks
    order = range(n_chunks - 1, -1, -1) if reverse else range(n_chunks)
    for n in order:
        starts[n] = s
        s = _dot(phi[:, n], s, _NN, 3) + psi[:, n]
    s_scr[...] = s
    sfin_ref[0] = s
    s_in = jnp.stack(starts, axis=1).reshape(nb, hd, hd)
    y = _dot(r_h, s_in, _NN, ps) + y_loc
    y_ref[0] = _join_heads(y.reshape(nh, tb, hd))


def rwkv_scan_blocked(r, lw, k, v, kk, b_, s0t, reverse):
    bsz, t, width = r.shape
    nh, hd = s0t.shape[1:3]
    tb = min(RWKV_TOKENS_PER_STEP, t)
    nt = t // tb
    assert t == nt * tb and tb % RWKV_CHUNK == 0
    if reverse:
        tmap = lambda bi, j: (bi, nt - 1 - j, 0)
    else:
        tmap = lambda bi, j: (bi, j, 0)
    seq_spec = pl.BlockSpec((1, tb, width), tmap)
    st_spec = pl.BlockSpec((1, nh, hd, hd), lambda bi, j: (bi, 0, 0, 0))
    y, sfin = pl.pallas_call(
        functools.partial(_rwkv_scan_kernel, reverse=reverse, n_chunks=tb // RWKV_CHUNK),
        grid=(bsz, nt),
        in_specs=[seq_spec] * 6 + [st_spec],
        out_specs=[seq_spec, st_spec],
        out_shape=[jax.ShapeDtypeStruct((bsz, t, width), F32),
                   jax.ShapeDtypeStruct((bsz, nh, hd, hd), F32)],
        scratch_shapes=[pltpu.VMEM((nh, hd, hd), F32)],
        compiler_params=pltpu.CompilerParams(dimension_semantics=("parallel", "arbitrary"),
                                             vmem_limit_bytes=RWKV_VMEM_LIMIT),
        name="rwkv_scan_rev" if reverse else "rwkv_scan_fwd",
    )(r, lw, k, v, kk, b_, s0t)
    return y, sfin


def rwkv_mix(feat, lp, s0f, s0b):
    r, v, g, kk, dirs = feat
    (lwf, kf, bf), (lwb, kb, bb) = dirs
    flat = lambda z: z.astype(F32).reshape(z.shape[0], z.shape[1], -1)
    rt, vt, kkt = flat(r), flat(v), flat(kk)
    yf, sf = rwkv_scan_blocked(rt, flat(lwf), flat(kf), vt, kkt, flat(bf), s0f, False)
    yb, sb = rwkv_scan_blocked(rt, flat(lwb), flat(kb), vt, kkt, flat(bb), s0b, True)
    y = head_norm(heads(yf + yb, RWKV_HEADS), RWKV_GN_EPS)
    y = y * heads(lp['rwkv_lnx_w'], RWKV_HEADS) + heads(lp['rwkv_lnx_b'], RWKV_HEADS)
    bonus = jnp.sum(r * (kf + kb) * lp['rwkv_r_k'], axis=-1, keepdims=True) * v
    out = (y + bonus).reshape(g.shape) * g
    return out.astype(g.dtype), sf, sb


MERGE_ROW_TILE = 512
MERGE_VMEM_LIMIT = 52 * 1024 * 1024


def _merge_kernel(u_ref, ya_ref, yb_ref, yc_ref, yd_ref, h_ref, mod_ref, g2_ref, wg_ref, bg_ref, wb_ref, wo_ref,
                  hn_ref, f_ref):
    u = u_ref[...].astype(BF16)
    acc = None
    for n, y_ref in enumerate((ya_ref, yb_ref, yc_ref, yd_ref)):
        gate = jax.nn.sigmoid(jnp.dot(u, wg_ref[n], preferred_element_type=F32) + bg_ref[n])
        term = gate * jnp.dot(y_ref[...].astype(BF16), wb_ref[n], preferred_element_type=F32)
        acc = term if acc is None else acc + term
    mix = jnp.dot(acc.astype(BF16), wo_ref[...], preferred_element_type=F32)
    mod = mod_ref[0]
    hn = h_ref[...] + mod[0:1] * mix
    hn_ref[...] = hn
    normed = hn * lax.rsqrt(jnp.mean(hn * hn, axis=-1, keepdims=True) + NORM_EPS) * g2_ref[...]
    f_ref[...] = normed * (1 + mod[2:3]) + mod[1:2]


def merge_branches(u, ys, lp, h, mod):
    lead, d = u.shape[:-1], u.shape[-1]
    rows = math.prod(lead)
    tm = min(MERGE_ROW_TILE, rows)
    rows_per_mod = rows // mod.shape[0]
    assert rows % tm == 0 and (mod.shape[0] == 1 or rows_per_mod % tm == 0)
    bw = ys[0].shape[-1]
    row_spec = lambda width: pl.BlockSpec((tm, width), lambda i: (i, 0))
    whole = lambda shape: pl.BlockSpec(shape, lambda i: (0,) * len(shape))
    hn, f = pl.pallas_call(
        _merge_kernel,
        grid=(rows // tm,),
        in_specs=[row_spec(d)] + [row_spec(bw)] * N_BRANCH
                 + [row_spec(d), pl.BlockSpec((1, 3, d), lambda i: ((i * tm) // rows_per_mod, 0, 0)), whole((1, d)),
                    whole((N_BRANCH, d, d)), whole((N_BRANCH, 1, d)), whole((N_BRANCH, bw, d)), whole((d, d))],
        out_specs=[row_spec(d), row_spec(d)],
        out_shape=[jax.ShapeDtypeStruct((rows, d), F32), jax.ShapeDtypeStruct((rows, d), F32)],
        compiler_params=pltpu.CompilerParams(dimension_semantics=("parallel",),
                                             vmem_limit_bytes=MERGE_VMEM_LIMIT),
        name="merge_branches",
    )(u.reshape(rows, d), *[y.reshape(rows, bw) for y in ys], h.reshape(rows, d), mod, lp['g_norm2'][None, :],
      lp['w_gate'].astype(BF16), lp['b_gate'][:, None, :], lp['w_branch'].astype(BF16), lp['w_out'].astype(BF16))
    return hn.reshape(lead + (d,)), f.reshape(lead + (d,))


INPROJ_ROW_TILE = 512
INPROJ_VMEM_LIMIT = 48 * 1024 * 1024


def _inproj_kernel(h_ref, mod_ref, g_ref, *refs):
    n = (len(refs) - 1) // 2
    w_refs, u_ref, p_refs = refs[:n], refs[n], refs[n + 1:]
    h = h_ref[...]
    mod = mod_ref[0]
    normed = h * lax.rsqrt(jnp.mean(h * h, axis=-1, keepdims=True) + NORM_EPS) * g_ref[...]
    u = (normed * (1 + mod[1:2]) + mod[0:1]).astype(BF16)
    u_ref[...] = u
    for w_ref, p_ref in zip(w_refs, p_refs):
        p_ref[...] = jnp.dot(u, w_ref[...], preferred_element_type=F32)


def adaln_in_proj(h, mod, g, w_in):
    bsz, t, d = h.shape
    rows = bsz * t
    tm = min(INPROJ_ROW_TILE, rows)
    rows_per_mod = rows // mod.shape[0]
    assert rows % tm == 0 and (mod.shape[0] == 1 or rows_per_mod % tm == 0)
    w_groups = [w.astype(BF16) for w in split_sizes(w_in, GROUP_SIZES)]
    row_spec = lambda width: pl.BlockSpec((tm, width), lambda i: (i, 0))
    whole = lambda shape: pl.BlockSpec(shape, lambda i: (0,) * len(shape))
    outs = pl.pallas_call(
        _inproj_kernel,
        grid=(rows // tm,),
        in_specs=[row_spec(d), pl.BlockSpec((1, 2, d), lambda i: ((i * tm) // rows_per_mod, 0, 0)), whole((1, d))]
                 + [whole(w.shape) for w in w_groups],
        out_specs=[row_spec(d)] + [row_spec(n) for n in GROUP_SIZES],
        out_shape=[jax.ShapeDtypeStruct((rows, d), BF16)]
                  + [jax.ShapeDtypeStruct((rows, n), F32) for n in GROUP_SIZES],
        compiler_params=pltpu.CompilerParams(dimension_semantics=("parallel",),
                                             vmem_limit_bytes=INPROJ_VMEM_LIMIT),
        name="adaln_in_proj",
    )(h.reshape(rows, d), mod, g[None, :], *w_groups)
    return [o.reshape(bsz, t, -1) for o in outs]


def token_mixers(lp, need_ctx, h_ctx, h_lat, mod1_ctx, mod1_lat, mod_ctx, mod_lat):
    b, s, _ = h_lat.shape
    l = h_ctx.shape[1]
    u_lat, pa_l, pb_l, pc_l, pd_l = adaln_in_proj(h_lat, mod1_lat, lp['g_norm1'], lp['w_in'])
    u_ctx, pa_c, pb_c, pc_c, pd_c = adaln_in_proj(h_ctx, mod1_ctx, lp['g_norm1'], lp['w_in'])
    tab_att = axial_tables(s, HEAD_DIM)
    tab_mla = axial_tables(s, MLA_ROPE)

    aq_l, ak_l, av_l = split_sizes(pa_l, A_SIZES)
    aq_c, ak_c, av_c = split_sizes(pa_c, A_SIZES)
    k_c = heads(ak_c, ATT_KV_HEADS)
    v_c = heads(av_c, ATT_KV_HEADS)
    ya_l = window_gqa(axial_rope(heads(aq_l, ATT_Q_HEADS), tab_att),
                      axial_rope(heads(ak_l, ATT_KV_HEADS), tab_att),
                      heads(av_l, ATT_KV_HEADS), k_c, v_c, lp['sink'])

    lg_f, lg_b = retention_log_gammas()
    cos_c, sin_c = rope_tables(jnp.arange(l, dtype=F32), HEAD_DIM)
    cos_l, sin_l = rope_tables(l + jnp.arange(s, dtype=F32), HEAD_DIM)
    zero_ret = jnp.zeros((b, RET_HEADS, HEAD_DIM, HEAD_DIM), F32)
    yb_c, sbf, sbb = retention_mix(pb_c, cos_c, sin_c, lg_f, lg_b, zero_ret, zero_ret)
    yb_l, _, _ = retention_mix(pb_l, cos_l, sin_l, lg_f, lg_b, sbf, sbb)

    qn_l, qr_l, kn_l, kr_l, vm_l = mla_project(pc_l, lp)
    qn_c, qr_c, kn_c, kr_c, vm_c = mla_project(pc_c, lp)
    qr_l = axial_rope(qr_l, tab_mla)
    kr_l = axial_rope(kr_l, tab_mla)
    yc_l = mla_attend(qn_l, qr_l, jnp.concatenate([kn_c, kn_l], axis=1),
                      jnp.concatenate([kr_c, kr_l], axis=1), jnp.concatenate([vm_c, vm_l], axis=1))

    zero_wkv = jnp.zeros((b, RWKV_HEADS, HEAD_DIM, HEAD_DIM), F32)
    yd_c, sdf, sdb = rwkv_mix(rwkv_features(pd_c, lp), lp, zero_wkv, zero_wkv)
    yd_l, _, _ = rwkv_mix(rwkv_features(pd_l, lp), lp, sdf, sdb)

    out_lat = merge_branches(u_lat, (ya_l, yb_l, yc_l, yd_l), lp, h_lat, mod_lat)
    if not need_ctx:
        return None, out_lat
    ya_c = ctx_gqa(heads(aq_c, ATT_Q_HEADS), k_c, v_c, lp['sink'])
    yc_c = mla_attend(qn_c, qr_c, kn_c, kr_c, vm_c)
    out_ctx = merge_branches(u_ctx, (ya_c, yb_c, yc_c, yd_c), lp, h_ctx, mod_ctx)
    return out_ctx, out_lat


def moe_ffn(x, router_w, router_b, layer_idx, w_e_gate, w_e_up, w_e_down):
    n_tok, dm = x.shape
    scores = jax.nn.sigmoid(x.astype(F32) @ router_w.astype(F32))
    grp = (scores + router_b.astype(F32)).reshape(n_tok, N_GROUPS, EXPERTS_PER_GROUP)
    g_sel = jnp.argmax(jnp.sum(lax.top_k(grp, TOP_K)[0], axis=-1), axis=-1)
    in_grp = lax.top_k(jnp.take_along_axis(grp, g_sel[:, None, None], axis=1)[:, 0], TOP_K)[1]
    e_idx = g_sel[:, None] * EXPERTS_PER_GROUP + in_grp
    w_sel = jnp.take_along_axis(scores, e_idx, axis=1)
    w_sel = w_sel / jnp.sum(w_sel, axis=-1, keepdims=True)
    n_asg = n_tok * TOP_K
    flat_e = e_idx.reshape(-1).astype(jnp.int32)
    flat_w = w_sel.reshape(-1)
    order = jnp.argsort(flat_e).astype(jnp.int32)
    rank = jnp.argsort(order).astype(jnp.int32)
    onehot = flat_e[:, None] == jnp.arange(N_EXPERTS, dtype=jnp.int32)[None, :]
    counts = jnp.sum(onehot, axis=0, dtype=jnp.int32)
    padded = (counts + MOE_BLOCK - 1) // MOE_BLOCK * MOE_BLOCK
    pad_end = jnp.cumsum(padded)
    pad_start = pad_end - padded
    start = jnp.cumsum(counts) - counts
    n_blocks = -(-n_asg // MOE_BLOCK) + N_EXPERTS
    n_rows = n_blocks * MOE_BLOCK
    blk_row0 = jnp.arange(n_blocks, dtype=jnp.int32) * MOE_BLOCK
    blk_e = jnp.minimum(jnp.sum(pad_end[None, :] <= blk_row0[:, None], axis=1, dtype=jnp.int32), N_EXPERTS - 1)
    off = (blk_row0 - pad_start[blk_e])[:, None] + jnp.arange(MOE_BLOCK, dtype=jnp.int32)[None, :]
    valid = (off < counts[blk_e][:, None]).reshape(-1)
    src = order[jnp.clip(start[blk_e][:, None] + off, 0, n_asg - 1).reshape(-1)]
    row_tok = jnp.where(valid, src // TOP_K, 0)
    row_w = jnp.where(valid, flat_w[src], 0.0)
    blk_valid = jnp.clip(counts[blk_e] - (blk_row0 - pad_start[blk_e]), 0, MOE_BLOCK)
    blk_groups = (blk_valid + MOE_GATHER_UNROLL - 1) // MOE_GATHER_UNROLL
    yb = moe_expert_blocks(blk_e, blk_groups.astype(jnp.int32), row_tok, x, row_w[:, None], layer_idx,
                           w_e_gate, w_e_up, w_e_down)
    shift = jnp.sum(jnp.where(onehot, (pad_start - start)[None, :], 0), axis=1, dtype=jnp.int32)
    slot = (rank + shift).reshape(n_tok, TOP_K)
    out = yb[slot[:, 0]]
    for j in range(1, TOP_K):
        out = out + yb[slot[:, j]]
    return out.astype(x.dtype)


MOE_VMEM_LIMIT = 40 * 1024 * 1024
MOE_GATHER_UNROLL = 8


def _moe_row_copy(x_hbm, xbuf, sem, tok, slot, r):
    return pltpu.make_async_copy(x_hbm.at[pl.ds(tok, 1)], xbuf.at[slot, pl.ds(r, 1)], sem.at[slot])


def _moe_expert_kernel(blk_e_ref, blk_groups_ref, row_tok_ref, x_hbm, w_ref, wg_ref, wu_ref, wd_ref, o_ref,
                       xbuf, sem):
    del blk_e_ref
    i = pl.program_id(0)
    slot = i % 2

    def for_each_row(block, fn):
        def body(gi, carry):
            for q in range(MOE_GATHER_UNROLL):
                fn(gi * MOE_GATHER_UNROLL + q, q)
            return carry
        lax.fori_loop(0, blk_groups_ref[block], body, 0)

    def issue(block, dst_slot):
        for_each_row(block, lambda r, q: _moe_row_copy(
            x_hbm, xbuf, sem, row_tok_ref[block * MOE_BLOCK + r], dst_slot, r).start(priority=q % 2))

    @pl.when(i == 0)
    def _():
        xbuf[...] = jnp.zeros_like(xbuf)
        issue(0, 0)

    @pl.when(i + 1 < pl.num_programs(0))
    def _():
        issue(i + 1, 1 - slot)

    for_each_row(i, lambda r, q: _moe_row_copy(x_hbm, xbuf, sem, 0, slot, r).wait())

    @pl.when(blk_groups_ref[i] == 0)
    def _():
        o_ref[...] = jnp.zeros_like(o_ref)

    @pl.when(blk_groups_ref[i] > 0)
    def _():
        x = jnp.concatenate([xbuf[slot, :, j, :] for j in range(xbuf.shape[2])], axis=-1).astype(BF16)
        g = jnp.dot(x, wg_ref[0, 0].astype(BF16), preferred_element_type=F32)
        u = jnp.dot(x, wu_ref[0, 0].astype(BF16), preferred_element_type=F32)
        hid = (g * jax.nn.sigmoid(g)) * u
        y = jnp.dot(hid.astype(BF16), wd_ref[0, 0].astype(BF16), preferred_element_type=F32)
        o_ref[...] = y * w_ref[...]


def moe_expert_blocks(blk_e, blk_groups, row_tok, x, row_w, layer_idx, w_e_gate, w_e_up, w_e_down):
    n_rows = row_tok.shape[0]
    n_tok, dm = x.shape
    n_blocks = n_rows // MOE_BLOCK
    ff = w_e_gate.shape[-1]
    assert dm == SUBLANES * LANES
    x = x.reshape(n_tok, SUBLANES, LANES)
    row_spec = lambda width: pl.BlockSpec((MOE_BLOCK, width), lambda i, be, bg, rt: (i, 0))
    expert_spec = lambda rows, cols: pl.BlockSpec((1, 1, rows, cols),
                                                  lambda i, be, bg, rt: (layer_idx, be[i], 0, 0))
    return pl.pallas_call(
        _moe_expert_kernel,
        grid_spec=pltpu.PrefetchScalarGridSpec(
            num_scalar_prefetch=3,
            grid=(n_blocks,),
            in_specs=[pl.BlockSpec(memory_space=pl.ANY), row_spec(1),
                      expert_spec(dm, ff), expert_spec(dm, ff), expert_spec(ff, dm)],
            out_specs=row_spec(dm),
            scratch_shapes=[pltpu.VMEM((2, MOE_BLOCK, SUBLANES, LANES), x.dtype), pltpu.SemaphoreType.DMA((2,))]),
        out_shape=jax.ShapeDtypeStruct((n_rows, dm), F32),
        compiler_params=pltpu.CompilerParams(dimension_semantics=("arbitrary",),
                                             vmem_limit_bytes=MOE_VMEM_LIMIT),
        name="moe_experts",
    )(blk_e, blk_groups, row_tok, x, row_w, w_e_gate, w_e_up, w_e_down)


def layer(h_ctx, h_lat, c, c_ctx, lp, router_w, router_b, need_ctx):
    b, s, d = h_lat.shape
    l = h_ctx.shape[1]
    m_lat = jnp.split((jax.nn.silu(c) @ lp['w_ada'] + lp['b_ada'])[:, None, :], 6, axis=-1)
    m_ctx = jnp.split(jax.nn.silu(c_ctx) @ lp['w_ada'] + lp['b_ada'], 6, axis=-1)
    mod1_lat = jnp.concatenate(m_lat[0:2], axis=1)
    mod1_ctx = jnp.stack(m_ctx[0:2], axis=0)[None]
    mod_lat = jnp.concatenate(m_lat[2:5], axis=1)
    mod_ctx = jnp.stack(m_ctx[2:5], axis=0)[None]
    out_ctx, (h_lat, f_lat) = token_mixers(lp, need_ctx, h_ctx, h_lat, mod1_ctx, mod1_lat, mod_ctx, mod_lat)
    f_lat = f_lat.reshape(b * s, d)
    if not need_ctx:
        ffn = moe_ffn(f_lat, router_w, router_b, lp['layer_idx'], *lp['experts'])
        return h_ctx, h_lat + m_lat[5] * ffn.reshape(b, s, d)
    h_ctx, f_ctx = out_ctx
    f_ctx = f_ctx.reshape(b * l, d)
    ffn = moe_ffn(jnp.concatenate([f_ctx, f_lat], axis=0), router_w, router_b, lp['layer_idx'], *lp['experts'])
    h_ctx = h_ctx + m_ctx[5] * ffn[:b * l].reshape(b, l, d)
    h_lat = h_lat + m_lat[5] * ffn[b * l:].reshape(b, s, d)
    return h_ctx, h_lat


def _final_norm_kernel(x_ref, g_ref, o_ref):
    x = x_ref[...]
    y = x * lax.rsqrt(jnp.mean(x * x, axis=-1, keepdims=True) + NORM_EPS)
    o_ref[...] = y * g_ref[...]


def final_rms_norm(x, g):
    b, s, d = x.shape
    rows = b * s
    tile = 1024
    out = pl.pallas_call(
        _final_norm_kernel,
        grid=(rows // tile,),
        in_specs=[pl.BlockSpec((tile, d), lambda i: (i, 0)), pl.BlockSpec((1, d), lambda i: (0, 0))],
        out_specs=pl.BlockSpec((tile, d), lambda i: (i, 0)),
        out_shape=jax.ShapeDtypeStruct((rows, d), x.dtype),
        name="final_rms_norm",
    )(x.reshape(rows, d), g.reshape(1, d))
    return out.reshape(b, s, d)


_LAYER_PARAM_NAMES = (
    'w_ada', 'b_ada', 'g_norm1', 'g_norm2', 'w_in', 'sink', 'g_qnorm', 'g_kvnorm', 'w_uq',
    'w_ukv', 'rwkv_mu', 'rwkv_w0', 'rwkv_w_up', 'rwkv_a0', 'rwkv_a_up', 'rwkv_g_up',
    'rwkv_k_k', 'rwkv_k_a', 'rwkv_r_k', 'rwkv_lnx_w', 'rwkv_lnx_b', 'w_gate', 'b_gate',
    'w_branch', 'w_out')


def kernel(x, c, ctx, c_ctx, w_ada, b_ada, g_norm1, g_norm2, w_in, sink, g_qnorm, g_kvnorm,
           w_uq, w_ukv, rwkv_mu, rwkv_w0, rwkv_w_up, rwkv_a0, rwkv_a_up, rwkv_g_up, rwkv_k_k,
           rwkv_k_a, rwkv_r_k, rwkv_lnx_w, rwkv_lnx_b, w_gate, b_gate, w_branch, w_out,
           router_w, router_b, w_e_gate, w_e_up, w_e_down, g_final):
    stacked = (w_ada, b_ada, g_norm1, g_norm2, w_in, sink, g_qnorm, g_kvnorm, w_uq, w_ukv, rwkv_mu,
               rwkv_w0, rwkv_w_up, rwkv_a0, rwkv_a_up, rwkv_g_up, rwkv_k_k, rwkv_k_a, rwkv_r_k,
               rwkv_lnx_w, rwkv_lnx_b, w_gate, b_gate, w_branch, w_out)
    h_ctx, h_lat = ctx, x
    for i in range(DEPTH):
        lp = {n: a[i] for n, a in zip(_LAYER_PARAM_NAMES, stacked)}
        lp['layer_idx'] = i
        lp['experts'] = (w_e_gate, w_e_up, w_e_down)
        h_ctx, h_lat = layer(h_ctx, h_lat, c, c_ctx, lp, router_w, router_b, i < DEPTH - 1)
    return final_rms_norm(h_lat, g_final)
```

```python
import functools
import math

import jax
import jax.numpy as jnp
from jax import lax
from jax.experimental import pallas as pl
from jax.experimental.pallas import tpu as pltpu

D_MODEL = 1024
BATCH = 4
SEQ = 8192
DEPTH = 2

GRID_W = 64
CTX_LEN = 256
N_BRANCH = 4
BRANCH_W = D_MODEL // N_BRANCH
HEAD_DIM = 64
BRANCH_HEADS = BRANCH_W // HEAD_DIM
BLOCK = 128
ROPE_BASE = 10000.0
NORM_EPS = 1e-6
NEG_INF = -1e30
ATT_Q_HEADS = BRANCH_HEADS
ATT_KV_HEADS = BRANCH_HEADS // 2
WINDOW = 128
RET_HEADS = BRANCH_HEADS
RET_CHUNK = 128
RET_GN_EPS = 1e-5
MLA_HEADS = BRANCH_HEADS
MLA_Q_RANK = 256
MLA_KV_RANK = 128
MLA_NOPE = HEAD_DIM
MLA_ROPE = HEAD_DIM // 2
MLA_V = HEAD_DIM
RWKV_HEADS = BRANCH_HEADS
RWKV_DECAY_RANK = 64
RWKV_A_RANK = 64
RWKV_GATE_RANK = 128
RWKV_GN_EPS = 64e-5
N_EXPERTS = 64
N_GROUPS = 8
EXPERTS_PER_GROUP = N_EXPERTS // N_GROUPS
TOP_K = 2
EXPERT_FF = 512
MOE_BLOCK = 256
A_SIZES = (ATT_Q_HEADS * HEAD_DIM, ATT_KV_HEADS * HEAD_DIM, ATT_KV_HEADS * HEAD_DIM)
RET_SIZES = (BRANCH_W,) * 5
MLA_SIZES = (MLA_Q_RANK, MLA_KV_RANK, MLA_ROPE)
RWKV_SIZES = (BRANCH_W, BRANCH_W, BRANCH_W, RWKV_DECAY_RANK, RWKV_DECAY_RANK,
              RWKV_A_RANK, RWKV_A_RANK, RWKV_GATE_RANK)
GROUP_SIZES = (sum(A_SIZES), sum(RET_SIZES), sum(MLA_SIZES), sum(RWKV_SIZES))
N_IN = sum(GROUP_SIZES)
F32 = jnp.float32


def split_sizes(x, sizes):
    out, o = [], 0
    for n in sizes:
        out.append(x[..., o:o + n])
        o += n
    return out


def heads(x, h):
    return x.reshape(x.shape[:-1] + (h, x.shape[-1] // h))


def rms_norm(x, g):
    xf = x.astype(F32)
    y = xf * lax.rsqrt(jnp.mean(xf * xf, axis=-1, keepdims=True) + NORM_EPS)
    return (y * g.astype(F32)).astype(x.dtype)


def head_norm(x, eps):
    xf = x.astype(F32)
    xc = xf - jnp.mean(xf, axis=-1, keepdims=True)
    return xc * lax.rsqrt(jnp.mean(xc * xc, axis=-1, keepdims=True) + eps)


def adaln(h, g, shift, scale):
    return rms_norm(h, g) * (1 + scale) + shift


def rope_tables(pos, dim):
    inv = ROPE_BASE ** (-jnp.arange(0, dim, 2, dtype=F32) / dim)
    ang = pos[:, None] * inv[None, :]
    return jnp.cos(ang), jnp.sin(ang)


def rope_rotate(x, cos, sin):
    m = x.shape[-1] // 2
    x1, x2 = x[..., :m], x[..., m:]
    c = cos[None, :, None, :]
    s = sin[None, :, None, :]
    return jnp.concatenate([x1 * c - x2 * s, x2 * c + x1 * s], axis=-1).astype(x.dtype)


def axial_tables(seq, dim):
    rows = seq // GRID_W
    row = jnp.repeat(jnp.arange(rows, dtype=F32), GRID_W)
    col = jnp.broadcast_to(jnp.arange(GRID_W, dtype=F32)[None, :], (rows, GRID_W)).reshape(-1)
    return rope_tables(row, dim // 2), rope_tables(col, dim // 2)


def axial_rope(x, tabs):
    (cos_r, sin_r), (cos_c, sin_c) = tabs
    half = x.shape[-1] // 2
    return jnp.concatenate([rope_rotate(x[..., :half], cos_r, sin_r),
                            rope_rotate(x[..., half:], cos_c, sin_c)], axis=-1)


def window_gqa(q, k, v, k_ctx, v_ctx, sink):
    b, s, hq, d = q.shape
    g = hq // ATT_KV_HEADS
    nb = s // BLOCK
    nw = 3 * BLOCK
    nc = k_ctx.shape[1]
    qb = (q * d ** -0.5).reshape(b, nb, BLOCK, ATT_KV_HEADS, g, d)
    pad = ((0, 0), (BLOCK, BLOCK), (0, 0), (0, 0))
    kp = jnp.pad(k, pad).reshape(b, nb + 2, BLOCK, ATT_KV_HEADS, d)
    vp = jnp.pad(v, pad).reshape(b, nb + 2, BLOCK, ATT_KV_HEADS, d)
    kw = jnp.concatenate([kp[:, :-2], kp[:, 1:-1], kp[:, 2:]], axis=2)
    vw = jnp.concatenate([vp[:, :-2], vp[:, 1:-1], vp[:, 2:]], axis=2)
    blk0 = jnp.arange(nb)[:, None, None] * BLOCK
    q_pos = blk0 + jnp.arange(BLOCK)[None, :, None]
    k_pos = blk0 - BLOCK + jnp.arange(nw)[None, None, :]
    valid = (jnp.abs(k_pos - q_pos) <= WINDOW) & (k_pos >= 0) & (k_pos < s)
    s_win = jnp.einsum('bnqhgd,bnkhd->bnhgqk', qb, kw).astype(F32)
    s_win = jnp.where(valid[None, :, None, None], s_win, NEG_INF)
    s_ctx = jnp.einsum('bnqhgd,bchd->bnhgqc', qb, k_ctx).astype(F32)
    s_sink = jnp.broadcast_to(sink.astype(F32).reshape(ATT_KV_HEADS, g, 1, 1), s_win.shape[:-1] + (1,))
    p = jax.nn.softmax(jnp.concatenate([s_win, s_ctx, s_sink], axis=-1), axis=-1).astype(v.dtype)
    o = (jnp.einsum('bnhgqk,bnkhd->bnqhgd', p[..., :nw], vw)
         + jnp.einsum('bnhgqc,bchd->bnqhgd', p[..., nw:nw + nc], v_ctx))
    return o.reshape(b, s, hq * d)


def ctx_gqa(q, k, v, sink):
    b, l, hq, d = q.shape
    g = hq // ATT_KV_HEADS
    qg = (q * d ** -0.5).reshape(b, l, ATT_KV_HEADS, g, d)
    sc = jnp.einsum('bqhgd,bkhd->bhgqk', qg, k).astype(F32)
    s_sink = jnp.broadcast_to(sink.astype(F32).reshape(ATT_KV_HEADS, g, 1, 1), sc.shape[:-1] + (1,))
    p = jax.nn.softmax(jnp.concatenate([sc, s_sink], axis=-1), axis=-1).astype(v.dtype)
    o = jnp.einsum('bhgqk,bkhd->bqhgd', p[..., :l], v)
    return o.reshape(b, l, hq * d)


def retention_log_gammas():
    lg = jnp.log(1.0 - jnp.exp(jnp.linspace(math.log(1.0 / 32), math.log(1.0 / 512), 2 * RET_HEADS, dtype=F32)))
    return lg[0::2], lg[1::2]


def retention_chunked(q, k, v, log_gamma, s0):
    b, t, h, d = q.shape
    n = t // RET_CHUNK
    cl = RET_CHUNK
    qc = q.reshape(b, n, cl, h, d).astype(F32)
    kc = k.reshape(b, n, cl, h, d).astype(F32)
    vc = v.reshape(b, n, cl, h, d).astype(F32)
    idx = jnp.arange(cl, dtype=F32)
    rel = idx[:, None] - idx[None, :]
    d_intra = jnp.where(rel >= 0, jnp.exp(jnp.maximum(rel, 0.0)[None] * log_gamma[:, None, None]), 0.0)
    a_int = jnp.einsum('bnihd,bnjhd->bnhij', qc, kc) * d_intra
    o = jnp.einsum('bnhij,bnjhe->bnihe', a_int, vc)
    k_dec = kc * jnp.exp((cl - 1 - idx)[:, None] * log_gamma[None, :])[..., None]
    kv = jnp.einsum('bnjhd,bnjhe->nbhde', k_dec, vc)
    g_chunk = jnp.exp(cl * log_gamma)[:, None, None]

    def step(state, kv_n):
        return state * g_chunk + kv_n, state

    s_fin, s_prev = lax.scan(step, s0, kv)
    q_dec = qc * jnp.exp((idx + 1)[:, None] * log_gamma[None, :])[..., None]
    o = o + jnp.einsum('bnihd,nbhde->bnihe', q_dec, s_prev)
    return o.reshape(b, t, h, d), s_fin


RET_TOKENS_PER_STEP = 1024


def _retention_kernel(q_ref, k_ref, v_ref, g_ref, cos_ref, sin_ref, d_ref, kdec_ref, qdec_ref, gch_ref, s0_ref,
                      y_ref, sfin_ref, s_scr, *, reverse, n_chunks):
    cl = RET_CHUNK
    nh, hd = s_scr.shape[:2]
    tb, width = q_ref.shape[1:]

    @pl.when(pl.program_id(1) == 0)
    def _():
        s_scr[...] = s0_ref[0]

    first_half = lax.broadcasted_iota(jnp.int32, (tb, width), 1) % hd < hd // 2
    cos, sin = cos_ref[...], sin_ref[...]

    def rope(x):
        swapped = jnp.where(first_half, -pltpu.roll(x, width - hd // 2, 1), pltpu.roll(x, hd // 2, 1))
        return x * cos + swapped * sin

    nb = nh * n_chunks
    shp = (nb, cl, hd)
    per_chunk = lambda z: jnp.broadcast_to(z[:, None], (nh, n_chunks) + z.shape[1:]).reshape((nb,) + z.shape[1:])
    q = _split_heads(rope(q_ref[0]), nh).reshape(shp)
    k = _split_heads(rope(k_ref[0]) * hd ** -0.5, nh).reshape(shp)
    v = _split_heads(v_ref[0], nh).reshape(shp)
    a = _dot(q, k, _NT, 1) * per_chunk(d_ref[...])
    o = _dot(a, v, _NN, 1)
    kv = _dot(k * per_chunk(kdec_ref[...]), v, _TN, 1).reshape(nh, n_chunks, hd, hd)

    s = s_scr[...]
    gch = gch_ref[...]
    starts = [None] * n_chunks
    order = range(n_chunks - 1, -1, -1) if reverse else range(n_chunks)
    for n in order:
        starts[n] = s
        s = s * gch + kv[:, n]
    s_scr[...] = s
    sfin_ref[0] = s
    s_in = jnp.stack(starts, axis=1).reshape(nb, hd, hd)
    o = o + _dot(q * per_chunk(qdec_ref[...]), s_in, _NN, 1)

    oc = o - jnp.mean(o, axis=-1, keepdims=True)
    normed = oc * lax.rsqrt(jnp.mean(oc * oc, axis=-1, keepdims=True) + RET_GN_EPS)
    g = g_ref[0]
    y_ref[0] = (g * jax.nn.sigmoid(g)) * _join_heads(normed.reshape(nh, tb, hd))


def _split_heads(x, nh):
    hd = x.shape[-1] // nh
    return jnp.stack([x[:, h * hd:(h + 1) * hd] for h in range(nh)], axis=0)


def _join_heads(y):
    return jnp.concatenate([y[h] for h in range(y.shape[0])], axis=-1)


def retention_direction(pb, gate_col, cos, sin, log_gamma, s0, reverse):
    bsz, t = pb.shape[:2]
    nh, hd = s0.shape[1:3]
    width = nh * hd
    tb = min(RET_TOKENS_PER_STEP, t)
    nt = t // tb
    assert t == nt * tb and tb % RET_CHUNK == 0
    cl = RET_CHUNK
    idx = jnp.arange(cl, dtype=F32)
    rel = (idx[None, :] - idx[:, None]) if reverse else (idx[:, None] - idx[None, :])
    lg = log_gamma[:, None, None]
    d_intra = jnp.where(rel >= 0, jnp.exp(jnp.maximum(rel, 0.0)[None] * lg), 0.0)
    kdec = jnp.exp((idx if reverse else cl - 1 - idx)[None, :, None] * lg)
    qdec = jnp.exp((cl - idx if reverse else idx + 1)[None, :, None] * lg)
    gch = jnp.exp(cl * lg)
    tblk = (lambda j: nt - 1 - j) if reverse else (lambda j: j)
    col_spec = lambda col: pl.BlockSpec((1, tb, width), lambda bi, j: (bi, tblk(j), col))
    tab_spec = pl.BlockSpec((tb, width), lambda bi, j: (tblk(j), 0))
    st_spec = pl.BlockSpec((1, nh, hd, hd), lambda bi, j: (bi, 0, 0, 0))
    const = lambda shape: pl.BlockSpec(shape, lambda bi, j: (0,) * len(shape))
    y, sfin = pl.pallas_call(
        functools.partial(_retention_kernel, reverse=reverse, n_chunks=tb // cl),
        grid=(bsz, nt),
        in_specs=[col_spec(0), col_spec(1), col_spec(2), col_spec(gate_col), tab_spec, tab_spec,
                  const((nh, cl, cl)), const((nh, cl, 1)), const((nh, cl, 1)), const((nh, 1, 1)), st_spec],
        out_specs=[col_spec(0), st_spec],
        out_shape=[jax.ShapeDtypeStruct((bsz, t, width), F32),
                   jax.ShapeDtypeStruct((bsz, nh, hd, hd), F32)],
        scratch_shapes=[pltpu.VMEM((nh, hd, hd), F32)],
        compiler_params=pltpu.CompilerParams(dimension_semantics=("parallel", "arbitrary"),
                                             vmem_limit_bytes=RWKV_VMEM_LIMIT),
        name="retention_rev" if reverse else "retention_fwd",
    )(pb, pb, pb, pb, cos, sin, d_intra, kdec, qdec, gch, s0)
    return y, sfin


def retention_mix(pb, cos, sin, lg_f, lg_b, s0f, s0b):
    tile = lambda tab: jnp.tile(jnp.concatenate([tab, tab], axis=-1), (1, RET_HEADS))
    cos_t, sin_t = tile(cos), tile(sin)
    yf, sf = retention_direction(pb, 3, cos_t, sin_t, lg_f, s0f, False)
    yb, sb = retention_direction(pb, 4, cos_t, sin_t, lg_b, s0b, True)
    return (yf + yb).astype(pb.dtype), sf, sb


def mla_project(pc, lp):
    cq, ckv, kr = split_sizes(pc, MLA_SIZES)
    q = heads(rms_norm(cq, lp['g_qnorm']) @ lp['w_uq'], MLA_HEADS)
    kv = heads(rms_norm(ckv, lp['g_kvnorm']) @ lp['w_ukv'], MLA_HEADS)
    return q[..., :MLA_NOPE], q[..., MLA_NOPE:], kv[..., :MLA_NOPE], kr[:, :, None, :], kv[..., MLA_NOPE:]


LANES = 128
SUBLANES = 8
MLA_Q_TILE = 512
MLA_KEY_UNIT = 256
MLA_KEY_TILE_MAX = 8448
MLA_VMEM_LIMIT = 56 * 1024 * 1024


def _mla_flash_kernel(q_ref, k_ref, v_ref, o_ref, m_scr, acc_scr, *, scale, tk, n_tiles):
    dv = o_ref.shape[-1]
    q = (q_ref[0, 0] * scale).astype(BF16)
    m_scr[...] = jnp.full_like(m_scr, NEG_INF)
    acc_scr[...] = jnp.zeros_like(acc_scr)

    def sweep(c, carry):
        keys = pl.ds(pl.multiple_of(c * tk, tk), tk)
        s = lax.dot_general(q, k_ref[0, 0, keys, :], (((1,), (1,)), ((), ())), preferred_element_type=F32)
        m_prev = m_scr[...]
        m_new = jnp.maximum(m_prev, jnp.max(s, axis=-1, keepdims=True))
        p = jnp.exp2(s - m_new).astype(BF16)
        acc_scr[...] = (jnp.exp2(m_prev - m_new) * acc_scr[...]
                        + jnp.dot(p, v_ref[0, 0, keys, :], preferred_element_type=F32))
        m_scr[...] = m_new
        return carry

    lax.fori_loop(0, n_tiles, sweep, 0)
    acc = acc_scr[...]
    o_ref[0, 0] = acc[:, :dv] / acc[:, dv:dv + 1]


def mla_attend(qn, qr, kn, kr, v):
    b, t, h, _ = qn.shape
    n = kn.shape[1]
    dv = v.shape[-1]
    scale = (MLA_NOPE + MLA_ROPE) ** -0.5 * math.log2(math.e)
    q = jnp.swapaxes(jnp.concatenate([qn, qr], axis=-1), 1, 2)
    kr_h = jnp.broadcast_to(kr, (b, n, h, MLA_ROPE))
    k = jnp.swapaxes(jnp.concatenate([kn, kr_h], axis=-1), 1, 2).astype(BF16)
    ones = jnp.ones((b, n, h, 1), v.dtype)
    zeros = jnp.zeros((b, n, h, LANES - dv - 1), v.dtype)
    vt = jnp.swapaxes(jnp.concatenate([v, ones, zeros], axis=-1), 1, 2).astype(BF16)
    dq = q.shape[-1]
    tq = min(MLA_Q_TILE, t)
    tk = max(d for d in range(MLA_KEY_UNIT, MLA_KEY_TILE_MAX + 1, MLA_KEY_UNIT) if n % d == 0)
    assert t % tq == 0
    o = pl.pallas_call(
        functools.partial(_mla_flash_kernel, scale=scale, tk=tk, n_tiles=n // tk),
        grid=(b, h, t // tq),
        in_specs=[pl.BlockSpec((1, 1, tq, dq), lambda bi, hi, qi: (bi, hi, qi, 0)),
                  pl.BlockSpec((1, 1, n, dq), lambda bi, hi, qi: (bi, hi, 0, 0)),
                  pl.BlockSpec((1, 1, n, LANES), lambda bi, hi, qi: (bi, hi, 0, 0))],
        out_specs=pl.BlockSpec((1, 1, tq, dv), lambda bi, hi, qi: (bi, hi, qi, 0)),
        out_shape=jax.ShapeDtypeStruct((b, h, t, dv), F32),
        scratch_shapes=[pltpu.VMEM((tq, 1), F32), pltpu.VMEM((tq, LANES), F32)],
        compiler_params=pltpu.CompilerParams(
            dimension_semantics=("parallel", "parallel", "arbitrary"),
            vmem_limit_bytes=MLA_VMEM_LIMIT),
        name="mla_flash",
    )(q, k, vt)
    return jnp.swapaxes(o, 1, 2).reshape(b, t, h * dv)


def centred_shift(p):
    prev = jnp.pad(p, ((0, 0), (1, 0), (0, 0)))[:, :-1]
    nxt = jnp.pad(p, ((0, 0), (0, 1), (0, 0)))[:, 1:]
    return 0.5 * (prev + nxt)


def rwkv_features(pd, lp):
    z = pd + (centred_shift(pd) - pd) * lp['rwkv_mu']
    zr, zk, zv, zwf, zwb, zaf, zab, zg = split_sizes(z, RWKV_SIZES)
    r = heads(zr, RWKV_HEADS)
    k = heads(zk, RWKV_HEADS)
    v = heads(zv, RWKV_HEADS)
    g = jax.nn.sigmoid(zg) @ lp['rwkv_g_up']
    kk = (k * heads(lp['rwkv_k_k'], RWKV_HEADS)).astype(F32)
    kk = kk / jnp.maximum(jnp.sqrt(jnp.sum(kk * kk, axis=-1, keepdims=True)), 1e-12)
    k_a = heads(lp['rwkv_k_a'], RWKV_HEADS)
    dirs = []
    for di, (zw, za) in enumerate(((zwf, zaf), (zwb, zab))):
        logw = -jax.nn.softplus(-(lp['rwkv_w0'][di] + jnp.tanh(zw) @ lp['rwkv_w_up'][di]).astype(F32)) - 0.5
        w = heads(-jnp.exp(logw), RWKV_HEADS)
        a = heads(jax.nn.sigmoid((lp['rwkv_a0'][di] + za @ lp['rwkv_a_up'][di]).astype(F32)), RWKV_HEADS)
        dirs.append((w, k * (1 + (a - 1) * k_a), kk * a))
    return r, v, g, kk, dirs


def rwkv_scan(r, w, k, v, kk, b_, s0, reverse):
    xs = tuple(jnp.moveaxis(z.astype(F32), 1, 0) for z in (r, w, k, v, kk, b_))

    def step(st, inp):
        r_t, w_t, k_t, v_t, kk_t, b_t = inp
        sa = jnp.einsum('bhvk,bhk->bhv', st, -kk_t)
        st = st * w_t[:, :, None, :] + sa[..., None] * b_t[:, :, None, :] + v_t[..., None] * k_t[:, :, None, :]
        return st, jnp.einsum('bhvk,bhk->bhv', st, r_t)

    s_fin, y = lax.scan(step, s0, xs, reverse=reverse)
    return jnp.moveaxis(y, 0, 1), s_fin


BF16 = jnp.bfloat16
RWKV_CHUNK = 64
RWKV_SUB = 16
RWKV_TOKENS_PER_STEP = 512
RWKV_PASSES = (1, 1)
RWKV_VMEM_LIMIT = 48 * 1024 * 1024
_NN = (((2,), (1,)), ((0,), (0,)))
_NT = (((2,), (2,)), ((0,), (0,)))
_TN = (((1,), (1,)), ((0,), (0,)))


def _bf16_parts(a, n):
    parts, rem = [], a
    for i in range(n):
        p = rem.astype(BF16)
        parts.append(p)
        if i + 1 < n:
            rem = rem - p.astype(F32)
    return parts


def _dot(a, b, dims, passes):
    if passes == 1:
        return lax.dot_general(a.astype(BF16), b.astype(BF16), dims, preferred_element_type=F32)
    a_hi, a_lo = _bf16_parts(a, 2)
    b_hi, b_lo = _bf16_parts(b, 2)
    out = lax.dot_general(a_hi, b_hi, dims, preferred_element_type=F32)
    out = out + lax.dot_general(a_hi, b_lo, dims, preferred_element_type=F32)
    return out + lax.dot_general(a_lo, b_hi, dims, preferred_element_type=F32)


def _unit_lower_inverse(l_mat, same_sub, passes):
    mm = lambda a, b: _dot(a, b, _NN, passes)
    ld = jnp.where(same_sub, l_mat, 0.0)
    lo = l_mat - ld
    p2 = mm(ld, ld)
    d = p2 - ld - mm(ld, p2)
    pw = p2
    span = 4
    while span < RWKV_SUB:
        pw = mm(pw, pw)
        d = d + pw + mm(d, pw)
        span *= 2
    n1 = lo + mm(d, lo)
    n2 = mm(n1, n1)
    x = n2 - n1 - mm(n1, n2)
    span = 4
    pw = n2
    while span < RWKV_CHUNK // RWKV_SUB:
        pw = mm(pw, pw)
        x = x + pw + mm(x, pw)
        span *= 2
    return x + d + mm(x, d)


def _rwkv_scan_kernel(r_ref, lw_ref, k_ref, v_ref, kk_ref, b_ref, s0_ref, y_ref, sfin_ref, s_scr,
                      *, reverse, n_chunks):
    cl = RWKV_CHUNK
    nh, hd = s_scr.shape[:2]
    tb = r_ref.shape[1]

    @pl.when(pl.program_id(1) == 0)
    def _():
        s_scr[...] = s0_ref[0]

    nb = nh * n_chunks
    shp = (nb, cl, hd)
    r = _split_heads(r_ref[0], nh).reshape(shp)
    lw = _split_heads(lw_ref[0], nh).reshape(shp)
    k = _split_heads(k_ref[0], nh).reshape(shp)
    v = _split_heads(v_ref[0], nh).reshape(shp)
    kk = _split_heads(kk_ref[0], nh).reshape(shp)
    b = _split_heads(b_ref[0], nh).reshape(shp)

    row = lax.broadcasted_iota(jnp.int32, (cl, cl), 0)
    col = lax.broadcasted_iota(jnp.int32, (cl, cl), 1)
    before = (col > row) if reverse else (col < row)
    upto = before | (col == row)
    same_sub = (row // RWKV_SUB) == (col // RWKV_SUB)

    tri = jnp.broadcast_to(jnp.where(upto, 1.0, 0.0).astype(BF16)[None], (nb, cl, cl))
    cum = sum(lax.dot_general(tri, p, _NN, preferred_element_type=F32) for p in _bf16_parts(lw, 3))
    last = 0 if reverse else cl - 1
    mid = cl // 2 if reverse else cl // 2 - 1
    tot = cum[:, last:last + 1, :]
    rho = cum[:, mid:mid + 1, :]
    cum_ex = cum - lw

    a_t = kk * jnp.exp(cum_ex - rho)
    r_t = r * jnp.exp(cum - rho)
    e_out = jnp.exp(rho - cum)
    b_t = b * e_out
    k_t = k * e_out
    a_0 = kk * jnp.exp(cum_ex)
    r_0 = r * jnp.exp(cum)
    e_end = jnp.exp(tot - cum)
    b_e = b * e_end
    k_e = k * e_end

    ps, pt = RWKV_PASSES
    sc = _dot(jnp.concatenate([a_t, r_t], axis=1), jnp.concatenate([b_t, k_t], axis=1), _NT, ps)
    l_mat = jnp.where(before, sc[:, :cl, :cl], 0.0)
    m_ak = jnp.where(before, sc[:, :cl, cl:], 0.0)
    m_rb = jnp.where(upto, sc[:, cl:, :cl], 0.0)
    m_rk = jnp.where(upto, sc[:, cl:, cl:], 0.0)
    t_m1 = _unit_lower_inverse(l_mat, same_sub, pt)

    mv = _dot(m_ak, v, _NN, ps)
    am = jnp.concatenate([a_0, mv], axis=2)
    qw = -(am + _dot(t_m1, am, _NN, ps))
    rq = _dot(m_rb, qw, _NN, ps)
    r_h = r_0 + rq[:, :, :hd]
    y_loc = rq[:, :, hd:] + _dot(m_rk, v, _NN, ps)
    eye = lax.broadcasted_iota(jnp.int32, (hd, hd), 0) == lax.broadcasted_iota(jnp.int32, (hd, hd), 1)
    pq = _dot(b_e, qw, _TN, ps)
    phi = (jnp.where(eye, jnp.exp(tot), 0.0) + pq[:, :, :hd]).reshape(nh, n_chunks, hd, hd)
    psi = (pq[:, :, hd:] + _dot(k_e, v, _TN, ps)).reshape(nh, n_chunks, hd, hd)

    s = s_scr[...]
    starts = [None] * n_chunks
    order = range(n_chunks - 1, -1, -1) if reverse else range(n_chunks)
    for n in order:
        starts[n] = s
        s = _dot(phi[:, n], s, _NN, 3) + psi[:, n]
    s_scr[...] = s
    sfin_ref[0] = s
    s_in = jnp.stack(starts, axis=1).reshape(nb, hd, hd)
    y = _dot(r_h, s_in, _NN, ps) + y_loc
    y_ref[0] = _join_heads(y.reshape(nh, tb, hd))


def rwkv_scan_blocked(r, lw, k, v, kk, b_, s0t, reverse):
    bsz, t, width = r.shape
    nh, hd = s0t.shape[1:3]
    tb = min(RWKV_TOKENS_PER_STEP, t)
    nt = t // tb
    assert t == nt * tb and tb % RWKV_CHUNK == 0
    if reverse:
        tmap = lambda bi, j: (bi, nt - 1 - j, 0)
    else:
        tmap = lambda bi, j: (bi, j, 0)
    seq_spec = pl.BlockSpec((1, tb, width), tmap)
    st_spec = pl.BlockSpec((1, nh, hd, hd), lambda bi, j: (bi, 0, 0, 0))
    y, sfin = pl.pallas_call(
        functools.partial(_rwkv_scan_kernel, reverse=reverse, n_chunks=tb // RWKV_CHUNK),
        grid=(bsz, nt),
        in_specs=[seq_spec] * 6 + [st_spec],
        out_specs=[seq_spec, st_spec],
        out_shape=[jax.ShapeDtypeStruct((bsz, t, width), F32),
                   jax.ShapeDtypeStruct((bsz, nh, hd, hd), F32)],
        scratch_shapes=[pltpu.VMEM((nh, hd, hd), F32)],
        compiler_params=pltpu.CompilerParams(dimension_semantics=("parallel", "arbitrary"),
                                             vmem_limit_bytes=RWKV_VMEM_LIMIT),
        name="rwkv_scan_rev" if reverse else "rwkv_scan_fwd",
    )(r, lw, k, v, kk, b_, s0t)
    return y, sfin


def rwkv_mix(feat, lp, s0f, s0b):
    r, v, g, kk, dirs = feat
    (lwf, kf, bf), (lwb, kb, bb) = dirs
    flat = lambda z: z.astype(F32).reshape(z.shape[0], z.shape[1], -1)
    rt, vt, kkt = flat(r), flat(v), flat(kk)
    yf, sf = rwkv_scan_blocked(rt, flat(lwf), flat(kf), vt, kkt, flat(bf), s0f, False)
    yb, sb = rwkv_scan_blocked(rt, flat(lwb), flat(kb), vt, kkt, flat(bb), s0b, True)
    y = head_norm(heads(yf + yb, RWKV_HEADS), RWKV_GN_EPS)
    y = y * heads(lp['rwkv_lnx_w'], RWKV_HEADS) + heads(lp['rwkv_lnx_b'], RWKV_HEADS)
    bonus = jnp.sum(r * (kf + kb) * lp['rwkv_r_k'], axis=-1, keepdims=True) * v
    out = (y + bonus).reshape(g.shape) * g
    return out.astype(g.dtype), sf, sb


MERGE_ROW_TILE = 512
MERGE_VMEM_LIMIT = 52 * 1024 * 1024


def _merge_kernel(u_ref, ya_ref, yb_ref, yc_ref, yd_ref, h_ref, mod_ref, g2_ref, wg_ref, bg_ref, wb_ref, wo_ref,
                  hn_ref, f_ref):
    u = u_ref[...].astype(BF16)
    acc = None
    for n, y_ref in enumerate((ya_ref, yb_ref, yc_ref, yd_ref)):
        gate = jax.nn.sigmoid(jnp.dot(u, wg_ref[n], preferred_element_type=F32) + bg_ref[n])
        term = gate * jnp.dot(y_ref[...].astype(BF16), wb_ref[n], preferred_element_type=F32)
        acc = term if acc is None else acc + term
    mix = jnp.dot(acc.astype(BF16), wo_ref[...], preferred_element_type=F32)
    mod = mod_ref[0]
    hn = h_ref[...] + mod[0:1] * mix
    hn_ref[...] = hn
    normed = hn * lax.rsqrt(jnp.mean(hn * hn, axis=-1, keepdims=True) + NORM_EPS) * g2_ref[...]
    f_ref[...] = normed * (1 + mod[2:3]) + mod[1:2]


def merge_branches(u, ys, lp, h, mod):
    lead, d = u.shape[:-1], u.shape[-1]
    rows = math.prod(lead)
    tm = min(MERGE_ROW_TILE, rows)
    rows_per_mod = rows // mod.shape[0]
    assert rows % tm == 0 and (mod.shape[0] == 1 or rows_per_mod % tm == 0)
    bw = ys[0].shape[-1]
    row_spec = lambda width: pl.BlockSpec((tm, width), lambda i: (i, 0))
    whole = lambda shape: pl.BlockSpec(shape, lambda i: (0,) * len(shape))
    hn, f = pl.pallas_call(
        _merge_kernel,
        grid=(rows // tm,),
        in_specs=[row_spec(d)] + [row_spec(bw)] * N_BRANCH
                 + [row_spec(d), pl.BlockSpec((1, 3, d), lambda i: ((i * tm) // rows_per_mod, 0, 0)), whole((1, d)),
                    whole((N_BRANCH, d, d)), whole((N_BRANCH, 1, d)), whole((N_BRANCH, bw, d)), whole((d, d))],
        out_specs=[row_spec(d), row_spec(d)],
        out_shape=[jax.ShapeDtypeStruct((rows, d), F32), jax.ShapeDtypeStruct((rows, d), F32)],
        compiler_params=pltpu.CompilerParams(dimension_semantics=("parallel",),
                                             vmem_limit_bytes=MERGE_VMEM_LIMIT),
        name="merge_branches",
    )(u.reshape(rows, d), *[y.reshape(rows, bw) for y in ys], h.reshape(rows, d), mod, lp['g_norm2'][None, :],
      lp['w_gate'].astype(BF16), lp['b_gate'][:, None, :], lp['w_branch'].astype(BF16), lp['w_out'].astype(BF16))
    return hn.reshape(lead + (d,)), f.reshape(lead + (d,))


INPROJ_ROW_TILE = 512
INPROJ_VMEM_LIMIT = 48 * 1024 * 1024


def _inproj_kernel(h_ref, mod_ref, g_ref, *refs):
    n = (len(refs) - 1) // 2
    w_refs, u_ref, p_refs = refs[:n], refs[n], refs[n + 1:]
    h = h_ref[...]
    mod = mod_ref[0]
    normed = h * lax.rsqrt(jnp.mean(h * h, axis=-1, keepdims=True) + NORM_EPS) * g_ref[...]
    u = (normed * (1 + mod[1:2]) + mod[0:1]).astype(BF16)
    u_ref[...] = u
    for w_ref, p_ref in zip(w_refs, p_refs):
        p_ref[...] = jnp.dot(u, w_ref[...], preferred_element_type=F32)


def adaln_in_proj(h, mod, g, w_in):
    bsz, t, d = h.shape
    rows = bsz * t
    tm = min(INPROJ_ROW_TILE, rows)
    rows_per_mod = rows // mod.shape[0]
    assert rows % tm == 0 and (mod.shape[0] == 1 or rows_per_mod % tm == 0)
    w_groups = [w.astype(BF16) for w in split_sizes(w_in, GROUP_SIZES)]
    row_spec = lambda width: pl.BlockSpec((tm, width), lambda i: (i, 0))
    whole = lambda shape: pl.BlockSpec(shape, lambda i: (0,) * len(shape))
    outs = pl.pallas_call(
        _inproj_kernel,
        grid=(rows // tm,),
        in_specs=[row_spec(d), pl.BlockSpec((1, 2, d), lambda i: ((i * tm) // rows_per_mod, 0, 0)), whole((1, d))]
                 + [whole(w.shape) for w in w_groups],
        out_specs=[row_spec(d)] + [row_spec(n) for n in GROUP_SIZES],
        out_shape=[jax.ShapeDtypeStruct((rows, d), BF16)]
                  + [jax.ShapeDtypeStruct((rows, n), F32) for n in GROUP_SIZES],
        compiler_params=pltpu.CompilerParams(dimension_semantics=("parallel",),
                                             vmem_limit_bytes=INPROJ_VMEM_LIMIT),
        name="adaln_in_proj",
    )(h.reshape(rows, d), mod, g[None, :], *w_groups)
    return [o.reshape(bsz, t, -1) for o in outs]


def token_mixers(lp, need_ctx, h_ctx, h_lat, mod1_ctx, mod1_lat, mod_ctx, mod_lat):
    b, s, _ = h_lat.shape
    l = h_ctx.shape[1]
    u_lat, pa_l, pb_l, pc_l, pd_l = adaln_in_proj(h_lat, mod1_lat, lp['g_norm1'], lp['w_in'])
    u_ctx, pa_c, pb_c, pc_c, pd_c = adaln_in_proj(h_ctx, mod1_ctx, lp['g_norm1'], lp['w_in'])
    tab_att = axial_tables(s, HEAD_DIM)
    tab_mla = axial_tables(s, MLA_ROPE)

    aq_l, ak_l, av_l = split_sizes(pa_l, A_SIZES)
    aq_c, ak_c, av_c = split_sizes(pa_c, A_SIZES)
    k_c = heads(ak_c, ATT_KV_HEADS)
    v_c = heads(av_c, ATT_KV_HEADS)
    ya_l = window_gqa(axial_rope(heads(aq_l, ATT_Q_HEADS), tab_att),
                      axial_rope(heads(ak_l, ATT_KV_HEADS), tab_att),
                      heads(av_l, ATT_KV_HEADS), k_c, v_c, lp['sink'])

    lg_f, lg_b = retention_log_gammas()
    cos_c, sin_c = rope_tables(jnp.arange(l, dtype=F32), HEAD_DIM)
    cos_l, sin_l = rope_tables(l + jnp.arange(s, dtype=F32), HEAD_DIM)
    zero_ret = jnp.zeros((b, RET_HEADS, HEAD_DIM, HEAD_DIM), F32)
    yb_c, sbf, sbb = retention_mix(pb_c, cos_c, sin_c, lg_f, lg_b, zero_ret, zero_ret)
    yb_l, _, _ = retention_mix(pb_l, cos_l, sin_l, lg_f, lg_b, sbf, sbb)

    qn_l, qr_l, kn_l, kr_l, vm_l = mla_project(pc_l, lp)
    qn_c, qr_c, kn_c, kr_c, vm_c = mla_project(pc_c, lp)
    qr_l = axial_rope(qr_l, tab_mla)
    kr_l = axial_rope(kr_l, tab_mla)
    yc_l = mla_attend(qn_l, qr_l, jnp.concatenate([kn_c, kn_l], axis=1),
                      jnp.concatenate([kr_c, kr_l], axis=1), jnp.concatenate([vm_c, vm_l], axis=1))

    zero_wkv = jnp.zeros((b, RWKV_HEADS, HEAD_DIM, HEAD_DIM), F32)
    yd_c, sdf, sdb = rwkv_mix(rwkv_features(pd_c, lp), lp, zero_wkv, zero_wkv)
    yd_l, _, _ = rwkv_mix(rwkv_features(pd_l, lp), lp, sdf, sdb)

    out_lat = merge_branches(u_lat, (ya_l, yb_l, yc_l, yd_l), lp, h_lat, mod_lat)
    if not need_ctx:
        return None, out_lat
    ya_c = ctx_gqa(heads(aq_c, ATT_Q_HEADS), k_c, v_c, lp['sink'])
    yc_c = mla_attend(qn_c, qr_c, kn_c, kr_c, vm_c)
    out_ctx = merge_branches(u_ctx, (ya_c, yb_c, yc_c, yd_c), lp, h_ctx, mod_ctx)
    return out_ctx, out_lat


def moe_ffn(x, router_w, router_b, layer_idx, w_e_gate, w_e_up, w_e_down):
    n_tok, dm = x.shape
    scores = jax.nn.sigmoid(x.astype(F32) @ router_w.astype(F32))
    grp = (scores + router_b.astype(F32)).reshape(n_tok, N_GROUPS, EXPERTS_PER_GROUP)
    g_sel = jnp.argmax(jnp.sum(lax.top_k(grp, TOP_K)[0], axis=-1), axis=-1)
    in_grp = lax.top_k(jnp.take_along_axis(grp, g_sel[:, None, None], axis=1)[:, 0], TOP_K)[1]
    e_idx = g_sel[:, None] * EXPERTS_PER_GROUP + in_grp
    w_sel = jnp.take_along_axis(scores, e_idx, axis=1)
    w_sel = w_sel / jnp.sum(w_sel, axis=-1, keepdims=True)
    n_asg = n_tok * TOP_K
    flat_e = e_idx.reshape(-1).astype(jnp.int32)
    flat_w = w_sel.reshape(-1)
    order = jnp.argsort(flat_e).astype(jnp.int32)
    rank = jnp.argsort(order).astype(jnp.int32)
    onehot = flat_e[:, None] == jnp.arange(N_EXPERTS, dtype=jnp.int32)[None, :]
    counts = jnp.sum(onehot, axis=0, dtype=jnp.int32)
    padded = (counts + MOE_BLOCK - 1) // MOE_BLOCK * MOE_BLOCK
    pad_end = jnp.cumsum(padded)
    pad_start = pad_end - padded
    start = jnp.cumsum(counts) - counts
    n_blocks = -(-n_asg // MOE_BLOCK) + N_EXPERTS
    n_rows = n_blocks * MOE_BLOCK
    blk_row0 = jnp.arange(n_blocks, dtype=jnp.int32) * MOE_BLOCK
    blk_e = jnp.minimum(jnp.sum(pad_end[None, :] <= blk_row0[:, None], axis=1, dtype=jnp.int32), N_EXPERTS - 1)
    off = (blk_row0 - pad_start[blk_e])[:, None] + jnp.arange(MOE_BLOCK, dtype=jnp.int32)[None, :]
    valid = (off < counts[blk_e][:, None]).reshape(-1)
    src = order[jnp.clip(start[blk_e][:, None] + off, 0, n_asg - 1).reshape(-1)]
    row_tok = jnp.where(valid, src // TOP_K, 0)
    row_w = jnp.where(valid, flat_w[src], 0.0)
    blk_valid = jnp.clip(counts[blk_e] - (blk_row0 - pad_start[blk_e]), 0, MOE_BLOCK)
    blk_groups = (blk_valid + MOE_GATHER_UNROLL - 1) // MOE_GATHER_UNROLL
    yb = moe_expert_blocks(blk_e, blk_groups.astype(jnp.int32), row_tok, x, row_w[:, None], layer_idx,
                           w_e_gate, w_e_up, w_e_down)
    shift = jnp.sum(jnp.where(onehot, (pad_start - start)[None, :], 0), axis=1, dtype=jnp.int32)
    slot = (rank + shift).reshape(n_tok, TOP_K)
    out = yb[slot[:, 0]]
    for j in range(1, TOP_K):
        out = out + yb[slot[:, j]]
    return out.astype(x.dtype)


MOE_VMEM_LIMIT = 40 * 1024 * 1024
MOE_GATHER_UNROLL = 8


def _moe_row_copy(x_hbm, xbuf, sem, tok, slot, r):
    return pltpu.make_async_copy(x_hbm.at[pl.ds(tok, 1)], xbuf.at[slot, pl.ds(r, 1)], sem.at[slot])


def _moe_expert_kernel(blk_e_ref, blk_groups_ref, row_tok_ref, x_hbm, w_ref, wg_ref, wu_ref, wd_ref, o_ref,
                       xbuf, sem):
    del blk_e_ref
    i = pl.program_id(0)
    slot = i % 2

    def for_each_row(block, fn):
        def body(gi, carry):
            for q in range(MOE_GATHER_UNROLL):
                fn(gi * MOE_GATHER_UNROLL + q, q)
            return carry
        lax.fori_loop(0, blk_groups_ref[block], body, 0)

    def issue(block, dst_slot):
        for_each_row(block, lambda r, q: _moe_row_copy(
            x_hbm, xbuf, sem, row_tok_ref[block * MOE_BLOCK + r], dst_slot, r).start(priority=q % 2))

    @pl.when(i == 0)
    def _():
        xbuf[...] = jnp.zeros_like(xbuf)
        issue(0, 0)

    @pl.when(i + 1 < pl.num_programs(0))
    def _():
        issue(i + 1, 1 - slot)

    for_each_row(i, lambda r, q: _moe_row_copy(x_hbm, xbuf, sem, 0, slot, r).wait())

    @pl.when(blk_groups_ref[i] == 0)
    def _():
        o_ref[...] = jnp.zeros_like(o_ref)

    @pl.when(blk_groups_ref[i] > 0)
    def _():
        x = jnp.concatenate([xbuf[slot, :, j, :] for j in range(xbuf.shape[2])], axis=-1).astype(BF16)
        g = jnp.dot(x, wg_ref[0, 0].astype(BF16), preferred_element_type=F32)
        u = jnp.dot(x, wu_ref[0, 0].astype(BF16), preferred_element_type=F32)
        hid = (g * jax.nn.sigmoid(g)) * u
        y = jnp.dot(hid.astype(BF16), wd_ref[0, 0].astype(BF16), preferred_element_type=F32)
        o_ref[...] = y * w_ref[...]


def moe_expert_blocks(blk_e, blk_groups, row_tok, x, row_w, layer_idx, w_e_gate, w_e_up, w_e_down):
    n_rows = row_tok.shape[0]
    n_tok, dm = x.shape
    n_blocks = n_rows // MOE_BLOCK
    ff = w_e_gate.shape[-1]
    assert dm == SUBLANES * LANES
    x = x.reshape(n_tok, SUBLANES, LANES)
    row_spec = lambda width: pl.BlockSpec((MOE_BLOCK, width), lambda i, be, bg, rt: (i, 0))
    expert_spec = lambda rows, cols: pl.BlockSpec((1, 1, rows, cols),
                                                  lambda i, be, bg, rt: (layer_idx, be[i], 0, 0))
    return pl.pallas_call(
        _moe_expert_kernel,
        grid_spec=pltpu.PrefetchScalarGridSpec(
            num_scalar_prefetch=3,
            grid=(n_blocks,),
            in_specs=[pl.BlockSpec(memory_space=pl.ANY), row_spec(1),
                      expert_spec(dm, ff), expert_spec(dm, ff), expert_spec(ff, dm)],
            out_specs=row_spec(dm),
            scratch_shapes=[pltpu.VMEM((2, MOE_BLOCK, SUBLANES, LANES), x.dtype), pltpu.SemaphoreType.DMA((2,))]),
        out_shape=jax.ShapeDtypeStruct((n_rows, dm), F32),
        compiler_params=pltpu.CompilerParams(dimension_semantics=("arbitrary",),
                                             vmem_limit_bytes=MOE_VMEM_LIMIT),
        name="moe_experts",
    )(blk_e, blk_groups, row_tok, x, row_w, w_e_gate, w_e_up, w_e_down)


def layer(h_ctx, h_lat, c, c_ctx, lp, router_w, router_b, need_ctx):
    b, s, d = h_lat.shape
    l = h_ctx.shape[1]
    m_lat = jnp.split((jax.nn.silu(c) @ lp['w_ada'] + lp['b_ada'])[:, None, :], 6, axis=-1)
    m_ctx = jnp.split(jax.nn.silu(c_ctx) @ lp['w_ada'] + lp['b_ada'], 6, axis=-1)
    mod1_lat = jnp.concatenate(m_lat[0:2], axis=1)
    mod1_ctx = jnp.stack(m_ctx[0:2], axis=0)[None]
    mod_lat = jnp.concatenate(m_lat[2:5], axis=1)
    mod_ctx = jnp.stack(m_ctx[2:5], axis=0)[None]
    out_ctx, (h_lat, f_lat) = token_mixers(lp, need_ctx, h_ctx, h_lat, mod1_ctx, mod1_lat, mod_ctx, mod_lat)
    f_lat = f_lat.reshape(b * s, d)
    if not need_ctx:
        ffn = moe_ffn(f_lat, router_w, router_b, lp['layer_idx'], *lp['experts'])
        return h_ctx, h_lat + m_lat[5] * ffn.reshape(b, s, d)
    h_ctx, f_ctx = out_ctx
    f_ctx = f_ctx.reshape(b * l, d)
    ffn = moe_ffn(jnp.concatenate([f_ctx, f_lat], axis=0), router_w, router_b, lp['layer_idx'], *lp['experts'])
    h_ctx = h_ctx + m_ctx[5] * ffn[:b * l].reshape(b, l, d)
    h_lat = h_lat + m_lat[5] * ffn[b * l:].reshape(b, s, d)
    return h_ctx, h_lat


def _final_norm_kernel(x_ref, g_ref, o_ref):
    x = x_ref[...]
    y = x * lax.rsqrt(jnp.mean(x * x, axis=-1, keepdims=True) + NORM_EPS)
    o_ref[...] = y * g_ref[...]


def final_rms_norm(x, g):
    b, s, d = x.shape
    rows = b * s
    tile = 1024
    out = pl.pallas_call(
        _final_norm_kernel,
        grid=(rows // tile,),
        in_specs=[pl.BlockSpec((tile, d), lambda i: (i, 0)), pl.BlockSpec((1, d), lambda i: (0, 0))],
        out_specs=pl.BlockSpec((tile, d), lambda i: (i, 0)),
        out_shape=jax.ShapeDtypeStruct((rows, d), x.dtype),
        name="final_rms_norm",
    )(x.reshape(rows, d), g.reshape(1, d))
    return out.reshape(b, s, d)


_LAYER_PARAM_NAMES = (
    'w_ada', 'b_ada', 'g_norm1', 'g_norm2', 'w_in', 'sink', 'g_qnorm', 'g_kvnorm', 'w_uq',
    'w_ukv', 'rwkv_mu', 'rwkv_w0', 'rwkv_w_up', 'rwkv_a0', 'rwkv_a_up', 'rwkv_g_up',
    'rwkv_k_k', 'rwkv_k_a', 'rwkv_r_k', 'rwkv_lnx_w', 'rwkv_lnx_b', 'w_gate', 'b_gate',
    'w_branch', 'w_out')


def kernel(x, c, ctx, c_ctx, w_ada, b_ada, g_norm1, g_norm2, w_in, sink, g_qnorm, g_kvnorm,
           w_uq, w_ukv, rwkv_mu, rwkv_w0, rwkv_w_up, rwkv_a0, rwkv_a_up, rwkv_g_up, rwkv_k_k,
           rwkv_k_a, rwkv_r_k, rwkv_lnx_w, rwkv_lnx_b, w_gate, b_gate, w_branch, w_out,
           router_w, router_b, w_e_gate, w_e_up, w_e_down, g_final):
    stacked = (w_ada, b_ada, g_norm1, g_norm2, w_in, sink, g_qnorm, g_kvnorm, w_uq, w_ukv, rwkv_mu,
               rwkv_w0, rwkv_w_up, rwkv_a0, rwkv_a_up, rwkv_g_up, rwkv_k_k, rwkv_k_a, rwkv_r_k,
               rwkv_lnx_w, rwkv_lnx_b, w_gate, b_gate, w_branch, w_out)
    h_ctx, h_lat = ctx, x
    for i in range(DEPTH):
        lp = {n: a[i] for n, a in zip(_LAYER_PARAM_NAMES, stacked)}
        lp['layer_idx'] = i
        lp['experts'] = (w_e_gate, w_e_up, w_e_down)
        h_ctx, h_lat = layer(h_ctx, h_lat, c, c_ctx, lp, router_w, router_b, i < DEPTH - 1)
    return final_rms_norm(h_lat, g_final)
```

```python
import functools
import math

import jax
import jax.numpy as jnp
from jax import lax
from jax.experimental import pallas as pl
from jax.experimental.pallas import tpu as pltpu

D_MODEL = 1024
BATCH = 4
SEQ = 8192
DEPTH = 2

GRID_W = 64
CTX_LEN = 256
N_BRANCH = 4
BRANCH_W = D_MODEL // N_BRANCH
HEAD_DIM = 64
BRANCH_HEADS = BRANCH_W // HEAD_DIM
BLOCK = 128
ROPE_BASE = 10000.0
NORM_EPS = 1e-6
NEG_INF = -1e30
ATT_Q_HEADS = BRANCH_HEADS
ATT_KV_HEADS = BRANCH_HEADS // 2
WINDOW = 128
RET_HEADS = BRANCH_HEADS
RET_CHUNK = 128
RET_GN_EPS = 1e-5
MLA_HEADS = BRANCH_HEADS
MLA_Q_RANK = 256
MLA_KV_RANK = 128
MLA_NOPE = HEAD_DIM
MLA_ROPE = HEAD_DIM // 2
MLA_V = HEAD_DIM
RWKV_HEADS = BRANCH_HEADS
RWKV_DECAY_RANK = 64
RWKV_A_RANK = 64
RWKV_GATE_RANK = 128
RWKV_GN_EPS = 64e-5
N_EXPERTS = 64
N_GROUPS = 8
EXPERTS_PER_GROUP = N_EXPERTS // N_GROUPS
TOP_K = 2
EXPERT_FF = 512
MOE_BLOCK = 256
A_SIZES = (ATT_Q_HEADS * HEAD_DIM, ATT_KV_HEADS * HEAD_DIM, ATT_KV_HEADS * HEAD_DIM)
RET_SIZES = (BRANCH_W,) * 5
MLA_SIZES = (MLA_Q_RANK, MLA_KV_RANK, MLA_ROPE)
RWKV_SIZES = (BRANCH_W, BRANCH_W, BRANCH_W, RWKV_DECAY_RANK, RWKV_DECAY_RANK,
              RWKV_A_RANK, RWKV_A_RANK, RWKV_GATE_RANK)
GROUP_SIZES = (sum(A_SIZES), sum(RET_SIZES), sum(MLA_SIZES), sum(RWKV_SIZES))
N_IN = sum(GROUP_SIZES)
F32 = jnp.float32


def split_sizes(x, sizes):
    out, o = [], 0
    for n in sizes:
        out.append(x[..., o:o + n])
        o += n
    return out


def heads(x, h):
    return x.reshape(x.shape[:-1] + (h, x.shape[-1] // h))


def rms_norm(x, g):
    xf = x.astype(F32)
    y = xf * lax.rsqrt(jnp.mean(xf * xf, axis=-1, keepdims=True) + NORM_EPS)
    return (y * g.astype(F32)).astype(x.dtype)


def head_norm(x, eps):
    xf = x.astype(F32)
    xc = xf - jnp.mean(xf, axis=-1, keepdims=True)
    return xc * lax.rsqrt(jnp.mean(xc * xc, axis=-1, keepdims=True) + eps)


def rope_tables(pos, dim):
    inv = ROPE_BASE ** (-jnp.arange(0, dim, 2, dtype=F32) / dim)
    ang = pos[:, None] * inv[None, :]
    return jnp.cos(ang), jnp.sin(ang)


def rope_rotate(x, cos, sin):
    m = x.shape[-1] // 2
    x1, x2 = x[..., :m], x[..., m:]
    c = cos[None, :, None, :]
    s = sin[None, :, None, :]
    return jnp.concatenate([x1 * c - x2 * s, x2 * c + x1 * s], axis=-1).astype(x.dtype)


def axial_tables(seq, dim):
    rows = seq // GRID_W
    row = jnp.repeat(jnp.arange(rows, dtype=F32), GRID_W)
    col = jnp.broadcast_to(jnp.arange(GRID_W, dtype=F32)[None, :], (rows, GRID_W)).reshape(-1)
    return rope_tables(row, dim // 2), rope_tables(col, dim // 2)


def axial_rope(x, tabs):
    (cos_r, sin_r), (cos_c, sin_c) = tabs
    half = x.shape[-1] // 2
    return jnp.concatenate([rope_rotate(x[..., :half], cos_r, sin_r),
                            rope_rotate(x[..., half:], cos_c, sin_c)], axis=-1)


def window_gqa(q, k, v, k_ctx, v_ctx, sink):
    b, s, hq, d = q.shape
    g = hq // ATT_KV_HEADS
    nb = s // BLOCK
    nw = 3 * BLOCK
    nc = k_ctx.shape[1]
    qb = (q * d ** -0.5).reshape(b, nb, BLOCK, ATT_KV_HEADS, g, d)
    pad = ((0, 0), (BLOCK, BLOCK), (0, 0), (0, 0))
    kp = jnp.pad(k, pad).reshape(b, nb + 2, BLOCK, ATT_KV_HEADS, d)
    vp = jnp.pad(v, pad).reshape(b, nb + 2, BLOCK, ATT_KV_HEADS, d)
    kw = jnp.concatenate([kp[:, :-2], kp[:, 1:-1], kp[:, 2:]], axis=2)
    vw = jnp.concatenate([vp[:, :-2], vp[:, 1:-1], vp[:, 2:]], axis=2)
    blk0 = jnp.arange(nb)[:, None, None] * BLOCK
    q_pos = blk0 + jnp.arange(BLOCK)[None, :, None]
    k_pos = blk0 - BLOCK + jnp.arange(nw)[None, None, :]
    valid = (jnp.abs(k_pos - q_pos) <= WINDOW) & (k_pos >= 0) & (k_pos < s)
    s_win = jnp.einsum('bnqhgd,bnkhd->bnhgqk', qb, kw).astype(F32)
    s_win = jnp.where(valid[None, :, None, None], s_win, NEG_INF)
    s_ctx = jnp.einsum('bnqhgd,bchd->bnhgqc', qb, k_ctx).astype(F32)
    s_sink = jnp.broadcast_to(sink.astype(F32).reshape(ATT_KV_HEADS, g, 1, 1), s_win.shape[:-1] + (1,))
    p = jax.nn.softmax(jnp.concatenate([s_win, s_ctx, s_sink], axis=-1), axis=-1).astype(v.dtype)
    o = (jnp.einsum('bnhgqk,bnkhd->bnqhgd', p[..., :nw], vw)
         + jnp.einsum('bnhgqc,bchd->bnqhgd', p[..., nw:nw + nc], v_ctx))
    return o.reshape(b, s, hq * d)


def ctx_gqa(q, k, v, sink):
    b, l, hq, d = q.shape
    g = hq // ATT_KV_HEADS
    qg = (q * d ** -0.5).reshape(b, l, ATT_KV_HEADS, g, d)
    sc = jnp.einsum('bqhgd,bkhd->bhgqk', qg, k).astype(F32)
    s_sink = jnp.broadcast_to(sink.astype(F32).reshape(ATT_KV_HEADS, g, 1, 1), sc.shape[:-1] + (1,))
    p = jax.nn.softmax(jnp.concatenate([sc, s_sink], axis=-1), axis=-1).astype(v.dtype)
    o = jnp.einsum('bhgqk,bkhd->bqhgd', p[..., :l], v)
    return o.reshape(b, l, hq * d)


def retention_log_gammas():
    lg = jnp.log(1.0 - jnp.exp(jnp.linspace(math.log(1.0 / 32), math.log(1.0 / 512), 2 * RET_HEADS, dtype=F32)))
    return lg[0::2], lg[1::2]


RET_TOKENS_PER_STEP = 1024


def _retention_kernel(q_ref, k_ref, v_ref, g_ref, cos_ref, sin_ref, d_ref, kdec_ref, qdec_ref, gch_ref, s0_ref,
                      y_ref, sfin_ref, s_scr, *, reverse, n_chunks):
    cl = RET_CHUNK
    nh, hd = s_scr.shape[:2]
    tb, width = q_ref.shape[1:]

    @pl.when(pl.program_id(1) == 0)
    def _():
        s_scr[...] = s0_ref[0]

    first_half = lax.broadcasted_iota(jnp.int32, (tb, width), 1) % hd < hd // 2
    cos, sin = cos_ref[...], sin_ref[...]

    def rope(x):
        swapped = jnp.where(first_half, -pltpu.roll(x, width - hd // 2, 1), pltpu.roll(x, hd // 2, 1))
        return x * cos + swapped * sin

    nb = nh * n_chunks
    shp = (nb, cl, hd)
    per_chunk = lambda z: jnp.broadcast_to(z[:, None], (nh, n_chunks) + z.shape[1:]).reshape((nb,) + z.shape[1:])
    q = _split_heads(rope(q_ref[0]), nh).reshape(shp)
    k = _split_heads(rope(k_ref[0]) * hd ** -0.5, nh).reshape(shp)
    v = _split_heads(v_ref[0], nh).reshape(shp)
    a = _dot(q, k, _NT, 1) * per_chunk(d_ref[...])
    o = _dot(a, v, _NN, 1)
    kv = _dot(k * per_chunk(kdec_ref[...]), v, _TN, 1).reshape(nh, n_chunks, hd, hd)

    s = s_scr[...]
    gch = gch_ref[...]
    starts = [None] * n_chunks
    order = range(n_chunks - 1, -1, -1) if reverse else range(n_chunks)
    for n in order:
        starts[n] = s
        s = s * gch + kv[:, n]
    s_scr[...] = s
    sfin_ref[0] = s
    s_in = jnp.stack(starts, axis=1).reshape(nb, hd, hd)
    o = o + _dot(q * per_chunk(qdec_ref[...]), s_in, _NN, 1)

    oc = o - jnp.mean(o, axis=-1, keepdims=True)
    normed = oc * lax.rsqrt(jnp.mean(oc * oc, axis=-1, keepdims=True) + RET_GN_EPS)
    g = g_ref[0]
    y_ref[0] = (g * jax.nn.sigmoid(g)) * _join_heads(normed.reshape(nh, tb, hd))


def _split_heads(x, nh):
    hd = x.shape[-1] // nh
    return jnp.stack([x[:, h * hd:(h + 1) * hd] for h in range(nh)], axis=0)


def _join_heads(y):
    return jnp.concatenate([y[h] for h in range(y.shape[0])], axis=-1)


def retention_direction(pb, gate_col, cos, sin, log_gamma, s0, reverse):
    bsz, t = pb.shape[:2]
    nh, hd = s0.shape[1:3]
    width = nh * hd
    tb = min(RET_TOKENS_PER_STEP, t)
    nt = t // tb
    assert t == nt * tb and tb % RET_CHUNK == 0
    cl = RET_CHUNK
    idx = jnp.arange(cl, dtype=F32)
    rel = (idx[None, :] - idx[:, None]) if reverse else (idx[:, None] - idx[None, :])
    lg = log_gamma[:, None, None]
    d_intra = jnp.where(rel >= 0, jnp.exp(jnp.maximum(rel, 0.0)[None] * lg), 0.0)
    kdec = jnp.exp((idx if reverse else cl - 1 - idx)[None, :, None] * lg)
    qdec = jnp.exp((cl - idx if reverse else idx + 1)[None, :, None] * lg)
    gch = jnp.exp(cl * lg)
    tblk = (lambda j: nt - 1 - j) if reverse else (lambda j: j)
    col_spec = lambda col: pl.BlockSpec((1, tb, width), lambda bi, j: (bi, tblk(j), col))
    tab_spec = pl.BlockSpec((tb, width), lambda bi, j: (tblk(j), 0))
    st_spec = pl.BlockSpec((1, nh, hd, hd), lambda bi, j: (bi, 0, 0, 0))
    const = lambda shape: pl.BlockSpec(shape, lambda bi, j: (0,) * len(shape))
    y, sfin = pl.pallas_call(
        functools.partial(_retention_kernel, reverse=reverse, n_chunks=tb // cl),
        grid=(bsz, nt),
        in_specs=[col_spec(0), col_spec(1), col_spec(2), col_spec(gate_col), tab_spec, tab_spec,
                  const((nh, cl, cl)), const((nh, cl, 1)), const((nh, cl, 1)), const((nh, 1, 1)), st_spec],
        out_specs=[col_spec(0), st_spec],
        out_shape=[jax.ShapeDtypeStruct((bsz, t, width), F32),
                   jax.ShapeDtypeStruct((bsz, nh, hd, hd), F32)],
        scratch_shapes=[pltpu.VMEM((nh, hd, hd), F32)],
        compiler_params=pltpu.CompilerParams(dimension_semantics=("parallel", "arbitrary"),
                                             vmem_limit_bytes=RWKV_VMEM_LIMIT),
        name="retention_rev" if reverse else "retention_fwd",
    )(pb, pb, pb, pb, cos, sin, d_intra, kdec, qdec, gch, s0)
    return y, sfin


def retention_mix(pb, cos, sin, lg_f, lg_b, s0f, s0b):
    tile = lambda tab: jnp.tile(jnp.concatenate([tab, tab], axis=-1), (1, RET_HEADS))
    cos_t, sin_t = tile(cos), tile(sin)
    yf, sf = retention_direction(pb, 3, cos_t, sin_t, lg_f, s0f, False)
    yb, sb = retention_direction(pb, 4, cos_t, sin_t, lg_b, s0b, True)
    return (yf + yb).astype(pb.dtype), sf, sb


def mla_project(pc, lp):
    cq, ckv, kr = split_sizes(pc, MLA_SIZES)
    q = heads(rms_norm(cq, lp['g_qnorm']) @ lp['w_uq'], MLA_HEADS)
    kv = heads(rms_norm(ckv, lp['g_kvnorm']) @ lp['w_ukv'], MLA_HEADS)
    return q[..., :MLA_NOPE], q[..., MLA_NOPE:], kv[..., :MLA_NOPE], kr[:, :, None, :], kv[..., MLA_NOPE:]


LANES = 128
SUBLANES = 8
MLA_Q_TILE = 512
MLA_KEY_UNIT = 256
MLA_KEY_TILE_MAX = 8448
MLA_VMEM_LIMIT = 56 * 1024 * 1024


def _mla_flash_kernel(q_ref, k_ref, v_ref, o_ref, m_scr, acc_scr, *, scale, tk, n_tiles):
    dv = o_ref.shape[-1]
    q = (q_ref[0, 0] * scale).astype(BF16)
    m_scr[...] = jnp.full_like(m_scr, NEG_INF)
    acc_scr[...] = jnp.zeros_like(acc_scr)

    def sweep(c, carry):
        keys = pl.ds(pl.multiple_of(c * tk, tk), tk)
        s = lax.dot_general(q, k_ref[0, 0, keys, :], (((1,), (1,)), ((), ())), preferred_element_type=F32)
        m_prev = m_scr[...]
        m_new = jnp.maximum(m_prev, jnp.max(s, axis=-1, keepdims=True))
        p = jnp.exp2(s - m_new).astype(BF16)
        acc_scr[...] = (jnp.exp2(m_prev - m_new) * acc_scr[...]
                        + jnp.dot(p, v_ref[0, 0, keys, :], preferred_element_type=F32))
        m_scr[...] = m_new
        return carry

    lax.fori_loop(0, n_tiles, sweep, 0)
    acc = acc_scr[...]
    o_ref[0, 0] = acc[:, :dv] / acc[:, dv:dv + 1]


def mla_attend(qn, qr, kn, kr, v):
    b, t, h, _ = qn.shape
    n = kn.shape[1]
    dv = v.shape[-1]
    scale = (MLA_NOPE + MLA_ROPE) ** -0.5 * math.log2(math.e)
    q = jnp.swapaxes(jnp.concatenate([qn, qr], axis=-1), 1, 2)
    kr_h = jnp.broadcast_to(kr, (b, n, h, MLA_ROPE))
    k = jnp.swapaxes(jnp.concatenate([kn, kr_h], axis=-1), 1, 2).astype(BF16)
    ones = jnp.ones((b, n, h, 1), v.dtype)
    zeros = jnp.zeros((b, n, h, LANES - dv - 1), v.dtype)
    vt = jnp.swapaxes(jnp.concatenate([v, ones, zeros], axis=-1), 1, 2).astype(BF16)
    dq = q.shape[-1]
    tq = min(MLA_Q_TILE, t)
    tk = max(d for d in range(MLA_KEY_UNIT, MLA_KEY_TILE_MAX + 1, MLA_KEY_UNIT) if n % d == 0)
    assert t % tq == 0
    o = pl.pallas_call(
        functools.partial(_mla_flash_kernel, scale=scale, tk=tk, n_tiles=n // tk),
        grid=(b, h, t // tq),
        in_specs=[pl.BlockSpec((1, 1, tq, dq), lambda bi, hi, qi: (bi, hi, qi, 0)),
                  pl.BlockSpec((1, 1, n, dq), lambda bi, hi, qi: (bi, hi, 0, 0)),
                  pl.BlockSpec((1, 1, n, LANES), lambda bi, hi, qi: (bi, hi, 0, 0))],
        out_specs=pl.BlockSpec((1, 1, tq, dv), lambda bi, hi, qi: (bi, hi, qi, 0)),
        out_shape=jax.ShapeDtypeStruct((b, h, t, dv), F32),
        scratch_shapes=[pltpu.VMEM((tq, 1), F32), pltpu.VMEM((tq, LANES), F32)],
        compiler_params=pltpu.CompilerParams(
            dimension_semantics=("parallel", "parallel", "arbitrary"),
            vmem_limit_bytes=MLA_VMEM_LIMIT),
        name="mla_flash",
    )(q, k, vt)
    return jnp.swapaxes(o, 1, 2).reshape(b, t, h * dv)


def centred_shift(p):
    prev = jnp.pad(p, ((0, 0), (1, 0), (0, 0)))[:, :-1]
    nxt = jnp.pad(p, ((0, 0), (0, 1), (0, 0)))[:, 1:]
    return 0.5 * (prev + nxt)


def rwkv_features(pd, lp):
    z = pd + (centred_shift(pd) - pd) * lp['rwkv_mu']
    zr, zk, zv, zwf, zwb, zaf, zab, zg = split_sizes(z, RWKV_SIZES)
    r = heads(zr, RWKV_HEADS)
    k = heads(zk, RWKV_HEADS)
    v = heads(zv, RWKV_HEADS)
    g = jax.nn.sigmoid(zg) @ lp['rwkv_g_up']
    kk = (k * heads(lp['rwkv_k_k'], RWKV_HEADS)).astype(F32)
    kk = kk / jnp.maximum(jnp.sqrt(jnp.sum(kk * kk, axis=-1, keepdims=True)), 1e-12)
    k_a = heads(lp['rwkv_k_a'], RWKV_HEADS)
    dirs = []
    for di, (zw, za) in enumerate(((zwf, zaf), (zwb, zab))):
        logw = -jax.nn.softplus(-(lp['rwkv_w0'][di] + jnp.tanh(zw) @ lp['rwkv_w_up'][di]).astype(F32)) - 0.5
        w = heads(-jnp.exp(logw), RWKV_HEADS)
        a = heads(jax.nn.sigmoid((lp['rwkv_a0'][di] + za @ lp['rwkv_a_up'][di]).astype(F32)), RWKV_HEADS)
        dirs.append((w, k * (1 + (a - 1) * k_a), kk * a))
    return r, v, g, kk, dirs


BF16 = jnp.bfloat16
RWKV_CHUNK = 64
RWKV_SUB = 16
RWKV_TOKENS_PER_STEP = 512
RWKV_PASSES = (1, 1)
RWKV_VMEM_LIMIT = 48 * 1024 * 1024
_NN = (((2,), (1,)), ((0,), (0,)))
_NT = (((2,), (2,)), ((0,), (0,)))
_TN = (((1,), (1,)), ((0,), (0,)))


def _bf16_parts(a, n):
    parts, rem = [], a
    for i in range(n):
        p = rem.astype(BF16)
        parts.append(p)
        if i + 1 < n:
            rem = rem - p.astype(F32)
    return parts


def _dot(a, b, dims, passes):
    if passes == 1:
        return lax.dot_general(a.astype(BF16), b.astype(BF16), dims, preferred_element_type=F32)
    a_hi, a_lo = _bf16_parts(a, 2)
    b_hi, b_lo = _bf16_parts(b, 2)
    out = lax.dot_general(a_hi, b_hi, dims, preferred_element_type=F32)
    out = out + lax.dot_general(a_hi, b_lo, dims, preferred_element_type=F32)
    return out + lax.dot_general(a_lo, b_hi, dims, preferred_element_type=F32)


def _unit_lower_inverse(l_mat, same_sub, passes):
    mm = lambda a, b: _dot(a, b, _NN, passes)
    ld = jnp.where(same_sub, l_mat, 0.0)
    lo = l_mat - ld
    p2 = mm(ld, ld)
    d = p2 - ld - mm(ld, p2)
    pw = p2
    span = 4
    while span < RWKV_SUB:
        pw = mm(pw, pw)
        d = d + pw + mm(d, pw)
        span *= 2
    n1 = lo + mm(d, lo)
    n2 = mm(n1, n1)
    x = n2 - n1 - mm(n1, n2)
    span = 4
    pw = n2
    while span < RWKV_CHUNK // RWKV_SUB:
        pw = mm(pw, pw)
        x = x + pw + mm(x, pw)
        span *= 2
    return x + d + mm(x, d)


def _rwkv_scan_kernel(r_ref, lw_ref, k_ref, v_ref, kk_ref, b_ref, s0_ref, y_ref, sfin_ref, s_scr,
                      *, reverse, n_chunks):
    cl = RWKV_CHUNK
    nh, hd = s_scr.shape[:2]
    tb = r_ref.shape[1]

    @pl.when(pl.program_id(1) == 0)
    def _():
        s_scr[...] = s0_ref[0]

    nb = nh * n_chunks
    shp = (nb, cl, hd)
    r = _split_heads(r_ref[0], nh).reshape(shp)
    lw = _split_heads(lw_ref[0], nh).reshape(shp)
    k = _split_heads(k_ref[0], nh).reshape(shp)
    v = _split_heads(v_ref[0], nh).reshape(shp)
    kk = _split_heads(kk_ref[0], nh).reshape(shp)
    b = _split_heads(b_ref[0], nh).reshape(shp)

    row = lax.broadcasted_iota(jnp.int32, (cl, cl), 0)
    col = lax.broadcasted_iota(jnp.int32, (cl, cl), 1)
    before = (col > row) if reverse else (col < row)
    upto = before | (col == row)
    same_sub = (row // RWKV_SUB) == (col // RWKV_SUB)

    tri = jnp.broadcast_to(jnp.where(upto, 1.0, 0.0).astype(BF16)[None], (nb, cl, cl))
    cum = sum(lax.dot_general(tri, p, _NN, preferred_element_type=F32) for p in _bf16_parts(lw, 3))
    last = 0 if reverse else cl - 1
    mid = cl // 2 if reverse else cl // 2 - 1
    tot = cum[:, last:last + 1, :]
    rho = cum[:, mid:mid + 1, :]
    cum_ex = cum - lw

    a_t = kk * jnp.exp(cum_ex - rho)
    r_t = r * jnp.exp(cum - rho)
    e_out = jnp.exp(rho - cum)
    b_t = b * e_out
    k_t = k * e_out
    a_0 = kk * jnp.exp(cum_ex)
    r_0 = r * jnp.exp(cum)
    e_end = jnp.exp(tot - cum)
    b_e = b * e_end
    k_e = k * e_end

    ps, pt = RWKV_PASSES
    sc = _dot(jnp.concatenate([a_t, r_t], axis=1), jnp.concatenate([b_t, k_t], axis=1), _NT, ps)
    l_mat = jnp.where(before, sc[:, :cl, :cl], 0.0)
    m_ak = jnp.where(before, sc[:, :cl, cl:], 0.0)
    m_rb = jnp.where(upto, sc[:, cl:, :cl], 0.0)
    m_rk = jnp.where(upto, sc[:, cl:, cl:], 0.0)
    t_m1 = _unit_lower_inverse(l_mat, same_sub, pt)

    mv = _dot(m_ak, v, _NN, ps)
    am = jnp.concatenate([a_0, mv], axis=2)
    qw = -(am + _dot(t_m1, am, _NN, ps))
    rq = _dot(m_rb, qw, _NN, ps)
    r_h = r_0 + rq[:, :, :hd]
    y_loc = rq[:, :, hd:] + _dot(m_rk, v, _NN, ps)
    eye = lax.broadcasted_iota(jnp.int32, (hd, hd), 0) == lax.broadcasted_iota(jnp.int32, (hd, hd), 1)
    pq = _dot(b_e, qw, _TN, ps)
    phi = (jnp.where(eye, jnp.exp(tot), 0.0) + pq[:, :, :hd]).reshape(nh, n_chunks, hd, hd)
    psi = (pq[:, :, hd:] + _dot(k_e, v, _TN, ps)).reshape(nh, n_chunks, hd, hd)

    s = s_scr[...]
    starts = [None] * n_chunks
    order = range(n_chunks - 1, -1, -1) if reverse else range(n_chunks)
    for n in order:
        starts[n] = s
        s = _dot(phi[:, n], s, _NN, 3) + psi[:, n]
    s_scr[...] = s
    sfin_ref[0] = s
    s_in = jnp.stack(starts, axis=1).reshape(nb, hd, hd)
    y = _dot(r_h, s_in, _NN, ps) + y_loc
    y_ref[0] = _join_heads(y.reshape(nh, tb, hd))


def rwkv_scan_blocked(r, lw, k, v, kk, b_, s0t, reverse):
    bsz, t, width = r.shape
    nh, hd = s0t.shape[1:3]
    tb = min(RWKV_TOKENS_PER_STEP, t)
    nt = t // tb
    assert t == nt * tb and tb % RWKV_CHUNK == 0
    if reverse:
        tmap = lambda bi, j: (bi, nt - 1 - j, 0)
    else:
        tmap = lambda bi, j: (bi, j, 0)
    seq_spec = pl.BlockSpec((1, tb, width), tmap)
    st_spec = pl.BlockSpec((1, nh, hd, hd), lambda bi, j: (bi, 0, 0, 0))
    y, sfin = pl.pallas_call(
        functools.partial(_rwkv_scan_kernel, reverse=reverse, n_chunks=tb // RWKV_CHUNK),
        grid=(bsz, nt),
        in_specs=[seq_spec] * 6 + [st_spec],
        out_specs=[seq_spec, st_spec],
        out_shape=[jax.ShapeDtypeStruct((bsz, t, width), F32),
                   jax.ShapeDtypeStruct((bsz, nh, hd, hd), F32)],
        scratch_shapes=[pltpu.VMEM((nh, hd, hd), F32)],
        compiler_params=pltpu.CompilerParams(dimension_semantics=("parallel", "arbitrary"),
                                             vmem_limit_bytes=RWKV_VMEM_LIMIT),
        name="rwkv_scan_rev" if reverse else "rwkv_scan_fwd",
    )(r, lw, k, v, kk, b_, s0t)
    return y, sfin


def rwkv_mix(feat, lp, s0f, s0b):
    r, v, g, kk, dirs = feat
    (lwf, kf, bf), (lwb, kb, bb) = dirs
    flat = lambda z: z.astype(F32).reshape(z.shape[0], z.shape[1], -1)
    rt, vt, kkt = flat(r), flat(v), flat(kk)
    yf, sf = rwkv_scan_blocked(rt, flat(lwf), flat(kf), vt, kkt, flat(bf), s0f, False)
    yb, sb = rwkv_scan_blocked(rt, flat(lwb), flat(kb), vt, kkt, flat(bb), s0b, True)
    y = head_norm(heads(yf + yb, RWKV_HEADS), RWKV_GN_EPS)
    y = y * heads(lp['rwkv_lnx_w'], RWKV_HEADS) + heads(lp['rwkv_lnx_b'], RWKV_HEADS)
    bonus = jnp.sum(r * (kf + kb) * lp['rwkv_r_k'], axis=-1, keepdims=True) * v
    out = (y + bonus).reshape(g.shape) * g
    return out.astype(g.dtype), sf, sb


MERGE_ROW_TILE = 512
MERGE_VMEM_LIMIT = 52 * 1024 * 1024


def _merge_kernel(u_ref, ya_ref, yb_ref, yc_ref, yd_ref, h_ref, mod_ref, g2_ref, wg_ref, bg_ref, wb_ref, wo_ref,
                  hn_ref, f_ref):
    u = u_ref[...].astype(BF16)
    acc = None
    for n, y_ref in enumerate((ya_ref, yb_ref, yc_ref, yd_ref)):
        gate = jax.nn.sigmoid(jnp.dot(u, wg_ref[n], preferred_element_type=F32) + bg_ref[n])
        term = gate * jnp.dot(y_ref[...].astype(BF16), wb_ref[n], preferred_element_type=F32)
        acc = term if acc is None else acc + term
    mix = jnp.dot(acc.astype(BF16), wo_ref[...], preferred_element_type=F32)
    mod = mod_ref[0]
    hn = h_ref[...] + mod[0:1] * mix
    hn_ref[...] = hn
    normed = hn * lax.rsqrt(jnp.mean(hn * hn, axis=-1, keepdims=True) + NORM_EPS) * g2_ref[...]
    f_ref[...] = normed * (1 + mod[2:3]) + mod[1:2]


def merge_branches(u, ys, lp, h, mod):
    lead, d = u.shape[:-1], u.shape[-1]
    rows = math.prod(lead)
    tm = min(MERGE_ROW_TILE, rows)
    rows_per_mod = rows // mod.shape[0]
    assert rows % tm == 0 and (mod.shape[0] == 1 or rows_per_mod % tm == 0)
    bw = ys[0].shape[-1]
    row_spec = lambda width: pl.BlockSpec((tm, width), lambda i: (i, 0))
    whole = lambda shape: pl.BlockSpec(shape, lambda i: (0,) * len(shape))
    hn, f = pl.pallas_call(
        _merge_kernel,
        grid=(rows // tm,),
        in_specs=[row_spec(d)] + [row_spec(bw)] * N_BRANCH
                 + [row_spec(d), pl.BlockSpec((1, 3, d), lambda i: ((i * tm) // rows_per_mod, 0, 0)), whole((1, d)),
                    whole((N_BRANCH, d, d)), whole((N_BRANCH, 1, d)), whole((N_BRANCH, bw, d)), whole((d, d))],
        out_specs=[row_spec(d), row_spec(d)],
        out_shape=[jax.ShapeDtypeStruct((rows, d), F32), jax.ShapeDtypeStruct((rows, d), F32)],
        compiler_params=pltpu.CompilerParams(dimension_semantics=("parallel",),
                                             vmem_limit_bytes=MERGE_VMEM_LIMIT),
        name="merge_branches",
    )(u.reshape(rows, d), *[y.reshape(rows, bw) for y in ys], h.reshape(rows, d), mod, lp['g_norm2'][None, :],
      lp['w_gate'].astype(BF16), lp['b_gate'][:, None, :], lp['w_branch'].astype(BF16), lp['w_out'].astype(BF16))
    return hn.reshape(lead + (d,)), f.reshape(lead + (d,))


INPROJ_ROW_TILE = 512
INPROJ_VMEM_LIMIT = 48 * 1024 * 1024


def _inproj_kernel(h_ref, mod_ref, g_ref, *refs):
    n = (len(refs) - 1) // 2
    w_refs, u_ref, p_refs = refs[:n], refs[n], refs[n + 1:]
    h = h_ref[...]
    mod = mod_ref[0]
    normed = h * lax.rsqrt(jnp.mean(h * h, axis=-1, keepdims=True) + NORM_EPS) * g_ref[...]
    u = (normed * (1 + mod[1:2]) + mod[0:1]).astype(BF16)
    u_ref[...] = u
    for w_ref, p_ref in zip(w_refs, p_refs):
        p_ref[...] = jnp.dot(u, w_ref[...], preferred_element_type=F32)


def adaln_in_proj(h, mod, g, w_in):
    bsz, t, d = h.shape
    rows = bsz * t
    tm = min(INPROJ_ROW_TILE, rows)
    rows_per_mod = rows // mod.shape[0]
    assert rows % tm == 0 and (mod.shape[0] == 1 or rows_per_mod % tm == 0)
    w_groups = [w.astype(BF16) for w in split_sizes(w_in, GROUP_SIZES)]
    row_spec = lambda width: pl.BlockSpec((tm, width), lambda i: (i, 0))
    whole = lambda shape: pl.BlockSpec(shape, lambda i: (0,) * len(shape))
    outs = pl.pallas_call(
        _inproj_kernel,
        grid=(rows // tm,),
        in_specs=[row_spec(d), pl.BlockSpec((1, 2, d), lambda i: ((i * tm) // rows_per_mod, 0, 0)), whole((1, d))]
                 + [whole(w.shape) for w in w_groups],
        out_specs=[row_spec(d)] + [row_spec(n) for n in GROUP_SIZES],
        out_shape=[jax.ShapeDtypeStruct((rows, d), BF16)]
                  + [jax.ShapeDtypeStruct((rows, n), F32) for n in GROUP_SIZES],
        compiler_params=pltpu.CompilerParams(dimension_semantics=("parallel",),
                                             vmem_limit_bytes=INPROJ_VMEM_LIMIT),
        name="adaln_in_proj",
    )(h.reshape(rows, d), mod, g[None, :], *w_groups)
    return [o.reshape(bsz, t, -1) for o in outs]


def token_mixers(lp, need_ctx, h_ctx, h_lat, mod1_ctx, mod1_lat, mod_ctx, mod_lat):
    b, s, _ = h_lat.shape
    l = h_ctx.shape[1]
    u_lat, pa_l, pb_l, pc_l, pd_l = adaln_in_proj(h_lat, mod1_lat, lp['g_norm1'], lp['w_in'])
    u_ctx, pa_c, pb_c, pc_c, pd_c = adaln_in_proj(h_ctx, mod1_ctx, lp['g_norm1'], lp['w_in'])
    tab_att = axial_tables(s, HEAD_DIM)
    tab_mla = axial_tables(s, MLA_ROPE)

    aq_l, ak_l, av_l = split_sizes(pa_l, A_SIZES)
    aq_c, ak_c, av_c = split_sizes(pa_c, A_SIZES)
    k_c = heads(ak_c, ATT_KV_HEADS)
    v_c = heads(av_c, ATT_KV_HEADS)
    ya_l = window_gqa(axial_rope(heads(aq_l, ATT_Q_HEADS), tab_att),
                      axial_rope(heads(ak_l, ATT_KV_HEADS), tab_att),
                      heads(av_l, ATT_KV_HEADS), k_c, v_c, lp['sink'])

    lg_f, lg_b = retention_log_gammas()
    cos_c, sin_c = rope_tables(jnp.arange(l, dtype=F32), HEAD_DIM)
    cos_l, sin_l = rope_tables(l + jnp.arange(s, dtype=F32), HEAD_DIM)
    zero_ret = jnp.zeros((b, RET_HEADS, HEAD_DIM, HEAD_DIM), F32)
    yb_c, sbf, sbb = retention_mix(pb_c, cos_c, sin_c, lg_f, lg_b, zero_ret, zero_ret)
    yb_l, _, _ = retention_mix(pb_l, cos_l, sin_l, lg_f, lg_b, sbf, sbb)

    qn_l, qr_l, kn_l, kr_l, vm_l = mla_project(pc_l, lp)
    qn_c, qr_c, kn_c, kr_c, vm_c = mla_project(pc_c, lp)
    qr_l = axial_rope(qr_l, tab_mla)
    kr_l = axial_rope(kr_l, tab_mla)
    yc_l = mla_attend(qn_l, qr_l, jnp.concatenate([kn_c, kn_l], axis=1),
                      jnp.concatenate([kr_c, kr_l], axis=1), jnp.concatenate([vm_c, vm_l], axis=1))

    zero_wkv = jnp.zeros((b, RWKV_HEADS, HEAD_DIM, HEAD_DIM), F32)
    yd_c, sdf, sdb = rwkv_mix(rwkv_features(pd_c, lp), lp, zero_wkv, zero_wkv)
    yd_l, _, _ = rwkv_mix(rwkv_features(pd_l, lp), lp, sdf, sdb)

    out_lat = merge_branches(u_lat, (ya_l, yb_l, yc_l, yd_l), lp, h_lat, mod_lat)
    if not need_ctx:
        return None, out_lat
    ya_c = ctx_gqa(heads(aq_c, ATT_Q_HEADS), k_c, v_c, lp['sink'])
    yc_c = mla_attend(qn_c, qr_c, kn_c, kr_c, vm_c)
    out_ctx = merge_branches(u_ctx, (ya_c, yb_c, yc_c, yd_c), lp, h_ctx, mod_ctx)
    return out_ctx, out_lat


def moe_ffn(x, router_w, router_b, layer_idx, w_e_gate, w_e_up, w_e_down):
    n_tok, dm = x.shape
    scores = jax.nn.sigmoid(x.astype(F32) @ router_w.astype(F32))
    grp = (scores + router_b.astype(F32)).reshape(n_tok, N_GROUPS, EXPERTS_PER_GROUP)
    g_sel = jnp.argmax(jnp.sum(lax.top_k(grp, TOP_K)[0], axis=-1), axis=-1)
    in_grp = lax.top_k(jnp.take_along_axis(grp, g_sel[:, None, None], axis=1)[:, 0], TOP_K)[1]
    e_idx = g_sel[:, None] * EXPERTS_PER_GROUP + in_grp
    w_sel = jnp.take_along_axis(scores, e_idx, axis=1)
    w_sel = w_sel / jnp.sum(w_sel, axis=-1, keepdims=True)
    n_asg = n_tok * TOP_K
    flat_e = e_idx.reshape(-1).astype(jnp.int32)
    flat_w = w_sel.reshape(-1)
    order = jnp.argsort(flat_e).astype(jnp.int32)
    rank = jnp.argsort(order).astype(jnp.int32)
    onehot = flat_e[:, None] == jnp.arange(N_EXPERTS, dtype=jnp.int32)[None, :]
    counts = jnp.sum(onehot, axis=0, dtype=jnp.int32)
    padded = (counts + MOE_BLOCK - 1) // MOE_BLOCK * MOE_BLOCK
    pad_end = jnp.cumsum(padded)
    pad_start = pad_end - padded
    start = jnp.cumsum(counts) - counts
    n_blocks = -(-n_asg // MOE_BLOCK) + N_EXPERTS
    n_rows = n_blocks * MOE_BLOCK
    blk_row0 = jnp.arange(n_blocks, dtype=jnp.int32) * MOE_BLOCK
    blk_e = jnp.minimum(jnp.sum(pad_end[None, :] <= blk_row0[:, None], axis=1, dtype=jnp.int32), N_EXPERTS - 1)
    off = (blk_row0 - pad_start[blk_e])[:, None] + jnp.arange(MOE_BLOCK, dtype=jnp.int32)[None, :]
    valid = (off < counts[blk_e][:, None]).reshape(-1)
    src = order[jnp.clip(start[blk_e][:, None] + off, 0, n_asg - 1).reshape(-1)]
    row_tok = jnp.where(valid, src // TOP_K, 0)
    row_w = jnp.where(valid, flat_w[src], 0.0)
    blk_valid = jnp.clip(counts[blk_e] - (blk_row0 - pad_start[blk_e]), 0, MOE_BLOCK)
    blk_groups = (blk_valid + MOE_GATHER_UNROLL - 1) // MOE_GATHER_UNROLL
    yb = moe_expert_blocks(blk_e, blk_groups.astype(jnp.int32), row_tok, x, row_w[:, None], layer_idx,
                           w_e_gate, w_e_up, w_e_down)
    shift = jnp.sum(jnp.where(onehot, (pad_start - start)[None, :], 0), axis=1, dtype=jnp.int32)
    slot = (rank + shift).reshape(n_tok, TOP_K)
    out = yb[slot[:, 0]]
    for j in range(1, TOP_K):
        out = out + yb[slot[:, j]]
    return out.astype(x.dtype)


MOE_VMEM_LIMIT = 40 * 1024 * 1024
MOE_GATHER_UNROLL = 8


def _moe_row_copy(x_hbm, xbuf, sem, tok, slot, r):
    return pltpu.make_async_copy(x_hbm.at[pl.ds(tok, 1)], xbuf.at[slot, pl.ds(r, 1)], sem.at[slot])


def _moe_expert_kernel(blk_e_ref, blk_groups_ref, row_tok_ref, x_hbm, w_ref, wg_ref, wu_ref, wd_ref, o_ref,
                       xbuf, sem):
    del blk_e_ref
    i = pl.program_id(0)
    slot = i % 2

    def for_each_row(block, fn):
        def body(gi, carry):
            for q in range(MOE_GATHER_UNROLL):
                fn(gi * MOE_GATHER_UNROLL + q, q)
            return carry
        lax.fori_loop(0, blk_groups_ref[block], body, 0)

    def issue(block, dst_slot):
        for_each_row(block, lambda r, q: _moe_row_copy(
            x_hbm, xbuf, sem, row_tok_ref[block * MOE_BLOCK + r], dst_slot, r).start(priority=q % 2))

    @pl.when(i == 0)
    def _():
        xbuf[...] = jnp.zeros_like(xbuf)
        issue(0, 0)

    @pl.when(i + 1 < pl.num_programs(0))
    def _():
        issue(i + 1, 1 - slot)

    for_each_row(i, lambda r, q: _moe_row_copy(x_hbm, xbuf, sem, 0, slot, r).wait())

    @pl.when(blk_groups_ref[i] == 0)
    def _():
        o_ref[...] = jnp.zeros_like(o_ref)

    @pl.when(blk_groups_ref[i] > 0)
    def _():
        x = jnp.concatenate([xbuf[slot, :, j, :] for j in range(xbuf.shape[2])], axis=-1).astype(BF16)
        g = jnp.dot(x, wg_ref[0, 0].astype(BF16), preferred_element_type=F32)
        u = jnp.dot(x, wu_ref[0, 0].astype(BF16), preferred_element_type=F32)
        hid = (g * jax.nn.sigmoid(g)) * u
        y = jnp.dot(hid.astype(BF16), wd_ref[0, 0].astype(BF16), preferred_element_type=F32)
        o_ref[...] = y * w_ref[...]


def moe_expert_blocks(blk_e, blk_groups, row_tok, x, row_w, layer_idx, w_e_gate, w_e_up, w_e_down):
    n_rows = row_tok.shape[0]
    n_tok, dm = x.shape
    n_blocks = n_rows // MOE_BLOCK
    ff = w_e_gate.shape[-1]
    assert dm == SUBLANES * LANES
    x = x.reshape(n_tok, SUBLANES, LANES)
    row_spec = lambda width: pl.BlockSpec((MOE_BLOCK, width), lambda i, be, bg, rt: (i, 0))
    expert_spec = lambda rows, cols: pl.BlockSpec((1, 1, rows, cols),
                                                  lambda i, be, bg, rt: (layer_idx, be[i], 0, 0))
    return pl.pallas_call(
        _moe_expert_kernel,
        grid_spec=pltpu.PrefetchScalarGridSpec(
            num_scalar_prefetch=3,
            grid=(n_blocks,),
            in_specs=[pl.BlockSpec(memory_space=pl.ANY), row_spec(1),
                      expert_spec(dm, ff), expert_spec(dm, ff), expert_spec(ff, dm)],
            out_specs=row_spec(dm),
            scratch_shapes=[pltpu.VMEM((2, MOE_BLOCK, SUBLANES, LANES), x.dtype), pltpu.SemaphoreType.DMA((2,))]),
        out_shape=jax.ShapeDtypeStruct((n_rows, dm), F32),
        compiler_params=pltpu.CompilerParams(dimension_semantics=("arbitrary",),
                                             vmem_limit_bytes=MOE_VMEM_LIMIT),
        name="moe_experts",
    )(blk_e, blk_groups, row_tok, x, row_w, w_e_gate, w_e_up, w_e_down)


def layer(h_ctx, h_lat, c, c_ctx, lp, router_w, router_b, need_ctx):
    b, s, d = h_lat.shape
    l = h_ctx.shape[1]
    m_lat = jnp.split((jax.nn.silu(c) @ lp['w_ada'] + lp['b_ada'])[:, None, :], 6, axis=-1)
    m_ctx = jnp.split(jax.nn.silu(c_ctx) @ lp['w_ada'] + lp['b_ada'], 6, axis=-1)
    mod1_lat = jnp.concatenate(m_lat[0:2], axis=1)
    mod1_ctx = jnp.stack(m_ctx[0:2], axis=0)[None]
    mod_lat = jnp.concatenate(m_lat[2:5], axis=1)
    mod_ctx = jnp.stack(m_ctx[2:5], axis=0)[None]
    out_ctx, (h_lat, f_lat) = token_mixers(lp, need_ctx, h_ctx, h_lat, mod1_ctx, mod1_lat, mod_ctx, mod_lat)
    f_lat = f_lat.reshape(b * s, d)
    if not need_ctx:
        ffn = moe_ffn(f_lat, router_w, router_b, lp['layer_idx'], *lp['experts'])
        return h_ctx, h_lat + m_lat[5] * ffn.reshape(b, s, d)
    h_ctx, f_ctx = out_ctx
    f_ctx = f_ctx.reshape(b * l, d)
    ffn = moe_ffn(jnp.concatenate([f_ctx, f_lat], axis=0), router_w, router_b, lp['layer_idx'], *lp['experts'])
    h_ctx = h_ctx + m_ctx[5] * ffn[:b * l].reshape(b, l, d)
    h_lat = h_lat + m_lat[5] * ffn[b * l:].reshape(b, s, d)
    return h_ctx, h_lat


def _final_norm_kernel(x_ref, g_ref, o_ref):
    x = x_ref[...]
    y = x * lax.rsqrt(jnp.mean(x * x, axis=-1, keepdims=True) + NORM_EPS)
    o_ref[...] = y * g_ref[...]


def final_rms_norm(x, g):
    b, s, d = x.shape
    rows = b * s
    tile = 1024
    out = pl.pallas_call(
        _final_norm_kernel,
        grid=(rows // tile,),
        in_specs=[pl.BlockSpec((tile, d), lambda i: (i, 0)), pl.BlockSpec((1, d), lambda i: (0, 0))],
        out_specs=pl.BlockSpec((tile, d), lambda i: (i, 0)),
        out_shape=jax.ShapeDtypeStruct((rows, d), x.dtype),
        name="final_rms_norm",
    )(x.reshape(rows, d), g.reshape(1, d))
    return out.reshape(b, s, d)


_LAYER_PARAM_NAMES = (
    'w_ada', 'b_ada', 'g_norm1', 'g_norm2', 'w_in', 'sink', 'g_qnorm', 'g_kvnorm', 'w_uq',
    'w_ukv', 'rwkv_mu', 'rwkv_w0', 'rwkv_w_up', 'rwkv_a0', 'rwkv_a_up', 'rwkv_g_up',
    'rwkv_k_k', 'rwkv_k_a', 'rwkv_r_k', 'rwkv_lnx_w', 'rwkv_lnx_b', 'w_gate', 'b_gate',
    'w_branch', 'w_out')


def kernel(x, c, ctx, c_ctx, w_ada, b_ada, g_norm1, g_norm2, w_in, sink, g_qnorm, g_kvnorm,
           w_uq, w_ukv, rwkv_mu, rwkv_w0, rwkv_w_up, rwkv_a0, rwkv_a_up, rwkv_g_up, rwkv_k_k,
           rwkv_k_a, rwkv_r_k, rwkv_lnx_w, rwkv_lnx_b, w_gate, b_gate, w_branch, w_out,
           router_w, router_b, w_e_gate, w_e_up, w_e_down, g_final):
    stacked = (w_ada, b_ada, g_norm1, g_norm2, w_in, sink, g_qnorm, g_kvnorm, w_uq, w_ukv, rwkv_mu,
               rwkv_w0, rwkv_w_up, rwkv_a0, rwkv_a_up, rwkv_g_up, rwkv_k_k, rwkv_k_a, rwkv_r_k,
               rwkv_lnx_w, rwkv_lnx_b, w_gate, b_gate, w_branch, w_out)
    h_ctx, h_lat = ctx, x
    for i in range(DEPTH):
        lp = {n: a[i] for n, a in zip(_LAYER_PARAM_NAMES, stacked)}
        lp['layer_idx'] = i
        lp['experts'] = (w_e_gate, w_e_up, w_e_down)
        h_ctx, h_lat = layer(h_ctx, h_lat, c, c_ctx, lp, router_w, router_b, i < DEPTH - 1)
    return final_rms_norm(h_lat, g_final)
```
